```python
import jax, jax.numpy as jnp
from jax import lax
import numpy as np

D_MODEL = 1024
BATCH = 4
SEQ = 8192
DEPTH = 2

CHUNK = 64
D_CONV = D_MODEL // 4
CONV_WIDTH = 31
SB_HEAD_DIM = 64
D_SB = D_MODEL // 2
N_SB_HEADS = D_SB // SB_HEAD_DIM
RET_HEAD_DIM = 64
D_RET = D_MODEL // 4
N_RET_HEADS = D_RET // RET_HEAD_DIM
D_MIX = D_CONV + D_SB + D_RET
D_IN_PROJ = 2 * D_CONV + 3 * D_SB + 4 * D_RET
D_FF = ((8 * D_MODEL // 3 + 127) // 128) * 128
Q_BLOCK = 128
ROPE_BASE = 10000.0
EPS = 1e-6

kernel_name = "hybrid_conv_stickbreak_retention_macaron"


def rms_norm(x, g):
    xf = x.astype(jnp.float32)
    y = xf * lax.rsqrt(jnp.mean(xf * xf, axis=-1, keepdims=True) + EPS)
    return (y * g.astype(jnp.float32)).astype(x.dtype)


def layer_norm(x, g, b):
    xf = x.astype(jnp.float32)
    mu = jnp.mean(xf, axis=-1, keepdims=True)
    xc = xf - mu
    var = jnp.mean(xc * xc, axis=-1, keepdims=True)
    return (xc * lax.rsqrt(var + EPS) * g.astype(jnp.float32) + b.astype(jnp.float32)).astype(x.dtype)


def swiglu(h, w_in, w_out):
    gate, up = jnp.split(h @ w_in, 2, axis=-1)
    return (jax.nn.silu(gate) * up) @ w_out


def conv_module(u, conv_w, conv_b, ln_g, ln_b):
    a, b = jnp.split(u, 2, axis=-1)
    v = a * jax.nn.sigmoid(b)
    v = jnp.pad(v, ((0, 0), (CONV_WIDTH - 1, 0), (0, 0)))
    y = lax.conv_general_dilated(
        v, conv_w[:, None, :].astype(v.dtype), window_strides=(1,), padding="VALID",
        dimension_numbers=("NWC", "WIO", "NWC"), feature_group_count=D_CONV)
    y = y + conv_b
    return jax.nn.silu(layer_norm(y, ln_g, ln_b))


def stick_breaking(q, k, v):
    B, S, H, Dh = q.shape
    nb = S // Q_BLOCK
    qb = q.reshape(B, nb, Q_BLOCK, H, Dh).transpose(1, 0, 2, 3, 4)
    kpos = jnp.arange(S)
    scale = Dh ** -0.5

    def block(args):
        qi, i = args
        z = jnp.einsum("bqhd,bkhd->bhqk", qi, k,
                       preferred_element_type=jnp.float32) * scale
        qpos = i * Q_BLOCK + jnp.arange(Q_BLOCK)
        mask = kpos[None, :] < qpos[:, None]
        log_beta = jax.nn.log_sigmoid(z)
        log_not = jnp.where(mask, jax.nn.log_sigmoid(-z), 0.0)
        remain = lax.cumsum(log_not, axis=3, reverse=True) - log_not
        w = jnp.where(mask, jnp.exp(log_beta + remain), 0.0)
        return jnp.einsum("bhqk,bkhd->bqhd", w.astype(v.dtype), v)

    out = lax.map(block, (qb, jnp.arange(nb)))
    return out.transpose(1, 0, 2, 3, 4).reshape(B, S, H, Dh)


def rotary(x, pos):
    half = x.shape[-1] // 2
    inv = 1.0 / (ROPE_BASE ** (jnp.arange(half, dtype=jnp.float32) / half))
    ang = pos.astype(jnp.float32)[:, None] * inv[None, :]
    cos = jnp.cos(ang)[None, :, None, :]
    sin = jnp.sin(ang)[None, :, None, :]
    x1 = x[..., :half].astype(jnp.float32)
    x2 = x[..., half:].astype(jnp.float32)
    return jnp.concatenate([x1 * cos - x2 * sin, x1 * sin + x2 * cos], axis=-1).astype(x.dtype)


def retention(q, k, v):
    B, S, H, Dh = q.shape
    nc = S // CHUNK
    log_gamma = jnp.log1p(-jnp.exp2(-5.0 - jnp.arange(H, dtype=jnp.float32)))
    qc = (q * (Dh ** -0.5)).reshape(B, nc, CHUNK, H, Dh)
    kc = k.reshape(B, nc, CHUNK, H, Dh)
    vc = v.reshape(B, nc, CHUNK, H, Dh)
    idx = jnp.arange(CHUNK, dtype=jnp.float32)
    d_intra = jnp.exp(log_gamma[:, None, None] * jnp.abs(idx[:, None] - idx[None, :]))
    scores = jnp.einsum("bnihd,bnjhd->bnhij", qc, kc,
                        preferred_element_type=jnp.float32) * d_intra
    y_intra = jnp.einsum("bnhij,bnjhe->bnihe", scores, vc.astype(jnp.float32))
    k_decay = jnp.exp(log_gamma[None, :] * (CHUNK - 1 - idx)[:, None])
    kv = jnp.einsum("bnjhd,jh,bnjhe->bnhde", kc.astype(jnp.float32), k_decay,
                    vc.astype(jnp.float32))
    chunk_decay = jnp.exp(log_gamma * CHUNK)[None, :, None, None]

    def step(state, kv_n):
        return chunk_decay * state + kv_n, state

    _, s_prev = lax.scan(step, jnp.zeros((B, H, Dh, Dh), jnp.float32),
                         kv.transpose(1, 0, 2, 3, 4))
    s_prev = s_prev.transpose(1, 0, 2, 3, 4)
    q_decay = jnp.exp(log_gamma[None, :] * (idx + 1.0)[:, None])
    y_cross = jnp.einsum("bnihd,ih,bnhde->bnihe", qc.astype(jnp.float32), q_decay, s_prev)
    return (y_intra + y_cross).reshape(B, S, H, Dh)


def head_norm(y, g):
    B, S, H, Dh = y.shape
    mu = jnp.mean(y, axis=-1, keepdims=True)
    yc = y - mu
    var = jnp.mean(yc * yc, axis=-1, keepdims=True)
    return (yc * lax.rsqrt(var + EPS)).reshape(B, S, H * Dh) * g.astype(jnp.float32)


def hybrid_mixer(h, w_in, conv_w, conv_b, conv_ln_g, conv_ln_b, ret_norm_g, w_out, pos):
    B, S, _ = h.shape
    o1 = 2 * D_CONV
    o2 = o1 + D_SB
    o3 = o2 + D_SB
    o4 = o3 + D_SB
    o5 = o4 + D_RET
    o6 = o5 + D_RET
    o7 = o6 + D_RET
    u_conv, q_sb, k_sb, v_sb, q_r, k_r, v_r, g_r = jnp.split(
        h @ w_in, [o1, o2, o3, o4, o5, o6, o7], axis=-1)
    y_conv = conv_module(u_conv, conv_w, conv_b, conv_ln_g, conv_ln_b)
    sb = lambda t: t.reshape(B, S, N_SB_HEADS, SB_HEAD_DIM)
    y_sb = stick_breaking(sb(q_sb), sb(k_sb), sb(v_sb)).reshape(B, S, D_SB)
    rt = lambda t: t.reshape(B, S, N_RET_HEADS, RET_HEAD_DIM)
    y_r = retention(rotary(rt(q_r), pos), rotary(rt(k_r), pos), rt(v_r))
    y_r = jax.nn.silu(g_r.astype(jnp.float32)) * head_norm(y_r, ret_norm_g)
    y = jnp.concatenate([y_conv, y_sb, y_r.astype(h.dtype)], axis=-1)
    return y @ w_out


def setup_inputs(seed: int = 0) -> dict:
    key = jax.random.key(seed)
    ks = jax.random.split(key, 20)
    f32 = jnp.float32
    nrm = lambda k, shape, scale: jax.random.normal(k, shape, f32) * scale
    gain = lambda k, shape: 1.0 + 0.02 * jax.random.normal(k, shape, f32)
    return {
        "x": jax.random.normal(ks[0], (BATCH, SEQ, D_MODEL), f32),
        "ffn1_norm": gain(ks[1], (DEPTH, D_MODEL)),
        "ffn1_w_in": nrm(ks[2], (DEPTH, D_MODEL, 2 * D_FF), D_MODEL ** -0.5),
        "ffn1_w_out": nrm(ks[3], (DEPTH, D_FF, D_MODEL), D_FF ** -0.5),
        "mix_norm": gain(ks[4], (DEPTH, D_MODEL)),
        "mix_w_in": nrm(ks[5], (DEPTH, D_MODEL, D_IN_PROJ), D_MODEL ** -0.5),
        "conv_w": nrm(ks[6], (DEPTH, CONV_WIDTH, D_CONV), CONV_WIDTH ** -0.5),
        "conv_b": nrm(ks[7], (DEPTH, D_CONV), 0.02),
        "conv_ln_g": gain(ks[8], (DEPTH, D_CONV)),
        "conv_ln_b": nrm(ks[9], (DEPTH, D_CONV), 0.02),
        "ret_norm_g": gain(ks[10], (DEPTH, D_RET)),
        "mix_w_out": nrm(ks[11], (DEPTH, D_MIX, D_MODEL), D_MIX ** -0.5),
        "ffn2_norm": gain(ks[12], (DEPTH, D_MODEL)),
        "ffn2_w_in": nrm(ks[13], (DEPTH, D_MODEL, 2 * D_FF), D_MODEL ** -0.5),
        "ffn2_w_out": nrm(ks[14], (DEPTH, D_FF, D_MODEL), D_FF ** -0.5),
        "final_norm": gain(ks[15], (D_MODEL,)),
    }


def reference(x, ffn1_norm, ffn1_w_in, ffn1_w_out, mix_norm, mix_w_in, conv_w, conv_b,
              conv_ln_g, conv_ln_b, ret_norm_g, mix_w_out, ffn2_norm, ffn2_w_in, ffn2_w_out,
              final_norm):
    S = x.shape[1]
    pos = jnp.arange(S)
    for l in range(DEPTH):
        x = x + 0.5 * swiglu(rms_norm(x, ffn1_norm[l]), ffn1_w_in[l], ffn1_w_out[l])
        x = x + hybrid_mixer(rms_norm(x, mix_norm[l]), mix_w_in[l], conv_w[l], conv_b[l],
                             conv_ln_g[l], conv_ln_b[l], ret_norm_g[l], mix_w_out[l], pos)
        x = x + 0.5 * swiglu(rms_norm(x, ffn2_norm[l]), ffn2_w_in[l], ffn2_w_out[l])
    return rms_norm(x, final_norm)
```

```python
import functools

import jax
import jax.numpy as jnp
import numpy as np
from jax import lax
from jax.experimental import pallas as pl
from jax.experimental.pallas import tpu as pltpu

F32 = jnp.float32
BF16 = jnp.bfloat16

EPS = 1e-6
ROPE_BASE = 10000.0
CHUNK = 64
CONV_WIDTH = 31
HEAD_DIM = 64
LANES = 128
MXU_DIM = 256

ROW_TILE = 512
FF_CHUNK = MXU_DIM
OUT_TILE = 512
CONV_TILE = 512
CONV_HALO = 32
CONV_SUB = 64
SB_BLOCK = 128
RET_BLOCK = 256
VMEM_LIMIT = 56 * 1024 * 1024


def _cparams(n_axes):
    return pltpu.CompilerParams(dimension_semantics=("arbitrary",) * n_axes,
                                vmem_limit_bytes=VMEM_LIMIT)


def _rms(x, g):
    return (x * lax.rsqrt(jnp.mean(x * x, axis=-1, keepdims=True) + EPS)) * g


def _silu(x):
    return x * jax.nn.sigmoid(x)


def _full(shape):
    return pl.BlockSpec(shape, lambda *_: (0,) * len(shape))


def _ffn_kernel(*refs, n_chunks, n_out, final):
    if final:
        x_ref, g_ref, win_ref, wout_ref, fg_ref, o_ref, h_scr, a_scr = refs
    else:
        x_ref, g_ref, win_ref, wout_ref, o_ref, h_scr, a_scr = refs
    h_scr[...] = _rms(x_ref[...], g_ref[...]).astype(BF16)
    for j in range(n_chunks):
        gu = jnp.dot(h_scr[...], win_ref[j], preferred_element_type=F32)
        a = _silu(gu[:, :FF_CHUNK]) * gu[:, FF_CHUNK:]
        a_scr[:, j * FF_CHUNK:(j + 1) * FF_CHUNK] = a.astype(BF16)
    for m in range(n_out):
        sl = slice(m * OUT_TILE, (m + 1) * OUT_TILE)
        y = jnp.dot(a_scr[...], wout_ref[:, sl], preferred_element_type=F32)
        o_ref[:, sl] = x_ref[:, sl] + 0.5 * y
    if final:
        o_ref[...] = _rms(o_ref[...], fg_ref[...])


def _ffn(x, g, win, wout, final_g=None):
    n, d = x.shape
    n_chunks = win.shape[0]
    d_ff = wout.shape[0]
    final = final_g is not None
    in_specs = [pl.BlockSpec((ROW_TILE, d), lambda i: (i, 0)), _full((1, d)),
                _full(win.shape), _full(wout.shape)]
    args = [x, g, win, wout]
    if final:
        in_specs.append(_full((1, d)))
        args.append(final_g)
    return pl.pallas_call(
        functools.partial(_ffn_kernel, n_chunks=n_chunks, n_out=d // OUT_TILE, final=final),
        out_shape=jax.ShapeDtypeStruct((n, d), F32),
        grid=(n // ROW_TILE,),
        in_specs=in_specs,
        out_specs=pl.BlockSpec((ROW_TILE, d), lambda i: (i, 0)),
        scratch_shapes=[pltpu.VMEM((ROW_TILE, d), BF16), pltpu.VMEM((ROW_TILE, d_ff), BF16)],
        compiler_params=_cparams(1),
        name="ffn",
    )(*args)


def _rotary(t, cos, sin_signed, first_half):
    partner = jnp.where(first_half, pltpu.roll(t, LANES - HEAD_DIM // 2, 1),
                        pltpu.roll(t, HEAD_DIM // 2, 1))
    return t * cos + partner * sin_signed


def _inproj_kernel(x_ref, g_ref, w_ref, cos_ref, sin_ref, uconv_ref, q_ref, k_ref, v_ref,
                   qkr_ref, vgr_ref, h_scr, *, d_conv2, d_sb, d_ret):
    h_scr[...] = _rms(x_ref[...], g_ref[...]).astype(BF16)

    def proj(lo, width):
        return jnp.dot(h_scr[...], w_ref[:, lo:lo + width], preferred_element_type=F32)

    uconv_ref[...] = proj(0, d_conv2)
    o = d_conv2
    q_ref[...] = proj(o, d_sb).astype(BF16)
    k_ref[...] = proj(o + d_sb, d_sb).astype(BF16)
    v_ref[...] = proj(o + 2 * d_sb, d_sb).astype(BF16)
    o += 3 * d_sb
    qk = proj(o, 2 * d_ret)
    cos = cos_ref[...]
    sin = sin_ref[...]
    lane = lax.broadcasted_iota(jnp.int32, (1, LANES), 1)
    first_half = (lane % HEAD_DIM) < (HEAD_DIM // 2)
    for c in range(2 * d_ret // LANES):
        sl = slice(c * LANES, (c + 1) * LANES)
        qkr_ref[:, sl] = _rotary(qk[:, sl], cos, sin, first_half)
    vgr_ref[...] = proj(o + 2 * d_ret, 2 * d_ret)


def _inproj(x, g, w, cos_tab, sin_tab, seq, d_conv2, d_sb, d_ret):
    n, d = x.shape
    tiles_per_seq = seq // ROW_TILE
    row = lambda i: (i, 0)
    pos = lambda i: (i % tiles_per_seq, 0)
    outs = [jax.ShapeDtypeStruct((n, d_conv2), F32)] + [jax.ShapeDtypeStruct((n, d_sb), BF16)] * 3 \
        + [jax.ShapeDtypeStruct((n, 2 * d_ret), F32)] * 2
    return pl.pallas_call(
        functools.partial(_inproj_kernel, d_conv2=d_conv2, d_sb=d_sb, d_ret=d_ret),
        out_shape=outs,
        grid=(n // ROW_TILE,),
        in_specs=[pl.BlockSpec((ROW_TILE, d), row), _full((1, d)), _full(w.shape),
                  pl.BlockSpec((ROW_TILE, LANES), pos), pl.BlockSpec((ROW_TILE, LANES), pos)],
        out_specs=[pl.BlockSpec((ROW_TILE, s.shape[1]), row) for s in outs],
        scratch_shapes=[pltpu.VMEM((ROW_TILE, d), BF16)],
        compiler_params=_cparams(1),
        name="mixer_inproj",
    )(x, g, w, cos_tab, sin_tab)


def _conv_kernel(u_ref, halo_ref, cw_ref, cb_ref, lg_ref, lb_ref, o_ref, v_scr, *, d_conv):
    def glu(u):
        return u[:, :d_conv] * jax.nn.sigmoid(u[:, d_conv:])

    first = pl.program_id(1) == 0
    v_scr[0:CONV_HALO, :] = jnp.where(first, 0.0, glu(halo_ref[0]))
    v_scr[CONV_HALO:, :] = glu(u_ref[0])
    off = CONV_HALO - (CONV_WIDTH - 1)
    for r in range(CONV_TILE // CONV_SUB):
        acc = jnp.zeros((CONV_SUB, d_conv), F32)
        for w in range(CONV_WIDTH):
            acc = acc + cw_ref[w:w + 1, :] * v_scr[pl.ds(r * CONV_SUB + off + w, CONV_SUB), :]
        y = acc + cb_ref[...]
        mu = jnp.mean(y, axis=-1, keepdims=True)
        yc = y - mu
        var = jnp.mean(yc * yc, axis=-1, keepdims=True)
        ln = yc * lax.rsqrt(var + EPS) * lg_ref[...] + lb_ref[...]
        o_ref[0, r * CONV_SUB:(r + 1) * CONV_SUB, :] = _silu(ln).astype(BF16)


def _conv(u, cw, cb, lg, lb):
    b, s, d2 = u.shape
    d_conv = d2 // 2
    halo_per_tile = CONV_TILE // CONV_HALO
    return pl.pallas_call(
        functools.partial(_conv_kernel, d_conv=d_conv),
        out_shape=jax.ShapeDtypeStruct((b, s, d_conv), BF16),
        grid=(b, s // CONV_TILE),
        in_specs=[pl.BlockSpec((1, CONV_TILE, d2), lambda bi, i: (bi, i, 0)),
                  pl.BlockSpec((1, CONV_HALO, d2),
                               lambda bi, i: (bi, jnp.maximum(i * halo_per_tile - 1, 0), 0)),
                  _full(cw.shape), _full((1, d_conv)), _full((1, d_conv)), _full((1, d_conv))],
        out_specs=pl.BlockSpec((1, CONV_TILE, d_conv), lambda bi, i: (bi, i, 0)),
        scratch_shapes=[pltpu.VMEM((CONV_HALO + CONV_TILE, d_conv), F32)],
        compiler_params=_cparams(2),
        name="conv_module",
    )(u, u, cw, cb, lg, lb)


def _sb_kernel(q_ref, k_ref, v_ref, tri_ref, o_ref):
    i = pl.program_id(2)
    q = q_ref[0]
    lane = lax.broadcasted_iota(jnp.int32, (1, LANES), 1)
    row = lax.broadcasted_iota(jnp.int32, (SB_BLOCK, SB_BLOCK), 0)
    col = lax.broadcasted_iota(jnp.int32, (SB_BLOCK, SB_BLOCK), 1)
    causal = col < row
    tri = tri_ref[...]

    def block(qh, j, acc, carry, mask):
        start = pl.multiple_of(j * SB_BLOCK, SB_BLOCK)
        k_blk = k_ref[0, pl.ds(start, SB_BLOCK), :]
        v_blk = v_ref[0, pl.ds(start, SB_BLOCK), :]
        z = lax.dot_general(qh, k_blk, (((1,), (1,)), ((), ())), preferred_element_type=F32)
        sp = jnp.maximum(z, 0.0) + jnp.log(1.0 + jnp.exp(-jnp.abs(z)))
        if mask is not None:
            sp = jnp.where(mask, sp, 0.0)
        hi = sp.astype(BF16)
        lo = (sp - hi.astype(F32)).astype(BF16)
        rt = jnp.dot(jnp.concatenate([hi, lo], axis=1), tri, preferred_element_type=F32)
        w = jnp.exp(z - (rt[:, :SB_BLOCK] + carry))
        if mask is not None:
            w = jnp.where(mask, w, 0.0)
        acc = acc + jnp.dot(w.astype(BF16), v_blk, preferred_element_type=F32)
        return acc, carry + rt[:, SB_BLOCK:]

    outs = []
    for hh in range(LANES // HEAD_DIM):
        qh = jnp.where(lane // HEAD_DIM == hh, q, jnp.zeros_like(q))
        zero = jnp.zeros((SB_BLOCK, LANES), F32)
        acc, carry = block(qh, i, zero, zero, causal)

        def body(t, st, qh=qh):
            return block(qh, i - 1 - t, st[0], st[1], None)

        acc, carry = lax.fori_loop(0, i, body, (acc, carry))
        outs.append(acc)
    o_ref[0] = jnp.where(lane < HEAD_DIM, outs[0], outs[1]).astype(BF16)


def _stick_breaking(q, k, v, tri):
    b, s, d_sb = q.shape
    blk = pl.BlockSpec((1, SB_BLOCK, LANES), lambda bi, p, i: (bi, i, p))
    whole = pl.BlockSpec((1, s, LANES), lambda bi, p, i: (bi, 0, p))
    return pl.pallas_call(
        _sb_kernel,
        out_shape=jax.ShapeDtypeStruct((b, s, d_sb), BF16),
        grid=(b, d_sb // LANES, s // SB_BLOCK),
        in_specs=[blk, whole, whole, _full(tri.shape)],
        out_specs=blk,
        compiler_params=_cparams(3),
        name="stick_breaking",
    )(q, k, v, tri)


def _ret_kernel(qk_ref, vg_ref, dmat_ref, qdec_ref, kdec_ref, cdec_ref, bd_ref, avg_ref, gain_ref,
                o_ref, state_scr, *, d_ret):
    n_heads = d_ret // HEAD_DIM

    @pl.when(pl.program_id(1) == 0)
    def _():
        state_scr[...] = jnp.zeros_like(state_scr)

    q = qk_ref[0, :, :d_ret]
    k = qk_ref[0, :, d_ret:]
    v = vg_ref[0, :, :d_ret]
    g = vg_ref[0, :, d_ret:]
    lane = lax.broadcasted_iota(jnp.int32, (1, d_ret), 1)
    kb = k.astype(BF16)
    probs, vals = [], []
    for h in range(n_heads):
        in_head = lane // HEAD_DIM == h
        qh = jnp.where(in_head, q, 0.0).astype(BF16)
        sc = lax.dot_general(qh, kb, (((1,), (1,)), ((), ())), preferred_element_type=F32)
        probs.append((sc * dmat_ref[h]).astype(BF16))
        vals.append(jnp.where(in_head, v, 0.0).astype(BF16))
    y = jnp.dot(jnp.concatenate(probs, axis=1), jnp.concatenate(vals, axis=0),
                preferred_element_type=F32)
    state = state_scr[...]
    y = y + jnp.dot((q * qdec_ref[...]).astype(BF16), state.astype(BF16),
                    preferred_element_type=F32)
    kv = lax.dot_general((k * kdec_ref[...]).astype(BF16), v.astype(BF16),
                         (((0,), (0,)), ((), ())), preferred_element_type=F32)
    state_scr[...] = cdec_ref[...] * state + bd_ref[...] * kv

    def head_mean(t):
        hi = t.astype(BF16)
        lo = (t - hi.astype(F32)).astype(BF16)
        return jnp.dot(jnp.concatenate([hi, lo], axis=1), avg_ref[...], preferred_element_type=F32)

    yc = y - head_mean(y)
    var = head_mean(yc * yc)
    o_ref[0] = (_silu(g) * (yc * lax.rsqrt(var + EPS) * gain_ref[...])).astype(BF16)


def _retention(qk, vg, tabs, gain):
    b, s, d2 = qk.shape
    d_ret = d2 // 2
    blk = pl.BlockSpec((1, RET_BLOCK, d2), lambda bi, i: (bi, i, 0))
    return pl.pallas_call(
        functools.partial(_ret_kernel, d_ret=d_ret),
        out_shape=jax.ShapeDtypeStruct((b, s, d_ret), BF16),
        grid=(b, s // RET_BLOCK),
        in_specs=[blk, blk] + [_full(t.shape) for t in tabs] + [_full((1, d_ret))],
        out_specs=pl.BlockSpec((1, RET_BLOCK, d_ret), lambda bi, i: (bi, i, 0)),
        scratch_shapes=[pltpu.VMEM((d_ret, d_ret), F32)],
        compiler_params=_cparams(2),
        name="retention",
    )(qk, vg, *tabs, gain)


def _outproj_kernel(x_ref, yc_ref, ys_ref, yr_ref, w_ref, o_ref, *, d_conv, d_sb):
    y = jnp.dot(yc_ref[...], w_ref[0:d_conv, :], preferred_element_type=F32)
    y = y + jnp.dot(ys_ref[...], w_ref[d_conv:d_conv + d_sb, :], preferred_element_type=F32)
    y = y + jnp.dot(yr_ref[...], w_ref[d_conv + d_sb:, :], preferred_element_type=F32)
    o_ref[...] = x_ref[...] + y


def _outproj(x, yc, ys, yr, w):
    n, d = x.shape
    row = lambda i: (i, 0)
    return pl.pallas_call(
        functools.partial(_outproj_kernel, d_conv=yc.shape[1], d_sb=ys.shape[1]),
        out_shape=jax.ShapeDtypeStruct((n, d), F32),
        grid=(n // ROW_TILE,),
        in_specs=[pl.BlockSpec((ROW_TILE, d), row)]
        + [pl.BlockSpec((ROW_TILE, t.shape[1]), row) for t in (yc, ys, yr)] + [_full(w.shape)],
        out_specs=pl.BlockSpec((ROW_TILE, d), row),
        compiler_params=_cparams(1),
        name="mixer_outproj",
    )(x, yc, ys, yr, w)


def _rotary_tables(seq):
    half = HEAD_DIM // 2
    inv = 1.0 / (ROPE_BASE ** (jnp.arange(half, dtype=F32) / half))
    lane = jnp.arange(LANES)
    ang = jnp.arange(seq).astype(F32)[:, None] * inv[lane % half][None, :]
    sign = jnp.where((lane % HEAD_DIM) < half, -1.0, 1.0).astype(F32)
    return jnp.cos(ang), jnp.sin(ang) * sign[None, :]


def _retention_tables(n_heads):
    d_ret = n_heads * HEAD_DIM
    log_gamma = jnp.log1p(-jnp.exp2(-5.0 - jnp.arange(n_heads, dtype=F32)))
    idx = jnp.arange(RET_BLOCK, dtype=F32)
    t, s = idx[:, None], idx[None, :]
    same = (t // CHUNK) == (s // CHUNK)
    dist = jnp.where(same, jnp.abs(t - s), t - s)
    seen = same | ((s // CHUNK) < (t // CHUNK))
    dmat = jnp.where(seen[None], jnp.exp(log_gamma[:, None, None] * dist[None]), 0.0)
    lane_gamma = jnp.repeat(log_gamma, HEAD_DIM)
    qdec = jnp.exp(lane_gamma[None, :] * (idx + 1.0)[:, None])
    kdec = jnp.exp(lane_gamma[None, :] * (RET_BLOCK - 1.0 - idx)[:, None])
    head = jnp.arange(d_ret) // HEAD_DIM
    bd = (head[:, None] == head[None, :]).astype(F32)
    cdec = bd * jnp.exp(lane_gamma * RET_BLOCK)[:, None]
    avg = jnp.concatenate([bd, bd], axis=0).astype(BF16) * (1.0 / HEAD_DIM)
    return dmat, qdec, kdec, cdec, bd, avg.astype(BF16)


def _tri_table():
    r = np.arange(2 * SB_BLOCK)[:, None] % SB_BLOCK
    c = np.arange(2 * SB_BLOCK)[None, :]
    return jnp.asarray((c >= SB_BLOCK) | (r >= c), dtype=BF16)


def _ffn_weights(w_in, w_out):
    d, two_ff = w_in.shape
    d_ff = two_ff // 2
    n_chunks = d_ff // FF_CHUNK
    gate = w_in[:, :d_ff].reshape(d, n_chunks, FF_CHUNK)
    up = w_in[:, d_ff:].reshape(d, n_chunks, FF_CHUNK)
    win = jnp.concatenate([gate, up], axis=2).transpose(1, 0, 2).astype(BF16)
    return win, w_out.astype(BF16)


def kernel(x, ffn1_norm, ffn1_w_in, ffn1_w_out, mix_norm, mix_w_in, conv_w, conv_b, conv_ln_g,
           conv_ln_b, ret_norm_g, mix_w_out, ffn2_norm, ffn2_w_in, ffn2_w_out, final_norm):
    b, s, d = x.shape
    depth = ffn1_norm.shape[0]
    d_conv = conv_w.shape[2]
    d_ret = ret_norm_g.shape[1]
    d_sb = d - d_conv - d_ret
    assert s % CONV_TILE == 0 and s % ROW_TILE == 0 and s % RET_BLOCK == 0 and s % SB_BLOCK == 0
    assert ffn1_w_out.shape[1] % FF_CHUNK == 0 and d % OUT_TILE == 0

    cos_tab, sin_tab = _rotary_tables(s)
    ret_tabs = _retention_tables(d_ret // HEAD_DIM)
    tri = _tri_table()
    scale = HEAD_DIM ** -0.5
    col = jnp.arange(mix_w_in.shape[2])
    q_sb_cols = (col >= 2 * d_conv) & (col < 2 * d_conv + d_sb)
    q_r_cols = (col >= 2 * d_conv + 3 * d_sb) & (col < 2 * d_conv + 3 * d_sb + d_ret)
    col_scale = jnp.where(q_sb_cols | q_r_cols, scale, 1.0).astype(F32)

    row = lambda g: g.reshape(1, -1)
    xf = x.reshape(b * s, d)
    for l in range(depth):
        win1, wout1 = _ffn_weights(ffn1_w_in[l], ffn1_w_out[l])
        win2, wout2 = _ffn_weights(ffn2_w_in[l], ffn2_w_out[l])
        w_mix = (mix_w_in[l] * col_scale[None, :]).astype(BF16)

        xf = _ffn(xf, row(ffn1_norm[l]), win1, wout1)
        uconv, q, k, v, qkr, vgr = _inproj(xf, row(mix_norm[l]), w_mix, cos_tab, sin_tab, s,
                                           2 * d_conv, d_sb, d_ret)
        seq3 = lambda t: t.reshape(b, s, t.shape[1])
        y_conv = _conv(seq3(uconv), conv_w[l], row(conv_b[l]), row(conv_ln_g[l]), row(conv_ln_b[l]))
        y_sb = _stick_breaking(seq3(q), seq3(k), seq3(v), tri)
        y_ret = _retention(seq3(qkr), seq3(vgr), ret_tabs, row(ret_norm_g[l]))
        flat = lambda t: t.reshape(b * s, t.shape[2])
        xf = _outproj(xf, flat(y_conv), flat(y_sb), flat(y_ret), mix_w_out[l].astype(BF16))
        last = l == depth - 1
        xf = _ffn(xf, row(ffn2_norm[l]), win2, wout2, row(final_norm) if last else None)
    return xf.reshape(b, s, d)
```

```python
import functools

import jax
import jax.numpy as jnp
import numpy as np
from jax import lax
from jax.experimental import pallas as pl
from jax.experimental.pallas import tpu as pltpu

F32 = jnp.float32
BF16 = jnp.bfloat16

EPS = 1e-6
ROPE_BASE = 10000.0
CHUNK = 64
CONV_WIDTH = 31
HEAD_DIM = 64
LANES = 128
MXU_DIM = 256

ROW_TILE = 512
FF_CHUNK = MXU_DIM
OUT_TILE = 512
CONV_TILE = 512
CONV_HALO = 32
CONV_SUB = 64
SB_BLOCK = 128
SB_HEADS_PER_GROUP = LANES // HEAD_DIM
SB_DEAD_LOG = 110.0
RET_BLOCK = 256
VMEM_LIMIT = 56 * 1024 * 1024


def _cparams(n_axes):
    return pltpu.CompilerParams(dimension_semantics=("arbitrary",) * n_axes,
                                vmem_limit_bytes=VMEM_LIMIT)


def _rms(x, g):
    return (x * lax.rsqrt(jnp.mean(x * x, axis=-1, keepdims=True) + EPS)) * g


def _silu(x):
    return x * jax.nn.sigmoid(x)


def _full(shape):
    return pl.BlockSpec(shape, lambda *_: (0,) * len(shape))


def _ffn_kernel(*refs, n_chunks, n_out, final):
    if final:
        x_ref, g_ref, win_ref, wout_ref, fg_ref, o_ref, h_scr, a_scr = refs
    else:
        x_ref, g_ref, win_ref, wout_ref, o_ref, h_scr, a_scr = refs
    h_scr[...] = _rms(x_ref[...], g_ref[...]).astype(BF16)
    for j in range(n_chunks):
        gu = jnp.dot(h_scr[...], win_ref[j], preferred_element_type=F32)
        a = _silu(gu[:, :FF_CHUNK]) * gu[:, FF_CHUNK:]
        a_scr[:, j * FF_CHUNK:(j + 1) * FF_CHUNK] = a.astype(BF16)
    for m in range(n_out):
        sl = slice(m * OUT_TILE, (m + 1) * OUT_TILE)
        y = jnp.dot(a_scr[...], wout_ref[:, sl], preferred_element_type=F32)
        o_ref[:, sl] = x_ref[:, sl] + 0.5 * y
    if final:
        o_ref[...] = _rms(o_ref[...], fg_ref[...])


def _ffn(x, g, win, wout, final_g=None):
    n, d = x.shape
    n_chunks = win.shape[0]
    d_ff = wout.shape[0]
    final = final_g is not None
    in_specs = [pl.BlockSpec((ROW_TILE, d), lambda i: (i, 0)), _full((1, d)),
                _full(win.shape), _full(wout.shape)]
    args = [x, g, win, wout]
    if final:
        in_specs.append(_full((1, d)))
        args.append(final_g)
    return pl.pallas_call(
        functools.partial(_ffn_kernel, n_chunks=n_chunks, n_out=d // OUT_TILE, final=final),
        out_shape=jax.ShapeDtypeStruct((n, d), F32),
        grid=(n // ROW_TILE,),
        in_specs=in_specs,
        out_specs=pl.BlockSpec((ROW_TILE, d), lambda i: (i, 0)),
        scratch_shapes=[pltpu.VMEM((ROW_TILE, d), BF16), pltpu.VMEM((ROW_TILE, d_ff), BF16)],
        compiler_params=_cparams(1),
        name="ffn",
    )(*args)


def _rotary(t, cos, sin_signed, first_half):
    partner = jnp.where(first_half, pltpu.roll(t, LANES - HEAD_DIM // 2, 1),
                        pltpu.roll(t, HEAD_DIM // 2, 1))
    return t * cos + partner * sin_signed


def _inproj_kernel(x_ref, g_ref, w_ref, cos_ref, sin_ref, uconv_ref, q_ref, k_ref, v_ref,
                   qkr_ref, vgr_ref, h_scr, *, d_conv2, d_sb, d_ret):
    h_scr[...] = _rms(x_ref[...], g_ref[...]).astype(BF16)

    def proj(lo, width):
        return jnp.dot(h_scr[...], w_ref[:, lo:lo + width], preferred_element_type=F32)

    uconv_ref[...] = proj(0, d_conv2)
    o = d_conv2
    q_ref[...] = proj(o, d_sb).astype(BF16)
    k_ref[...] = proj(o + d_sb, d_sb).astype(BF16)
    v_ref[...] = proj(o + 2 * d_sb, d_sb).astype(BF16)
    o += 3 * d_sb
    qk = proj(o, 2 * d_ret)
    cos = cos_ref[...]
    sin = sin_ref[...]
    lane = lax.broadcasted_iota(jnp.int32, (1, LANES), 1)
    first_half = (lane % HEAD_DIM) < (HEAD_DIM // 2)
    for c in range(2 * d_ret // LANES):
        sl = slice(c * LANES, (c + 1) * LANES)
        qkr_ref[:, sl] = _rotary(qk[:, sl], cos, sin, first_half)
    vgr_ref[...] = proj(o + 2 * d_ret, 2 * d_ret)


def _inproj(x, g, w, cos_tab, sin_tab, seq, d_conv2, d_sb, d_ret):
    n, d = x.shape
    tiles_per_seq = seq // ROW_TILE
    row = lambda i: (i, 0)
    pos = lambda i: (i % tiles_per_seq, 0)
    outs = [jax.ShapeDtypeStruct((n, d_conv2), F32)] + [jax.ShapeDtypeStruct((n, d_sb), BF16)] * 3 \
        + [jax.ShapeDtypeStruct((n, 2 * d_ret), F32)] * 2
    return pl.pallas_call(
        functools.partial(_inproj_kernel, d_conv2=d_conv2, d_sb=d_sb, d_ret=d_ret),
        out_shape=outs,
        grid=(n // ROW_TILE,),
        in_specs=[pl.BlockSpec((ROW_TILE, d), row), _full((1, d)), _full(w.shape),
                  pl.BlockSpec((ROW_TILE, LANES), pos), pl.BlockSpec((ROW_TILE, LANES), pos)],
        out_specs=[pl.BlockSpec((ROW_TILE, s.shape[1]), row) for s in outs],
        scratch_shapes=[pltpu.VMEM((ROW_TILE, d), BF16)],
        compiler_params=_cparams(1),
        name="mixer_inproj",
    )(x, g, w, cos_tab, sin_tab)


def _conv_kernel(u_ref, halo_ref, cw_ref, cb_ref, lg_ref, lb_ref, o_ref, v_scr, *, d_conv):
    def glu(u):
        return u[:, :d_conv] * jax.nn.sigmoid(u[:, d_conv:])

    first = pl.program_id(1) == 0
    v_scr[0:CONV_HALO, :] = jnp.where(first, 0.0, glu(halo_ref[0]))
    v_scr[CONV_HALO:, :] = glu(u_ref[0])
    off = CONV_HALO - (CONV_WIDTH - 1)
    for r in range(CONV_TILE // CONV_SUB):
        acc = jnp.zeros((CONV_SUB, d_conv), F32)
        for w in range(CONV_WIDTH):
            acc = acc + cw_ref[w:w + 1, :] * v_scr[pl.ds(r * CONV_SUB + off + w, CONV_SUB), :]
        y = acc + cb_ref[...]
        mu = jnp.mean(y, axis=-1, keepdims=True)
        yc = y - mu
        var = jnp.mean(yc * yc, axis=-1, keepdims=True)
        ln = yc * lax.rsqrt(var + EPS) * lg_ref[...] + lb_ref[...]
        o_ref[0, r * CONV_SUB:(r + 1) * CONV_SUB, :] = _silu(ln).astype(BF16)


def _conv(u, cw, cb, lg, lb):
    b, s, d2 = u.shape
    d_conv = d2 // 2
    halo_per_tile = CONV_TILE // CONV_HALO
    return pl.pallas_call(
        functools.partial(_conv_kernel, d_conv=d_conv),
        out_shape=jax.ShapeDtypeStruct((b, s, d_conv), BF16),
        grid=(b, s // CONV_TILE),
        in_specs=[pl.BlockSpec((1, CONV_TILE, d2), lambda bi, i: (bi, i, 0)),
                  pl.BlockSpec((1, CONV_HALO, d2),
                               lambda bi, i: (bi, jnp.maximum(i * halo_per_tile - 1, 0), 0)),
                  _full(cw.shape), _full((1, d_conv)), _full((1, d_conv)), _full((1, d_conv))],
        out_specs=pl.BlockSpec((1, CONV_TILE, d_conv), lambda bi, i: (bi, i, 0)),
        scratch_shapes=[pltpu.VMEM((CONV_HALO + CONV_TILE, d_conv), F32)],
        compiler_params=_cparams(2),
        name="conv_module",
    )(u, u, cw, cb, lg, lb)


def _sb_kernel(q_ref, k_ref, v_ref, tri_ref, o_ref, qs_scr, acc_scr, carry_scr, *, n_pairs):
    i = pl.program_id(1)
    lane = lax.broadcasted_iota(jnp.int32, (1, LANES), 1)
    row = lax.broadcasted_iota(jnp.int32, (SB_BLOCK, SB_BLOCK), 0)
    col = lax.broadcasted_iota(jnp.int32, (SB_BLOCK, SB_BLOCK), 1)
    causal = jnp.concatenate([col < row] * SB_HEADS_PER_GROUP, axis=0)
    for p in range(n_pairs):
        q = q_ref[0, :, p * LANES:(p + 1) * LANES]
        qs_scr[p] = jnp.concatenate(
            [jnp.where(lane // HEAD_DIM == hh, q, jnp.zeros_like(q))
             for hh in range(SB_HEADS_PER_GROUP)], axis=0)

    def sweep(j, mask):
        start = pl.multiple_of(j * SB_BLOCK, SB_BLOCK)
        lowest = None
        for p in range(n_pairs):
            k_blk = k_ref[0, pl.ds(start, SB_BLOCK), p * LANES:(p + 1) * LANES]
            v_blk = v_ref[0, pl.ds(start, SB_BLOCK), p * LANES:(p + 1) * LANES]
            z = lax.dot_general(qs_scr[p], k_blk, (((1,), (1,)), ((), ())),
                                preferred_element_type=F32)
            sp = jnp.maximum(z, 0.0) + jnp.log(1.0 + jnp.exp(-jnp.abs(z)))
            if mask is not None:
                sp = jnp.where(mask, sp, 0.0)
            hi = sp.astype(BF16)
            lo = (sp - hi.astype(F32)).astype(BF16)
            rt = jnp.dot(jnp.concatenate([hi, lo], axis=1), tri_ref[...],
                         preferred_element_type=F32)
            if mask is not None:
                w = jnp.where(mask, jnp.exp(z - rt[:, :SB_BLOCK]), 0.0)
                carry = rt[:, SB_BLOCK:]
                acc_scr[p] = jnp.dot(w.astype(BF16), v_blk, preferred_element_type=F32)
            else:
                carry = carry_scr[p]
                w = jnp.exp(z - (rt[:, :SB_BLOCK] + carry))
                carry = carry + rt[:, SB_BLOCK:]
                acc_scr[p] += jnp.dot(w.astype(BF16), v_blk, preferred_element_type=F32)
            carry_scr[p] = carry
            lowest = carry if lowest is None else jnp.minimum(lowest, carry)
        return jnp.min(lowest)

    def more(st):
        return jnp.logical_and(st[0] >= 0, st[1] < SB_DEAD_LOG)

    def body(st):
        return st[0] - 1, sweep(st[0], None)

    lax.while_loop(more, body, (i - 1, sweep(i, causal)))
    for p in range(n_pairs):
        acc = acc_scr[p]
        out = acc[0:SB_BLOCK]
        for hh in range(1, SB_HEADS_PER_GROUP):
            out = jnp.where(lane // HEAD_DIM == hh, acc[hh * SB_BLOCK:(hh + 1) * SB_BLOCK], out)
        o_ref[0, :, p * LANES:(p + 1) * LANES] = out.astype(BF16)


def _stick_breaking(q, k, v, tri):
    b, s, d_sb = q.shape
    n_pairs = d_sb // LANES
    rows = SB_HEADS_PER_GROUP * SB_BLOCK
    blk = pl.BlockSpec((1, SB_BLOCK, d_sb), lambda bi, i: (bi, i, 0))
    whole = pl.BlockSpec((1, s, d_sb), lambda bi, i: (bi, 0, 0))
    return pl.pallas_call(
        functools.partial(_sb_kernel, n_pairs=n_pairs),
        out_shape=jax.ShapeDtypeStruct((b, s, d_sb), BF16),
        grid=(b, s // SB_BLOCK),
        in_specs=[blk, whole, whole, _full(tri.shape)],
        out_specs=blk,
        scratch_shapes=[pltpu.VMEM((n_pairs, rows, LANES), BF16),
                        pltpu.VMEM((n_pairs, rows, LANES), F32),
                        pltpu.VMEM((n_pairs, rows, LANES), F32)],
        compiler_params=_cparams(2),
        name="stick_breaking",
    )(q, k, v, tri)


def _ret_kernel(qk_ref, vg_ref, dmat_ref, qdec_ref, kdec_ref, cdec_ref, bd_ref, avg_ref, gain_ref,
                o_ref, state_scr, *, d_ret):
    n_heads = d_ret // HEAD_DIM

    @pl.when(pl.program_id(1) == 0)
    def _():
        state_scr[...] = jnp.zeros_like(state_scr)

    q = qk_ref[0, :, :d_ret]
    k = qk_ref[0, :, d_ret:]
    v = vg_ref[0, :, :d_ret]
    g = vg_ref[0, :, d_ret:]
    lane = lax.broadcasted_iota(jnp.int32, (1, d_ret), 1)
    kb = k.astype(BF16)
    probs, vals = [], []
    for h in range(n_heads):
        in_head = lane // HEAD_DIM == h
        qh = jnp.where(in_head, q, 0.0).astype(BF16)
        sc = lax.dot_general(qh, kb, (((1,), (1,)), ((), ())), preferred_element_type=F32)
        probs.append((sc * dmat_ref[h]).astype(BF16))
        vals.append(jnp.where(in_head, v, 0.0).astype(BF16))
    y = jnp.dot(jnp.concatenate(probs, axis=1), jnp.concatenate(vals, axis=0),
                preferred_element_type=F32)
    state = state_scr[...]
    y = y + jnp.dot((q * qdec_ref[...]).astype(BF16), state.astype(BF16),
                    preferred_element_type=F32)
    kv = lax.dot_general((k * kdec_ref[...]).astype(BF16), v.astype(BF16),
                         (((0,), (0,)), ((), ())), preferred_element_type=F32)
    state_scr[...] = cdec_ref[...] * state + bd_ref[...] * kv

    def head_mean(t):
        hi = t.astype(BF16)
        lo = (t - hi.astype(F32)).astype(BF16)
        return jnp.dot(jnp.concatenate([hi, lo], axis=1), avg_ref[...], preferred_element_type=F32)

    yc = y - head_mean(y)
    var = head_mean(yc * yc)
    o_ref[0] = (_silu(g) * (yc * lax.rsqrt(var + EPS) * gain_ref[...])).astype(BF16)


def _retention(qk, vg, tabs, gain):
    b, s, d2 = qk.shape
    d_ret = d2 // 2
    blk = pl.BlockSpec((1, RET_BLOCK, d2), lambda bi, i: (bi, i, 0))
    return pl.pallas_call(
        functools.partial(_ret_kernel, d_ret=d_ret),
        out_shape=jax.ShapeDtypeStruct((b, s, d_ret), BF16),
        grid=(b, s // RET_BLOCK),
        in_specs=[blk, blk] + [_full(t.shape) for t in tabs] + [_full((1, d_ret))],
        out_specs=pl.BlockSpec((1, RET_BLOCK, d_ret), lambda bi, i: (bi, i, 0)),
        scratch_shapes=[pltpu.VMEM((d_ret, d_ret), F32)],
        compiler_params=_cparams(2),
        name="retention",
    )(qk, vg, *tabs, gain)


def _outproj_kernel(x_ref, yc_ref, ys_ref, yr_ref, w_ref, o_ref, *, d_conv, d_sb):
    y = jnp.dot(yc_ref[...], w_ref[0:d_conv, :], preferred_element_type=F32)
    y = y + jnp.dot(ys_ref[...], w_ref[d_conv:d_conv + d_sb, :], preferred_element_type=F32)
    y = y + jnp.dot(yr_ref[...], w_ref[d_conv + d_sb:, :], preferred_element_type=F32)
    o_ref[...] = x_ref[...] + y


def _outproj(x, yc, ys, yr, w):
    n, d = x.shape
    row = lambda i: (i, 0)
    return pl.pallas_call(
        functools.partial(_outproj_kernel, d_conv=yc.shape[1], d_sb=ys.shape[1]),
        out_shape=jax.ShapeDtypeStruct((n, d), F32),
        grid=(n // ROW_TILE,),
        in_specs=[pl.BlockSpec((ROW_TILE, d), row)]
        + [pl.BlockSpec((ROW_TILE, t.shape[1]), row) for t in (yc, ys, yr)] + [_full(w.shape)],
        out_specs=pl.BlockSpec((ROW_TILE, d), row),
        compiler_params=_cparams(1),
        name="mixer_outproj",
    )(x, yc, ys, yr, w)


def _rotary_tables(seq):
    half = HEAD_DIM // 2
    inv = 1.0 / (ROPE_BASE ** (jnp.arange(half, dtype=F32) / half))
    lane = jnp.arange(LANES)
    ang = jnp.arange(seq).astype(F32)[:, None] * inv[lane % half][None, :]
    sign = jnp.where((lane % HEAD_DIM) < half, -1.0, 1.0).astype(F32)
    return jnp.cos(ang), jnp.sin(ang) * sign[None, :]


def _retention_tables(n_heads):
    d_ret = n_heads * HEAD_DIM
    log_gamma = jnp.log1p(-jnp.exp2(-5.0 - jnp.arange(n_heads, dtype=F32)))
    idx = jnp.arange(RET_BLOCK, dtype=F32)
    t, s = idx[:, None], idx[None, :]
    same = (t // CHUNK) == (s // CHUNK)
    dist = jnp.where(same, jnp.abs(t - s), t - s)
    seen = same | ((s // CHUNK) < (t // CHUNK))
    dmat = jnp.where(seen[None], jnp.exp(log_gamma[:, None, None] * dist[None]), 0.0)
    lane_gamma = jnp.repeat(log_gamma, HEAD_DIM)
    qdec = jnp.exp(lane_gamma[None, :] * (idx + 1.0)[:, None])
    kdec = jnp.exp(lane_gamma[None, :] * (RET_BLOCK - 1.0 - idx)[:, None])
    head = jnp.arange(d_ret) // HEAD_DIM
    bd = (head[:, None] == head[None, :]).astype(F32)
    cdec = bd * jnp.exp(lane_gamma * RET_BLOCK)[:, None]
    avg = jnp.concatenate([bd, bd], axis=0).astype(BF16) * (1.0 / HEAD_DIM)
    return dmat, qdec, kdec, cdec, bd, avg.astype(BF16)


def _tri_table():
    r = np.arange(2 * SB_BLOCK)[:, None] % SB_BLOCK
    c = np.arange(2 * SB_BLOCK)[None, :]
    return jnp.asarray((c >= SB_BLOCK) | (r >= c), dtype=BF16)


def _ffn_weights(w_in, w_out):
    d, two_ff = w_in.shape
    d_ff = two_ff // 2
    n_chunks = d_ff // FF_CHUNK
    gate = w_in[:, :d_ff].reshape(d, n_chunks, FF_CHUNK)
    up = w_in[:, d_ff:].reshape(d, n_chunks, FF_CHUNK)
    win = jnp.concatenate([gate, up], axis=2).transpose(1, 0, 2).astype(BF16)
    return win, w_out.astype(BF16)


def kernel(x, ffn1_norm, ffn1_w_in, ffn1_w_out, mix_norm, mix_w_in, conv_w, conv_b, conv_ln_g,
           conv_ln_b, ret_norm_g, mix_w_out, ffn2_norm, ffn2_w_in, ffn2_w_out, final_norm):
    b, s, d = x.shape
    depth = ffn1_norm.shape[0]
    d_conv = conv_w.shape[2]
    d_ret = ret_norm_g.shape[1]
    d_sb = d - d_conv - d_ret
    assert s % CONV_TILE == 0 and s % ROW_TILE == 0 and s % RET_BLOCK == 0 and s % SB_BLOCK == 0
    assert ffn1_w_out.shape[1] % FF_CHUNK == 0 and d % OUT_TILE == 0

    cos_tab, sin_tab = _rotary_tables(s)
    ret_tabs = _retention_tables(d_ret // HEAD_DIM)
    tri = _tri_table()
    scale = HEAD_DIM ** -0.5
    col = jnp.arange(mix_w_in.shape[2])
    q_sb_cols = (col >= 2 * d_conv) & (col < 2 * d_conv + d_sb)
    q_r_cols = (col >= 2 * d_conv + 3 * d_sb) & (col < 2 * d_conv + 3 * d_sb + d_ret)
    col_scale = jnp.where(q_sb_cols | q_r_cols, scale, 1.0).astype(F32)

    row = lambda g: g.reshape(1, -1)
    xf = x.reshape(b * s, d)
    for l in range(depth):
        win1, wout1 = _ffn_weights(ffn1_w_in[l], ffn1_w_out[l])
        win2, wout2 = _ffn_weights(ffn2_w_in[l], ffn2_w_out[l])
        w_mix = (mix_w_in[l] * col_scale[None, :]).astype(BF16)

        xf = _ffn(xf, row(ffn1_norm[l]), win1, wout1)
        uconv, q, k, v, qkr, vgr = _inproj(xf, row(mix_norm[l]), w_mix, cos_tab, sin_tab, s,
                                           2 * d_conv, d_sb, d_ret)
        seq3 = lambda t: t.reshape(b, s, t.shape[1])
        y_conv = _conv(seq3(uconv), conv_w[l], row(conv_b[l]), row(conv_ln_g[l]), row(conv_ln_b[l]))
        y_sb = _stick_breaking(seq3(q), seq3(k), seq3(v), tri)
        y_ret = _retention(seq3(qkr), seq3(vgr), ret_tabs, row(ret_norm_g[l]))
        flat = lambda t: t.reshape(b * s, t.shape[2])
        xf = _outproj(xf, flat(y_conv), flat(y_sb), flat(y_ret), mix_w_out[l].astype(BF16))
        last = l == depth - 1
        xf = _ffn(xf, row(ffn2_norm[l]), win2, wout2, row(final_norm) if last else None)
    return xf.reshape(b, s, d)
```

```python
import functools

import jax
import jax.numpy as jnp
import numpy as np
from jax import lax
from jax.experimental import pallas as pl
from jax.experimental.pallas import tpu as pltpu

F32 = jnp.float32
BF16 = jnp.bfloat16

EPS = 1e-6
ROPE_BASE = 10000.0
CHUNK = 64
CONV_WIDTH = 31
HEAD_DIM = 64
LANES = 128
MXU_DIM = 256

ROW_TILE = 512
FF_CHUNK = MXU_DIM
OUT_TILE = 512
CONV_TILE = 512
CONV_HALO = 32
CONV_SUB = 64
SB_BLOCK = 128
SB_HEADS_PER_GROUP = LANES // HEAD_DIM
SB_DEAD_LOG = 110.0
SIGN_BIT = 0x80000000
BF16_BITS = 0xFFFF0000
RET_BLOCK = 256
VMEM_LIMIT = 56 * 1024 * 1024


def _cparams(n_axes):
    return pltpu.CompilerParams(dimension_semantics=("arbitrary",) * n_axes,
                                vmem_limit_bytes=VMEM_LIMIT)


def _rms(x, g):
    return (x * lax.rsqrt(jnp.mean(x * x, axis=-1, keepdims=True) + EPS)) * g


def _silu(x):
    return x * jax.nn.sigmoid(x)


def _full(shape):
    return pl.BlockSpec(shape, lambda *_: (0,) * len(shape))


def _resident(shape):
    return pl.BlockSpec(shape, lambda *_: (0,) * len(shape), pipeline_mode=pl.Buffered(1))


def _ffn_kernel(*refs, n_chunks, n_out, n_mix, final):
    refs = list(refs)
    x_ref = refs.pop(0)
    mix_refs = [refs.pop(0) for _ in range(n_mix)]
    wmix_ref = refs.pop(0) if n_mix else None
    g_ref, win_ref, wout_ref = refs.pop(0), refs.pop(0), refs.pop(0)
    fg_ref = refs.pop(0) if final else None
    o_ref, h_scr, a_scr = refs.pop(0), refs.pop(0), refs.pop(0)
    out_tiles = [slice(m * OUT_TILE, (m + 1) * OUT_TILE) for m in range(n_out)]
    if n_mix:
        for sl in out_tiles:
            y, lo = None, 0
            for y_ref in mix_refs:
                part = jnp.dot(y_ref[...], wmix_ref[lo:lo + y_ref.shape[1], sl],
                               preferred_element_type=F32)
                y = part if y is None else y + part
                lo += y_ref.shape[1]
            o_ref[:, sl] = x_ref[:, sl] + y
        x_ref = o_ref
    h_scr[...] = _rms(x_ref[...], g_ref[...]).astype(BF16)
    for j in range(n_chunks):
        gu = jnp.dot(h_scr[...], win_ref[j], preferred_element_type=F32)
        a = _silu(gu[:, :FF_CHUNK]) * gu[:, FF_CHUNK:]
        a_scr[:, j * FF_CHUNK:(j + 1) * FF_CHUNK] = a.astype(BF16)
    for sl in out_tiles:
        y = jnp.dot(a_scr[...], wout_ref[:, sl], preferred_element_type=F32)
        o_ref[:, sl] = x_ref[:, sl] + 0.5 * y
    if final:
        o_ref[...] = _rms(o_ref[...], fg_ref[...])


def _ffn(x, g, win, wout, mix=(), wmix=None, final_g=None):
    n, d = x.shape
    n_chunks = win.shape[0]
    d_ff = wout.shape[0]
    final = final_g is not None
    row = lambda i: (i, 0)
    in_specs = [pl.BlockSpec((ROW_TILE, d), row)]
    in_specs += [pl.BlockSpec((ROW_TILE, y.shape[1]), row) for y in mix]
    args = [x, *mix]
    if mix:
        in_specs.append(_resident(wmix.shape))
        args.append(wmix)
    in_specs += [_full((1, d)), _resident(win.shape), _resident(wout.shape)]
    args += [g, win, wout]
    if final:
        in_specs.append(_full((1, d)))
        args.append(final_g)
    return pl.pallas_call(
        functools.partial(_ffn_kernel, n_chunks=n_chunks, n_out=d // OUT_TILE, n_mix=len(mix),
                          final=final),
        out_shape=jax.ShapeDtypeStruct((n, d), F32),
        grid=(n // ROW_TILE,),
        in_specs=in_specs,
        out_specs=pl.BlockSpec((ROW_TILE, d), row),
        scratch_shapes=[pltpu.VMEM((ROW_TILE, d), BF16), pltpu.VMEM((ROW_TILE, d_ff), BF16)],
        compiler_params=_cparams(1),
        name="ffn",
    )(*args)


def _rotary(t, cos, sin_signed, first_half):
    partner = jnp.where(first_half, pltpu.roll(t, LANES - HEAD_DIM // 2, 1),
                        pltpu.roll(t, HEAD_DIM // 2, 1))
    return t * cos + partner * sin_signed


def _inproj_kernel(x_ref, g_ref, w_ref, cos_ref, sin_ref, uconv_ref, q_ref, k_ref, v_ref,
                   qkr_ref, vgr_ref, h_scr, *, d_conv2, d_sb, d_ret):
    h_scr[...] = _rms(x_ref[...], g_ref[...]).astype(BF16)

    def proj(lo, width):
        return jnp.dot(h_scr[...], w_ref[:, lo:lo + width], preferred_element_type=F32)

    uconv_ref[...] = proj(0, d_conv2)
    o = d_conv2
    q_ref[...] = proj(o, d_sb).astype(BF16)
    k_ref[...] = proj(o + d_sb, d_sb).astype(BF16)
    v_ref[...] = proj(o + 2 * d_sb, d_sb).astype(BF16)
    o += 3 * d_sb
    qk = proj(o, 2 * d_ret)
    cos = cos_ref[...]
    sin = sin_ref[...]
    lane = lax.broadcasted_iota(jnp.int32, (1, LANES), 1)
    first_half = (lane % HEAD_DIM) < (HEAD_DIM // 2)
    for c in range(2 * d_ret // LANES):
        sl = slice(c * LANES, (c + 1) * LANES)
        qkr_ref[:, sl] = _rotary(qk[:, sl], cos, sin, first_half)
    vgr_ref[...] = proj(o + 2 * d_ret, 2 * d_ret)


def _inproj(x, g, w, cos_tab, sin_tab, seq, d_conv2, d_sb, d_ret):
    n, d = x.shape
    tiles_per_seq = seq // ROW_TILE
    row = lambda i: (i, 0)
    pos = lambda i: (i % tiles_per_seq, 0)
    outs = [jax.ShapeDtypeStruct((n, d_conv2), F32)] + [jax.ShapeDtypeStruct((n, d_sb), BF16)] * 3 \
        + [jax.ShapeDtypeStruct((n, 2 * d_ret), F32)] * 2
    return pl.pallas_call(
        functools.partial(_inproj_kernel, d_conv2=d_conv2, d_sb=d_sb, d_ret=d_ret),
        out_shape=outs,
        grid=(n // ROW_TILE,),
        in_specs=[pl.BlockSpec((ROW_TILE, d), row), _full((1, d)), _full(w.shape),
                  pl.BlockSpec((ROW_TILE, LANES), pos), pl.BlockSpec((ROW_TILE, LANES), pos)],
        out_specs=[pl.BlockSpec((ROW_TILE, s.shape[1]), row) for s in outs],
        scratch_shapes=[pltpu.VMEM((ROW_TILE, d), BF16)],
        compiler_params=_cparams(1),
        name="mixer_inproj",
    )(x, g, w, cos_tab, sin_tab)


def _conv_kernel(u_ref, halo_ref, cw_ref, cb_ref, lg_ref, lb_ref, o_ref, v_scr, *, d_conv):
    def glu(u):
        return u[:, :d_conv] * jax.nn.sigmoid(u[:, d_conv:])

    first = pl.program_id(1) == 0
    v_scr[0:CONV_HALO, :] = jnp.where(first, 0.0, glu(halo_ref[0]))
    v_scr[CONV_HALO:, :] = glu(u_ref[0])
    off = CONV_HALO - (CONV_WIDTH - 1)
    for r in range(CONV_TILE // CONV_SUB):
        acc = jnp.zeros((CONV_SUB, d_conv), F32)
        for w in range(CONV_WIDTH):
            acc = acc + cw_ref[w:w + 1, :] * v_scr[pl.ds(r * CONV_SUB + off + w, CONV_SUB), :]
        y = acc + cb_ref[...]
        mu = jnp.mean(y, axis=-1, keepdims=True)
        yc = y - mu
        var = jnp.mean(yc * yc, axis=-1, keepdims=True)
        ln = yc * lax.rsqrt(var + EPS) * lg_ref[...] + lb_ref[...]
        o_ref[0, r * CONV_SUB:(r + 1) * CONV_SUB, :] = _silu(ln).astype(BF16)


def _conv(u, cw, cb, lg, lb):
    b, s, d2 = u.shape
    d_conv = d2 // 2
    halo_per_tile = CONV_TILE // CONV_HALO
    return pl.pallas_call(
        functools.partial(_conv_kernel, d_conv=d_conv),
        out_shape=jax.ShapeDtypeStruct((b, s, d_conv), BF16),
        grid=(b, s // CONV_TILE),
        in_specs=[pl.BlockSpec((1, CONV_TILE, d2), lambda bi, i: (bi, i, 0)),
                  pl.BlockSpec((1, CONV_HALO, d2),
                               lambda bi, i: (bi, jnp.maximum(i * halo_per_tile - 1, 0), 0)),
                  _full(cw.shape), _full((1, d_conv)), _full((1, d_conv)), _full((1, d_conv))],
        out_specs=pl.BlockSpec((1, CONV_TILE, d_conv), lambda bi, i: (bi, i, 0)),
        scratch_shapes=[pltpu.VMEM((CONV_HALO + CONV_TILE, d_conv), F32)],
        compiler_params=_cparams(2),
        name="conv_module",
    )(u, u, cw, cb, lg, lb)


def _sb_kernel(q_ref, k_ref, v_ref, tri_ref, o_ref, qs_scr, acc_scr, carry_scr, *, n_pairs):
    i = pl.program_id(1)
    lane = lax.broadcasted_iota(jnp.int32, (1, LANES), 1)
    row = lax.broadcasted_iota(jnp.int32, (SB_BLOCK, SB_BLOCK), 0)
    col = lax.broadcasted_iota(jnp.int32, (SB_BLOCK, SB_BLOCK), 1)
    causal = jnp.concatenate([col < row] * SB_HEADS_PER_GROUP, axis=0)
    for p in range(n_pairs):
        q = q_ref[0, :, p * LANES:(p + 1) * LANES]
        qs_scr[p] = jnp.concatenate(
            [jnp.where(lane // HEAD_DIM == hh, q, jnp.zeros_like(q))
             for hh in range(SB_HEADS_PER_GROUP)], axis=0)

    def sweep(j, mask):
        start = pl.multiple_of(j * SB_BLOCK, SB_BLOCK)
        zs = [lax.dot_general(qs_scr[p], k_ref[0, pl.ds(start, SB_BLOCK), p * LANES:(p + 1) * LANES],
                              (((1,), (1,)), ((), ())), preferred_element_type=F32)
              for p in range(n_pairs)]
        rts = []
        for p in range(n_pairs):
            zbits = lax.bitcast_convert_type(zs[p], jnp.uint32)
            neg_abs = lax.bitcast_convert_type(zbits | jnp.uint32(SIGN_BIT), F32)
            sp = jnp.maximum(zs[p], 0.0) + jnp.log(1.0 + jnp.exp(neg_abs))
            if mask is not None:
                sp = jnp.where(mask, sp, 0.0)
            hi = lax.bitcast_convert_type(
                lax.bitcast_convert_type(sp, jnp.uint32) & jnp.uint32(BF16_BITS), F32)
            hl = jnp.concatenate([hi.astype(BF16), (sp - hi).astype(BF16)], axis=1)
            rts.append(jnp.dot(hl, tri_ref[...], preferred_element_type=F32))
        lowest = None
        for p in range(n_pairs):
            rt = rts[p]
            if mask is not None:
                w = jnp.where(mask, jnp.exp(zs[p] - rt[:, :SB_BLOCK]), 0.0)
                carry = rt[:, SB_BLOCK:]
            else:
                carry = carry_scr[p]
                w = jnp.exp(zs[p] - (rt[:, :SB_BLOCK] + carry))
                carry = carry + rt[:, SB_BLOCK:]
            carry_scr[p] = carry
            lowest = carry if lowest is None else jnp.minimum(lowest, carry)
            pv = jnp.dot(w.astype(BF16), v_ref[0, pl.ds(start, SB_BLOCK), p * LANES:(p + 1) * LANES],
                         preferred_element_type=F32)
            if mask is not None:
                acc_scr[p] = pv
            else:
                acc_scr[p] += pv
        return jnp.min(lowest)

    def more(st):
        return jnp.logical_and(st[0] >= 0, st[1] < SB_DEAD_LOG)

    def body(st):
        return st[0] - 1, sweep(st[0], None)

    lax.while_loop(more, body, (i - 1, sweep(i, causal)))
    for p in range(n_pairs):
        acc = acc_scr[p]
        out = acc[0:SB_BLOCK]
        for hh in range(1, SB_HEADS_PER_GROUP):
            out = jnp.where(lane // HEAD_DIM == hh, acc[hh * SB_BLOCK:(hh + 1) * SB_BLOCK], out)
        o_ref[0, :, p * LANES:(p + 1) * LANES] = out.astype(BF16)


def _stick_breaking(q, k, v, tri):
    b, s, d_sb = q.shape
    n_pairs = d_sb // LANES
    rows = SB_HEADS_PER_GROUP * SB_BLOCK
    blk = pl.BlockSpec((1, SB_BLOCK, d_sb), lambda bi, i: (bi, i, 0))
    whole = pl.BlockSpec((1, s, d_sb), lambda bi, i: (bi, 0, 0))
    return pl.pallas_call(
        functools.partial(_sb_kernel, n_pairs=n_pairs),
        out_shape=jax.ShapeDtypeStruct((b, s, d_sb), BF16),
        grid=(b, s // SB_BLOCK),
        in_specs=[blk, whole, whole, _full(tri.shape)],
        out_specs=blk,
        scratch_shapes=[pltpu.VMEM((n_pairs, rows, LANES), BF16),
                        pltpu.VMEM((n_pairs, rows, LANES), F32),
                        pltpu.VMEM((n_pairs, rows, LANES), F32)],
        compiler_params=_cparams(2),
        name="stick_breaking",
    )(q, k, v, tri)


def _ret_kernel(qk_ref, vg_ref, dmat_ref, qdec_ref, kdec_ref, cdec_ref, bd_ref, avg_ref, gain_ref,
                o_ref, state_scr, *, d_ret):
    n_heads = d_ret // HEAD_DIM

    @pl.when(pl.program_id(1) == 0)
    def _():
        state_scr[...] = jnp.zeros_like(state_scr)

    q = qk_ref[0, :, :d_ret]
    k = qk_ref[0, :, d_ret:]
    v = vg_ref[0, :, :d_ret]
    g = vg_ref[0, :, d_ret:]
    lane = lax.broadcasted_iota(jnp.int32, (1, d_ret), 1)
    kb = k.astype(BF16)
    probs, vals = [], []
    for h in range(n_heads):
        in_head = lane // HEAD_DIM == h
        qh = jnp.where(in_head, q, 0.0).astype(BF16)
        sc = lax.dot_general(qh, kb, (((1,), (1,)), ((), ())), preferred_element_type=F32)
        probs.append((sc * dmat_ref[h]).astype(BF16))
        vals.append(jnp.where(in_head, v, 0.0).astype(BF16))
    y = jnp.dot(jnp.concatenate(probs, axis=1), jnp.concatenate(vals, axis=0),
                preferred_element_type=F32)
    state = state_scr[...]
    y = y + jnp.dot((q * qdec_ref[...]).astype(BF16), state.astype(BF16),
                    preferred_element_type=F32)
    kv = lax.dot_general((k * kdec_ref[...]).astype(BF16), v.astype(BF16),
                         (((0,), (0,)), ((), ())), preferred_element_type=F32)
    state_scr[...] = cdec_ref[...] * state + bd_ref[...] * kv

    def head_mean(t):
        hi = t.astype(BF16)
        lo = (t - hi.astype(F32)).astype(BF16)
        return jnp.dot(jnp.concatenate([hi, lo], axis=1), avg_ref[...], preferred_element_type=F32)

    yc = y - head_mean(y)
    var = head_mean(yc * yc)
    o_ref[0] = (_silu(g) * (yc * lax.rsqrt(var + EPS) * gain_ref[...])).astype(BF16)


def _retention(qk, vg, tabs, gain):
    b, s, d2 = qk.shape
    d_ret = d2 // 2
    blk = pl.BlockSpec((1, RET_BLOCK, d2), lambda bi, i: (bi, i, 0))
    return pl.pallas_call(
        functools.partial(_ret_kernel, d_ret=d_ret),
        out_shape=jax.ShapeDtypeStruct((b, s, d_ret), BF16),
        grid=(b, s // RET_BLOCK),
        in_specs=[blk, blk] + [_full(t.shape) for t in tabs] + [_full((1, d_ret))],
        out_specs=pl.BlockSpec((1, RET_BLOCK, d_ret), lambda bi, i: (bi, i, 0)),
        scratch_shapes=[pltpu.VMEM((d_ret, d_ret), F32)],
        compiler_params=_cparams(2),
        name="retention",
    )(qk, vg, *tabs, gain)


def _rotary_tables(seq):
    half = HEAD_DIM // 2
    inv = 1.0 / (ROPE_BASE ** (jnp.arange(half, dtype=F32) / half))
    lane = jnp.arange(LANES)
    ang = jnp.arange(seq).astype(F32)[:, None] * inv[lane % half][None, :]
    sign = jnp.where((lane % HEAD_DIM) < half, -1.0, 1.0).astype(F32)
    return jnp.cos(ang), jnp.sin(ang) * sign[None, :]


def _retention_tables(n_heads):
    d_ret = n_heads * HEAD_DIM
    log_gamma = jnp.log1p(-jnp.exp2(-5.0 - jnp.arange(n_heads, dtype=F32)))
    idx = jnp.arange(RET_BLOCK, dtype=F32)
    t, s = idx[:, None], idx[None, :]
    same = (t // CHUNK) == (s // CHUNK)
    dist = jnp.where(same, jnp.abs(t - s), t - s)
    seen = same | ((s // CHUNK) < (t // CHUNK))
    dmat = jnp.where(seen[None], jnp.exp(log_gamma[:, None, None] * dist[None]), 0.0)
    lane_gamma = jnp.repeat(log_gamma, HEAD_DIM)
    qdec = jnp.exp(lane_gamma[None, :] * (idx + 1.0)[:, None])
    kdec = jnp.exp(lane_gamma[None, :] * (RET_BLOCK - 1.0 - idx)[:, None])
    head = jnp.arange(d_ret) // HEAD_DIM
    bd = (head[:, None] == head[None, :]).astype(F32)
    cdec = bd * jnp.exp(lane_gamma * RET_BLOCK)[:, None]
    avg = jnp.concatenate([bd, bd], axis=0).astype(BF16) * (1.0 / HEAD_DIM)
    return dmat, qdec, kdec, cdec, bd, avg.astype(BF16)


def _tri_table():
    r = np.arange(2 * SB_BLOCK)[:, None] % SB_BLOCK
    c = np.arange(2 * SB_BLOCK)[None, :]
    return jnp.asarray((c >= SB_BLOCK) | (r >= c), dtype=BF16)


def _ffn_weights(w_in, w_out):
    d, two_ff = w_in.shape
    d_ff = two_ff // 2
    n_chunks = d_ff // FF_CHUNK
    gate = w_in[:, :d_ff].reshape(d, n_chunks, FF_CHUNK)
    up = w_in[:, d_ff:].reshape(d, n_chunks, FF_CHUNK)
    win = jnp.concatenate([gate, up], axis=2).transpose(1, 0, 2).astype(BF16)
    return win, w_out.astype(BF16)


def kernel(x, ffn1_norm, ffn1_w_in, ffn1_w_out, mix_norm, mix_w_in, conv_w, conv_b, conv_ln_g,
           conv_ln_b, ret_norm_g, mix_w_out, ffn2_norm, ffn2_w_in, ffn2_w_out, final_norm):
    b, s, d = x.shape
    depth = ffn1_norm.shape[0]
    d_conv = conv_w.shape[2]
    d_ret = ret_norm_g.shape[1]
    d_sb = d - d_conv - d_ret
    assert s % CONV_TILE == 0 and s % ROW_TILE == 0 and s % RET_BLOCK == 0 and s % SB_BLOCK == 0
    assert ffn1_w_out.shape[1] % FF_CHUNK == 0 and d % OUT_TILE == 0

    cos_tab, sin_tab = _rotary_tables(s)
    ret_tabs = _retention_tables(d_ret // HEAD_DIM)
    tri = _tri_table()
    scale = HEAD_DIM ** -0.5
    col = jnp.arange(mix_w_in.shape[2])
    q_sb_cols = (col >= 2 * d_conv) & (col < 2 * d_conv + d_sb)
    q_r_cols = (col >= 2 * d_conv + 3 * d_sb) & (col < 2 * d_conv + 3 * d_sb + d_ret)
    col_scale = jnp.where(q_sb_cols | q_r_cols, scale, 1.0).astype(F32)

    row = lambda g: g.reshape(1, -1)
    xf = x.reshape(b * s, d)
    for l in range(depth):
        win1, wout1 = _ffn_weights(ffn1_w_in[l], ffn1_w_out[l])
        win2, wout2 = _ffn_weights(ffn2_w_in[l], ffn2_w_out[l])
        w_mix = (mix_w_in[l] * col_scale[None, :]).astype(BF16)

        xf = _ffn(xf, row(ffn1_norm[l]), win1, wout1)
        uconv, q, k, v, qkr, vgr = _inproj(xf, row(mix_norm[l]), w_mix, cos_tab, sin_tab, s,
                                           2 * d_conv, d_sb, d_ret)
        seq3 = lambda t: t.reshape(b, s, t.shape[1])
        y_conv = _conv(seq3(uconv), conv_w[l], row(conv_b[l]), row(conv_ln_g[l]), row(conv_ln_b[l]))
        y_sb = _stick_breaking(seq3(q), seq3(k), seq3(v), tri)
        y_ret = _retention(seq3(qkr), seq3(vgr), ret_tabs, row(ret_norm_g[l]))
        flat = lambda t: t.reshape(b * s, t.shape[2])
        last = l == depth - 1
        xf = _ffn(xf, row(ffn2_norm[l]), win2, wout2,
                  mix=(flat(y_conv), flat(y_sb), flat(y_ret)), wmix=mix_w_out[l].astype(BF16),
                  final_g=row(final_norm) if last else None)
    return xf.reshape(b, s, d)
```

```python
import functools

import jax
import jax.numpy as jnp
import numpy as np
from jax import lax
from jax.experimental import pallas as pl
from jax.experimental.pallas import tpu as pltpu

F32 = jnp.float32
BF16 = jnp.bfloat16

EPS = 1e-6
ROPE_BASE = 10000.0
CHUNK = 64
CONV_WIDTH = 31
HEAD_DIM = 64
LANES = 128
SUBLANES = 8
MXU_DIM = 256

ROW_TILE = 512
FF_CHUNK = MXU_DIM
OUT_TILE = 512
CONV_TILE = 512
CONV_HALO = 32
CONV_SUB = 128
SB_BLOCK = 128
SB_HEADS_PER_GROUP = LANES // HEAD_DIM
SB_DEAD_LOG = 110.0
SIGN_BIT = 0x80000000
BF16_BITS = 0xFFFF0000
RET_BLOCK = 256
VMEM_LIMIT = 56 * 1024 * 1024


def _cparams(n_axes):
    return pltpu.CompilerParams(dimension_semantics=("arbitrary",) * n_axes,
                                vmem_limit_bytes=VMEM_LIMIT)


def _rms(x, g):
    return (x * lax.rsqrt(jnp.mean(x * x, axis=-1, keepdims=True) + EPS)) * g


def _silu(x):
    return x * jax.nn.sigmoid(x)


def _full(shape):
    return pl.BlockSpec(shape, lambda *_: (0,) * len(shape))


def _resident(shape):
    return pl.BlockSpec(shape, lambda *_: (0,) * len(shape), pipeline_mode=pl.Buffered(1))


def _ffn_kernel(*refs, n_chunks, n_out, n_mix, final):
    refs = list(refs)
    x_ref = refs.pop(0)
    mix_refs = [refs.pop(0) for _ in range(n_mix)]
    wmix_ref = refs.pop(0) if n_mix else None
    g_ref, win_ref, wout_ref = refs.pop(0), refs.pop(0), refs.pop(0)
    fg_ref = refs.pop(0) if final else None
    o_ref, h_scr, a_scr = refs.pop(0), refs.pop(0), refs.pop(0)
    out_tiles = [slice(m * OUT_TILE, (m + 1) * OUT_TILE) for m in range(n_out)]
    d_ff = wout_ref.shape[0]
    if n_mix:
        for sl in out_tiles:
            y, lo = None, 0
            for y_ref in mix_refs:
                part = jnp.dot(y_ref[...], wmix_ref[lo:lo + y_ref.shape[1], sl],
                               preferred_element_type=F32)
                y = part if y is None else y + part
                lo += y_ref.shape[1]
            o_ref[:, sl] = x_ref[:, sl] + y
        x_ref = o_ref
    h_scr[...] = _rms(x_ref[...], g_ref[...]).astype(BF16)
    for j in range(n_chunks):
        sl = slice(j * FF_CHUNK, (j + 1) * FF_CHUNK)
        up_sl = slice(d_ff + j * FF_CHUNK, d_ff + (j + 1) * FF_CHUNK)
        gate = jnp.dot(h_scr[...], win_ref[:, sl], preferred_element_type=F32)
        up = jnp.dot(h_scr[...], win_ref[:, up_sl], preferred_element_type=F32)
        a_scr[:, sl] = (_silu(gate) * up).astype(BF16)
    for sl in out_tiles:
        y = jnp.dot(a_scr[...], wout_ref[:, sl], preferred_element_type=F32)
        o_ref[:, sl] = x_ref[:, sl] + 0.5 * y
    if final:
        o_ref[...] = _rms(o_ref[...], fg_ref[...])


def _ffn(x, g, win, wout, mix=(), wmix=None, final_g=None):
    n, d = x.shape
    d_ff = wout.shape[0]
    n_chunks = d_ff // FF_CHUNK
    final = final_g is not None
    row = lambda i: (i, 0)
    in_specs = [pl.BlockSpec((ROW_TILE, d), row)]
    in_specs += [pl.BlockSpec((ROW_TILE, y.shape[1]), row) for y in mix]
    args = [x, *mix]
    if mix:
        in_specs.append(_resident(wmix.shape))
        args.append(wmix)
    in_specs += [_full((1, d)), _resident(win.shape), _resident(wout.shape)]
    args += [g, win, wout]
    if final:
        in_specs.append(_full((1, d)))
        args.append(final_g)
    return pl.pallas_call(
        functools.partial(_ffn_kernel, n_chunks=n_chunks, n_out=d // OUT_TILE, n_mix=len(mix),
                          final=final),
        out_shape=jax.ShapeDtypeStruct((n, d), F32),
        grid=(n // ROW_TILE,),
        in_specs=in_specs,
        out_specs=pl.BlockSpec((ROW_TILE, d), row),
        scratch_shapes=[pltpu.VMEM((ROW_TILE, d), BF16), pltpu.VMEM((ROW_TILE, d_ff), BF16)],
        compiler_params=_cparams(1),
        name="ffn",
    )(*args)


def _rotary(t, cos, sin_signed, first_half):
    partner = jnp.where(first_half, pltpu.roll(t, LANES - HEAD_DIM // 2, 1),
                        pltpu.roll(t, HEAD_DIM // 2, 1))
    return t * cos + partner * sin_signed


def _inproj_kernel(x_ref, g_ref, w_ref, cos_ref, sin_ref, uconv_ref, q_ref, k_ref, v_ref,
                   qkr_ref, vgr_ref, h_scr, *, d_conv2, d_sb, d_ret):
    h_scr[...] = _rms(x_ref[...], g_ref[...]).astype(BF16)

    def proj(lo, width):
        return jnp.dot(h_scr[...], w_ref[:, lo:lo + width], preferred_element_type=F32)

    uconv_ref[...] = proj(0, d_conv2)
    o = d_conv2
    q_ref[...] = proj(o, d_sb).astype(BF16)
    k_ref[...] = proj(o + d_sb, d_sb).astype(BF16)
    v_ref[...] = proj(o + 2 * d_sb, d_sb).astype(BF16)
    o += 3 * d_sb
    qk = proj(o, 2 * d_ret)
    cos = cos_ref[...]
    sin = sin_ref[...]
    lane = lax.broadcasted_iota(jnp.int32, (1, LANES), 1)
    first_half = (lane % HEAD_DIM) < (HEAD_DIM // 2)
    for c in range(2 * d_ret // LANES):
        sl = slice(c * LANES, (c + 1) * LANES)
        qkr_ref[:, sl] = _rotary(qk[:, sl], cos, sin, first_half)
    vgr_ref[...] = proj(o + 2 * d_ret, 2 * d_ret)


def _inproj(x, g, w, cos_tab, sin_tab, seq, d_conv2, d_sb, d_ret):
    n, d = x.shape
    tiles_per_seq = seq // ROW_TILE
    row = lambda i: (i, 0)
    pos = lambda i: (i % tiles_per_seq, 0)
    outs = [jax.ShapeDtypeStruct((n, d_conv2), F32)] + [jax.ShapeDtypeStruct((n, d_sb), BF16)] * 3 \
        + [jax.ShapeDtypeStruct((n, 2 * d_ret), F32)] * 2
    return pl.pallas_call(
        functools.partial(_inproj_kernel, d_conv2=d_conv2, d_sb=d_sb, d_ret=d_ret),
        out_shape=outs,
        grid=(n // ROW_TILE,),
        in_specs=[pl.BlockSpec((ROW_TILE, d), row), _full((1, d)), _full(w.shape),
                  pl.BlockSpec((ROW_TILE, LANES), pos), pl.BlockSpec((ROW_TILE, LANES), pos)],
        out_specs=[pl.BlockSpec((ROW_TILE, s.shape[1]), row) for s in outs],
        scratch_shapes=[pltpu.VMEM((ROW_TILE, d), BF16)],
        compiler_params=_cparams(1),
        name="mixer_inproj",
    )(x, g, w, cos_tab, sin_tab)


def _conv_kernel(u_ref, halo_ref, cw_ref, cb_ref, lg_ref, lb_ref, o_ref, v_scr, p_scr, *, d_conv):
    def glu(u):
        return u[:, :d_conv] * jax.nn.sigmoid(u[:, d_conv:])

    first = pl.program_id(1) == 0
    v_scr[0:CONV_HALO, :] = jnp.where(first, 0.0, glu(halo_ref[0]))
    v_scr[CONV_HALO:CONV_HALO + CONV_TILE, :] = glu(u_ref[0])
    v_scr[CONV_HALO + CONV_TILE:, :] = jnp.zeros((SUBLANES, d_conv), F32)
    off = CONV_HALO - (CONV_WIDTH - 1)
    for r in range(CONV_TILE // CONV_SUB):
        for shift in range(SUBLANES):
            part = None
            for o in range(shift, off + CONV_WIDTH, SUBLANES):
                if o < off:
                    continue
                rows = v_scr[pl.ds(r * CONV_SUB + o - shift, CONV_SUB + SUBLANES), :]
                term = cw_ref[o - off:o - off + 1, :] * rows
                part = term if part is None else part + term
            p_scr[shift] = part
        y = cb_ref[...] + p_scr[0, 0:CONV_SUB, :]
        for shift in range(1, SUBLANES):
            y = y + p_scr[shift, pl.ds(shift, CONV_SUB), :]
        mu = jnp.mean(y, axis=-1, keepdims=True)
        yc = y - mu
        var = jnp.mean(yc * yc, axis=-1, keepdims=True)
        ln = yc * lax.rsqrt(var + EPS) * lg_ref[...] + lb_ref[...]
        o_ref[0, r * CONV_SUB:(r + 1) * CONV_SUB, :] = _silu(ln).astype(BF16)


def _conv(u, cw, cb, lg, lb):
    b, s, d2 = u.shape
    d_conv = d2 // 2
    halo_per_tile = CONV_TILE // CONV_HALO
    return pl.pallas_call(
        functools.partial(_conv_kernel, d_conv=d_conv),
        out_shape=jax.ShapeDtypeStruct((b, s, d_conv), BF16),
        grid=(b, s // CONV_TILE),
        in_specs=[pl.BlockSpec((1, CONV_TILE, d2), lambda bi, i: (bi, i, 0)),
                  pl.BlockSpec((1, CONV_HALO, d2),
                               lambda bi, i: (bi, jnp.maximum(i * halo_per_tile - 1, 0), 0)),
                  _full(cw.shape), _full((1, d_conv)), _full((1, d_conv)), _full((1, d_conv))],
        out_specs=pl.BlockSpec((1, CONV_TILE, d_conv), lambda bi, i: (bi, i, 0)),
        scratch_shapes=[pltpu.VMEM((CONV_HALO + CONV_TILE + SUBLANES, d_conv), F32),
                        pltpu.VMEM((SUBLANES, CONV_SUB + SUBLANES, d_conv), F32)],
        compiler_params=_cparams(2),
        name="conv_module",
    )(u, u, cw, cb, lg, lb)


def _sb_kernel(q_ref, k_ref, v_ref, tri_ref, o_ref, qs_scr, acc_scr, carry_scr, *, n_pairs):
    i = pl.program_id(1)
    lane = lax.broadcasted_iota(jnp.int32, (1, LANES), 1)
    row = lax.broadcasted_iota(jnp.int32, (SB_BLOCK, SB_BLOCK), 0)
    col = lax.broadcasted_iota(jnp.int32, (SB_BLOCK, SB_BLOCK), 1)
    causal = jnp.concatenate([col < row] * SB_HEADS_PER_GROUP, axis=0)
    for p in range(n_pairs):
        q = q_ref[0, :, p * LANES:(p + 1) * LANES]
        qs_scr[p] = jnp.concatenate(
            [jnp.where(lane // HEAD_DIM == hh, q, jnp.zeros_like(q))
             for hh in range(SB_HEADS_PER_GROUP)], axis=0)

    def sweep(blocks, resume):
        pairs = range(n_pairs)
        starts = [pl.multiple_of(j * SB_BLOCK, SB_BLOCK) for j, _ in blocks]

        def rows_of(ref, n, p):
            return ref[0, pl.ds(starts[n], SB_BLOCK), p * LANES:(p + 1) * LANES]

        zs = [[lax.dot_general(qs_scr[p], rows_of(k_ref, n, p), (((1,), (1,)), ((), ())),
                               preferred_element_type=F32) for p in pairs]
              for n in range(len(blocks))]
        rts = []
        for n, (_, mask) in enumerate(blocks):
            rts.append([])
            for p in pairs:
                z = zs[n][p]
                zbits = lax.bitcast_convert_type(z, jnp.uint32)
                neg_abs = lax.bitcast_convert_type(zbits | jnp.uint32(SIGN_BIT), F32)
                sp = jnp.maximum(z, 0.0) + jnp.log(1.0 + jnp.exp(neg_abs))
                if mask is not None:
                    sp = jnp.where(mask, sp, 0.0)
                hi = lax.bitcast_convert_type(
                    lax.bitcast_convert_type(sp, jnp.uint32) & jnp.uint32(BF16_BITS), F32)
                hl = jnp.concatenate([hi.astype(BF16), (sp - hi).astype(BF16)], axis=1)
                rts[n].append(jnp.dot(hl, tri_ref[...], preferred_element_type=F32))
        carry = [carry_scr[p] if resume else None for p in pairs]
        acc = [acc_scr[p] if resume else None for p in pairs]
        for n, (_, mask) in enumerate(blocks):
            for p in pairs:
                within, total = rts[n][p][:, :SB_BLOCK], rts[n][p][:, SB_BLOCK:]
                w = jnp.exp(zs[n][p] - (within if carry[p] is None else within + carry[p]))
                if mask is not None:
                    w = jnp.where(mask, w, 0.0)
                carry[p] = total if carry[p] is None else carry[p] + total
                pv = jnp.dot(w.astype(BF16), rows_of(v_ref, n, p), preferred_element_type=F32)
                acc[p] = pv if acc[p] is None else acc[p] + pv
        lowest = None
        for p in pairs:
            carry_scr[p] = carry[p]
            acc_scr[p] = acc[p]
            lowest = carry[p] if lowest is None else jnp.minimum(lowest, carry[p])
        return jnp.min(lowest)

    def more(st):
        return jnp.logical_and(st[0] >= 0, st[1] < SB_DEAD_LOG)

    def body(st):
        return st[0] - 1, sweep([(st[0], None)], True)

    lowest = lax.cond(i > 0, lambda: sweep([(i, causal), (i - 1, None)], False),
                      lambda: sweep([(i, causal)], False))
    lax.while_loop(more, body, (i - 2, lowest))
    for p in range(n_pairs):
        acc = acc_scr[p]
        out = acc[0:SB_BLOCK]
        for hh in range(1, SB_HEADS_PER_GROUP):
            out = jnp.where(lane // HEAD_DIM == hh, acc[hh * SB_BLOCK:(hh + 1) * SB_BLOCK], out)
        o_ref[0, :, p * LANES:(p + 1) * LANES] = out.astype(BF16)


def _stick_breaking(q, k, v, tri):
    b, s, d_sb = q.shape
    n_pairs = d_sb // LANES
    rows = SB_HEADS_PER_GROUP * SB_BLOCK
    blk = pl.BlockSpec((1, SB_BLOCK, d_sb), lambda bi, i: (bi, i, 0))
    whole = pl.BlockSpec((1, s, d_sb), lambda bi, i: (bi, 0, 0))
    return pl.pallas_call(
        functools.partial(_sb_kernel, n_pairs=n_pairs),
        out_shape=jax.ShapeDtypeStruct((b, s, d_sb), BF16),
        grid=(b, s // SB_BLOCK),
        in_specs=[blk, whole, whole, _full(tri.shape)],
        out_specs=blk,
        scratch_shapes=[pltpu.VMEM((n_pairs, rows, LANES), BF16),
                        pltpu.VMEM((n_pairs, rows, LANES), F32),
                        pltpu.VMEM((n_pairs, rows, LANES), F32)],
        compiler_params=_cparams(2),
        name="stick_breaking",
    )(q, k, v, tri)


def _ret_kernel(qk_ref, vg_ref, dmat_ref, qdec_ref, kdec_ref, cdec_ref, bd_ref, avg_ref, gain_ref,
                o_ref, state_scr, *, d_ret, n_batch):
    n_heads = d_ret // HEAD_DIM
    batches = range(n_batch)
    nt = (((1,), (1,)), ((), ()))

    @pl.when(pl.program_id(0) == 0)
    def _():
        state_scr[...] = jnp.zeros_like(state_scr)

    lane = lax.broadcasted_iota(jnp.int32, (1, d_ret), 1)
    in_head = [lane // HEAD_DIM == h for h in range(n_heads)]
    q = [qk_ref[b, :, :d_ret] for b in batches]
    k = [qk_ref[b, :, d_ret:] for b in batches]
    v = [vg_ref[b, :, :d_ret] for b in batches]
    probs = []
    for b in batches:
        kb = k[b].astype(BF16)
        probs.append(jnp.concatenate(
            [(lax.dot_general(jnp.where(in_head[h], q[b], 0.0).astype(BF16), kb, nt,
                              preferred_element_type=F32) * dmat_ref[h]).astype(BF16)
             for h in range(n_heads)], axis=1))
    y = []
    for b in batches:
        vals = jnp.concatenate([jnp.where(in_head[h], v[b], 0.0).astype(BF16)
                                for h in range(n_heads)], axis=0)
        y.append(jnp.dot(probs[b], vals, preferred_element_type=F32))
    for b in batches:
        state = state_scr[b]
        y[b] = y[b] + jnp.dot((q[b] * qdec_ref[...]).astype(BF16), state.astype(BF16),
                              preferred_element_type=F32)
        kv = lax.dot_general((k[b] * kdec_ref[...]).astype(BF16), v[b].astype(BF16),
                             (((0,), (0,)), ((), ())), preferred_element_type=F32)
        state_scr[b] = cdec_ref[...] * state + bd_ref[...] * kv

    def head_mean(t):
        hi = t.astype(BF16)
        lo = (t - hi.astype(F32)).astype(BF16)
        return jnp.dot(jnp.concatenate([hi, lo], axis=1), avg_ref[...], preferred_element_type=F32)

    yc = [y[b] - head_mean(y[b]) for b in batches]
    var = [head_mean(yc[b] * yc[b]) for b in batches]
    for b in batches:
        g = vg_ref[b, :, d_ret:]
        o_ref[b] = (_silu(g) * (yc[b] * lax.rsqrt(var[b] + EPS) * gain_ref[...])).astype(BF16)


def _retention(qk, vg, tabs, gain):
    b, s, d2 = qk.shape
    d_ret = d2 // 2
    blk = pl.BlockSpec((b, RET_BLOCK, d2), lambda i: (0, i, 0))
    return pl.pallas_call(
        functools.partial(_ret_kernel, d_ret=d_ret, n_batch=b),
        out_shape=jax.ShapeDtypeStruct((b, s, d_ret), BF16),
        grid=(s // RET_BLOCK,),
        in_specs=[blk, blk] + [_full(t.shape) for t in tabs] + [_full((1, d_ret))],
        out_specs=pl.BlockSpec((b, RET_BLOCK, d_ret), lambda i: (0, i, 0)),
        scratch_shapes=[pltpu.VMEM((b, d_ret, d_ret), F32)],
        compiler_params=_cparams(1),
        name="retention",
    )(qk, vg, *tabs, gain)


def _rotary_tables(seq):
    half = HEAD_DIM // 2
    inv = 1.0 / (ROPE_BASE ** (jnp.arange(half, dtype=F32) / half))
    lane = jnp.arange(LANES)
    ang = jnp.arange(seq).astype(F32)[:, None] * inv[lane % half][None, :]
    sign = jnp.where((lane % HEAD_DIM) < half, -1.0, 1.0).astype(F32)
    return jnp.cos(ang), jnp.sin(ang) * sign[None, :]


def _retention_tables(n_heads):
    d_ret = n_heads * HEAD_DIM
    log_gamma = jnp.log1p(-jnp.exp2(-5.0 - jnp.arange(n_heads, dtype=F32)))
    idx = jnp.arange(RET_BLOCK, dtype=F32)
    t, s = idx[:, None], idx[None, :]
    same = (t // CHUNK) == (s // CHUNK)
    dist = jnp.where(same, jnp.abs(t - s), t - s)
    seen = same | ((s // CHUNK) < (t // CHUNK))
    dmat = jnp.where(seen[None], jnp.exp(log_gamma[:, None, None] * dist[None]), 0.0)
    lane_gamma = jnp.repeat(log_gamma, HEAD_DIM)
    qdec = jnp.exp(lane_gamma[None, :] * (idx + 1.0)[:, None])
    kdec = jnp.exp(lane_gamma[None, :] * (RET_BLOCK - 1.0 - idx)[:, None])
    head = jnp.arange(d_ret) // HEAD_DIM
    bd = (head[:, None] == head[None, :]).astype(F32)
    cdec = bd * jnp.exp(lane_gamma * RET_BLOCK)[:, None]
    avg = jnp.concatenate([bd, bd], axis=0).astype(BF16) * (1.0 / HEAD_DIM)
    return dmat, qdec, kdec, cdec, bd, avg.astype(BF16)


def _tri_table():
    r = np.arange(2 * SB_BLOCK)[:, None] % SB_BLOCK
    c = np.arange(2 * SB_BLOCK)[None, :]
    return jnp.asarray((c >= SB_BLOCK) | (r >= c), dtype=BF16)


def kernel(x, ffn1_norm, ffn1_w_in, ffn1_w_out, mix_norm, mix_w_in, conv_w, conv_b, conv_ln_g,
           conv_ln_b, ret_norm_g, mix_w_out, ffn2_norm, ffn2_w_in, ffn2_w_out, final_norm):
    b, s, d = x.shape
    depth = ffn1_norm.shape[0]
    d_conv = conv_w.shape[2]
    d_ret = ret_norm_g.shape[1]
    d_sb = d - d_conv - d_ret
    assert s % CONV_TILE == 0 and s % ROW_TILE == 0 and s % RET_BLOCK == 0 and s % SB_BLOCK == 0
    assert ffn1_w_out.shape[1] % FF_CHUNK == 0 and d % OUT_TILE == 0

    cos_tab, sin_tab = _rotary_tables(s)
    ret_tabs = _retention_tables(d_ret // HEAD_DIM)
    tri = _tri_table()
    scale = HEAD_DIM ** -0.5
    col = jnp.arange(mix_w_in.shape[2])
    q_sb_cols = (col >= 2 * d_conv) & (col < 2 * d_conv + d_sb)
    q_r_cols = (col >= 2 * d_conv + 3 * d_sb) & (col < 2 * d_conv + 3 * d_sb + d_ret)
    col_scale = jnp.where(q_sb_cols | q_r_cols, scale, 1.0).astype(F32)

    row = lambda g: g.reshape(1, -1)
    xf = x.reshape(b * s, d)
    for l in range(depth):
        win1, wout1 = ffn1_w_in[l].astype(BF16), ffn1_w_out[l].astype(BF16)
        win2, wout2 = ffn2_w_in[l].astype(BF16), ffn2_w_out[l].astype(BF16)
        w_mix = (mix_w_in[l] * col_scale[None, :]).astype(BF16)

        xf = _ffn(xf, row(ffn1_norm[l]), win1, wout1)
        uconv, q, k, v, qkr, vgr = _inproj(xf, row(mix_norm[l]), w_mix, cos_tab, sin_tab, s,
                                           2 * d_conv, d_sb, d_ret)
        seq3 = lambda t: t.reshape(b, s, t.shape[1])
        y_conv = _conv(seq3(uconv), conv_w[l], row(conv_b[l]), row(conv_ln_g[l]), row(conv_ln_b[l]))
        y_sb = _stick_breaking(seq3(q), seq3(k), seq3(v), tri)
        y_ret = _retention(seq3(qkr), seq3(vgr), ret_tabs, row(ret_norm_g[l]))
        flat = lambda t: t.reshape(b * s, t.shape[2])
        last = l == depth - 1
        xf = _ffn(xf, row(ffn2_norm[l]), win2, wout2,
                  mix=(flat(y_conv), flat(y_sb), flat(y_ret)), wmix=mix_w_out[l].astype(BF16),
                  final_g=row(final_norm) if last else None)
    return xf.reshape(b, s, d)
```

```python
import functools

import jax
import jax.numpy as jnp
import numpy as np
from jax import lax
from jax.experimental import pallas as pl
from jax.experimental.pallas import tpu as pltpu

F32 = jnp.float32
BF16 = jnp.bfloat16

EPS = 1e-6
ROPE_BASE = 10000.0
CHUNK = 64
CONV_WIDTH = 31
HEAD_DIM = 64
LANES = 128
SUBLANES = 8
MXU_DIM = 256

ROW_TILE = 512
FF_CHUNK = MXU_DIM
OUT_TILE = 512
CONV_TILE = 512
CONV_HALO = 32
CONV_SUB = 128
SB_BLOCK = 128
SB_HEADS_PER_GROUP = LANES // HEAD_DIM
SB_FIRST_BLOCKS = 3
SB_DEAD_LOG = 110.0
SIGN_BIT = 0x80000000
BF16_BITS = 0xFFFF0000
RET_BLOCK = 256
VMEM_LIMIT = 56 * 1024 * 1024


def _cparams(n_axes):
    return pltpu.CompilerParams(dimension_semantics=("arbitrary",) * n_axes,
                                vmem_limit_bytes=VMEM_LIMIT)


def _rms(x, g):
    return (x * lax.rsqrt(jnp.mean(x * x, axis=-1, keepdims=True) + EPS)) * g


def _silu(x):
    return x * jax.nn.sigmoid(x)


def _full(shape):
    return pl.BlockSpec(shape, lambda *_: (0,) * len(shape))


def _resident(shape):
    return pl.BlockSpec(shape, lambda *_: (0,) * len(shape), pipeline_mode=pl.Buffered(1))


def _ffn_kernel(*refs, n_chunks, n_out, n_mix, final):
    refs = list(refs)
    x_ref = refs.pop(0)
    mix_refs = [refs.pop(0) for _ in range(n_mix)]
    wmix_ref = refs.pop(0) if n_mix else None
    g_ref, win_ref, wout_ref = refs.pop(0), refs.pop(0), refs.pop(0)
    fg_ref = refs.pop(0) if final else None
    o_ref, h_scr, a_scr = refs.pop(0), refs.pop(0), refs.pop(0)
    out_tiles = [slice(m * OUT_TILE, (m + 1) * OUT_TILE) for m in range(n_out)]
    d_ff = wout_ref.shape[0]
    if n_mix:
        for sl in out_tiles:
            y, lo = None, 0
            for y_ref in mix_refs:
                part = jnp.dot(y_ref[...], wmix_ref[lo:lo + y_ref.shape[1], sl],
                               preferred_element_type=F32)
                y = part if y is None else y + part
                lo += y_ref.shape[1]
            o_ref[:, sl] = x_ref[:, sl] + y
        x_ref = o_ref
    h_scr[...] = _rms(x_ref[...], g_ref[...]).astype(BF16)
    for j in range(n_chunks):
        sl = slice(j * FF_CHUNK, (j + 1) * FF_CHUNK)
        up_sl = slice(d_ff + j * FF_CHUNK, d_ff + (j + 1) * FF_CHUNK)
        gate = jnp.dot(h_scr[...], win_ref[:, sl], preferred_element_type=F32)
        up = jnp.dot(h_scr[...], win_ref[:, up_sl], preferred_element_type=F32)
        a_scr[:, sl] = (_silu(gate) * up).astype(BF16)
    for sl in out_tiles:
        y = jnp.dot(a_scr[...], wout_ref[:, sl], preferred_element_type=F32)
        o_ref[:, sl] = x_ref[:, sl] + 0.5 * y
    if final:
        o_ref[...] = _rms(o_ref[...], fg_ref[...])


def _ffn(x, g, win, wout, mix=(), wmix=None, final_g=None):
    n, d = x.shape
    d_ff = wout.shape[0]
    n_chunks = d_ff // FF_CHUNK
    final = final_g is not None
    row = lambda i: (i, 0)
    in_specs = [pl.BlockSpec((ROW_TILE, d), row)]
    in_specs += [pl.BlockSpec((ROW_TILE, y.shape[1]), row) for y in mix]
    args = [x, *mix]
    if mix:
        in_specs.append(_resident(wmix.shape))
        args.append(wmix)
    in_specs += [_full((1, d)), _resident(win.shape), _resident(wout.shape)]
    args += [g, win, wout]
    if final:
        in_specs.append(_full((1, d)))
        args.append(final_g)
    return pl.pallas_call(
        functools.partial(_ffn_kernel, n_chunks=n_chunks, n_out=d // OUT_TILE, n_mix=len(mix),
                          final=final),
        out_shape=jax.ShapeDtypeStruct((n, d), F32),
        grid=(n // ROW_TILE,),
        in_specs=in_specs,
        out_specs=pl.BlockSpec((ROW_TILE, d), row),
        scratch_shapes=[pltpu.VMEM((ROW_TILE, d), BF16), pltpu.VMEM((ROW_TILE, d_ff), BF16)],
        compiler_params=_cparams(1),
        name="ffn",
    )(*args)


def _rotary(t, cos, sin_signed, first_half):
    partner = jnp.where(first_half, pltpu.roll(t, LANES - HEAD_DIM // 2, 1),
                        pltpu.roll(t, HEAD_DIM // 2, 1))
    return t * cos + partner * sin_signed


def _inproj_kernel(x_ref, g_ref, w_ref, cos_ref, sin_ref, uconv_ref, q_ref, k_ref, v_ref,
                   qkr_ref, vgr_ref, h_scr, *, d_conv2, d_sb, d_ret):
    h_scr[...] = _rms(x_ref[...], g_ref[...]).astype(BF16)

    def proj(lo, width):
        return jnp.dot(h_scr[...], w_ref[:, lo:lo + width], preferred_element_type=F32)

    uconv_ref[...] = proj(0, d_conv2)
    o = d_conv2
    q_ref[...] = proj(o, d_sb).astype(BF16)
    k_ref[...] = proj(o + d_sb, d_sb).astype(BF16)
    v_ref[...] = proj(o + 2 * d_sb, d_sb).astype(BF16)
    o += 3 * d_sb
    qk = proj(o, 2 * d_ret)
    cos = cos_ref[...]
    sin = sin_ref[...]
    lane = lax.broadcasted_iota(jnp.int32, (1, LANES), 1)
    first_half = (lane % HEAD_DIM) < (HEAD_DIM // 2)
    for c in range(2 * d_ret // LANES):
        sl = slice(c * LANES, (c + 1) * LANES)
        qkr_ref[:, sl] = _rotary(qk[:, sl], cos, sin, first_half)
    vgr_ref[...] = proj(o + 2 * d_ret, 2 * d_ret)


def _inproj(x, g, w, cos_tab, sin_tab, seq, d_conv2, d_sb, d_ret):
    n, d = x.shape
    tiles_per_seq = seq // ROW_TILE
    row = lambda i: (i, 0)
    pos = lambda i: (i % tiles_per_seq, 0)
    outs = [jax.ShapeDtypeStruct((n, d_conv2), F32)] + [jax.ShapeDtypeStruct((n, d_sb), BF16)] * 3 \
        + [jax.ShapeDtypeStruct((n, 2 * d_ret), F32)] * 2
    return pl.pallas_call(
        functools.partial(_inproj_kernel, d_conv2=d_conv2, d_sb=d_sb, d_ret=d_ret),
        out_shape=outs,
        grid=(n // ROW_TILE,),
        in_specs=[pl.BlockSpec((ROW_TILE, d), row), _full((1, d)), _full(w.shape),
                  pl.BlockSpec((ROW_TILE, LANES), pos), pl.BlockSpec((ROW_TILE, LANES), pos)],
        out_specs=[pl.BlockSpec((ROW_TILE, s.shape[1]), row) for s in outs],
        scratch_shapes=[pltpu.VMEM((ROW_TILE, d), BF16)],
        compiler_params=_cparams(1),
        name="mixer_inproj",
    )(x, g, w, cos_tab, sin_tab)


def _conv_kernel(u_ref, halo_ref, cw_ref, cb_ref, lg_ref, lb_ref, o_ref, v_scr, p_scr, *, d_conv):
    def glu(u):
        return u[:, :d_conv] * jax.nn.sigmoid(u[:, d_conv:])

    first = pl.program_id(1) == 0
    v_scr[0:CONV_HALO, :] = jnp.where(first, 0.0, glu(halo_ref[0]))
    v_scr[CONV_HALO:CONV_HALO + CONV_TILE, :] = glu(u_ref[0])
    v_scr[CONV_HALO + CONV_TILE:, :] = jnp.zeros((SUBLANES, d_conv), F32)
    off = CONV_HALO - (CONV_WIDTH - 1)
    for r in range(CONV_TILE // CONV_SUB):
        for shift in range(SUBLANES):
            part = None
            for o in range(shift, off + CONV_WIDTH, SUBLANES):
                if o < off:
                    continue
                rows = v_scr[pl.ds(r * CONV_SUB + o - shift, CONV_SUB + SUBLANES), :]
                term = cw_ref[o - off:o - off + 1, :] * rows
                part = term if part is None else part + term
            p_scr[shift] = part
        y = cb_ref[...] + p_scr[0, 0:CONV_SUB, :]
        for shift in range(1, SUBLANES):
            y = y + p_scr[shift, pl.ds(shift, CONV_SUB), :]
        mu = jnp.mean(y, axis=-1, keepdims=True)
        yc = y - mu
        var = jnp.mean(yc * yc, axis=-1, keepdims=True)
        ln = yc * lax.rsqrt(var + EPS) * lg_ref[...] + lb_ref[...]
        o_ref[0, r * CONV_SUB:(r + 1) * CONV_SUB, :] = _silu(ln).astype(BF16)


def _conv(u, cw, cb, lg, lb):
    b, s, d2 = u.shape
    d_conv = d2 // 2
    halo_per_tile = CONV_TILE // CONV_HALO
    return pl.pallas_call(
        functools.partial(_conv_kernel, d_conv=d_conv),
        out_shape=jax.ShapeDtypeStruct((b, s, d_conv), BF16),
        grid=(b, s // CONV_TILE),
        in_specs=[pl.BlockSpec((1, CONV_TILE, d2), lambda bi, i: (bi, i, 0)),
                  pl.BlockSpec((1, CONV_HALO, d2),
                               lambda bi, i: (bi, jnp.maximum(i * halo_per_tile - 1, 0), 0)),
                  _full(cw.shape), _full((1, d_conv)), _full((1, d_conv)), _full((1, d_conv))],
        out_specs=pl.BlockSpec((1, CONV_TILE, d_conv), lambda bi, i: (bi, i, 0)),
        scratch_shapes=[pltpu.VMEM((CONV_HALO + CONV_TILE + SUBLANES, d_conv), F32),
                        pltpu.VMEM((SUBLANES, CONV_SUB + SUBLANES, d_conv), F32)],
        compiler_params=_cparams(2),
        name="conv_module",
    )(u, u, cw, cb, lg, lb)


def _sb_kernel(q_ref, k_ref, v_ref, tri_ref, o_ref, qs_scr, acc_scr, carry_scr, *, n_pairs):
    i = pl.program_id(1)
    lane = lax.broadcasted_iota(jnp.int32, (1, LANES), 1)
    row = lax.broadcasted_iota(jnp.int32, (SB_BLOCK, SB_BLOCK), 0)
    col = lax.broadcasted_iota(jnp.int32, (SB_BLOCK, SB_BLOCK), 1)
    causal = jnp.concatenate([col < row] * SB_HEADS_PER_GROUP, axis=0)
    for p in range(n_pairs):
        q = q_ref[0, :, p * LANES:(p + 1) * LANES]
        qs_scr[p] = jnp.concatenate(
            [jnp.where(lane // HEAD_DIM == hh, q, jnp.zeros_like(q))
             for hh in range(SB_HEADS_PER_GROUP)], axis=0)

    def sweep(blocks, resume):
        pairs = range(n_pairs)
        starts = [pl.multiple_of(j * SB_BLOCK, SB_BLOCK) for j, _ in blocks]

        def rows_of(ref, n, p):
            return ref[0, pl.ds(starts[n], SB_BLOCK), p * LANES:(p + 1) * LANES]

        zs = [[lax.dot_general(qs_scr[p], rows_of(k_ref, n, p), (((1,), (1,)), ((), ())),
                               preferred_element_type=F32) for p in pairs]
              for n in range(len(blocks))]
        rts = []
        for n, (_, mask) in enumerate(blocks):
            rts.append([])
            for p in pairs:
                z = zs[n][p]
                zbits = lax.bitcast_convert_type(z, jnp.uint32)
                neg_abs = lax.bitcast_convert_type(zbits | jnp.uint32(SIGN_BIT), F32)
                sp = jnp.maximum(z, 0.0) + jnp.log(1.0 + jnp.exp(neg_abs))
                if mask is not None:
                    sp = jnp.where(mask, sp, 0.0)
                hi = lax.bitcast_convert_type(
                    lax.bitcast_convert_type(sp, jnp.uint32) & jnp.uint32(BF16_BITS), F32)
                hl = jnp.concatenate([hi.astype(BF16), (sp - hi).astype(BF16)], axis=1)
                rts[n].append(jnp.dot(hl, tri_ref[...], preferred_element_type=F32))
        carry = [carry_scr[p] if resume else None for p in pairs]
        acc = [acc_scr[p] if resume else None for p in pairs]
        for n, (_, mask) in enumerate(blocks):
            for p in pairs:
                within, total = rts[n][p][:, :SB_BLOCK], rts[n][p][:, SB_BLOCK:]
                w = jnp.exp(zs[n][p] - (within if carry[p] is None else within + carry[p]))
                if mask is not None:
                    w = jnp.where(mask, w, 0.0)
                carry[p] = total if carry[p] is None else carry[p] + total
                pv = jnp.dot(w.astype(BF16), rows_of(v_ref, n, p), preferred_element_type=F32)
                acc[p] = pv if acc[p] is None else acc[p] + pv
        lowest = None
        for p in pairs:
            carry_scr[p] = carry[p]
            acc_scr[p] = acc[p]
            lowest = carry[p] if lowest is None else jnp.minimum(lowest, carry[p])
        return jnp.min(lowest)

    def more(st):
        return jnp.logical_and(st[0] >= 0, st[1] < SB_DEAD_LOG)

    def body(st):
        return st[0] - 1, sweep([(st[0], None)], True)

    n_first = jnp.minimum(i + 1, SB_FIRST_BLOCKS)
    first_sweeps = [
        functools.partial(sweep, [(i, causal)] + [(i - n, None) for n in range(1, count)], False)
        for count in range(1, SB_FIRST_BLOCKS + 1)]
    lowest = lax.switch(n_first - 1, first_sweeps)
    lax.while_loop(more, body, (i - n_first, lowest))
    for p in range(n_pairs):
        acc = acc_scr[p]
        out = acc[0:SB_BLOCK]
        for hh in range(1, SB_HEADS_PER_GROUP):
            out = jnp.where(lane // HEAD_DIM == hh, acc[hh * SB_BLOCK:(hh + 1) * SB_BLOCK], out)
        o_ref[0, :, p * LANES:(p + 1) * LANES] = out.astype(BF16)


def _stick_breaking(q, k, v, tri):
    b, s, d_sb = q.shape
    n_pairs = d_sb // LANES
    rows = SB_HEADS_PER_GROUP * SB_BLOCK
    blk = pl.BlockSpec((1, SB_BLOCK, d_sb), lambda bi, i: (bi, i, 0))
    whole = pl.BlockSpec((1, s, d_sb), lambda bi, i: (bi, 0, 0))
    return pl.pallas_call(
        functools.partial(_sb_kernel, n_pairs=n_pairs),
        out_shape=jax.ShapeDtypeStruct((b, s, d_sb), BF16),
        grid=(b, s // SB_BLOCK),
        in_specs=[blk, whole, whole, _full(tri.shape)],
        out_specs=blk,
        scratch_shapes=[pltpu.VMEM((n_pairs, rows, LANES), BF16),
                        pltpu.VMEM((n_pairs, rows, LANES), F32),
                        pltpu.VMEM((n_pairs, rows, LANES), F32)],
        compiler_params=_cparams(2),
        name="stick_breaking",
    )(q, k, v, tri)


def _ret_kernel(qk_ref, vg_ref, dmat_ref, qdec_ref, kdec_ref, cdec_ref, bd_ref, avg_ref, gain_ref,
                o_ref, state_scr, *, d_ret, n_batch):
    n_heads = d_ret // HEAD_DIM
    batches = range(n_batch)
    nt = (((1,), (1,)), ((), ()))

    @pl.when(pl.program_id(0) == 0)
    def _():
        state_scr[...] = jnp.zeros_like(state_scr)

    lane = lax.broadcasted_iota(jnp.int32, (1, d_ret), 1)
    in_head = [lane // HEAD_DIM == h for h in range(n_heads)]
    q = [qk_ref[b, :, :d_ret] for b in batches]
    k = [qk_ref[b, :, d_ret:] for b in batches]
    v = [vg_ref[b, :, :d_ret] for b in batches]
    probs = []
    for b in batches:
        kb = k[b].astype(BF16)
        probs.append(jnp.concatenate(
            [(lax.dot_general(jnp.where(in_head[h], q[b], 0.0).astype(BF16), kb, nt,
                              preferred_element_type=F32) * dmat_ref[h]).astype(BF16)
             for h in range(n_heads)], axis=1))
    y = []
    for b in batches:
        vals = jnp.concatenate([jnp.where(in_head[h], v[b], 0.0).astype(BF16)
                                for h in range(n_heads)], axis=0)
        y.append(jnp.dot(probs[b], vals, preferred_element_type=F32))
    for b in batches:
        state = state_scr[b]
        y[b] = y[b] + jnp.dot((q[b] * qdec_ref[...]).astype(BF16), state.astype(BF16),
                              preferred_element_type=F32)
        kv = lax.dot_general((k[b] * kdec_ref[...]).astype(BF16), v[b].astype(BF16),
                             (((0,), (0,)), ((), ())), preferred_element_type=F32)
        state_scr[b] = cdec_ref[...] * state + bd_ref[...] * kv

    def head_mean(t):
        hi = t.astype(BF16)
        lo = (t - hi.astype(F32)).astype(BF16)
        return jnp.dot(jnp.concatenate([hi, lo], axis=1), avg_ref[...], preferred_element_type=F32)

    yc = [y[b] - head_mean(y[b]) for b in batches]
    var = [head_mean(yc[b] * yc[b]) for b in batches]
    for b in batches:
        g = vg_ref[b, :, d_ret:]
        o_ref[b] = (_silu(g) * (yc[b] * lax.rsqrt(var[b] + EPS) * gain_ref[...])).astype(BF16)


def _retention(qk, vg, tabs, gain):
    b, s, d2 = qk.shape
    d_ret = d2 // 2
    blk = pl.BlockSpec((b, RET_BLOCK, d2), lambda i: (0, i, 0))
    return pl.pallas_call(
        functools.partial(_ret_kernel, d_ret=d_ret, n_batch=b),
        out_shape=jax.ShapeDtypeStruct((b, s, d_ret), BF16),
        grid=(s // RET_BLOCK,),
        in_specs=[blk, blk] + [_full(t.shape) for t in tabs] + [_full((1, d_ret))],
        out_specs=pl.BlockSpec((b, RET_BLOCK, d_ret), lambda i: (0, i, 0)),
        scratch_shapes=[pltpu.VMEM((b, d_ret, d_ret), F32)],
        compiler_params=_cparams(1),
        name="retention",
    )(qk, vg, *tabs, gain)


def _rotary_tables(seq):
    half = HEAD_DIM // 2
    inv = 1.0 / (ROPE_BASE ** (jnp.arange(half, dtype=F32) / half))
    lane = jnp.arange(LANES)
    ang = jnp.arange(seq).astype(F32)[:, None] * inv[lane % half][None, :]
    sign = jnp.where((lane % HEAD_DIM) < half, -1.0, 1.0).astype(F32)
    return jnp.cos(ang), jnp.sin(ang) * sign[None, :]


def _retention_tables(n_heads):
    d_ret = n_heads * HEAD_DIM
    log_gamma = jnp.log1p(-jnp.exp2(-5.0 - jnp.arange(n_heads, dtype=F32)))
    idx = jnp.arange(RET_BLOCK, dtype=F32)
    t, s = idx[:, None], idx[None, :]
    same = (t // CHUNK) == (s // CHUNK)
    dist = jnp.where(same, jnp.abs(t - s), t - s)
    seen = same | ((s // CHUNK) < (t // CHUNK))
    dmat = jnp.where(seen[None], jnp.exp(log_gamma[:, None, None] * dist[None]), 0.0)
    lane_gamma = jnp.repeat(log_gamma, HEAD_DIM)
    qdec = jnp.exp(lane_gamma[None, :] * (idx + 1.0)[:, None])
    kdec = jnp.exp(lane_gamma[None, :] * (RET_BLOCK - 1.0 - idx)[:, None])
    head = jnp.arange(d_ret) // HEAD_DIM
    bd = (head[:, None] == head[None, :]).astype(F32)
    cdec = bd * jnp.exp(lane_gamma * RET_BLOCK)[:, None]
    avg = jnp.concatenate([bd, bd], axis=0).astype(BF16) * (1.0 / HEAD_DIM)
    return dmat, qdec, kdec, cdec, bd, avg.astype(BF16)


def _tri_table():
    r = np.arange(2 * SB_BLOCK)[:, None] % SB_BLOCK
    c = np.arange(2 * SB_BLOCK)[None, :]
    return jnp.asarray((c >= SB_BLOCK) | (r >= c), dtype=BF16)


def kernel(x, ffn1_norm, ffn1_w_in, ffn1_w_out, mix_norm, mix_w_in, conv_w, conv_b, conv_ln_g,
           conv_ln_b, ret_norm_g, mix_w_out, ffn2_norm, ffn2_w_in, ffn2_w_out, final_norm):
    b, s, d = x.shape
    depth = ffn1_norm.shape[0]
    d_conv = conv_w.shape[2]
    d_ret = ret_norm_g.shape[1]
    d_sb = d - d_conv - d_ret
    assert s % CONV_TILE == 0 and s % ROW_TILE == 0 and s % RET_BLOCK == 0 and s % SB_BLOCK == 0
    assert ffn1_w_out.shape[1] % FF_CHUNK == 0 and d % OUT_TILE == 0

    cos_tab, sin_tab = _rotary_tables(s)
    ret_tabs = _retention_tables(d_ret // HEAD_DIM)
    tri = _tri_table()
    scale = HEAD_DIM ** -0.5
    col = jnp.arange(mix_w_in.shape[2])
    q_sb_cols = (col >= 2 * d_conv) & (col < 2 * d_conv + d_sb)
    q_r_cols = (col >= 2 * d_conv + 3 * d_sb) & (col < 2 * d_conv + 3 * d_sb + d_ret)
    col_scale = jnp.where(q_sb_cols | q_r_cols, scale, 1.0).astype(F32)

    row = lambda g: g.reshape(1, -1)
    xf = x.reshape(b * s, d)
    for l in range(depth):
        win1, wout1 = ffn1_w_in[l].astype(BF16), ffn1_w_out[l].astype(BF16)
        win2, wout2 = ffn2_w_in[l].astype(BF16), ffn2_w_out[l].astype(BF16)
        w_mix = (mix_w_in[l] * col_scale[None, :]).astype(BF16)

        xf = _ffn(xf, row(ffn1_norm[l]), win1, wout1)
        uconv, q, k, v, qkr, vgr = _inproj(xf, row(mix_norm[l]), w_mix, cos_tab, sin_tab, s,
                                           2 * d_conv, d_sb, d_ret)
        seq3 = lambda t: t.reshape(b, s, t.shape[1])
        y_conv = _conv(seq3(uconv), conv_w[l], row(conv_b[l]), row(conv_ln_g[l]), row(conv_ln_b[l]))
        y_sb = _stick_breaking(seq3(q), seq3(k), seq3(v), tri)
        y_ret = _retention(seq3(qkr), seq3(vgr), ret_tabs, row(ret_norm_g[l]))
        flat = lambda t: t.reshape(b * s, t.shape[2])
        last = l == depth - 1
        xf = _ffn(xf, row(ffn2_norm[l]), win2, wout2,
                  mix=(flat(y_conv), flat(y_sb), flat(y_ret)), wmix=mix_w_out[l].astype(BF16),
                  final_g=row(final_norm) if last else None)
    return xf.reshape(b, s, d)
```

```python
import functools

import jax
import jax.numpy as jnp
import numpy as np
from jax import lax
from jax.experimental import pallas as pl
from jax.experimental.pallas import tpu as pltpu

F32 = jnp.float32
BF16 = jnp.bfloat16

EPS = 1e-6
ROPE_BASE = 10000.0
CHUNK = 64
CONV_WIDTH = 31
HEAD_DIM = 64
LANES = 128
SUBLANES = 8
MXU_DIM = 256

ROW_TILE = 512
FF_CHUNK = MXU_DIM
OUT_TILE = 512
CONV_HALO = 32
CONV_SUB = 128
SB_BLOCK = 128
SB_HEADS_PER_GROUP = LANES // HEAD_DIM
SB_FIRST_BLOCKS = 3
SB_DEAD_LOG = 110.0
SIGN_BIT = 0x80000000
BF16_BITS = 0xFFFF0000
RET_BLOCK = 256
VMEM_LIMIT = 56 * 1024 * 1024


def _cparams(n_axes):
    return pltpu.CompilerParams(dimension_semantics=("arbitrary",) * n_axes,
                                vmem_limit_bytes=VMEM_LIMIT)


def _rms(x, g):
    return (x * lax.rsqrt(jnp.mean(x * x, axis=-1, keepdims=True) + EPS)) * g


def _silu(x):
    return x * jax.nn.sigmoid(x)


def _full(shape):
    return pl.BlockSpec(shape, lambda *_: (0,) * len(shape))


def _resident(shape):
    return pl.BlockSpec(shape, lambda *_: (0,) * len(shape), pipeline_mode=pl.Buffered(1))


def _ffn_kernel(*refs, n_chunks, n_out, n_mix, final):
    refs = list(refs)
    x_ref = refs.pop(0)
    mix_refs = [refs.pop(0) for _ in range(n_mix)]
    wmix_ref = refs.pop(0) if n_mix else None
    g_ref, win_ref, wout_ref = refs.pop(0), refs.pop(0), refs.pop(0)
    fg_ref = refs.pop(0) if final else None
    o_ref, h_scr, a_scr = refs.pop(0), refs.pop(0), refs.pop(0)
    out_tiles = [slice(m * OUT_TILE, (m + 1) * OUT_TILE) for m in range(n_out)]
    d_ff = wout_ref.shape[0]
    if n_mix:
        for sl in out_tiles:
            y, lo = None, 0
            for y_ref in mix_refs:
                part = jnp.dot(y_ref[...], wmix_ref[lo:lo + y_ref.shape[1], sl],
                               preferred_element_type=F32)
                y = part if y is None else y + part
                lo += y_ref.shape[1]
            o_ref[:, sl] = x_ref[:, sl] + y
        x_ref = o_ref
    h_scr[...] = _rms(x_ref[...], g_ref[...]).astype(BF16)
    for j in range(n_chunks):
        sl = slice(j * FF_CHUNK, (j + 1) * FF_CHUNK)
        up_sl = slice(d_ff + j * FF_CHUNK, d_ff + (j + 1) * FF_CHUNK)
        gate = jnp.dot(h_scr[...], win_ref[:, sl], preferred_element_type=F32)
        up = jnp.dot(h_scr[...], win_ref[:, up_sl], preferred_element_type=F32)
        a_scr[:, sl] = (_silu(gate) * up).astype(BF16)
    for sl in out_tiles:
        y = jnp.dot(a_scr[...], wout_ref[:, sl], preferred_element_type=F32)
        o_ref[:, sl] = x_ref[:, sl] + 0.5 * y
    if final:
        o_ref[...] = _rms(o_ref[...], fg_ref[...])


def _ffn(x, g, win, wout, mix=(), wmix=None, final_g=None):
    n, d = x.shape
    d_ff = wout.shape[0]
    n_chunks = d_ff // FF_CHUNK
    final = final_g is not None
    row = lambda i: (i, 0)
    in_specs = [pl.BlockSpec((ROW_TILE, d), row)]
    in_specs += [pl.BlockSpec((ROW_TILE, y.shape[1]), row) for y in mix]
    args = [x, *mix]
    if mix:
        in_specs.append(_resident(wmix.shape))
        args.append(wmix)
    in_specs += [_full((1, d)), _resident(win.shape), _resident(wout.shape)]
    args += [g, win, wout]
    if final:
        in_specs.append(_full((1, d)))
        args.append(final_g)
    return pl.pallas_call(
        functools.partial(_ffn_kernel, n_chunks=n_chunks, n_out=d // OUT_TILE, n_mix=len(mix),
                          final=final),
        out_shape=jax.ShapeDtypeStruct((n, d), F32),
        grid=(n // ROW_TILE,),
        in_specs=in_specs,
        out_specs=pl.BlockSpec((ROW_TILE, d), row),
        scratch_shapes=[pltpu.VMEM((ROW_TILE, d), BF16), pltpu.VMEM((ROW_TILE, d_ff), BF16)],
        compiler_params=_cparams(1),
        name="ffn",
    )(*args)


def _rotary(t, cos, sin_signed, first_half):
    partner = jnp.where(first_half, pltpu.roll(t, LANES - HEAD_DIM // 2, 1),
                        pltpu.roll(t, HEAD_DIM // 2, 1))
    return t * cos + partner * sin_signed


def _inproj_kernel(x_ref, g_ref, w_ref, cos_ref, sin_ref, cw_ref, cb_ref, lg_ref, lb_ref,
                   yconv_ref, q_ref, k_ref, v_ref, qkr_ref, vgr_ref, h_scr, v_scr, p_scr,
                   *, d_conv, d_sb, d_ret, tiles_per_seq):
    h_scr[...] = _rms(x_ref[...], g_ref[...]).astype(BF16)

    def proj(lo, width):
        return jnp.dot(h_scr[...], w_ref[:, lo:lo + width], preferred_element_type=F32)

    starts_sequence = pl.program_id(0) % tiles_per_seq == 0

    @pl.when(starts_sequence)
    def _():
        v_scr[0:CONV_HALO, :] = jnp.zeros((CONV_HALO, d_conv), F32)

    @pl.when(jnp.logical_not(starts_sequence))
    def _():
        v_scr[0:CONV_HALO, :] = v_scr[ROW_TILE:ROW_TILE + CONV_HALO, :]

    u = proj(0, 2 * d_conv)
    v_scr[CONV_HALO:CONV_HALO + ROW_TILE, :] = u[:, :d_conv] * jax.nn.sigmoid(u[:, d_conv:])
    v_scr[CONV_HALO + ROW_TILE:, :] = jnp.zeros((SUBLANES, d_conv), F32)

    def conv_rows(r):
        off = CONV_HALO - (CONV_WIDTH - 1)
        for shift in range(SUBLANES):
            part = None
            for o in range(shift, off + CONV_WIDTH, SUBLANES):
                if o < off:
                    continue
                rows = v_scr[pl.ds(r * CONV_SUB + o - shift, CONV_SUB + SUBLANES), :]
                term = cw_ref[o - off:o - off + 1, :] * rows
                part = term if part is None else part + term
            p_scr[r % 2, shift] = part
        y = cb_ref[...] + p_scr[r % 2, 0, 0:CONV_SUB, :]
        for shift in range(1, SUBLANES):
            y = y + p_scr[r % 2, shift, pl.ds(shift, CONV_SUB), :]
        mu = jnp.mean(y, axis=-1, keepdims=True)
        yc = y - mu
        var = jnp.mean(yc * yc, axis=-1, keepdims=True)
        ln = yc * lax.rsqrt(var + EPS) * lg_ref[...] + lb_ref[...]
        yconv_ref[r * CONV_SUB:(r + 1) * CONV_SUB, :] = _silu(ln).astype(BF16)

    o = 2 * d_conv
    conv_rows(0)
    q_ref[...] = proj(o, d_sb).astype(BF16)
    conv_rows(1)
    k_ref[...] = proj(o + d_sb, d_sb).astype(BF16)
    conv_rows(2)
    v_ref[...] = proj(o + 2 * d_sb, d_sb).astype(BF16)
    conv_rows(3)
    o += 3 * d_sb
    qk = proj(o, 2 * d_ret)
    cos = cos_ref[...]
    sin = sin_ref[...]
    lane = lax.broadcasted_iota(jnp.int32, (1, LANES), 1)
    first_half = (lane % HEAD_DIM) < (HEAD_DIM // 2)
    for c in range(2 * d_ret // LANES):
        sl = slice(c * LANES, (c + 1) * LANES)
        qkr_ref[:, sl] = _rotary(qk[:, sl], cos, sin, first_half)
    vgr_ref[...] = proj(o + 2 * d_ret, 2 * d_ret)


def _inproj(x, g, w, cos_tab, sin_tab, cw, cb, lg, lb, seq, d_sb, d_ret):
    n, d = x.shape
    d_conv = cw.shape[1]
    assert ROW_TILE == 4 * CONV_SUB and CONV_HALO >= CONV_WIDTH - 1
    tiles_per_seq = seq // ROW_TILE
    row = lambda i: (i, 0)
    pos = lambda i: (i % tiles_per_seq, 0)
    outs = [jax.ShapeDtypeStruct((n, d_conv), BF16)] + [jax.ShapeDtypeStruct((n, d_sb), BF16)] * 3 \
        + [jax.ShapeDtypeStruct((n, 2 * d_ret), F32)] * 2
    return pl.pallas_call(
        functools.partial(_inproj_kernel, d_conv=d_conv, d_sb=d_sb, d_ret=d_ret,
                          tiles_per_seq=tiles_per_seq),
        out_shape=outs,
        grid=(n // ROW_TILE,),
        in_specs=[pl.BlockSpec((ROW_TILE, d), row), _full((1, d)), _resident(w.shape),
                  pl.BlockSpec((ROW_TILE, LANES), pos), pl.BlockSpec((ROW_TILE, LANES), pos),
                  _full(cw.shape), _full((1, d_conv)), _full((1, d_conv)), _full((1, d_conv))],
        out_specs=[pl.BlockSpec((ROW_TILE, s.shape[1]), row) for s in outs],
        scratch_shapes=[pltpu.VMEM((ROW_TILE, d), BF16),
                        pltpu.VMEM((CONV_HALO + ROW_TILE + SUBLANES, d_conv), F32),
                        pltpu.VMEM((2, SUBLANES, CONV_SUB + SUBLANES, d_conv), F32)],
        compiler_params=_cparams(1),
        name="mixer_inproj",
    )(x, g, w, cos_tab, sin_tab, cw, cb, lg, lb)


def _sb_kernel(q_ref, k_ref, v_ref, tri_ref, o_ref, qs_scr, acc_scr, carry_scr, *, n_pairs):
    i = pl.program_id(1)
    lane = lax.broadcasted_iota(jnp.int32, (1, LANES), 1)
    row = lax.broadcasted_iota(jnp.int32, (SB_BLOCK, SB_BLOCK), 0)
    col = lax.broadcasted_iota(jnp.int32, (SB_BLOCK, SB_BLOCK), 1)
    causal = jnp.concatenate([col < row] * SB_HEADS_PER_GROUP, axis=0)
    for p in range(n_pairs):
        q = q_ref[0, :, p * LANES:(p + 1) * LANES]
        qs_scr[p] = jnp.concatenate(
            [jnp.where(lane // HEAD_DIM == hh, q, jnp.zeros_like(q))
             for hh in range(SB_HEADS_PER_GROUP)], axis=0)

    def sweep(blocks, resume):
        pairs = range(n_pairs)
        starts = [pl.multiple_of(j * SB_BLOCK, SB_BLOCK) for j, _ in blocks]

        def rows_of(ref, n, p):
            return ref[0, pl.ds(starts[n], SB_BLOCK), p * LANES:(p + 1) * LANES]

        zs = [[lax.dot_general(qs_scr[p], rows_of(k_ref, n, p), (((1,), (1,)), ((), ())),
                               preferred_element_type=F32) for p in pairs]
              for n in range(len(blocks))]
        rts = []
        for n, (_, mask) in enumerate(blocks):
            rts.append([])
            for p in pairs:
                z = zs[n][p]
                zbits = lax.bitcast_convert_type(z, jnp.uint32)
                neg_abs = lax.bitcast_convert_type(zbits | jnp.uint32(SIGN_BIT), F32)
                sp = jnp.maximum(z, 0.0) + jnp.log(1.0 + jnp.exp(neg_abs))
                if mask is not None:
                    sp = jnp.where(mask, sp, 0.0)
                hi = lax.bitcast_convert_type(
                    lax.bitcast_convert_type(sp, jnp.uint32) & jnp.uint32(BF16_BITS), F32)
                hl = jnp.concatenate([hi.astype(BF16), (sp - hi).astype(BF16)], axis=1)
                rts[n].append(jnp.dot(hl, tri_ref[...], preferred_element_type=F32))
        carry = [carry_scr[p] if resume else None for p in pairs]
        acc = [acc_scr[p] if resume else None for p in pairs]
        for n, (_, mask) in enumerate(blocks):
            for p in pairs:
                within, total = rts[n][p][:, :SB_BLOCK], rts[n][p][:, SB_BLOCK:]
                w = jnp.exp(zs[n][p] - (within if carry[p] is None else within + carry[p]))
                if mask is not None:
                    w = jnp.where(mask, w, 0.0)
                carry[p] = total if carry[p] is None else carry[p] + total
                pv = jnp.dot(w.astype(BF16), rows_of(v_ref, n, p), preferred_element_type=F32)
                acc[p] = pv if acc[p] is None else acc[p] + pv
        lowest = None
        for p in pairs:
            carry_scr[p] = carry[p]
            acc_scr[p] = acc[p]
            lowest = carry[p] if lowest is None else jnp.minimum(lowest, carry[p])
        return jnp.min(lowest)

    def more(st):
        return jnp.logical_and(st[0] >= 0, st[1] < SB_DEAD_LOG)

    def body(st):
        return st[0] - 1, sweep([(st[0], None)], True)

    n_first = jnp.minimum(i + 1, SB_FIRST_BLOCKS)
    first_sweeps = [
        functools.partial(sweep, [(i, causal)] + [(i - n, None) for n in range(1, count)], False)
        for count in range(1, SB_FIRST_BLOCKS + 1)]
    lowest = lax.switch(n_first - 1, first_sweeps)
    lax.while_loop(more, body, (i - n_first, lowest))
    for p in range(n_pairs):
        acc = acc_scr[p]
        out = acc[0:SB_BLOCK]
        for hh in range(1, SB_HEADS_PER_GROUP):
            out = jnp.where(lane // HEAD_DIM == hh, acc[hh * SB_BLOCK:(hh + 1) * SB_BLOCK], out)
        o_ref[0, :, p * LANES:(p + 1) * LANES] = out.astype(BF16)


def _stick_breaking(q, k, v, tri):
    b, s, d_sb = q.shape
    n_pairs = d_sb // LANES
    rows = SB_HEADS_PER_GROUP * SB_BLOCK
    blk = pl.BlockSpec((1, SB_BLOCK, d_sb), lambda bi, i: (bi, i, 0))
    whole = pl.BlockSpec((1, s, d_sb), lambda bi, i: (bi, 0, 0))
    return pl.pallas_call(
        functools.partial(_sb_kernel, n_pairs=n_pairs),
        out_shape=jax.ShapeDtypeStruct((b, s, d_sb), BF16),
        grid=(b, s // SB_BLOCK),
        in_specs=[blk, whole, whole, _full(tri.shape)],
        out_specs=blk,
        scratch_shapes=[pltpu.VMEM((n_pairs, rows, LANES), BF16),
                        pltpu.VMEM((n_pairs, rows, LANES), F32),
                        pltpu.VMEM((n_pairs, rows, LANES), F32)],
        compiler_params=_cparams(2),
        name="stick_breaking",
    )(q, k, v, tri)


def _ret_kernel(qk_ref, vg_ref, dmat_ref, qdec_ref, kdec_ref, cdec_ref, bd_ref, avg_ref, gain_ref,
                o_ref, state_scr, *, d_ret, n_batch):
    n_heads = d_ret // HEAD_DIM
    batches = range(n_batch)
    nt = (((1,), (1,)), ((), ()))

    @pl.when(pl.program_id(0) == 0)
    def _():
        state_scr[...] = jnp.zeros_like(state_scr)

    lane = lax.broadcasted_iota(jnp.int32, (1, d_ret), 1)
    in_head = [lane // HEAD_DIM == h for h in range(n_heads)]
    q = [qk_ref[b, :, :d_ret] for b in batches]
    k = [qk_ref[b, :, d_ret:] for b in batches]
    v = [vg_ref[b, :, :d_ret] for b in batches]
    probs = []
    for b in batches:
        kb = k[b].astype(BF16)
        probs.append(jnp.concatenate(
            [(lax.dot_general(jnp.where(in_head[h], q[b], 0.0).astype(BF16), kb, nt,
                              preferred_element_type=F32) * dmat_ref[h]).astype(BF16)
             for h in range(n_heads)], axis=1))
    y = []
    for b in batches:
        vals = jnp.concatenate([jnp.where(in_head[h], v[b], 0.0).astype(BF16)
                                for h in range(n_heads)], axis=0)
        y.append(jnp.dot(probs[b], vals, preferred_element_type=F32))
    for b in batches:
        state = state_scr[b]
        y[b] = y[b] + jnp.dot((q[b] * qdec_ref[...]).astype(BF16), state.astype(BF16),
                              preferred_element_type=F32)
        kv = lax.dot_general((k[b] * kdec_ref[...]).astype(BF16), v[b].astype(BF16),
                             (((0,), (0,)), ((), ())), preferred_element_type=F32)
        state_scr[b] = cdec_ref[...] * state + bd_ref[...] * kv

    def head_mean(t):
        hi = t.astype(BF16)
        lo = (t - hi.astype(F32)).astype(BF16)
        return jnp.dot(jnp.concatenate([hi, lo], axis=1), avg_ref[...], preferred_element_type=F32)

    yc = [y[b] - head_mean(y[b]) for b in batches]
    var = [head_mean(yc[b] * yc[b]) for b in batches]
    for b in batches:
        g = vg_ref[b, :, d_ret:]
        o_ref[b] = (_silu(g) * (yc[b] * lax.rsqrt(var[b] + EPS) * gain_ref[...])).astype(BF16)


def _retention(qk, vg, tabs, gain):
    b, s, d2 = qk.shape
    d_ret = d2 // 2
    blk = pl.BlockSpec((b, RET_BLOCK, d2), lambda i: (0, i, 0))
    return pl.pallas_call(
        functools.partial(_ret_kernel, d_ret=d_ret, n_batch=b),
        out_shape=jax.ShapeDtypeStruct((b, s, d_ret), BF16),
        grid=(s // RET_BLOCK,),
        in_specs=[blk, blk] + [_full(t.shape) for t in tabs] + [_full((1, d_ret))],
        out_specs=pl.BlockSpec((b, RET_BLOCK, d_ret), lambda i: (0, i, 0)),
        scratch_shapes=[pltpu.VMEM((b, d_ret, d_ret), F32)],
        compiler_params=_cparams(1),
        name="retention",
    )(qk, vg, *tabs, gain)


def _rotary_tables(seq):
    half = HEAD_DIM // 2
    inv = 1.0 / (ROPE_BASE ** (jnp.arange(half, dtype=F32) / half))
    lane = jnp.arange(LANES)
    ang = jnp.arange(seq).astype(F32)[:, None] * inv[lane % half][None, :]
    sign = jnp.where((lane % HEAD_DIM) < half, -1.0, 1.0).astype(F32)
    return jnp.cos(ang), jnp.sin(ang) * sign[None, :]


def _retention_tables(n_heads):
    d_ret = n_heads * HEAD_DIM
    log_gamma = jnp.log1p(-jnp.exp2(-5.0 - jnp.arange(n_heads, dtype=F32)))
    idx = jnp.arange(RET_BLOCK, dtype=F32)
    t, s = idx[:, None], idx[None, :]
    same = (t // CHUNK) == (s // CHUNK)
    dist = jnp.where(same, jnp.abs(t - s), t - s)
    seen = same | ((s // CHUNK) < (t // CHUNK))
    dmat = jnp.where(seen[None], jnp.exp(log_gamma[:, None, None] * dist[None]), 0.0)
    lane_gamma = jnp.repeat(log_gamma, HEAD_DIM)
    qdec = jnp.exp(lane_gamma[None, :] * (idx + 1.0)[:, None])
    kdec = jnp.exp(lane_gamma[None, :] * (RET_BLOCK - 1.0 - idx)[:, None])
    head = jnp.arange(d_ret) // HEAD_DIM
    bd = (head[:, None] == head[None, :]).astype(F32)
    cdec = bd * jnp.exp(lane_gamma * RET_BLOCK)[:, None]
    avg = jnp.concatenate([bd, bd], axis=0).astype(BF16) * (1.0 / HEAD_DIM)
    return dmat, qdec, kdec, cdec, bd, avg.astype(BF16)


def _tri_table():
    r = np.arange(2 * SB_BLOCK)[:, None] % SB_BLOCK
    c = np.arange(2 * SB_BLOCK)[None, :]
    return jnp.asarray((c >= SB_BLOCK) | (r >= c), dtype=BF16)


def kernel(x, ffn1_norm, ffn1_w_in, ffn1_w_out, mix_norm, mix_w_in, conv_w, conv_b, conv_ln_g,
           conv_ln_b, ret_norm_g, mix_w_out, ffn2_norm, ffn2_w_in, ffn2_w_out, final_norm):
    b, s, d = x.shape
    depth = ffn1_norm.shape[0]
    d_conv = conv_w.shape[2]
    d_ret = ret_norm_g.shape[1]
    d_sb = d - d_conv - d_ret
    assert s % ROW_TILE == 0 and s % RET_BLOCK == 0 and s % SB_BLOCK == 0
    assert ffn1_w_out.shape[1] % FF_CHUNK == 0 and d % OUT_TILE == 0

    cos_tab, sin_tab = _rotary_tables(s)
    ret_tabs = _retention_tables(d_ret // HEAD_DIM)
    tri = _tri_table()
    scale = HEAD_DIM ** -0.5
    col = jnp.arange(mix_w_in.shape[2])
    q_sb_cols = (col >= 2 * d_conv) & (col < 2 * d_conv + d_sb)
    q_r_cols = (col >= 2 * d_conv + 3 * d_sb) & (col < 2 * d_conv + 3 * d_sb + d_ret)
    col_scale = jnp.where(q_sb_cols | q_r_cols, scale, 1.0).astype(F32)

    row = lambda g: g.reshape(1, -1)
    xf = x.reshape(b * s, d)
    for l in range(depth):
        win1, wout1 = ffn1_w_in[l].astype(BF16), ffn1_w_out[l].astype(BF16)
        win2, wout2 = ffn2_w_in[l].astype(BF16), ffn2_w_out[l].astype(BF16)
        w_mix = (mix_w_in[l] * col_scale[None, :]).astype(BF16)

        xf = _ffn(xf, row(ffn1_norm[l]), win1, wout1)
        y_conv, q, k, v, qkr, vgr = _inproj(
            xf, row(mix_norm[l]), w_mix, cos_tab, sin_tab, conv_w[l], row(conv_b[l]),
            row(conv_ln_g[l]), row(conv_ln_b[l]), s, d_sb, d_ret)
        seq3 = lambda t: t.reshape(b, s, t.shape[1])
        y_sb = _stick_breaking(seq3(q), seq3(k), seq3(v), tri)
        y_ret = _retention(seq3(qkr), seq3(vgr), ret_tabs, row(ret_norm_g[l]))
        flat = lambda t: t.reshape(b * s, t.shape[2])
        last = l == depth - 1
        xf = _ffn(xf, row(ffn2_norm[l]), win2, wout2,
                  mix=(y_conv, flat(y_sb), flat(y_ret)), wmix=mix_w_out[l].astype(BF16),
                  final_g=row(final_norm) if last else None)
    return xf.reshape(b, s, d)
```

```python
import functools

import jax
import jax.numpy as jnp
import numpy as np
from jax import lax
from jax.experimental import pallas as pl
from jax.experimental.pallas import tpu as pltpu

F32 = jnp.float32
BF16 = jnp.bfloat16

EPS = 1e-6
ROPE_BASE = 10000.0
CHUNK = 64
CONV_WIDTH = 31
HEAD_DIM = 64
LANES = 128
SUBLANES = 8
MXU_DIM = 256

ROW_TILE = 512
FF_CHUNK = MXU_DIM
OUT_TILE = 512
CONV_HALO = 32
CONV_SUB = 128
SB_BLOCK = 128
SB_TILES = 2
SB_TILE = SB_BLOCK // SB_TILES
SB_HEADS_PER_GROUP = LANES // HEAD_DIM
SB_FIRST_BLOCKS = 2
SB_DEAD_LOG = 110.0
SIGN_BIT = 0x80000000
BF16_BITS = 0xFFFF0000
RET_BLOCK = 256
VMEM_LIMIT = 56 * 1024 * 1024


def _cparams(n_axes):
    return pltpu.CompilerParams(dimension_semantics=("arbitrary",) * n_axes,
                                vmem_limit_bytes=VMEM_LIMIT)


def _rms(x, g):
    return (x * lax.rsqrt(jnp.mean(x * x, axis=-1, keepdims=True) + EPS)) * g


def _silu(x):
    return x * jax.nn.sigmoid(x)


def _full(shape):
    return pl.BlockSpec(shape, lambda *_: (0,) * len(shape))


def _resident(shape):
    return pl.BlockSpec(shape, lambda *_: (0,) * len(shape), pipeline_mode=pl.Buffered(1))


def _ffn_kernel(*refs, n_chunks, n_out, n_mix, final):
    refs = list(refs)
    x_ref = refs.pop(0)
    mix_refs = [refs.pop(0) for _ in range(n_mix)]
    wmix_ref = refs.pop(0) if n_mix else None
    g_ref, win_ref, wout_ref = refs.pop(0), refs.pop(0), refs.pop(0)
    fg_ref = refs.pop(0) if final else None
    o_ref, h_scr, a_scr = refs.pop(0), refs.pop(0), refs.pop(0)
    out_tiles = [slice(m * OUT_TILE, (m + 1) * OUT_TILE) for m in range(n_out)]
    d_ff = wout_ref.shape[0]
    if n_mix:
        for sl in out_tiles:
            y, lo = None, 0
            for y_ref in mix_refs:
                part = jnp.dot(y_ref[...], wmix_ref[lo:lo + y_ref.shape[1], sl],
                               preferred_element_type=F32)
                y = part if y is None else y + part
                lo += y_ref.shape[1]
            o_ref[:, sl] = x_ref[:, sl] + y
        x_ref = o_ref
    h_scr[...] = _rms(x_ref[...], g_ref[...]).astype(BF16)
    for j in range(n_chunks):
        sl = slice(j * FF_CHUNK, (j + 1) * FF_CHUNK)
        up_sl = slice(d_ff + j * FF_CHUNK, d_ff + (j + 1) * FF_CHUNK)
        gate = jnp.dot(h_scr[...], win_ref[:, sl], preferred_element_type=F32)
        up = jnp.dot(h_scr[...], win_ref[:, up_sl], preferred_element_type=F32)
        a_scr[:, sl] = (_silu(gate) * up).astype(BF16)
    for sl in out_tiles:
        y = jnp.dot(a_scr[...], wout_ref[:, sl], preferred_element_type=F32)
        o_ref[:, sl] = x_ref[:, sl] + 0.5 * y
    if final:
        o_ref[...] = _rms(o_ref[...], fg_ref[...])


def _ffn(x, g, win, wout, mix=(), wmix=None, final_g=None):
    n, d = x.shape
    d_ff = wout.shape[0]
    n_chunks = d_ff // FF_CHUNK
    final = final_g is not None
    row = lambda i: (i, 0)
    in_specs = [pl.BlockSpec((ROW_TILE, d), row)]
    in_specs += [pl.BlockSpec((ROW_TILE, y.shape[1]), row) for y in mix]
    args = [x, *mix]
    if mix:
        in_specs.append(_resident(wmix.shape))
        args.append(wmix)
    in_specs += [_full((1, d)), _resident(win.shape), _resident(wout.shape)]
    args += [g, win, wout]
    if final:
        in_specs.append(_full((1, d)))
        args.append(final_g)
    return pl.pallas_call(
        functools.partial(_ffn_kernel, n_chunks=n_chunks, n_out=d // OUT_TILE, n_mix=len(mix),
                          final=final),
        out_shape=jax.ShapeDtypeStruct((n, d), F32),
        grid=(n // ROW_TILE,),
        in_specs=in_specs,
        out_specs=pl.BlockSpec((ROW_TILE, d), row),
        scratch_shapes=[pltpu.VMEM((ROW_TILE, d), BF16), pltpu.VMEM((ROW_TILE, d_ff), BF16)],
        compiler_params=_cparams(1),
        name="ffn",
    )(*args)


def _rotary(t, cos, sin_signed, first_half):
    partner = jnp.where(first_half, pltpu.roll(t, LANES - HEAD_DIM // 2, 1),
                        pltpu.roll(t, HEAD_DIM // 2, 1))
    return t * cos + partner * sin_signed


def _inproj_kernel(x_ref, g_ref, w_ref, cos_ref, sin_ref, cw_ref, cb_ref, lg_ref, lb_ref,
                   yconv_ref, q_ref, k_ref, v_ref, qkr_ref, vgr_ref, h_scr, v_scr, p_scr,
                   *, d_conv, d_sb, d_ret, tiles_per_seq):
    h_scr[...] = _rms(x_ref[...], g_ref[...]).astype(BF16)

    def proj(lo, width):
        return jnp.dot(h_scr[...], w_ref[:, lo:lo + width], preferred_element_type=F32)

    starts_sequence = pl.program_id(0) % tiles_per_seq == 0

    @pl.when(starts_sequence)
    def _():
        v_scr[0:CONV_HALO, :] = jnp.zeros((CONV_HALO, d_conv), F32)

    @pl.when(jnp.logical_not(starts_sequence))
    def _():
        v_scr[0:CONV_HALO, :] = v_scr[ROW_TILE:ROW_TILE + CONV_HALO, :]

    u = proj(0, 2 * d_conv)
    v_scr[CONV_HALO:CONV_HALO + ROW_TILE, :] = u[:, :d_conv] * jax.nn.sigmoid(u[:, d_conv:])
    v_scr[CONV_HALO + ROW_TILE:, :] = jnp.zeros((SUBLANES, d_conv), F32)

    def conv_rows(r):
        off = CONV_HALO - (CONV_WIDTH - 1)
        for shift in range(SUBLANES):
            part = None
            for o in range(shift, off + CONV_WIDTH, SUBLANES):
                if o < off:
                    continue
                rows = v_scr[pl.ds(r * CONV_SUB + o - shift, CONV_SUB + SUBLANES), :]
                term = cw_ref[o - off:o - off + 1, :] * rows
                part = term if part is None else part + term
            p_scr[r % 2, shift] = part
        y = cb_ref[...] + p_scr[r % 2, 0, 0:CONV_SUB, :]
        for shift in range(1, SUBLANES):
            y = y + p_scr[r % 2, shift, pl.ds(shift, CONV_SUB), :]
        mu = jnp.mean(y, axis=-1, keepdims=True)
        yc = y - mu
        var = jnp.mean(yc * yc, axis=-1, keepdims=True)
        ln = yc * lax.rsqrt(var + EPS) * lg_ref[...] + lb_ref[...]
        yconv_ref[r * CONV_SUB:(r + 1) * CONV_SUB, :] = _silu(ln).astype(BF16)

    o = 2 * d_conv
    conv_rows(0)
    q_ref[...] = proj(o, d_sb).astype(BF16)
    conv_rows(1)
    k_ref[...] = proj(o + d_sb, d_sb).astype(BF16)
    conv_rows(2)
    v_ref[...] = proj(o + 2 * d_sb, d_sb).astype(BF16)
    conv_rows(3)
    o += 3 * d_sb
    qk = proj(o, 2 * d_ret)
    cos = cos_ref[...]
    sin = sin_ref[...]
    lane = lax.broadcasted_iota(jnp.int32, (1, LANES), 1)
    first_half = (lane % HEAD_DIM) < (HEAD_DIM // 2)
    for c in range(2 * d_ret // LANES):
        sl = slice(c * LANES, (c + 1) * LANES)
        qkr_ref[:, sl] = _rotary(qk[:, sl], cos, sin, first_half)
    vgr_ref[...] = proj(o + 2 * d_ret, 2 * d_ret)


def _inproj(x, g, w, cos_tab, sin_tab, cw, cb, lg, lb, seq, d_sb, d_ret):
    n, d = x.shape
    d_conv = cw.shape[1]
    assert ROW_TILE == 4 * CONV_SUB and CONV_HALO >= CONV_WIDTH - 1
    tiles_per_seq = seq // ROW_TILE
    row = lambda i: (i, 0)
    pos = lambda i: (i % tiles_per_seq, 0)
    outs = [jax.ShapeDtypeStruct((n, d_conv), BF16)] + [jax.ShapeDtypeStruct((n, d_sb), BF16)] * 3 \
        + [jax.ShapeDtypeStruct((n, 2 * d_ret), F32)] * 2
    return pl.pallas_call(
        functools.partial(_inproj_kernel, d_conv=d_conv, d_sb=d_sb, d_ret=d_ret,
                          tiles_per_seq=tiles_per_seq),
        out_shape=outs,
        grid=(n // ROW_TILE,),
        in_specs=[pl.BlockSpec((ROW_TILE, d), row), _full((1, d)), _resident(w.shape),
                  pl.BlockSpec((ROW_TILE, LANES), pos), pl.BlockSpec((ROW_TILE, LANES), pos),
                  _full(cw.shape), _full((1, d_conv)), _full((1, d_conv)), _full((1, d_conv))],
        out_specs=[pl.BlockSpec((ROW_TILE, s.shape[1]), row) for s in outs],
        scratch_shapes=[pltpu.VMEM((ROW_TILE, d), BF16),
                        pltpu.VMEM((CONV_HALO + ROW_TILE + SUBLANES, d_conv), F32),
                        pltpu.VMEM((2, SUBLANES, CONV_SUB + SUBLANES, d_conv), F32)],
        compiler_params=_cparams(1),
        name="mixer_inproj",
    )(x, g, w, cos_tab, sin_tab, cw, cb, lg, lb)


def _sb_kernel(q_ref, k_ref, v_ref, tri_ref, o_ref, qs_scr, acc_scr, carry_scr, *, n_pairs):
    q0 = pl.program_id(1) * SB_BLOCK
    tiles, pairs = range(SB_TILES), range(n_pairs)
    rows = SB_HEADS_PER_GROUP * SB_TILE
    lane = lax.broadcasted_iota(jnp.int32, (1, LANES), 1)
    col = lax.broadcasted_iota(jnp.int32, (rows, SB_BLOCK), 1)
    row = lax.broadcasted_iota(jnp.int32, (rows, SB_BLOCK), 0) % SB_TILE
    for p in pairs:
        q = q_ref[0, :, p * LANES:(p + 1) * LANES]
        for t in tiles:
            qt = q[t * SB_TILE:(t + 1) * SB_TILE]
            qs_scr[t, p] = jnp.concatenate(
                [jnp.where(lane // HEAD_DIM == hh, qt, jnp.zeros_like(qt))
                 for hh in range(SB_HEADS_PER_GROUP)], axis=0)

    def sweep(blocks, resume):
        depth = range(len(blocks[0]))

        def rows_of(ref, t, n, p):
            return ref[0, pl.ds(blocks[t][n][0], SB_BLOCK), p * LANES:(p + 1) * LANES]

        zs = [[[lax.dot_general(qs_scr[t, p], rows_of(k_ref, t, n, p), (((1,), (1,)), ((), ())),
                                preferred_element_type=F32) for p in pairs]
               for n in depth] for t in tiles]
        hls = []
        for t in tiles:
            for n in depth:
                for p in pairs:
                    z = zs[t][n][p]
                    zbits = lax.bitcast_convert_type(z, jnp.uint32)
                    neg_abs = lax.bitcast_convert_type(zbits | jnp.uint32(SIGN_BIT), F32)
                    sp = jnp.maximum(z, 0.0) + jnp.log(1.0 + jnp.exp(neg_abs))
                    if blocks[t][n][1] is not None:
                        sp = jnp.where(blocks[t][n][1], sp, 0.0)
                    hi = lax.bitcast_convert_type(
                        lax.bitcast_convert_type(sp, jnp.uint32) & jnp.uint32(BF16_BITS), F32)
                    hls.append(jnp.concatenate([hi.astype(BF16), (sp - hi).astype(BF16)], axis=1))
        rt = jnp.dot(jnp.concatenate(hls, axis=0), tri_ref[...], preferred_element_type=F32)
        lowest = None
        for t in tiles:
            for p in pairs:
                carry = carry_scr[t, p] if resume else None
                acc = acc_scr[t, p] if resume else None
                for n in depth:
                    unit = ((t * len(depth) + n) * n_pairs + p) * rows
                    within = rt[unit:unit + rows, :SB_BLOCK]
                    total = rt[unit:unit + rows, SB_BLOCK:]
                    w = jnp.exp(zs[t][n][p] - (within if carry is None else within + carry))
                    if blocks[t][n][1] is not None:
                        w = jnp.where(blocks[t][n][1], w, 0.0)
                    carry = total if carry is None else carry + total
                    pv = jnp.dot(w.astype(BF16), rows_of(v_ref, t, n, p),
                                 preferred_element_type=F32)
                    acc = pv if acc is None else acc + pv
                carry_scr[t, p] = carry
                acc_scr[t, p] = acc
                lowest = carry if lowest is None else jnp.minimum(lowest, carry)
        return jnp.min(lowest)

    def aligned(start):
        return pl.multiple_of(start, SB_TILE)

    def first_fast():
        near_mask = col < row + (SB_BLOCK - SB_TILE)
        return sweep([[(aligned(q0 + (t + 1) * SB_TILE - (n + 1) * SB_BLOCK),
                        near_mask if n == 0 else None) for n in range(SB_FIRST_BLOCKS)]
                      for t in tiles], False)

    def first_clamped():
        blocks = []
        for t in tiles:
            limit = q0 + t * SB_TILE + row
            blocks.append([])
            for n in range(SB_FIRST_BLOCKS):
                start = aligned(jnp.maximum(q0 + (t + 1) * SB_TILE - (n + 1) * SB_BLOCK, 0))
                blocks[t].append((start, start + col < limit))
                limit = start
        return sweep(blocks, False)

    def more(st):
        reach = q0 + SB_BLOCK - st[0] * SB_BLOCK
        return jnp.logical_and(reach > 0, st[1] < SB_DEAD_LOG)

    def body(st):
        blocks = []
        for t in tiles:
            limit = jnp.maximum(q0 + (t + 1) * SB_TILE - st[0] * SB_BLOCK, 0)
            start = aligned(jnp.maximum(limit - SB_BLOCK, 0))
            blocks.append([(start, start + col < limit)])
        return st[0] + 1, sweep(blocks, True)

    fits = q0 + SB_TILE - SB_FIRST_BLOCKS * SB_BLOCK >= 0
    lowest = lax.cond(fits, first_fast, first_clamped)
    lax.while_loop(more, body, (SB_FIRST_BLOCKS, lowest))
    for p in pairs:
        for t in tiles:
            acc = acc_scr[t, p]
            out = acc[0:SB_TILE]
            for hh in range(1, SB_HEADS_PER_GROUP):
                out = jnp.where(lane // HEAD_DIM == hh, acc[hh * SB_TILE:(hh + 1) * SB_TILE], out)
            o_ref[0, t * SB_TILE:(t + 1) * SB_TILE, p * LANES:(p + 1) * LANES] = out.astype(BF16)


def _stick_breaking(q, k, v, tri):
    b, s, d_sb = q.shape
    n_pairs = d_sb // LANES
    rows = SB_HEADS_PER_GROUP * SB_TILE
    blk = pl.BlockSpec((1, SB_BLOCK, d_sb), lambda bi, i: (bi, i, 0))
    whole = pl.BlockSpec((1, s, d_sb), lambda bi, i: (bi, 0, 0))
    return pl.pallas_call(
        functools.partial(_sb_kernel, n_pairs=n_pairs),
        out_shape=jax.ShapeDtypeStruct((b, s, d_sb), BF16),
        grid=(b, s // SB_BLOCK),
        in_specs=[blk, whole, whole, _full(tri.shape)],
        out_specs=blk,
        scratch_shapes=[pltpu.VMEM((SB_TILES, n_pairs, rows, LANES), BF16),
                        pltpu.VMEM((SB_TILES, n_pairs, rows, LANES), F32),
                        pltpu.VMEM((SB_TILES, n_pairs, rows, LANES), F32)],
        compiler_params=_cparams(2),
        name="stick_breaking",
    )(q, k, v, tri)


def _ret_kernel(qk_ref, vg_ref, dmat_ref, qdec_ref, kdec_ref, cdec_ref, bd_ref, avg_ref, gain_ref,
                o_ref, state_scr, *, d_ret, n_batch):
    n_heads = d_ret // HEAD_DIM
    batches = range(n_batch)
    nt = (((1,), (1,)), ((), ()))

    @pl.when(pl.program_id(0) == 0)
    def _():
        state_scr[...] = jnp.zeros_like(state_scr)

    lane = lax.broadcasted_iota(jnp.int32, (1, d_ret), 1)
    in_head = [lane // HEAD_DIM == h for h in range(n_heads)]
    q = [qk_ref[b, :, :d_ret] for b in batches]
    k = [qk_ref[b, :, d_ret:] for b in batches]
    v = [vg_ref[b, :, :d_ret] for b in batches]
    probs = []
    for b in batches:
        kb = k[b].astype(BF16)
        probs.append(jnp.concatenate(
            [(lax.dot_general(jnp.where(in_head[h], q[b], 0.0).astype(BF16), kb, nt,
                              preferred_element_type=F32) * dmat_ref[h]).astype(BF16)
             for h in range(n_heads)], axis=1))
    y = []
    for b in batches:
        vals = jnp.concatenate([jnp.where(in_head[h], v[b], 0.0).astype(BF16)
                                for h in range(n_heads)], axis=0)
        y.append(jnp.dot(probs[b], vals, preferred_element_type=F32))
    for b in batches:
        state = state_scr[b]
        y[b] = y[b] + jnp.dot((q[b] * qdec_ref[...]).astype(BF16), state.astype(BF16),
                              preferred_element_type=F32)
        kv = lax.dot_general((k[b] * kdec_ref[...]).astype(BF16), v[b].astype(BF16),
                             (((0,), (0,)), ((), ())), preferred_element_type=F32)
        state_scr[b] = cdec_ref[...] * state + bd_ref[...] * kv

    def head_mean(t):
        hi = t.astype(BF16)
        lo = (t - hi.astype(F32)).astype(BF16)
        return jnp.dot(jnp.concatenate([hi, lo], axis=1), avg_ref[...], preferred_element_type=F32)

    yc = [y[b] - head_mean(y[b]) for b in batches]
    var = [head_mean(yc[b] * yc[b]) for b in batches]
    for b in batches:
        g = vg_ref[b, :, d_ret:]
        o_ref[b] = (_silu(g) * (yc[b] * lax.rsqrt(var[b] + EPS) * gain_ref[...])).astype(BF16)


def _retention(qk, vg, tabs, gain):
    b, s, d2 = qk.shape
    d_ret = d2 // 2
    blk = pl.BlockSpec((b, RET_BLOCK, d2), lambda i: (0, i, 0))
    return pl.pallas_call(
        functools.partial(_ret_kernel, d_ret=d_ret, n_batch=b),
        out_shape=jax.ShapeDtypeStruct((b, s, d_ret), BF16),
        grid=(s // RET_BLOCK,),
        in_specs=[blk, blk] + [_full(t.shape) for t in tabs] + [_full((1, d_ret))],
        out_specs=pl.BlockSpec((b, RET_BLOCK, d_ret), lambda i: (0, i, 0)),
        scratch_shapes=[pltpu.VMEM((b, d_ret, d_ret), F32)],
        compiler_params=_cparams(1),
        name="retention",
    )(qk, vg, *tabs, gain)


def _rotary_tables(seq):
    half = HEAD_DIM // 2
    inv = 1.0 / (ROPE_BASE ** (jnp.arange(half, dtype=F32) / half))
    lane = jnp.arange(LANES)
    ang = jnp.arange(seq).astype(F32)[:, None] * inv[lane % half][None, :]
    sign = jnp.where((lane % HEAD_DIM) < half, -1.0, 1.0).astype(F32)
    return jnp.cos(ang), jnp.sin(ang) * sign[None, :]


def _retention_tables(n_heads):
    d_ret = n_heads * HEAD_DIM
    log_gamma = jnp.log1p(-jnp.exp2(-5.0 - jnp.arange(n_heads, dtype=F32)))
    idx = jnp.arange(RET_BLOCK, dtype=F32)
    t, s = idx[:, None], idx[None, :]
    same = (t // CHUNK) == (s // CHUNK)
    dist = jnp.where(same, jnp.abs(t - s), t - s)
    seen = same | ((s // CHUNK) < (t // CHUNK))
    dmat = jnp.where(seen[None], jnp.exp(log_gamma[:, None, None] * dist[None]), 0.0)
    lane_gamma = jnp.repeat(log_gamma, HEAD_DIM)
    qdec = jnp.exp(lane_gamma[None, :] * (idx + 1.0)[:, None])
    kdec = jnp.exp(lane_gamma[None, :] * (RET_BLOCK - 1.0 - idx)[:, None])
    head = jnp.arange(d_ret) // HEAD_DIM
    bd = (head[:, None] == head[None, :]).astype(F32)
    cdec = bd * jnp.exp(lane_gamma * RET_BLOCK)[:, None]
    avg = jnp.concatenate([bd, bd], axis=0).astype(BF16) * (1.0 / HEAD_DIM)
    return dmat, qdec, kdec, cdec, bd, avg.astype(BF16)


def _tri_table():
    r = np.arange(2 * SB_BLOCK)[:, None] % SB_BLOCK
    c = np.arange(2 * SB_BLOCK)[None, :]
    return jnp.asarray((c >= SB_BLOCK) | (r >= c), dtype=BF16)


def kernel(x, ffn1_norm, ffn1_w_in, ffn1_w_out, mix_norm, mix_w_in, conv_w, conv_b, conv_ln_g,
           conv_ln_b, ret_norm_g, mix_w_out, ffn2_norm, ffn2_w_in, ffn2_w_out, final_norm):
    b, s, d = x.shape
    depth = ffn1_norm.shape[0]
    d_conv = conv_w.shape[2]
    d_ret = ret_norm_g.shape[1]
    d_sb = d - d_conv - d_ret
    assert s % ROW_TILE == 0 and s % RET_BLOCK == 0 and s % SB_BLOCK == 0
    assert ffn1_w_out.shape[1] % FF_CHUNK == 0 and d % OUT_TILE == 0

    cos_tab, sin_tab = _rotary_tables(s)
    ret_tabs = _retention_tables(d_ret // HEAD_DIM)
    tri = _tri_table()
    scale = HEAD_DIM ** -0.5
    col = jnp.arange(mix_w_in.shape[2])
    q_sb_cols = (col >= 2 * d_conv) & (col < 2 * d_conv + d_sb)
    q_r_cols = (col >= 2 * d_conv + 3 * d_sb) & (col < 2 * d_conv + 3 * d_sb + d_ret)
    col_scale = jnp.where(q_sb_cols | q_r_cols, scale, 1.0).astype(F32)

    row = lambda g: g.reshape(1, -1)
    xf = x.reshape(b * s, d)
    for l in range(depth):
        win1, wout1 = ffn1_w_in[l].astype(BF16), ffn1_w_out[l].astype(BF16)
        win2, wout2 = ffn2_w_in[l].astype(BF16), ffn2_w_out[l].astype(BF16)
        w_mix = (mix_w_in[l] * col_scale[None, :]).astype(BF16)

        xf = _ffn(xf, row(ffn1_norm[l]), win1, wout1)
        y_conv, q, k, v, qkr, vgr = _inproj(
            xf, row(mix_norm[l]), w_mix, cos_tab, sin_tab, conv_w[l], row(conv_b[l]),
            row(conv_ln_g[l]), row(conv_ln_b[l]), s, d_sb, d_ret)
        seq3 = lambda t: t.reshape(b, s, t.shape[1])
        y_sb = _stick_breaking(seq3(q), seq3(k), seq3(v), tri)
        y_ret = _retention(seq3(qkr), seq3(vgr), ret_tabs, row(ret_norm_g[l]))
        flat = lambda t: t.reshape(b * s, t.shape[2])
        last = l == depth - 1
        xf = _ffn(xf, row(ffn2_norm[l]), win2, wout2,
                  mix=(y_conv, flat(y_sb), flat(y_ret)), wmix=mix_w_out[l].astype(BF16),
                  final_g=row(final_norm) if last else None)
    return xf.reshape(b, s, d)
```

```python
import functools

import jax
import jax.numpy as jnp
import numpy as np
from jax import lax
from jax.experimental import pallas as pl
from jax.experimental.pallas import tpu as pltpu

F32 = jnp.float32
BF16 = jnp.bfloat16

EPS = 1e-6
ROPE_BASE = 10000.0
CHUNK = 64
CONV_WIDTH = 31
HEAD_DIM = 64
LANES = 128
SUBLANES = 8
MXU_DIM = 256

ROW_TILE = 512
FFN_TILE = 1024
FF_CHUNK = MXU_DIM
OUT_TILE = 512
CONV_HALO = 32
CONV_SUB = 128
SB_BLOCK = 128
SB_QROWS = 256
SB_TILE = 64
SB_TILES = SB_QROWS // SB_TILE
SB_HEADS_PER_GROUP = LANES // HEAD_DIM
SB_FIRST_BLOCKS = 2
SB_DEAD_LOG = 110.0
SIGN_BIT = 0x80000000
BF16_BITS = 0xFFFF0000
RET_BLOCK = 256
VMEM_LIMIT = 56 * 1024 * 1024


def _cparams(n_axes):
    return pltpu.CompilerParams(dimension_semantics=("arbitrary",) * n_axes,
                                vmem_limit_bytes=VMEM_LIMIT)


def _rms(x, g):
    return (x * lax.rsqrt(jnp.mean(x * x, axis=-1, keepdims=True) + EPS)) * g


def _silu(x):
    return x * jax.nn.sigmoid(x)


def _full(shape):
    return pl.BlockSpec(shape, lambda *_: (0,) * len(shape))


def _resident(shape):
    return pl.BlockSpec(shape, lambda *_: (0,) * len(shape), pipeline_mode=pl.Buffered(1))


def _ffn_kernel(*refs, n_chunks, n_out, n_mix, final):
    refs = list(refs)
    x_ref = refs.pop(0)
    mix_refs = [refs.pop(0) for _ in range(n_mix)]
    wmix_ref = refs.pop(0) if n_mix else None
    g_ref, win_ref, wout_ref = refs.pop(0), refs.pop(0), refs.pop(0)
    fg_ref = refs.pop(0) if final else None
    o_ref, h_scr, a_scr = refs.pop(0), refs.pop(0), refs.pop(0)
    out_tiles = [slice(m * OUT_TILE, (m + 1) * OUT_TILE) for m in range(n_out)]
    d_ff = wout_ref.shape[0]
    if n_mix:
        for sl in out_tiles:
            y, lo = None, 0
            for y_ref in mix_refs:
                part = jnp.dot(y_ref[...], wmix_ref[lo:lo + y_ref.shape[1], sl],
                               preferred_element_type=F32)
                y = part if y is None else y + part
                lo += y_ref.shape[1]
            o_ref[:, sl] = x_ref[:, sl] + y
        x_ref = o_ref
    h_scr[...] = _rms(x_ref[...], g_ref[...]).astype(BF16)
    for j in range(n_chunks):
        sl = slice(j * FF_CHUNK, (j + 1) * FF_CHUNK)
        up_sl = slice(d_ff + j * FF_CHUNK, d_ff + (j + 1) * FF_CHUNK)
        gate = jnp.dot(h_scr[...], win_ref[:, sl], preferred_element_type=F32)
        up = jnp.dot(h_scr[...], win_ref[:, up_sl], preferred_element_type=F32)
        a_scr[:, sl] = (_silu(gate) * up).astype(BF16)
    for sl in out_tiles:
        y = jnp.dot(a_scr[...], wout_ref[:, sl], preferred_element_type=F32)
        o_ref[:, sl] = x_ref[:, sl] + 0.5 * y
    if final:
        o_ref[...] = _rms(o_ref[...], fg_ref[...])


def _ffn(x, g, win, wout, mix=(), wmix=None, final_g=None):
    n, d = x.shape
    d_ff = wout.shape[0]
    n_chunks = d_ff // FF_CHUNK
    final = final_g is not None
    row = lambda i: (i, 0)
    in_specs = [pl.BlockSpec((FFN_TILE, d), row)]
    in_specs += [pl.BlockSpec((FFN_TILE, y.shape[1]), row) for y in mix]
    args = [x, *mix]
    if mix:
        in_specs.append(_resident(wmix.shape))
        args.append(wmix)
    in_specs += [_full((1, d)), _resident(win.shape), _resident(wout.shape)]
    args += [g, win, wout]
    if final:
        in_specs.append(_full((1, d)))
        args.append(final_g)
    return pl.pallas_call(
        functools.partial(_ffn_kernel, n_chunks=n_chunks, n_out=d // OUT_TILE, n_mix=len(mix),
                          final=final),
        out_shape=jax.ShapeDtypeStruct((n, d), F32),
        grid=(n // FFN_TILE,),
        in_specs=in_specs,
        out_specs=pl.BlockSpec((FFN_TILE, d), row),
        scratch_shapes=[pltpu.VMEM((FFN_TILE, d), BF16), pltpu.VMEM((FFN_TILE, d_ff), BF16)],
        compiler_params=_cparams(1),
        name="ffn",
    )(*args)


def _rotary(t, cos, sin_signed, first_half):
    partner = jnp.where(first_half, pltpu.roll(t, LANES - HEAD_DIM // 2, 1),
                        pltpu.roll(t, HEAD_DIM // 2, 1))
    return t * cos + partner * sin_signed


def _inproj_kernel(x_ref, g_ref, w_ref, cos_ref, sin_ref, cw_ref, cb_ref, lg_ref, lb_ref,
                   yconv_ref, q_ref, k_ref, v_ref, qkr_ref, vgr_ref, h_scr, v_scr, p_scr,
                   *, d_conv, d_sb, d_ret, tiles_per_seq):
    h_scr[...] = _rms(x_ref[...], g_ref[...]).astype(BF16)

    def proj(lo, width):
        return jnp.dot(h_scr[...], w_ref[:, lo:lo + width], preferred_element_type=F32)

    starts_sequence = pl.program_id(0) % tiles_per_seq == 0

    @pl.when(starts_sequence)
    def _():
        v_scr[0:CONV_HALO, :] = jnp.zeros((CONV_HALO, d_conv), F32)

    @pl.when(jnp.logical_not(starts_sequence))
    def _():
        v_scr[0:CONV_HALO, :] = v_scr[ROW_TILE:ROW_TILE + CONV_HALO, :]

    u = proj(0, 2 * d_conv)
    v_scr[CONV_HALO:CONV_HALO + ROW_TILE, :] = u[:, :d_conv] * jax.nn.sigmoid(u[:, d_conv:])
    v_scr[CONV_HALO + ROW_TILE:, :] = jnp.zeros((SUBLANES, d_conv), F32)

    def conv_rows(r):
        off = CONV_HALO - (CONV_WIDTH - 1)
        for shift in range(SUBLANES):
            part = None
            for o in range(shift, off + CONV_WIDTH, SUBLANES):
                if o < off:
                    continue
                rows = v_scr[pl.ds(r * CONV_SUB + o - shift, CONV_SUB + SUBLANES), :]
                term = cw_ref[o - off:o - off + 1, :] * rows
                part = term if part is None else part + term
            p_scr[r % 2, shift] = part
        y = cb_ref[...] + p_scr[r % 2, 0, 0:CONV_SUB, :]
        for shift in range(1, SUBLANES):
            y = y + p_scr[r % 2, shift, pl.ds(shift, CONV_SUB), :]
        mu = jnp.mean(y, axis=-1, keepdims=True)
        yc = y - mu
        var = jnp.mean(yc * yc, axis=-1, keepdims=True)
        ln = yc * lax.rsqrt(var + EPS) * lg_ref[...] + lb_ref[...]
        yconv_ref[r * CONV_SUB:(r + 1) * CONV_SUB, :] = _silu(ln).astype(BF16)

    o = 2 * d_conv
    conv_rows(0)
    q_ref[...] = proj(o, d_sb).astype(BF16)
    conv_rows(1)
    k_ref[...] = proj(o + d_sb, d_sb).astype(BF16)
    conv_rows(2)
    v_ref[...] = proj(o + 2 * d_sb, d_sb).astype(BF16)
    conv_rows(3)
    o += 3 * d_sb
    qk = proj(o, 2 * d_ret)
    cos = cos_ref[...]
    sin = sin_ref[...]
    lane = lax.broadcasted_iota(jnp.int32, (1, LANES), 1)
    first_half = (lane % HEAD_DIM) < (HEAD_DIM // 2)
    for c in range(2 * d_ret // LANES):
        sl = slice(c * LANES, (c + 1) * LANES)
        qkr_ref[:, sl] = _rotary(qk[:, sl], cos, sin, first_half)
    vgr_ref[...] = proj(o + 2 * d_ret, 2 * d_ret)


def _inproj(x, g, w, cos_tab, sin_tab, cw, cb, lg, lb, seq, d_sb, d_ret):
    n, d = x.shape
    d_conv = cw.shape[1]
    assert ROW_TILE == 4 * CONV_SUB and CONV_HALO >= CONV_WIDTH - 1
    tiles_per_seq = seq // ROW_TILE
    row = lambda i: (i, 0)
    pos = lambda i: (i % tiles_per_seq, 0)
    outs = [jax.ShapeDtypeStruct((n, d_conv), BF16)] + [jax.ShapeDtypeStruct((n, d_sb), BF16)] * 3 \
        + [jax.ShapeDtypeStruct((n, 2 * d_ret), F32)] * 2
    return pl.pallas_call(
        functools.partial(_inproj_kernel, d_conv=d_conv, d_sb=d_sb, d_ret=d_ret,
                          tiles_per_seq=tiles_per_seq),
        out_shape=outs,
        grid=(n // ROW_TILE,),
        in_specs=[pl.BlockSpec((ROW_TILE, d), row), _full((1, d)), _resident(w.shape),
                  pl.BlockSpec((ROW_TILE, LANES), pos), pl.BlockSpec((ROW_TILE, LANES), pos),
                  _full(cw.shape), _full((1, d_conv)), _full((1, d_conv)), _full((1, d_conv))],
        out_specs=[pl.BlockSpec((ROW_TILE, s.shape[1]), row) for s in outs],
        scratch_shapes=[pltpu.VMEM((ROW_TILE, d), BF16),
                        pltpu.VMEM((CONV_HALO + ROW_TILE + SUBLANES, d_conv), F32),
                        pltpu.VMEM((2, SUBLANES, CONV_SUB + SUBLANES, d_conv), F32)],
        compiler_params=_cparams(1),
        name="mixer_inproj",
    )(x, g, w, cos_tab, sin_tab, cw, cb, lg, lb)


def _sb_kernel(q_ref, k_ref, v_ref, tri_ref, o_ref, qs_scr, acc_scr, carry_scr, *, n_pairs):
    q0 = pl.program_id(1) * SB_QROWS
    tiles, pairs = range(SB_TILES), range(n_pairs)
    rows = SB_HEADS_PER_GROUP * SB_TILE
    lane = lax.broadcasted_iota(jnp.int32, (1, LANES), 1)
    col = lax.broadcasted_iota(jnp.int32, (rows, SB_BLOCK), 1)
    row = lax.broadcasted_iota(jnp.int32, (rows, SB_BLOCK), 0) % SB_TILE
    for p in pairs:
        q = q_ref[0, :, p * LANES:(p + 1) * LANES]
        for t in tiles:
            qt = q[t * SB_TILE:(t + 1) * SB_TILE]
            qs_scr[t, p] = jnp.concatenate(
                [jnp.where(lane // HEAD_DIM == hh, qt, jnp.zeros_like(qt))
                 for hh in range(SB_HEADS_PER_GROUP)], axis=0)

    def sweep(blocks, resume):
        depth = range(len(blocks[0]))

        def rows_of(ref, t, n, p):
            return ref[0, pl.ds(blocks[t][n][0], SB_BLOCK), p * LANES:(p + 1) * LANES]

        zs = [[[lax.dot_general(qs_scr[t, p], rows_of(k_ref, t, n, p), (((1,), (1,)), ((), ())),
                                preferred_element_type=F32) for p in pairs]
               for n in depth] for t in tiles]
        hls = []
        for t in tiles:
            for n in depth:
                for p in pairs:
                    z = zs[t][n][p]
                    zbits = lax.bitcast_convert_type(z, jnp.uint32)
                    neg_abs = lax.bitcast_convert_type(zbits | jnp.uint32(SIGN_BIT), F32)
                    sp = jnp.maximum(z, 0.0) + jnp.log(1.0 + jnp.exp(neg_abs))
                    if blocks[t][n][1] is not None:
                        sp = jnp.where(blocks[t][n][1], sp, 0.0)
                    hi = lax.bitcast_convert_type(
                        lax.bitcast_convert_type(sp, jnp.uint32) & jnp.uint32(BF16_BITS), F32)
                    hls.append(jnp.concatenate([hi.astype(BF16), (sp - hi).astype(BF16)], axis=1))
        rt = jnp.dot(jnp.concatenate(hls, axis=0), tri_ref[...], preferred_element_type=F32)
        lowest = None
        for t in tiles:
            for p in pairs:
                carry = carry_scr[t, p] if resume else None
                acc = acc_scr[t, p] if resume else None
                for n in depth:
                    unit = ((t * len(depth) + n) * n_pairs + p) * rows
                    within = rt[unit:unit + rows, :SB_BLOCK]
                    total = rt[unit:unit + rows, SB_BLOCK:]
                    w = jnp.exp(zs[t][n][p] - (within if carry is None else within + carry))
                    if blocks[t][n][1] is not None:
                        w = jnp.where(blocks[t][n][1], w, 0.0)
                    carry = total if carry is None else carry + total
                    pv = jnp.dot(w.astype(BF16), rows_of(v_ref, t, n, p),
                                 preferred_element_type=F32)
                    acc = pv if acc is None else acc + pv
                carry_scr[t, p] = carry
                acc_scr[t, p] = acc
                lowest = carry if lowest is None else jnp.minimum(lowest, carry)
        return jnp.min(lowest)

    def aligned(start):
        return pl.multiple_of(start, SB_TILE)

    def first_fast():
        near_mask = col < row + (SB_BLOCK - SB_TILE)
        return sweep([[(aligned(q0 + (t + 1) * SB_TILE - (n + 1) * SB_BLOCK),
                        near_mask if n == 0 else None) for n in range(SB_FIRST_BLOCKS)]
                      for t in tiles], False)

    def first_clamped():
        blocks = []
        for t in tiles:
            limit = q0 + t * SB_TILE + row
            blocks.append([])
            for n in range(SB_FIRST_BLOCKS):
                start = aligned(jnp.maximum(q0 + (t + 1) * SB_TILE - (n + 1) * SB_BLOCK, 0))
                blocks[t].append((start, start + col < limit))
                limit = start
        return sweep(blocks, False)

    def more(st):
        reach = q0 + SB_QROWS - st[0] * SB_BLOCK
        return jnp.logical_and(reach > 0, st[1] < SB_DEAD_LOG)

    def body(st):
        blocks = []
        for t in tiles:
            limit = jnp.maximum(q0 + (t + 1) * SB_TILE - st[0] * SB_BLOCK, 0)
            start = aligned(jnp.maximum(limit - SB_BLOCK, 0))
            blocks.append([(start, start + col < limit)])
        return st[0] + 1, sweep(blocks, True)

    fits = q0 + SB_TILE - SB_FIRST_BLOCKS * SB_BLOCK >= 0
    lowest = lax.cond(fits, first_fast, first_clamped)
    lax.while_loop(more, body, (SB_FIRST_BLOCKS, lowest))
    for p in pairs:
        for t in tiles:
            acc = acc_scr[t, p]
            out = acc[0:SB_TILE]
            for hh in range(1, SB_HEADS_PER_GROUP):
                out = jnp.where(lane // HEAD_DIM == hh, acc[hh * SB_TILE:(hh + 1) * SB_TILE], out)
            o_ref[0, t * SB_TILE:(t + 1) * SB_TILE, p * LANES:(p + 1) * LANES] = out.astype(BF16)


def _stick_breaking(q, k, v, tri):
    b, s, d_sb = q.shape
    n_pairs = d_sb // LANES
    rows = SB_HEADS_PER_GROUP * SB_TILE
    blk = pl.BlockSpec((1, SB_QROWS, d_sb), lambda bi, i: (bi, i, 0))
    whole = pl.BlockSpec((1, s, d_sb), lambda bi, i: (bi, 0, 0))
    return pl.pallas_call(
        functools.partial(_sb_kernel, n_pairs=n_pairs),
        out_shape=jax.ShapeDtypeStruct((b, s, d_sb), BF16),
        grid=(b, s // SB_QROWS),
        in_specs=[blk, whole, whole, _full(tri.shape)],
        out_specs=blk,
        scratch_shapes=[pltpu.VMEM((SB_TILES, n_pairs, rows, LANES), BF16),
                        pltpu.VMEM((SB_TILES, n_pairs, rows, LANES), F32),
                        pltpu.VMEM((SB_TILES, n_pairs, rows, LANES), F32)],
        compiler_params=_cparams(2),
        name="stick_breaking",
    )(q, k, v, tri)


def _ret_kernel(qk_ref, vg_ref, dmat_ref, qdec_ref, kdec_ref, cdec_ref, bd_ref, avg_ref, gain_ref,
                o_ref, state_scr, *, d_ret, n_batch):
    n_heads = d_ret // HEAD_DIM
    batches = range(n_batch)
    nt = (((1,), (1,)), ((), ()))

    @pl.when(pl.program_id(0) == 0)
    def _():
        state_scr[...] = jnp.zeros_like(state_scr)

    lane = lax.broadcasted_iota(jnp.int32, (1, d_ret), 1)
    in_head = [lane // HEAD_DIM == h for h in range(n_heads)]
    q = [qk_ref[b, :, :d_ret] for b in batches]
    k = [qk_ref[b, :, d_ret:] for b in batches]
    v = [vg_ref[b, :, :d_ret] for b in batches]
    probs = []
    for b in batches:
        kb = k[b].astype(BF16)
        probs.append(jnp.concatenate(
            [(lax.dot_general(jnp.where(in_head[h], q[b], 0.0).astype(BF16), kb, nt,
                              preferred_element_type=F32) * dmat_ref[h]).astype(BF16)
             for h in range(n_heads)], axis=1))
    y = []
    for b in batches:
        vals = jnp.concatenate([jnp.where(in_head[h], v[b], 0.0).astype(BF16)
                                for h in range(n_heads)], axis=0)
        y.append(jnp.dot(probs[b], vals, preferred_element_type=F32))
    for b in batches:
        state = state_scr[b]
        y[b] = y[b] + jnp.dot((q[b] * qdec_ref[...]).astype(BF16), state.astype(BF16),
                              preferred_element_type=F32)
        kv = lax.dot_general((k[b] * kdec_ref[...]).astype(BF16), v[b].astype(BF16),
                             (((0,), (0,)), ((), ())), preferred_element_type=F32)
        state_scr[b] = cdec_ref[...] * state + bd_ref[...] * kv

    def head_mean(t):
        hi = t.astype(BF16)
        lo = (t - hi.astype(F32)).astype(BF16)
        return jnp.dot(jnp.concatenate([hi, lo], axis=1), avg_ref[...], preferred_element_type=F32)

    yc = [y[b] - head_mean(y[b]) for b in batches]
    var = [head_mean(yc[b] * yc[b]) for b in batches]
    for b in batches:
        g = vg_ref[b, :, d_ret:]
        o_ref[b] = (_silu(g) * (yc[b] * lax.rsqrt(var[b] + EPS) * gain_ref[...])).astype(BF16)


def _retention(qk, vg, tabs, gain):
    b, s, d2 = qk.shape
    d_ret = d2 // 2
    blk = pl.BlockSpec((b, RET_BLOCK, d2), lambda i: (0, i, 0))
    return pl.pallas_call(
        functools.partial(_ret_kernel, d_ret=d_ret, n_batch=b),
        out_shape=jax.ShapeDtypeStruct((b, s, d_ret), BF16),
        grid=(s // RET_BLOCK,),
        in_specs=[blk, blk] + [_full(t.shape) for t in tabs] + [_full((1, d_ret))],
        out_specs=pl.BlockSpec((b, RET_BLOCK, d_ret), lambda i: (0, i, 0)),
        scratch_shapes=[pltpu.VMEM((b, d_ret, d_ret), F32)],
        compiler_params=_cparams(1),
        name="retention",
    )(qk, vg, *tabs, gain)


def _rotary_tables(seq):
    half = HEAD_DIM // 2
    inv = 1.0 / (ROPE_BASE ** (jnp.arange(half, dtype=F32) / half))
    lane = jnp.arange(LANES)
    ang = jnp.arange(seq).astype(F32)[:, None] * inv[lane % half][None, :]
    sign = jnp.where((lane % HEAD_DIM) < half, -1.0, 1.0).astype(F32)
    return jnp.cos(ang), jnp.sin(ang) * sign[None, :]


def _retention_tables(n_heads):
    d_ret = n_heads * HEAD_DIM
    log_gamma = jnp.log1p(-jnp.exp2(-5.0 - jnp.arange(n_heads, dtype=F32)))
    idx = jnp.arange(RET_BLOCK, dtype=F32)
    t, s = idx[:, None], idx[None, :]
    same = (t // CHUNK) == (s // CHUNK)
    dist = jnp.where(same, jnp.abs(t - s), t - s)
    seen = same | ((s // CHUNK) < (t // CHUNK))
    dmat = jnp.where(seen[None], jnp.exp(log_gamma[:, None, None] * dist[None]), 0.0)
    lane_gamma = jnp.repeat(log_gamma, HEAD_DIM)
    qdec = jnp.exp(lane_gamma[None, :] * (idx + 1.0)[:, None])
    kdec = jnp.exp(lane_gamma[None, :] * (RET_BLOCK - 1.0 - idx)[:, None])
    head = jnp.arange(d_ret) // HEAD_DIM
    bd = (head[:, None] == head[None, :]).astype(F32)
    cdec = bd * jnp.exp(lane_gamma * RET_BLOCK)[:, None]
    avg = jnp.concatenate([bd, bd], axis=0).astype(BF16) * (1.0 / HEAD_DIM)
    return dmat, qdec, kdec, cdec, bd, avg.astype(BF16)


def _tri_table():
    r = np.arange(2 * SB_BLOCK)[:, None] % SB_BLOCK
    c = np.arange(2 * SB_BLOCK)[None, :]
    return jnp.asarray((c >= SB_BLOCK) | (r >= c), dtype=BF16)


def kernel(x, ffn1_norm, ffn1_w_in, ffn1_w_out, mix_norm, mix_w_in, conv_w, conv_b, conv_ln_g,
           conv_ln_b, ret_norm_g, mix_w_out, ffn2_norm, ffn2_w_in, ffn2_w_out, final_norm):
    b, s, d = x.shape
    depth = ffn1_norm.shape[0]
    d_conv = conv_w.shape[2]
    d_ret = ret_norm_g.shape[1]
    d_sb = d - d_conv - d_ret
    assert s % ROW_TILE == 0 and s % RET_BLOCK == 0 and s % SB_QROWS == 0
    assert (b * s) % FFN_TILE == 0
    assert ffn1_w_out.shape[1] % FF_CHUNK == 0 and d % OUT_TILE == 0

    cos_tab, sin_tab = _rotary_tables(s)
    ret_tabs = _retention_tables(d_ret // HEAD_DIM)
    tri = _tri_table()
    scale = HEAD_DIM ** -0.5
    col = jnp.arange(mix_w_in.shape[2])
    q_sb_cols = (col >= 2 * d_conv) & (col < 2 * d_conv + d_sb)
    q_r_cols = (col >= 2 * d_conv + 3 * d_sb) & (col < 2 * d_conv + 3 * d_sb + d_ret)
    col_scale = jnp.where(q_sb_cols | q_r_cols, scale, 1.0).astype(F32)

    row = lambda g: g.reshape(1, -1)
    xf = x.reshape(b * s, d)
    for l in range(depth):
        win1, wout1 = ffn1_w_in[l].astype(BF16), ffn1_w_out[l].astype(BF16)
        win2, wout2 = ffn2_w_in[l].astype(BF16), ffn2_w_out[l].astype(BF16)
        w_mix = (mix_w_in[l] * col_scale[None, :]).astype(BF16)

        xf = _ffn(xf, row(ffn1_norm[l]), win1, wout1)
        y_conv, q, k, v, qkr, vgr = _inproj(
            xf, row(mix_norm[l]), w_mix, cos_tab, sin_tab, conv_w[l], row(conv_b[l]),
            row(conv_ln_g[l]), row(conv_ln_b[l]), s, d_sb, d_ret)
        seq3 = lambda t: t.reshape(b, s, t.shape[1])
        y_sb = _stick_breaking(seq3(q), seq3(k), seq3(v), tri)
        y_ret = _retention(seq3(qkr), seq3(vgr), ret_tabs, row(ret_norm_g[l]))
        flat = lambda t: t.reshape(b * s, t.shape[2])
        last = l == depth - 1
        xf = _ffn(xf, row(ffn2_norm[l]), win2, wout2,
                  mix=(y_conv, flat(y_sb), flat(y_ret)), wmix=mix_w_out[l].astype(BF16),
                  final_g=row(final_norm) if last else None)
    return xf.reshape(b, s, d)
```

```python
import functools

import jax
import jax.numpy as jnp
import numpy as np
from jax import lax
from jax.experimental import pallas as pl
from jax.experimental.pallas import tpu as pltpu

F32 = jnp.float32
BF16 = jnp.bfloat16

EPS = 1e-6
ROPE_BASE = 10000.0
CHUNK = 64
CONV_WIDTH = 31
HEAD_DIM = 64
LANES = 128
SUBLANES = 8
MXU_DIM = 256

ROW_TILE = 512
FFN_TILE = 1024
FF_CHUNK = MXU_DIM
OUT_TILE = 512
CONV_HALO = 32
CONV_SUB = 128
CONV_CHUNK_GAP = 2
CONV_CHUNK_SPAN = 2
SB_BLOCK = 128
SB_QROWS = 256
SB_TILE = 64
SB_TILES = SB_QROWS // SB_TILE
SB_HEADS_PER_GROUP = LANES // HEAD_DIM
SB_FIRST_BLOCKS = 2
SB_DEAD_LOG = 110.0
SIGN_BIT = 0x80000000
BF16_BITS = 0xFFFF0000
RET_BLOCK = 256
VMEM_LIMIT = 56 * 1024 * 1024


def _cparams(n_axes):
    return pltpu.CompilerParams(dimension_semantics=("arbitrary",) * n_axes,
                                vmem_limit_bytes=VMEM_LIMIT)


def _rms(x, g):
    return (x * lax.rsqrt(jnp.mean(x * x, axis=-1, keepdims=True) + EPS)) * g


def _silu(x):
    return x * jax.nn.sigmoid(x)


def _full(shape):
    return pl.BlockSpec(shape, lambda *_: (0,) * len(shape))


def _resident(shape):
    return pl.BlockSpec(shape, lambda *_: (0,) * len(shape), pipeline_mode=pl.Buffered(1))


def _bit_zero(x):
    bits = lax.bitcast_convert_type(x, jnp.uint32)
    return lax.bitcast_convert_type((bits >> 16) >> 16, F32)


def _conv_fill(u, v_scr, starts_sequence, tile_rows):
    d_conv = v_scr.shape[1]
    if starts_sequence is None:
        v_scr[0:CONV_HALO, :] = jnp.zeros((CONV_HALO, d_conv), F32)
    else:
        v_scr[0:CONV_HALO, :] = jnp.where(starts_sequence, 0.0,
                                          v_scr[tile_rows:tile_rows + CONV_HALO, :])
    v_scr[CONV_HALO:CONV_HALO + tile_rows, :] = u[:, :d_conv] * jax.nn.sigmoid(u[:, d_conv:])
    v_scr[CONV_HALO + tile_rows:, :] = jnp.zeros((SUBLANES, d_conv), F32)


def _conv_rows(v_scr, p_scr, cw_ref, cb_ref, lg_ref, lb_ref, out_ref, r, anchor=None):
    off = CONV_HALO - (CONV_WIDTH - 1)
    cw = cw_ref[...] if anchor is None else cw_ref[...] + anchor
    for shift in range(SUBLANES):
        part = None
        for o in range(shift, off + CONV_WIDTH, SUBLANES):
            if o < off:
                continue
            rows = v_scr[pl.ds(r * CONV_SUB + o - shift, CONV_SUB + SUBLANES), :]
            term = cw[o - off:o - off + 1, :] * rows
            part = term if part is None else part + term
        p_scr[r % 2, shift] = part
    y = cb_ref[...] + p_scr[r % 2, 0, 0:CONV_SUB, :]
    for shift in range(1, SUBLANES):
        y = y + p_scr[r % 2, shift, pl.ds(shift, CONV_SUB), :]
    mu = jnp.mean(y, axis=-1, keepdims=True)
    yc = y - mu
    var = jnp.mean(yc * yc, axis=-1, keepdims=True)
    ln = yc * lax.rsqrt(var + EPS) * lg_ref[...] + lb_ref[...]
    out = _silu(ln)
    out_ref[r * CONV_SUB:(r + 1) * CONV_SUB, :] = out.astype(BF16)
    return out[0:1, :]


def _ffn_kernel(*refs, tile, n_chunks, n_out, n_mix, final, conv_tiles_per_seq):
    refs = list(refs)
    x_ref = refs.pop(0)
    mix_refs = [refs.pop(0) for _ in range(n_mix)]
    wmix_ref = refs.pop(0) if n_mix else None
    g_ref, win_ref, wout_ref = refs.pop(0), refs.pop(0), refs.pop(0)
    fg_ref = refs.pop(0) if final else None
    has_conv = conv_tiles_per_seq is not None
    if has_conv:
        u_first_ref, u_next_ref = refs.pop(0), refs.pop(0)
        conv_params = [refs.pop(0) for _ in range(4)]
    o_ref, h_scr, a_scr = refs.pop(0), refs.pop(0), refs.pop(0)
    out_tiles = [slice(m * OUT_TILE, (m + 1) * OUT_TILE) for m in range(n_out)]
    d_ff = wout_ref.shape[0]
    n_sub = tile // CONV_SUB
    if has_conv:
        v_scr, p_scr, yc_scr = refs.pop(0), refs.pop(0), refs.pop(0)
        step = pl.program_id(0)

        @pl.when(step == 0)
        def _():
            _conv_fill(u_first_ref[...], v_scr, None, tile)
            for r in range(n_sub):
                _conv_rows(v_scr, p_scr, *conv_params, yc_scr, r)

        mix_refs = [yc_scr] + mix_refs
    if mix_refs:
        for sl in out_tiles:
            y, lo = None, 0
            for y_ref in mix_refs:
                part = jnp.dot(y_ref[...], wmix_ref[lo:lo + y_ref.shape[1], sl],
                               preferred_element_type=F32)
                y = part if y is None else y + part
                lo += y_ref.shape[1]
            o_ref[:, sl] = x_ref[:, sl] + y
        x_ref = o_ref
    h_scr[...] = _rms(x_ref[...], g_ref[...]).astype(BF16)
    if has_conv:
        nxt = jnp.minimum(step + 1, pl.num_programs(0) - 1)
        _conv_fill(u_next_ref[...], v_scr, nxt % conv_tiles_per_seq == 0, tile)
    done = {}
    for j in range(n_chunks):
        sl = slice(j * FF_CHUNK, (j + 1) * FF_CHUNK)
        up_sl = slice(d_ff + j * FF_CHUNK, d_ff + (j + 1) * FF_CHUNK)
        gate = jnp.dot(h_scr[...], win_ref[:, sl], preferred_element_type=F32)
        up = jnp.dot(h_scr[...], win_ref[:, up_sl], preferred_element_type=F32)
        if j in done:
            gate = gate + done.pop(j)
        a_scr[:, sl] = (_silu(gate) * up).astype(BF16)
        if has_conv and j % CONV_CHUNK_GAP == 0 and j // CONV_CHUNK_GAP < n_sub:
            r = j // CONV_CHUNK_GAP
            out_row = _conv_rows(v_scr, p_scr, *conv_params, yc_scr, r,
                                 anchor=_bit_zero(gate[0:1, :conv_params[0].shape[1]]))
            done[j + CONV_CHUNK_SPAN] = _bit_zero(out_row)
    assert not done
    for sl in out_tiles:
        y = jnp.dot(a_scr[...], wout_ref[:, sl], preferred_element_type=F32)
        o_ref[:, sl] = x_ref[:, sl] + 0.5 * y
    if final:
        o_ref[...] = _rms(o_ref[...], fg_ref[...])


def _ffn(x, g, win, wout, tile, mix=(), wmix=None, final_g=None, conv=None):
    n, d = x.shape
    d_ff = wout.shape[0]
    n_chunks = d_ff // FF_CHUNK
    n_tiles = n // tile
    final = final_g is not None
    row = lambda i: (i, 0)
    in_specs = [pl.BlockSpec((tile, d), row)]
    in_specs += [pl.BlockSpec((tile, y.shape[1]), row) for y in mix]
    args = [x, *mix]
    if mix:
        in_specs.append(_resident(wmix.shape))
        args.append(wmix)
    in_specs += [_full((1, d)), _resident(win.shape), _resident(wout.shape)]
    args += [g, win, wout]
    if final:
        in_specs.append(_full((1, d)))
        args.append(final_g)
    scratch = [pltpu.VMEM((tile, d), BF16), pltpu.VMEM((tile, d_ff), BF16)]
    conv_tiles_per_seq = None
    if conv is not None:
        u, seq, cw, cb, lg, lb = conv
        d_conv = cw.shape[1]
        assert seq % tile == 0 and tile % CONV_SUB == 0 and CONV_HALO >= CONV_WIDTH - 1
        conv_tiles_per_seq = seq // tile
        in_specs += [pl.BlockSpec((tile, u.shape[1]), lambda i: (0, 0), pipeline_mode=pl.Buffered(1)),
                     pl.BlockSpec((tile, u.shape[1]), lambda i: (jnp.minimum(i + 1, n_tiles - 1), 0)),
                     _full(cw.shape), _full((1, d_conv)), _full((1, d_conv)), _full((1, d_conv))]
        args += [u, u, cw, cb, lg, lb]
        scratch += [pltpu.VMEM((CONV_HALO + tile + SUBLANES, d_conv), F32),
                    pltpu.VMEM((2, SUBLANES, CONV_SUB + SUBLANES, d_conv), F32),
                    pltpu.VMEM((tile, d_conv), BF16)]
    return pl.pallas_call(
        functools.partial(_ffn_kernel, tile=tile, n_chunks=n_chunks, n_out=d // OUT_TILE,
                          n_mix=len(mix), final=final, conv_tiles_per_seq=conv_tiles_per_seq),
        out_shape=jax.ShapeDtypeStruct((n, d), F32),
        grid=(n_tiles,),
        in_specs=in_specs,
        out_specs=pl.BlockSpec((tile, d), row),
        scratch_shapes=scratch,
        compiler_params=_cparams(1),
        name="ffn",
    )(*args)


def _rotary(t, cos, sin_signed, first_half):
    partner = jnp.where(first_half, pltpu.roll(t, LANES - HEAD_DIM // 2, 1),
                        pltpu.roll(t, HEAD_DIM // 2, 1))
    return t * cos + partner * sin_signed


def _inproj_kernel(x_ref, g_ref, w_ref, cos_ref, sin_ref, uconv_ref, q_ref, k_ref, v_ref,
                   qkr_ref, vgr_ref, h_scr, *, d_conv2, d_sb, d_ret):
    h_scr[...] = _rms(x_ref[...], g_ref[...]).astype(BF16)

    def proj(lo, width):
        return jnp.dot(h_scr[...], w_ref[:, lo:lo + width], preferred_element_type=F32)

    uconv_ref[...] = proj(0, d_conv2)
    o = d_conv2
    q_ref[...] = proj(o, d_sb).astype(BF16)
    k_ref[...] = proj(o + d_sb, d_sb).astype(BF16)
    v_ref[...] = proj(o + 2 * d_sb, d_sb).astype(BF16)
    o += 3 * d_sb
    qk = proj(o, 2 * d_ret)
    cos = cos_ref[...]
    sin = sin_ref[...]
    lane = lax.broadcasted_iota(jnp.int32, (1, LANES), 1)
    first_half = (lane % HEAD_DIM) < (HEAD_DIM // 2)
    for c in range(2 * d_ret // LANES):
        sl = slice(c * LANES, (c + 1) * LANES)
        qkr_ref[:, sl] = _rotary(qk[:, sl], cos, sin, first_half)
    vgr_ref[...] = proj(o + 2 * d_ret, 2 * d_ret)


def _inproj(x, g, w, cos_tab, sin_tab, seq, d_conv2, d_sb, d_ret):
    n, d = x.shape
    tiles_per_seq = seq // ROW_TILE
    row = lambda i: (i, 0)
    pos = lambda i: (i % tiles_per_seq, 0)
    outs = [jax.ShapeDtypeStruct((n, d_conv2), F32)] + [jax.ShapeDtypeStruct((n, d_sb), BF16)] * 3 \
        + [jax.ShapeDtypeStruct((n, 2 * d_ret), F32)] * 2
    return pl.pallas_call(
        functools.partial(_inproj_kernel, d_conv2=d_conv2, d_sb=d_sb, d_ret=d_ret),
        out_shape=outs,
        grid=(n // ROW_TILE,),
        in_specs=[pl.BlockSpec((ROW_TILE, d), row), _full((1, d)), _resident(w.shape),
                  pl.BlockSpec((ROW_TILE, LANES), pos), pl.BlockSpec((ROW_TILE, LANES), pos)],
        out_specs=[pl.BlockSpec((ROW_TILE, s.shape[1]), row) for s in outs],
        scratch_shapes=[pltpu.VMEM((ROW_TILE, d), BF16)],
        compiler_params=_cparams(1),
        name="mixer_inproj",
    )(x, g, w, cos_tab, sin_tab)


def _sb_kernel(q_ref, k_ref, v_ref, tri_ref, o_ref, qs_scr, acc_scr, carry_scr, *, n_pairs):
    q0 = pl.program_id(1) * SB_QROWS
    tiles, pairs = range(SB_TILES), range(n_pairs)
    rows = SB_HEADS_PER_GROUP * SB_TILE
    lane = lax.broadcasted_iota(jnp.int32, (1, LANES), 1)
    col = lax.broadcasted_iota(jnp.int32, (rows, SB_BLOCK), 1)
    row = lax.broadcasted_iota(jnp.int32, (rows, SB_BLOCK), 0) % SB_TILE
    for p in pairs:
        q = q_ref[0, :, p * LANES:(p + 1) * LANES]
        for t in tiles:
            qt = q[t * SB_TILE:(t + 1) * SB_TILE]
            qs_scr[t, p] = jnp.concatenate(
                [jnp.where(lane // HEAD_DIM == hh, qt, jnp.zeros_like(qt))
                 for hh in range(SB_HEADS_PER_GROUP)], axis=0)

    def sweep(blocks, resume):
        depth = range(len(blocks[0]))

        def rows_of(ref, t, n, p):
            return ref[0, pl.ds(blocks[t][n][0], SB_BLOCK), p * LANES:(p + 1) * LANES]

        zs = [[[lax.dot_general(qs_scr[t, p], rows_of(k_ref, t, n, p), (((1,), (1,)), ((), ())),
                                preferred_element_type=F32) for p in pairs]
               for n in depth] for t in tiles]
        hls = []
        for t in tiles:
            for n in depth:
                for p in pairs:
                    z = zs[t][n][p]
                    zbits = lax.bitcast_convert_type(z, jnp.uint32)
                    neg_abs = lax.bitcast_convert_type(zbits | jnp.uint32(SIGN_BIT), F32)
                    sp = jnp.maximum(z, 0.0) + jnp.log(1.0 + jnp.exp(neg_abs))
                    if blocks[t][n][1] is not None:
                        sp = jnp.where(blocks[t][n][1], sp, 0.0)
                    hi = lax.bitcast_convert_type(
                        lax.bitcast_convert_type(sp, jnp.uint32) & jnp.uint32(BF16_BITS), F32)
                    hls.append(jnp.concatenate([hi.astype(BF16), (sp - hi).astype(BF16)], axis=1))
        rt = jnp.dot(jnp.concatenate(hls, axis=0), tri_ref[...], preferred_element_type=F32)
        lowest = None
        for t in tiles:
            for p in pairs:
                carry = carry_scr[t, p] if resume else None
                acc = acc_scr[t, p] if resume else None
                for n in depth:
                    unit = ((t * len(depth) + n) * n_pairs + p) * rows
                    within = rt[unit:unit + rows, :SB_BLOCK]
                    total = rt[unit:unit + rows, SB_BLOCK:]
                    w = jnp.exp(zs[t][n][p] - (within if carry is None else within + carry))
                    if blocks[t][n][1] is not None:
                        w = jnp.where(blocks[t][n][1], w, 0.0)
                    carry = total if carry is None else carry + total
                    pv = jnp.dot(w.astype(BF16), rows_of(v_ref, t, n, p),
                                 preferred_element_type=F32)
                    acc = pv if acc is None else acc + pv
                carry_scr[t, p] = carry
                acc_scr[t, p] = acc
                lowest = carry if lowest is None else jnp.minimum(lowest, carry)
        return jnp.min(lowest)

    def aligned(start):
        return pl.multiple_of(start, SB_TILE)

    def first_fast():
        near_mask = col < row + (SB_BLOCK - SB_TILE)
        return sweep([[(aligned(q0 + (t + 1) * SB_TILE - (n + 1) * SB_BLOCK),
                        near_mask if n == 0 else None) for n in range(SB_FIRST_BLOCKS)]
                      for t in tiles], False)

    def first_clamped():
        blocks = []
        for t in tiles:
            limit = q0 + t * SB_TILE + row
            blocks.append([])
            for n in range(SB_FIRST_BLOCKS):
                start = aligned(jnp.maximum(q0 + (t + 1) * SB_TILE - (n + 1) * SB_BLOCK, 0))
                blocks[t].append((start, start + col < limit))
                limit = start
        return sweep(blocks, False)

    def more(st):
        reach = q0 + SB_QROWS - st[0] * SB_BLOCK
        return jnp.logical_and(reach > 0, st[1] < SB_DEAD_LOG)

    def body(st):
        blocks = []
        for t in tiles:
            limit = jnp.maximum(q0 + (t + 1) * SB_TILE - st[0] * SB_BLOCK, 0)
            start = aligned(jnp.maximum(limit - SB_BLOCK, 0))
            blocks.append([(start, start + col < limit)])
        return st[0] + 1, sweep(blocks, True)

    fits = q0 + SB_TILE - SB_FIRST_BLOCKS * SB_BLOCK >= 0
    lowest = lax.cond(fits, first_fast, first_clamped)
    lax.while_loop(more, body, (SB_FIRST_BLOCKS, lowest))
    for p in pairs:
        for t in tiles:
            acc = acc_scr[t, p]
            out = acc[0:SB_TILE]
            for hh in range(1, SB_HEADS_PER_GROUP):
                out = jnp.where(lane // HEAD_DIM == hh, acc[hh * SB_TILE:(hh + 1) * SB_TILE], out)
            o_ref[0, t * SB_TILE:(t + 1) * SB_TILE, p * LANES:(p + 1) * LANES] = out.astype(BF16)


def _stick_breaking(q, k, v, tri):
    b, s, d_sb = q.shape
    n_pairs = d_sb // LANES
    rows = SB_HEADS_PER_GROUP * SB_TILE
    blk = pl.BlockSpec((1, SB_QROWS, d_sb), lambda bi, i: (bi, i, 0))
    whole = pl.BlockSpec((1, s, d_sb), lambda bi, i: (bi, 0, 0))
    return pl.pallas_call(
        functools.partial(_sb_kernel, n_pairs=n_pairs),
        out_shape=jax.ShapeDtypeStruct((b, s, d_sb), BF16),
        grid=(b, s // SB_QROWS),
        in_specs=[blk, whole, whole, _full(tri.shape)],
        out_specs=blk,
        scratch_shapes=[pltpu.VMEM((SB_TILES, n_pairs, rows, LANES), BF16),
                        pltpu.VMEM((SB_TILES, n_pairs, rows, LANES), F32),
                        pltpu.VMEM((SB_TILES, n_pairs, rows, LANES), F32)],
        compiler_params=_cparams(2),
        name="stick_breaking",
    )(q, k, v, tri)


def _ret_kernel(qk_ref, vg_ref, dmat_ref, qdec_ref, kdec_ref, cdec_ref, bd_ref, avg_ref, gain_ref,
                o_ref, state_scr, *, d_ret, n_batch):
    n_heads = d_ret // HEAD_DIM
    batches = range(n_batch)
    nt = (((1,), (1,)), ((), ()))

    @pl.when(pl.program_id(0) == 0)
    def _():
        state_scr[...] = jnp.zeros_like(state_scr)

    lane = lax.broadcasted_iota(jnp.int32, (1, d_ret), 1)
    in_head = [lane // HEAD_DIM == h for h in range(n_heads)]
    q = [qk_ref[b, :, :d_ret] for b in batches]
    k = [qk_ref[b, :, d_ret:] for b in batches]
    v = [vg_ref[b, :, :d_ret] for b in batches]
    probs = []
    for b in batches:
        kb = k[b].astype(BF16)
        probs.append(jnp.concatenate(
            [(lax.dot_general(jnp.where(in_head[h], q[b], 0.0).astype(BF16), kb, nt,
                              preferred_element_type=F32) * dmat_ref[h]).astype(BF16)
             for h in range(n_heads)], axis=1))
    y = []
    for b in batches:
        vals = jnp.concatenate([jnp.where(in_head[h], v[b], 0.0).astype(BF16)
                                for h in range(n_heads)], axis=0)
        y.append(jnp.dot(probs[b], vals, preferred_element_type=F32))
    for b in batches:
        state = state_scr[b]
        y[b] = y[b] + jnp.dot((q[b] * qdec_ref[...]).astype(BF16), state.astype(BF16),
                              preferred_element_type=F32)
        kv = lax.dot_general((k[b] * kdec_ref[...]).astype(BF16), v[b].astype(BF16),
                             (((0,), (0,)), ((), ())), preferred_element_type=F32)
        state_scr[b] = cdec_ref[...] * state + bd_ref[...] * kv

    def head_mean(t):
        hi = t.astype(BF16)
        lo = (t - hi.astype(F32)).astype(BF16)
        return jnp.dot(jnp.concatenate([hi, lo], axis=1), avg_ref[...], preferred_element_type=F32)

    yc = [y[b] - head_mean(y[b]) for b in batches]
    var = [head_mean(yc[b] * yc[b]) for b in batches]
    for b in batches:
        g = vg_ref[b, :, d_ret:]
        o_ref[b] = (_silu(g) * (yc[b] * lax.rsqrt(var[b] + EPS) * gain_ref[...])).astype(BF16)


def _retention(qk, vg, tabs, gain):
    b, s, d2 = qk.shape
    d_ret = d2 // 2
    blk = pl.BlockSpec((b, RET_BLOCK, d2), lambda i: (0, i, 0))
    return pl.pallas_call(
        functools.partial(_ret_kernel, d_ret=d_ret, n_batch=b),
        out_shape=jax.ShapeDtypeStruct((b, s, d_ret), BF16),
        grid=(s // RET_BLOCK,),
        in_specs=[blk, blk] + [_full(t.shape) for t in tabs] + [_full((1, d_ret))],
        out_specs=pl.BlockSpec((b, RET_BLOCK, d_ret), lambda i: (0, i, 0)),
        scratch_shapes=[pltpu.VMEM((b, d_ret, d_ret), F32)],
        compiler_params=_cparams(1),
        name="retention",
    )(qk, vg, *tabs, gain)


def _rotary_tables(seq):
    half = HEAD_DIM // 2
    inv = 1.0 / (ROPE_BASE ** (jnp.arange(half, dtype=F32) / half))
    lane = jnp.arange(LANES)
    ang = jnp.arange(seq).astype(F32)[:, None] * inv[lane % half][None, :]
    sign = jnp.where((lane % HEAD_DIM) < half, -1.0, 1.0).astype(F32)
    return jnp.cos(ang), jnp.sin(ang) * sign[None, :]


def _retention_tables(n_heads):
    d_ret = n_heads * HEAD_DIM
    log_gamma = jnp.log1p(-jnp.exp2(-5.0 - jnp.arange(n_heads, dtype=F32)))
    idx = jnp.arange(RET_BLOCK, dtype=F32)
    t, s = idx[:, None], idx[None, :]
    same = (t // CHUNK) == (s // CHUNK)
    dist = jnp.where(same, jnp.abs(t - s), t - s)
    seen = same | ((s // CHUNK) < (t // CHUNK))
    dmat = jnp.where(seen[None], jnp.exp(log_gamma[:, None, None] * dist[None]), 0.0)
    lane_gamma = jnp.repeat(log_gamma, HEAD_DIM)
    qdec = jnp.exp(lane_gamma[None, :] * (idx + 1.0)[:, None])
    kdec = jnp.exp(lane_gamma[None, :] * (RET_BLOCK - 1.0 - idx)[:, None])
    head = jnp.arange(d_ret) // HEAD_DIM
    bd = (head[:, None] == head[None, :]).astype(F32)
    cdec = bd * jnp.exp(lane_gamma * RET_BLOCK)[:, None]
    avg = jnp.concatenate([bd, bd], axis=0).astype(BF16) * (1.0 / HEAD_DIM)
    return dmat, qdec, kdec, cdec, bd, avg.astype(BF16)


def _tri_table():
    r = np.arange(2 * SB_BLOCK)[:, None] % SB_BLOCK
    c = np.arange(2 * SB_BLOCK)[None, :]
    return jnp.asarray((c >= SB_BLOCK) | (r >= c), dtype=BF16)


def kernel(x, ffn1_norm, ffn1_w_in, ffn1_w_out, mix_norm, mix_w_in, conv_w, conv_b, conv_ln_g,
           conv_ln_b, ret_norm_g, mix_w_out, ffn2_norm, ffn2_w_in, ffn2_w_out, final_norm):
    b, s, d = x.shape
    depth = ffn1_norm.shape[0]
    d_conv = conv_w.shape[2]
    d_ret = ret_norm_g.shape[1]
    d_sb = d - d_conv - d_ret
    assert s % ROW_TILE == 0 and s % RET_BLOCK == 0 and s % SB_QROWS == 0
    assert (b * s) % FFN_TILE == 0
    assert ffn1_w_out.shape[1] % FF_CHUNK == 0 and d % OUT_TILE == 0

    cos_tab, sin_tab = _rotary_tables(s)
    ret_tabs = _retention_tables(d_ret // HEAD_DIM)
    tri = _tri_table()
    scale = HEAD_DIM ** -0.5
    col = jnp.arange(mix_w_in.shape[2])
    q_sb_cols = (col >= 2 * d_conv) & (col < 2 * d_conv + d_sb)
    q_r_cols = (col >= 2 * d_conv + 3 * d_sb) & (col < 2 * d_conv + 3 * d_sb + d_ret)
    col_scale = jnp.where(q_sb_cols | q_r_cols, scale, 1.0).astype(F32)

    row = lambda g: g.reshape(1, -1)
    xf = x.reshape(b * s, d)
    for l in range(depth):
        win1, wout1 = ffn1_w_in[l].astype(BF16), ffn1_w_out[l].astype(BF16)
        win2, wout2 = ffn2_w_in[l].astype(BF16), ffn2_w_out[l].astype(BF16)
        w_mix = (mix_w_in[l] * col_scale[None, :]).astype(BF16)

        xf = _ffn(xf, row(ffn1_norm[l]), win1, wout1, FFN_TILE)
        uconv, q, k, v, qkr, vgr = _inproj(xf, row(mix_norm[l]), w_mix, cos_tab, sin_tab, s,
                                           2 * d_conv, d_sb, d_ret)
        seq3 = lambda t: t.reshape(b, s, t.shape[1])
        y_sb = _stick_breaking(seq3(q), seq3(k), seq3(v), tri)
        y_ret = _retention(seq3(qkr), seq3(vgr), ret_tabs, row(ret_norm_g[l]))
        flat = lambda t: t.reshape(b * s, t.shape[2])
        last = l == depth - 1
        xf = _ffn(xf, row(ffn2_norm[l]), win2, wout2, ROW_TILE,
                  mix=(flat(y_sb), flat(y_ret)), wmix=mix_w_out[l].astype(BF16),
                  final_g=row(final_norm) if last else None,
                  conv=(uconv, s, conv_w[l], row(conv_b[l]), row(conv_ln_g[l]), row(conv_ln_b[l])))
    return xf.reshape(b, s, d)
```

```python
import functools

import jax
import jax.numpy as jnp
import numpy as np
from jax import lax
from jax.experimental import pallas as pl
from jax.experimental.pallas import tpu as pltpu

F32 = jnp.float32
BF16 = jnp.bfloat16

EPS = 1e-6
ROPE_BASE = 10000.0
CHUNK = 64
CONV_WIDTH = 31
HEAD_DIM = 64
LANES = 128
SUBLANES = 8
MXU_DIM = 256

ROW_TILE = 512
FFN_TILE = 1024
FF_CHUNK = MXU_DIM
OUT_TILE = 512
CONV_HALO = 32
CONV_SUB = 128
CONV_CHUNK_GAP = 2
CONV_CHUNK_SPAN = 2
SB_BLOCK = 128
SB_QROWS = 256
SB_TILE = 64
SB_TILES = SB_QROWS // SB_TILE
SB_HEADS_PER_GROUP = LANES // HEAD_DIM
SB_FIRST_BLOCKS = 2
SB_DEAD_LOG = 110.0
SIGN_BIT = 0x80000000
BF16_BITS = 0xFFFF0000
RET_BLOCK = 256
VMEM_LIMIT = 56 * 1024 * 1024


def _cparams(n_axes):
    return pltpu.CompilerParams(dimension_semantics=("arbitrary",) * n_axes,
                                vmem_limit_bytes=VMEM_LIMIT)


def _rms(x, g):
    return (x * lax.rsqrt(jnp.mean(x * x, axis=-1, keepdims=True) + EPS)) * g


def _silu(x):
    return x * jax.nn.sigmoid(x)


def _full(shape):
    return pl.BlockSpec(shape, lambda *_: (0,) * len(shape))


def _resident(shape):
    return pl.BlockSpec(shape, lambda *_: (0,) * len(shape), pipeline_mode=pl.Buffered(1))


def _bit_zero(x):
    bits = lax.bitcast_convert_type(x, jnp.uint32)
    return lax.bitcast_convert_type((bits >> 16) >> 16, F32)


def _conv_fill(u, v_scr, starts_sequence, tile_rows):
    d_conv = v_scr.shape[1]
    if starts_sequence is None:
        v_scr[0:CONV_HALO, :] = jnp.zeros((CONV_HALO, d_conv), F32)
    else:
        v_scr[0:CONV_HALO, :] = jnp.where(starts_sequence, 0.0,
                                          v_scr[tile_rows:tile_rows + CONV_HALO, :])
    v_scr[CONV_HALO:CONV_HALO + tile_rows, :] = u[:, :d_conv] * jax.nn.sigmoid(u[:, d_conv:])
    v_scr[CONV_HALO + tile_rows:, :] = jnp.zeros((SUBLANES, d_conv), F32)


def _conv_rows(v_scr, p_scr, cw_ref, cb_ref, lg_ref, lb_ref, out_ref, r, anchor=None):
    off = CONV_HALO - (CONV_WIDTH - 1)
    cw = cw_ref[...] if anchor is None else cw_ref[...] + anchor
    for shift in range(SUBLANES):
        part = None
        for o in range(shift, off + CONV_WIDTH, SUBLANES):
            if o < off:
                continue
            rows = v_scr[pl.ds(r * CONV_SUB + o - shift, CONV_SUB + SUBLANES), :]
            term = cw[o - off:o - off + 1, :] * rows
            part = term if part is None else part + term
        p_scr[r % 2, shift] = part
    y = cb_ref[...] + p_scr[r % 2, 0, 0:CONV_SUB, :]
    for shift in range(1, SUBLANES):
        y = y + p_scr[r % 2, shift, pl.ds(shift, CONV_SUB), :]
    mu = jnp.mean(y, axis=-1, keepdims=True)
    yc = y - mu
    var = jnp.mean(yc * yc, axis=-1, keepdims=True)
    ln = yc * lax.rsqrt(var + EPS) * lg_ref[...] + lb_ref[...]
    out = _silu(ln)
    out_ref[r * CONV_SUB:(r + 1) * CONV_SUB, :] = out.astype(BF16)
    return out[0:1, :]


def _ffn_kernel(*refs, tile, n_chunks, n_out, n_mix, final, conv_tiles_per_seq):
    refs = list(refs)
    x_ref = refs.pop(0)
    mix_refs = [refs.pop(0) for _ in range(n_mix)]
    wmix_ref = refs.pop(0) if n_mix else None
    g_ref, win_ref, wout_ref = refs.pop(0), refs.pop(0), refs.pop(0)
    fg_ref = refs.pop(0) if final else None
    has_conv = conv_tiles_per_seq is not None
    if has_conv:
        u_first_ref, u_next_ref = refs.pop(0), refs.pop(0)
        conv_params = [refs.pop(0) for _ in range(4)]
    o_ref, h_scr, a_scr = refs.pop(0), refs.pop(0), refs.pop(0)
    out_tiles = [slice(m * OUT_TILE, (m + 1) * OUT_TILE) for m in range(n_out)]
    d_ff = wout_ref.shape[0]
    n_sub = tile // CONV_SUB
    if has_conv:
        v_scr, p_scr, yc_scr = refs.pop(0), refs.pop(0), refs.pop(0)
        step = pl.program_id(0)

        @pl.when(step == 0)
        def _():
            _conv_fill(u_first_ref[...], v_scr, None, tile)
            for r in range(n_sub):
                _conv_rows(v_scr, p_scr, *conv_params, yc_scr, r)

        mix_refs = [yc_scr] + mix_refs
    if mix_refs:
        for sl in out_tiles:
            y, lo = None, 0
            for y_ref in mix_refs:
                part = jnp.dot(y_ref[...], wmix_ref[lo:lo + y_ref.shape[1], sl],
                               preferred_element_type=F32)
                y = part if y is None else y + part
                lo += y_ref.shape[1]
            o_ref[:, sl] = x_ref[:, sl] + y
        x_ref = o_ref
    h_scr[...] = _rms(x_ref[...], g_ref[...]).astype(BF16)
    if has_conv:
        nxt = jnp.minimum(step + 1, pl.num_programs(0) - 1)
        _conv_fill(u_next_ref[...], v_scr, nxt % conv_tiles_per_seq == 0, tile)
    done = {}
    for j in range(n_chunks):
        sl = slice(j * FF_CHUNK, (j + 1) * FF_CHUNK)
        up_sl = slice(d_ff + j * FF_CHUNK, d_ff + (j + 1) * FF_CHUNK)
        gate = jnp.dot(h_scr[...], win_ref[:, sl], preferred_element_type=F32)
        up = jnp.dot(h_scr[...], win_ref[:, up_sl], preferred_element_type=F32)
        if j in done:
            gate = gate + done.pop(j)
        a_scr[:, sl] = (_silu(gate) * up).astype(BF16)
        if has_conv and j % CONV_CHUNK_GAP == 0 and j // CONV_CHUNK_GAP < n_sub:
            r = j // CONV_CHUNK_GAP
            out_row = _conv_rows(v_scr, p_scr, *conv_params, yc_scr, r,
                                 anchor=_bit_zero(gate[0:1, :conv_params[0].shape[1]]))
            done[j + CONV_CHUNK_SPAN] = _bit_zero(out_row)
    assert not done
    for sl in out_tiles:
        y = jnp.dot(a_scr[...], wout_ref[:, sl], preferred_element_type=F32)
        o_ref[:, sl] = x_ref[:, sl] + 0.5 * y
    if final:
        o_ref[...] = _rms(o_ref[...], fg_ref[...])


def _ffn(x, g, win, wout, tile, mix=(), wmix=None, final_g=None, conv=None):
    n, d = x.shape
    d_ff = wout.shape[0]
    n_chunks = d_ff // FF_CHUNK
    n_tiles = n // tile
    final = final_g is not None
    row = lambda i: (i, 0)
    in_specs = [pl.BlockSpec((tile, d), row)]
    in_specs += [pl.BlockSpec((tile, y.shape[1]), row) for y in mix]
    args = [x, *mix]
    if mix:
        in_specs.append(_resident(wmix.shape))
        args.append(wmix)
    in_specs += [_full((1, d)), _resident(win.shape), _resident(wout.shape)]
    args += [g, win, wout]
    if final:
        in_specs.append(_full((1, d)))
        args.append(final_g)
    scratch = [pltpu.VMEM((tile, d), BF16), pltpu.VMEM((tile, d_ff), BF16)]
    conv_tiles_per_seq = None
    if conv is not None:
        u, seq, cw, cb, lg, lb = conv
        d_conv = cw.shape[1]
        assert seq % tile == 0 and tile % CONV_SUB == 0 and CONV_HALO >= CONV_WIDTH - 1
        conv_tiles_per_seq = seq // tile
        in_specs += [pl.BlockSpec((tile, u.shape[1]), lambda i: (0, 0), pipeline_mode=pl.Buffered(1)),
                     pl.BlockSpec((tile, u.shape[1]), lambda i: (jnp.minimum(i + 1, n_tiles - 1), 0)),
                     _full(cw.shape), _full((1, d_conv)), _full((1, d_conv)), _full((1, d_conv))]
        args += [u, u, cw, cb, lg, lb]
        scratch += [pltpu.VMEM((CONV_HALO + tile + SUBLANES, d_conv), F32),
                    pltpu.VMEM((2, SUBLANES, CONV_SUB + SUBLANES, d_conv), F32),
                    pltpu.VMEM((tile, d_conv), BF16)]
    return pl.pallas_call(
        functools.partial(_ffn_kernel, tile=tile, n_chunks=n_chunks, n_out=d // OUT_TILE,
                          n_mix=len(mix), final=final, conv_tiles_per_seq=conv_tiles_per_seq),
        out_shape=jax.ShapeDtypeStruct((n, d), F32),
        grid=(n_tiles,),
        in_specs=in_specs,
        out_specs=pl.BlockSpec((tile, d), row),
        scratch_shapes=scratch,
        compiler_params=_cparams(1),
        name="ffn",
    )(*args)


def _rotary(t, cos, sin_signed, first_half):
    partner = jnp.where(first_half, pltpu.roll(t, LANES - HEAD_DIM // 2, 1),
                        pltpu.roll(t, HEAD_DIM // 2, 1))
    return t * cos + partner * sin_signed


def _inproj_kernel(x_ref, g_ref, w_ref, cos_ref, sin_ref, uconv_ref, q_ref, k_ref, v_ref,
                   qkr_ref, vgr_ref, h_scr, *, d_conv2, d_sb, d_ret):
    h_scr[...] = _rms(x_ref[...], g_ref[...]).astype(BF16)

    def proj(lo, width):
        return jnp.dot(h_scr[...], w_ref[:, lo:lo + width], preferred_element_type=F32)

    uconv_ref[...] = proj(0, d_conv2)
    o = d_conv2
    q_ref[...] = proj(o, d_sb).astype(BF16)
    k_ref[...] = proj(o + d_sb, d_sb).astype(BF16)
    v_ref[...] = proj(o + 2 * d_sb, d_sb).astype(BF16)
    o += 3 * d_sb
    qk = proj(o, 2 * d_ret)
    cos = cos_ref[...]
    sin = sin_ref[...]
    lane = lax.broadcasted_iota(jnp.int32, (1, LANES), 1)
    first_half = (lane % HEAD_DIM) < (HEAD_DIM // 2)
    for c in range(2 * d_ret // LANES):
        sl = slice(c * LANES, (c + 1) * LANES)
        qkr_ref[:, sl] = _rotary(qk[:, sl], cos, sin, first_half)
    vgr_ref[...] = proj(o + 2 * d_ret, 2 * d_ret)


def _inproj(x, g, w, cos_tab, sin_tab, seq, d_conv2, d_sb, d_ret):
    n, d = x.shape
    tiles_per_seq = seq // FFN_TILE
    row = lambda i: (i, 0)
    pos = lambda i: (i % tiles_per_seq, 0)
    outs = [jax.ShapeDtypeStruct((n, d_conv2), F32)] + [jax.ShapeDtypeStruct((n, d_sb), BF16)] * 3 \
        + [jax.ShapeDtypeStruct((n, 2 * d_ret), F32)] * 2
    return pl.pallas_call(
        functools.partial(_inproj_kernel, d_conv2=d_conv2, d_sb=d_sb, d_ret=d_ret),
        out_shape=outs,
        grid=(n // FFN_TILE,),
        in_specs=[pl.BlockSpec((FFN_TILE, d), row), _full((1, d)), _resident(w.shape),
                  pl.BlockSpec((FFN_TILE, LANES), pos), pl.BlockSpec((FFN_TILE, LANES), pos)],
        out_specs=[pl.BlockSpec((FFN_TILE, s.shape[1]), row) for s in outs],
        scratch_shapes=[pltpu.VMEM((FFN_TILE, d), BF16)],
        compiler_params=_cparams(1),
        name="mixer_inproj",
    )(x, g, w, cos_tab, sin_tab)


def _sb_kernel(q_ref, k_ref, v_ref, tri_ref, o_ref, qs_scr, acc_scr, carry_scr, *, n_pairs):
    q0 = pl.program_id(1) * SB_QROWS
    tiles, pairs = range(SB_TILES), range(n_pairs)
    rows = SB_HEADS_PER_GROUP * SB_TILE
    lane = lax.broadcasted_iota(jnp.int32, (1, LANES), 1)
    col = lax.broadcasted_iota(jnp.int32, (rows, SB_BLOCK), 1)
    row = lax.broadcasted_iota(jnp.int32, (rows, SB_BLOCK), 0) % SB_TILE
    for p in pairs:
        q = q_ref[0, :, p * LANES:(p + 1) * LANES]
        for t in tiles:
            qt = q[t * SB_TILE:(t + 1) * SB_TILE]
            qs_scr[t, p] = jnp.concatenate(
                [jnp.where(lane // HEAD_DIM == hh, qt, jnp.zeros_like(qt))
                 for hh in range(SB_HEADS_PER_GROUP)], axis=0)

    def sweep(blocks, resume):
        depth = range(len(blocks[0]))

        def rows_of(ref, t, n, p):
            return ref[0, pl.ds(blocks[t][n][0], SB_BLOCK), p * LANES:(p + 1) * LANES]

        units = [(t, n) for t in tiles for n in depth]

        def logits(t, n):
            return [lax.dot_general(qs_scr[t, p], rows_of(k_ref, t, n, p),
                                    (((1,), (1,)), ((), ())), preferred_element_type=F32)
                    for p in pairs]

        def cumsums(t, n, zs):
            hls = []
            for p in pairs:
                zbits = lax.bitcast_convert_type(zs[p], jnp.uint32)
                neg_abs = lax.bitcast_convert_type(zbits | jnp.uint32(SIGN_BIT), F32)
                sp = jnp.maximum(zs[p], 0.0) + jnp.log(1.0 + jnp.exp(neg_abs))
                if blocks[t][n][1] is not None:
                    sp = jnp.where(blocks[t][n][1], sp, 0.0)
                hi = lax.bitcast_convert_type(
                    lax.bitcast_convert_type(sp, jnp.uint32) & jnp.uint32(BF16_BITS), F32)
                hls.append(jnp.concatenate([hi.astype(BF16), (sp - hi).astype(BF16)], axis=1))
            return jnp.dot(jnp.concatenate(hls, axis=0), tri_ref[...], preferred_element_type=F32)

        carry = {(t, p): carry_scr[t, p] if resume else None for t in tiles for p in pairs}
        acc = {(t, p): acc_scr[t, p] if resume else None for t in tiles for p in pairs}

        def weigh(t, n, zs, rt):
            for p in pairs:
                within = rt[p * rows:(p + 1) * rows, :SB_BLOCK]
                total = rt[p * rows:(p + 1) * rows, SB_BLOCK:]
                before = carry[t, p]
                w = jnp.exp(zs[p] - (within if before is None else within + before))
                if blocks[t][n][1] is not None:
                    w = jnp.where(blocks[t][n][1], w, 0.0)
                carry[t, p] = total if before is None else before + total
                pv = jnp.dot(w.astype(BF16), rows_of(v_ref, t, n, p), preferred_element_type=F32)
                acc[t, p] = pv if acc[t, p] is None else acc[t, p] + pv

        zs, rts = {}, {}
        for u in range(len(units) + 2):
            if u < len(units):
                zs[u] = logits(*units[u])
            if 1 <= u <= len(units):
                rts[u - 1] = cumsums(*units[u - 1], zs[u - 1])
            if u >= 2:
                weigh(*units[u - 2], zs.pop(u - 2), rts.pop(u - 2))
        lowest = None
        for t in tiles:
            for p in pairs:
                carry_scr[t, p] = carry[t, p]
                acc_scr[t, p] = acc[t, p]
                lowest = carry[t, p] if lowest is None else jnp.minimum(lowest, carry[t, p])
        return jnp.min(lowest)

    def aligned(start):
        return pl.multiple_of(start, SB_TILE)

    def first_fast():
        near_mask = col < row + (SB_BLOCK - SB_TILE)
        return sweep([[(aligned(q0 + (t + 1) * SB_TILE - (n + 1) * SB_BLOCK),
                        near_mask if n == 0 else None) for n in range(SB_FIRST_BLOCKS)]
                      for t in tiles], False)

    def first_clamped():
        blocks = []
        for t in tiles:
            limit = q0 + t * SB_TILE + row
            blocks.append([])
            for n in range(SB_FIRST_BLOCKS):
                start = aligned(jnp.maximum(q0 + (t + 1) * SB_TILE - (n + 1) * SB_BLOCK, 0))
                blocks[t].append((start, start + col < limit))
                limit = start
        return sweep(blocks, False)

    def more(st):
        reach = q0 + SB_QROWS - st[0] * SB_BLOCK
        return jnp.logical_and(reach > 0, st[1] < SB_DEAD_LOG)

    def body(st):
        blocks = []
        for t in tiles:
            limit = jnp.maximum(q0 + (t + 1) * SB_TILE - st[0] * SB_BLOCK, 0)
            start = aligned(jnp.maximum(limit - SB_BLOCK, 0))
            blocks.append([(start, start + col < limit)])
        return st[0] + 1, sweep(blocks, True)

    fits = q0 + SB_TILE - SB_FIRST_BLOCKS * SB_BLOCK >= 0
    lowest = lax.cond(fits, first_fast, first_clamped)
    lax.while_loop(more, body, (SB_FIRST_BLOCKS, lowest))
    for p in pairs:
        for t in tiles:
            acc = acc_scr[t, p]
            out = acc[0:SB_TILE]
            for hh in range(1, SB_HEADS_PER_GROUP):
                out = jnp.where(lane // HEAD_DIM == hh, acc[hh * SB_TILE:(hh + 1) * SB_TILE], out)
            o_ref[0, t * SB_TILE:(t + 1) * SB_TILE, p * LANES:(p + 1) * LANES] = out.astype(BF16)


def _stick_breaking(q, k, v, tri):
    b, s, d_sb = q.shape
    n_pairs = d_sb // LANES
    rows = SB_HEADS_PER_GROUP * SB_TILE
    blk = pl.BlockSpec((1, SB_QROWS, d_sb), lambda bi, i: (bi, i, 0))
    whole = pl.BlockSpec((1, s, d_sb), lambda bi, i: (bi, 0, 0))
    return pl.pallas_call(
        functools.partial(_sb_kernel, n_pairs=n_pairs),
        out_shape=jax.ShapeDtypeStruct((b, s, d_sb), BF16),
        grid=(b, s // SB_QROWS),
        in_specs=[blk, whole, whole, _full(tri.shape)],
        out_specs=blk,
        scratch_shapes=[pltpu.VMEM((SB_TILES, n_pairs, rows, LANES), BF16),
                        pltpu.VMEM((SB_TILES, n_pairs, rows, LANES), F32),
                        pltpu.VMEM((SB_TILES, n_pairs, rows, LANES), F32)],
        compiler_params=_cparams(2),
        name="stick_breaking",
    )(q, k, v, tri)


def _ret_kernel(qk_ref, vg_ref, dmat_ref, qdec_ref, kdec_ref, cdec_ref, bd_ref, avg_ref, gain_ref,
                o_ref, state_scr, *, d_ret, n_batch):
    n_heads = d_ret // HEAD_DIM
    batches = range(n_batch)
    nt = (((1,), (1,)), ((), ()))

    @pl.when(pl.program_id(0) == 0)
    def _():
        state_scr[...] = jnp.zeros_like(state_scr)

    lane = lax.broadcasted_iota(jnp.int32, (1, d_ret), 1)
    in_head = [lane // HEAD_DIM == h for h in range(n_heads)]
    q = [qk_ref[b, :, :d_ret] for b in batches]
    k = [qk_ref[b, :, d_ret:] for b in batches]
    v = [vg_ref[b, :, :d_ret] for b in batches]
    probs = []
    for b in batches:
        kb = k[b].astype(BF16)
        probs.append(jnp.concatenate(
            [(lax.dot_general(jnp.where(in_head[h], q[b], 0.0).astype(BF16), kb, nt,
                              preferred_element_type=F32) * dmat_ref[h]).astype(BF16)
             for h in range(n_heads)], axis=1))
    y = []
    for b in batches:
        vals = jnp.concatenate([jnp.where(in_head[h], v[b], 0.0).astype(BF16)
                                for h in range(n_heads)], axis=0)
        y.append(jnp.dot(probs[b], vals, preferred_element_type=F32))
    for b in batches:
        state = state_scr[b]
        y[b] = y[b] + jnp.dot((q[b] * qdec_ref[...]).astype(BF16), state.astype(BF16),
                              preferred_element_type=F32)
        kv = lax.dot_general((k[b] * kdec_ref[...]).astype(BF16), v[b].astype(BF16),
                             (((0,), (0,)), ((), ())), preferred_element_type=F32)
        state_scr[b] = cdec_ref[...] * state + bd_ref[...] * kv

    def head_mean(t):
        hi = t.astype(BF16)
        lo = (t - hi.astype(F32)).astype(BF16)
        return jnp.dot(jnp.concatenate([hi, lo], axis=1), avg_ref[...], preferred_element_type=F32)

    yc = [y[b] - head_mean(y[b]) for b in batches]
    var = [head_mean(yc[b] * yc[b]) for b in batches]
    for b in batches:
        g = vg_ref[b, :, d_ret:]
        o_ref[b] = (_silu(g) * (yc[b] * lax.rsqrt(var[b] + EPS) * gain_ref[...])).astype(BF16)


def _retention(qk, vg, tabs, gain):
    b, s, d2 = qk.shape
    d_ret = d2 // 2
    blk = pl.BlockSpec((b, RET_BLOCK, d2), lambda i: (0, i, 0))
    return pl.pallas_call(
        functools.partial(_ret_kernel, d_ret=d_ret, n_batch=b),
        out_shape=jax.ShapeDtypeStruct((b, s, d_ret), BF16),
        grid=(s // RET_BLOCK,),
        in_specs=[blk, blk] + [_full(t.shape) for t in tabs] + [_full((1, d_ret))],
        out_specs=pl.BlockSpec((b, RET_BLOCK, d_ret), lambda i: (0, i, 0)),
        scratch_shapes=[pltpu.VMEM((b, d_ret, d_ret), F32)],
        compiler_params=_cparams(1),
        name="retention",
    )(qk, vg, *tabs, gain)


def _rotary_tables(seq):
    half = HEAD_DIM // 2
    inv = 1.0 / (ROPE_BASE ** (jnp.arange(half, dtype=F32) / half))
    lane = jnp.arange(LANES)
    ang = jnp.arange(seq).astype(F32)[:, None] * inv[lane % half][None, :]
    sign = jnp.where((lane % HEAD_DIM) < half, -1.0, 1.0).astype(F32)
    return jnp.cos(ang), jnp.sin(ang) * sign[None, :]


def _retention_tables(n_heads):
    d_ret = n_heads * HEAD_DIM
    log_gamma = jnp.log1p(-jnp.exp2(-5.0 - jnp.arange(n_heads, dtype=F32)))
    idx = jnp.arange(RET_BLOCK, dtype=F32)
    t, s = idx[:, None], idx[None, :]
    same = (t // CHUNK) == (s // CHUNK)
    dist = jnp.where(same, jnp.abs(t - s), t - s)
    seen = same | ((s // CHUNK) < (t // CHUNK))
    dmat = jnp.where(seen[None], jnp.exp(log_gamma[:, None, None] * dist[None]), 0.0)
    lane_gamma = jnp.repeat(log_gamma, HEAD_DIM)
    qdec = jnp.exp(lane_gamma[None, :] * (idx + 1.0)[:, None])
    kdec = jnp.exp(lane_gamma[None, :] * (RET_BLOCK - 1.0 - idx)[:, None])
    head = jnp.arange(d_ret) // HEAD_DIM
    bd = (head[:, None] == head[None, :]).astype(F32)
    cdec = bd * jnp.exp(lane_gamma * RET_BLOCK)[:, None]
    avg = jnp.concatenate([bd, bd], axis=0).astype(BF16) * (1.0 / HEAD_DIM)
    return dmat, qdec, kdec, cdec, bd, avg.astype(BF16)


def _tri_table():
    r = np.arange(2 * SB_BLOCK)[:, None] % SB_BLOCK
    c = np.arange(2 * SB_BLOCK)[None, :]
    return jnp.asarray((c >= SB_BLOCK) | (r >= c), dtype=BF16)


def kernel(x, ffn1_norm, ffn1_w_in, ffn1_w_out, mix_norm, mix_w_in, conv_w, conv_b, conv_ln_g,
           conv_ln_b, ret_norm_g, mix_w_out, ffn2_norm, ffn2_w_in, ffn2_w_out, final_norm):
    b, s, d = x.shape
    depth = ffn1_norm.shape[0]
    d_conv = conv_w.shape[2]
    d_ret = ret_norm_g.shape[1]
    d_sb = d - d_conv - d_ret
    assert s % ROW_TILE == 0 and s % FFN_TILE == 0 and s % RET_BLOCK == 0 and s % SB_QROWS == 0
    assert ffn1_w_out.shape[1] % FF_CHUNK == 0 and d % OUT_TILE == 0

    cos_tab, sin_tab = _rotary_tables(s)
    ret_tabs = _retention_tables(d_ret // HEAD_DIM)
    tri = _tri_table()
    scale = HEAD_DIM ** -0.5
    col = jnp.arange(mix_w_in.shape[2])
    q_sb_cols = (col >= 2 * d_conv) & (col < 2 * d_conv + d_sb)
    q_r_cols = (col >= 2 * d_conv + 3 * d_sb) & (col < 2 * d_conv + 3 * d_sb + d_ret)
    col_scale = jnp.where(q_sb_cols | q_r_cols, scale, 1.0).astype(F32)

    row = lambda g: g.reshape(1, -1)
    xf = x.reshape(b * s, d)
    for l in range(depth):
        win1, wout1 = ffn1_w_in[l].astype(BF16), ffn1_w_out[l].astype(BF16)
        win2, wout2 = ffn2_w_in[l].astype(BF16), ffn2_w_out[l].astype(BF16)
        w_mix = (mix_w_in[l] * col_scale[None, :]).astype(BF16)

        xf = _ffn(xf, row(ffn1_norm[l]), win1, wout1, FFN_TILE)
        uconv, q, k, v, qkr, vgr = _inproj(xf, row(mix_norm[l]), w_mix, cos_tab, sin_tab, s,
                                           2 * d_conv, d_sb, d_ret)
        seq3 = lambda t: t.reshape(b, s, t.shape[1])
        y_sb = _stick_breaking(seq3(q), seq3(k), seq3(v), tri)
        y_ret = _retention(seq3(qkr), seq3(vgr), ret_tabs, row(ret_norm_g[l]))
        flat = lambda t: t.reshape(b * s, t.shape[2])
        last = l == depth - 1
        xf = _ffn(xf, row(ffn2_norm[l]), win2, wout2, ROW_TILE,
                  mix=(flat(y_sb), flat(y_ret)), wmix=mix_w_out[l].astype(BF16),
                  final_g=row(final_norm) if last else None,
                  conv=(uconv, s, conv_w[l], row(conv_b[l]), row(conv_ln_g[l]), row(conv_ln_b[l])))
    return xf.reshape(b, s, d)
```

```python
import functools

import jax
import jax.numpy as jnp
import numpy as np
from jax import lax
from jax.experimental import pallas as pl
from jax.experimental.pallas import tpu as pltpu

F32 = jnp.float32
BF16 = jnp.bfloat16

EPS = 1e-6
ROPE_BASE = 10000.0
CHUNK = 64
CONV_WIDTH = 31
HEAD_DIM = 64
LANES = 128
SUBLANES = 8
MXU_DIM = 256

ROW_TILE = 512
FFN_TILE = 1024
FF_CHUNK = MXU_DIM
OUT_TILE = 512
CONV_HALO = 32
CONV_SUBTILES = 8
CONV_CHUNK_GAP = 1
CONV_CHUNK_SPAN = 2
SB_BLOCK = 128
SB_QROWS = 256
SB_TILE = 64
SB_TILES = SB_QROWS // SB_TILE
SB_HEADS_PER_GROUP = LANES // HEAD_DIM
SB_FIRST_BLOCKS = 2
SB_DEAD_LOG = 110.0
SIGN_BIT = 0x80000000
BF16_BITS = 0xFFFF0000
RET_BLOCK = 256
VMEM_LIMIT = 56 * 1024 * 1024


def _cparams(n_axes):
    return pltpu.CompilerParams(dimension_semantics=("arbitrary",) * n_axes,
                                vmem_limit_bytes=VMEM_LIMIT)


def _rms(x, g):
    return (x * lax.rsqrt(jnp.mean(x * x, axis=-1, keepdims=True) + EPS)) * g


def _silu(x):
    return x * jax.nn.sigmoid(x)


def _full(shape):
    return pl.BlockSpec(shape, lambda *_: (0,) * len(shape))


def _resident(shape):
    return pl.BlockSpec(shape, lambda *_: (0,) * len(shape), pipeline_mode=pl.Buffered(1))


def _bit_zero(x):
    bits = lax.bitcast_convert_type(x, jnp.uint32)
    return lax.bitcast_convert_type((bits >> 16) >> 16, F32)


def _conv_fill(u, v_scr, starts_sequence, tile_rows):
    d_conv = v_scr.shape[1]
    if starts_sequence is None:
        v_scr[0:CONV_HALO, :] = jnp.zeros((CONV_HALO, d_conv), F32)
    else:
        v_scr[0:CONV_HALO, :] = jnp.where(starts_sequence, 0.0,
                                          v_scr[tile_rows:tile_rows + CONV_HALO, :])
    v_scr[CONV_HALO:CONV_HALO + tile_rows, :] = u[:, :d_conv] * jax.nn.sigmoid(u[:, d_conv:])
    v_scr[CONV_HALO + tile_rows:, :] = jnp.zeros((SUBLANES, d_conv), F32)


def _conv_rows(v_scr, p_scr, cw_ref, cb_ref, lg_ref, lb_ref, out_ref, r, anchor=None):
    off = CONV_HALO - (CONV_WIDTH - 1)
    sub = p_scr.shape[2] - SUBLANES
    cw = cw_ref[...] if anchor is None else cw_ref[...] + anchor
    for shift in range(SUBLANES):
        part = None
        for o in range(shift, off + CONV_WIDTH, SUBLANES):
            if o < off:
                continue
            rows = v_scr[pl.ds(r * sub + o - shift, sub + SUBLANES), :]
            term = cw[o - off:o - off + 1, :] * rows
            part = term if part is None else part + term
        p_scr[r % 2, shift] = part
    y = cb_ref[...] + p_scr[r % 2, 0, 0:sub, :]
    for shift in range(1, SUBLANES):
        y = y + p_scr[r % 2, shift, pl.ds(shift, sub), :]
    mu = jnp.mean(y, axis=-1, keepdims=True)
    yc = y - mu
    var = jnp.mean(yc * yc, axis=-1, keepdims=True)
    ln = yc * lax.rsqrt(var + EPS) * lg_ref[...] + lb_ref[...]
    out = _silu(ln)
    out_ref[r * sub:(r + 1) * sub, :] = out.astype(BF16)
    return out[0:1, :]


def _ffn_kernel(*refs, tile, n_chunks, n_out, n_mix, final, conv_tiles_per_seq):
    refs = list(refs)
    x_ref = refs.pop(0)
    mix_refs = [refs.pop(0) for _ in range(n_mix)]
    wmix_ref = refs.pop(0) if n_mix else None
    g_ref, win_ref, wout_ref = refs.pop(0), refs.pop(0), refs.pop(0)
    fg_ref = refs.pop(0) if final else None
    has_conv = conv_tiles_per_seq is not None
    if has_conv:
        u_first_ref, u_next_ref = refs.pop(0), refs.pop(0)
        conv_params = [refs.pop(0) for _ in range(4)]
    o_ref, h_scr, a_scr = refs.pop(0), refs.pop(0), refs.pop(0)
    out_tiles = [slice(m * OUT_TILE, (m + 1) * OUT_TILE) for m in range(n_out)]
    d_ff = wout_ref.shape[0]
    n_sub = CONV_SUBTILES
    if has_conv:
        v_scr, p_scr, yc_scr = refs.pop(0), refs.pop(0), refs.pop(0)
        step = pl.program_id(0)

        @pl.when(step == 0)
        def _():
            _conv_fill(u_first_ref[...], v_scr, None, tile)
            for r in range(n_sub):
                _conv_rows(v_scr, p_scr, *conv_params, yc_scr, r)

        mix_refs = [yc_scr] + mix_refs
    if mix_refs:
        for sl in out_tiles:
            y, lo = None, 0
            for y_ref in mix_refs:
                part = jnp.dot(y_ref[...], wmix_ref[lo:lo + y_ref.shape[1], sl],
                               preferred_element_type=F32)
                y = part if y is None else y + part
                lo += y_ref.shape[1]
            o_ref[:, sl] = x_ref[:, sl] + y
        x_ref = o_ref
    h_scr[...] = _rms(x_ref[...], g_ref[...]).astype(BF16)
    if has_conv:
        nxt = jnp.minimum(step + 1, pl.num_programs(0) - 1)
        _conv_fill(u_next_ref[...], v_scr, nxt % conv_tiles_per_seq == 0, tile)
    done = {}
    for j in range(n_chunks):
        sl = slice(j * FF_CHUNK, (j + 1) * FF_CHUNK)
        up_sl = slice(d_ff + j * FF_CHUNK, d_ff + (j + 1) * FF_CHUNK)
        gate = jnp.dot(h_scr[...], win_ref[:, sl], preferred_element_type=F32)
        up = jnp.dot(h_scr[...], win_ref[:, up_sl], preferred_element_type=F32)
        if j in done:
            gate = gate + done.pop(j)
        a_scr[:, sl] = (_silu(gate) * up).astype(BF16)
        if has_conv and j % CONV_CHUNK_GAP == 0 and j // CONV_CHUNK_GAP < n_sub:
            r = j // CONV_CHUNK_GAP
            out_row = _conv_rows(v_scr, p_scr, *conv_params, yc_scr, r,
                                 anchor=_bit_zero(gate[0:1, :conv_params[0].shape[1]]))
            done[j + CONV_CHUNK_SPAN] = _bit_zero(out_row)
    assert not done
    for sl in out_tiles:
        y = jnp.dot(a_scr[...], wout_ref[:, sl], preferred_element_type=F32)
        o_ref[:, sl] = x_ref[:, sl] + 0.5 * y
    if final:
        o_ref[...] = _rms(o_ref[...], fg_ref[...])


def _ffn(x, g, win, wout, tile, mix=(), wmix=None, final_g=None, conv=None):
    n, d = x.shape
    d_ff = wout.shape[0]
    n_chunks = d_ff // FF_CHUNK
    n_tiles = n // tile
    final = final_g is not None
    row = lambda i: (i, 0)
    in_specs = [pl.BlockSpec((tile, d), row)]
    in_specs += [pl.BlockSpec((tile, y.shape[1]), row) for y in mix]
    args = [x, *mix]
    if mix:
        in_specs.append(_resident(wmix.shape))
        args.append(wmix)
    in_specs += [_full((1, d)), _resident(win.shape), _resident(wout.shape)]
    args += [g, win, wout]
    if final:
        in_specs.append(_full((1, d)))
        args.append(final_g)
    scratch = [pltpu.VMEM((tile, d), BF16), pltpu.VMEM((tile, d_ff), BF16)]
    conv_tiles_per_seq = None
    if conv is not None:
        u, seq, cw, cb, lg, lb = conv
        d_conv = cw.shape[1]
        assert seq % tile == 0 and tile % (CONV_SUBTILES * SUBLANES) == 0
        assert CONV_HALO >= CONV_WIDTH - 1
        conv_tiles_per_seq = seq // tile
        in_specs += [pl.BlockSpec((tile, u.shape[1]), lambda i: (0, 0), pipeline_mode=pl.Buffered(1)),
                     pl.BlockSpec((tile, u.shape[1]), lambda i: (jnp.minimum(i + 1, n_tiles - 1), 0)),
                     _full(cw.shape), _full((1, d_conv)), _full((1, d_conv)), _full((1, d_conv))]
        args += [u, u, cw, cb, lg, lb]
        scratch += [pltpu.VMEM((CONV_HALO + tile + SUBLANES, d_conv), F32),
                    pltpu.VMEM((2, SUBLANES, tile // CONV_SUBTILES + SUBLANES, d_conv), F32),
                    pltpu.VMEM((tile, d_conv), BF16)]
    return pl.pallas_call(
        functools.partial(_ffn_kernel, tile=tile, n_chunks=n_chunks, n_out=d // OUT_TILE,
                          n_mix=len(mix), final=final, conv_tiles_per_seq=conv_tiles_per_seq),
        out_shape=jax.ShapeDtypeStruct((n, d), F32),
        grid=(n_tiles,),
        in_specs=in_specs,
        out_specs=pl.BlockSpec((tile, d), row),
        scratch_shapes=scratch,
        compiler_params=_cparams(1),
        name="ffn",
    )(*args)


def _rotary(t, cos, sin_signed, first_half):
    partner = jnp.where(first_half, pltpu.roll(t, LANES - HEAD_DIM // 2, 1),
                        pltpu.roll(t, HEAD_DIM // 2, 1))
    return t * cos + partner * sin_signed


def _inproj_kernel(x_ref, g_ref, w_ref, cos_ref, sin_ref, uconv_ref, q_ref, k_ref, v_ref,
                   qkr_ref, vgr_ref, h_scr, *, d_conv2, d_sb, d_ret):
    h_scr[...] = _rms(x_ref[...], g_ref[...]).astype(BF16)

    def proj(lo, width):
        return jnp.dot(h_scr[...], w_ref[:, lo:lo + width], preferred_element_type=F32)

    uconv_ref[...] = proj(0, d_conv2)
    o = d_conv2
    q_ref[...] = proj(o, d_sb).astype(BF16)
    k_ref[...] = proj(o + d_sb, d_sb).astype(BF16)
    v_ref[...] = proj(o + 2 * d_sb, d_sb).astype(BF16)
    o += 3 * d_sb
    qk = proj(o, 2 * d_ret)
    cos = cos_ref[...]
    sin = sin_ref[...]
    lane = lax.broadcasted_iota(jnp.int32, (1, LANES), 1)
    first_half = (lane % HEAD_DIM) < (HEAD_DIM // 2)
    for c in range(2 * d_ret // LANES):
        sl = slice(c * LANES, (c + 1) * LANES)
        qkr_ref[:, sl] = _rotary(qk[:, sl], cos, sin, first_half)
    vgr_ref[...] = proj(o + 2 * d_ret, 2 * d_ret)


def _inproj(x, g, w, cos_tab, sin_tab, seq, d_conv2, d_sb, d_ret):
    n, d = x.shape
    tiles_per_seq = seq // FFN_TILE
    row = lambda i: (i, 0)
    pos = lambda i: (i % tiles_per_seq, 0)
    outs = [jax.ShapeDtypeStruct((n, d_conv2), F32)] + [jax.ShapeDtypeStruct((n, d_sb), BF16)] * 3 \
        + [jax.ShapeDtypeStruct((n, 2 * d_ret), F32)] * 2
    return pl.pallas_call(
        functools.partial(_inproj_kernel, d_conv2=d_conv2, d_sb=d_sb, d_ret=d_ret),
        out_shape=outs,
        grid=(n // FFN_TILE,),
        in_specs=[pl.BlockSpec((FFN_TILE, d), row), _full((1, d)), _resident(w.shape),
                  pl.BlockSpec((FFN_TILE, LANES), pos), pl.BlockSpec((FFN_TILE, LANES), pos)],
        out_specs=[pl.BlockSpec((FFN_TILE, s.shape[1]), row) for s in outs],
        scratch_shapes=[pltpu.VMEM((FFN_TILE, d), BF16)],
        compiler_params=_cparams(1),
        name="mixer_inproj",
    )(x, g, w, cos_tab, sin_tab)


def _sb_kernel(q_ref, k_ref, v_ref, tri_ref, o_ref, qs_scr, acc_scr, carry_scr, *, n_pairs):
    q0 = pl.program_id(1) * SB_QROWS
    tiles, pairs = range(SB_TILES), range(n_pairs)
    rows = SB_HEADS_PER_GROUP * SB_TILE
    lane = lax.broadcasted_iota(jnp.int32, (1, LANES), 1)
    col = lax.broadcasted_iota(jnp.int32, (rows, SB_BLOCK), 1)
    row = lax.broadcasted_iota(jnp.int32, (rows, SB_BLOCK), 0) % SB_TILE
    for p in pairs:
        q = q_ref[0, :, p * LANES:(p + 1) * LANES]
        for t in tiles:
            qt = q[t * SB_TILE:(t + 1) * SB_TILE]
            qs_scr[t, p] = jnp.concatenate(
                [jnp.where(lane // HEAD_DIM == hh, qt, jnp.zeros_like(qt))
                 for hh in range(SB_HEADS_PER_GROUP)], axis=0)

    def sweep(blocks, resume):
        depth = range(len(blocks[0]))

        def rows_of(ref, t, n, p):
            return ref[0, pl.ds(blocks[t][n][0], SB_BLOCK), p * LANES:(p + 1) * LANES]

        units = [(t, n) for t in tiles for n in depth]

        def logits(t, n):
            return [lax.dot_general(qs_scr[t, p], rows_of(k_ref, t, n, p),
                                    (((1,), (1,)), ((), ())), preferred_element_type=F32)
                    for p in pairs]

        def cumsums(t, n, zs):
            hls = []
            for p in pairs:
                zbits = lax.bitcast_convert_type(zs[p], jnp.uint32)
                neg_abs = lax.bitcast_convert_type(zbits | jnp.uint32(SIGN_BIT), F32)
                sp = jnp.maximum(zs[p], 0.0) + jnp.log(1.0 + jnp.exp(neg_abs))
                if blocks[t][n][1] is not None:
                    sp = jnp.where(blocks[t][n][1], sp, 0.0)
                hi = lax.bitcast_convert_type(
                    lax.bitcast_convert_type(sp, jnp.uint32) & jnp.uint32(BF16_BITS), F32)
                hls.append(jnp.concatenate([hi.astype(BF16), (sp - hi).astype(BF16)], axis=1))
            return jnp.dot(jnp.concatenate(hls, axis=0), tri_ref[...], preferred_element_type=F32)

        carry = {(t, p): carry_scr[t, p] if resume else None for t in tiles for p in pairs}
        acc = {(t, p): acc_scr[t, p] if resume else None for t in tiles for p in pairs}

        def weigh(t, n, zs, rt):
            for p in pairs:
                within = rt[p * rows:(p + 1) * rows, :SB_BLOCK]
                total = rt[p * rows:(p + 1) * rows, SB_BLOCK:]
                before = carry[t, p]
                w = jnp.exp(zs[p] - (within if before is None else within + before))
                if blocks[t][n][1] is not None:
                    w = jnp.where(blocks[t][n][1], w, 0.0)
                carry[t, p] = total if before is None else before + total
                pv = jnp.dot(w.astype(BF16), rows_of(v_ref, t, n, p), preferred_element_type=F32)
                acc[t, p] = pv if acc[t, p] is None else acc[t, p] + pv

        zs, rts = {}, {}
        for u in range(len(units) + 2):
            if u < len(units):
                zs[u] = logits(*units[u])
            if 1 <= u <= len(units):
                rts[u - 1] = cumsums(*units[u - 1], zs[u - 1])
            if u >= 2:
                weigh(*units[u - 2], zs.pop(u - 2), rts.pop(u - 2))
        lowest = None
        for t in tiles:
            for p in pairs:
                carry_scr[t, p] = carry[t, p]
                acc_scr[t, p] = acc[t, p]
                lowest = carry[t, p] if lowest is None else jnp.minimum(lowest, carry[t, p])
        return jnp.min(lowest)

    def aligned(start):
        return pl.multiple_of(start, SB_TILE)

    def first_fast():
        near_mask = col < row + (SB_BLOCK - SB_TILE)
        return sweep([[(aligned(q0 + (t + 1) * SB_TILE - (n + 1) * SB_BLOCK),
                        near_mask if n == 0 else None) for n in range(SB_FIRST_BLOCKS)]
                      for t in tiles], False)

    def first_clamped():
        blocks = []
        for t in tiles:
            limit = q0 + t * SB_TILE + row
            blocks.append([])
            for n in range(SB_FIRST_BLOCKS):
                start = aligned(jnp.maximum(q0 + (t + 1) * SB_TILE - (n + 1) * SB_BLOCK, 0))
                blocks[t].append((start, start + col < limit))
                limit = start
        return sweep(blocks, False)

    def more(st):
        reach = q0 + SB_QROWS - st[0] * SB_BLOCK
        return jnp.logical_and(reach > 0, st[1] < SB_DEAD_LOG)

    def body(st):
        blocks = []
        for t in tiles:
            limit = jnp.maximum(q0 + (t + 1) * SB_TILE - st[0] * SB_BLOCK, 0)
            start = aligned(jnp.maximum(limit - SB_BLOCK, 0))
            blocks.append([(start, start + col < limit)])
        return st[0] + 1, sweep(blocks, True)

    fits = q0 + SB_TILE - SB_FIRST_BLOCKS * SB_BLOCK >= 0
    lowest = lax.cond(fits, first_fast, first_clamped)
    lax.while_loop(more, body, (SB_FIRST_BLOCKS, lowest))
    for p in pairs:
        for t in tiles:
            acc = acc_scr[t, p]
            out = acc[0:SB_TILE]
            for hh in range(1, SB_HEADS_PER_GROUP):
                out = jnp.where(lane // HEAD_DIM == hh, acc[hh * SB_TILE:(hh + 1) * SB_TILE], out)
            o_ref[0, t * SB_TILE:(t + 1) * SB_TILE, p * LANES:(p + 1) * LANES] = out.astype(BF16)


def _stick_breaking(q, k, v, tri):
    b, s, d_sb = q.shape
    n_pairs = d_sb // LANES
    rows = SB_HEADS_PER_GROUP * SB_TILE
    blk = pl.BlockSpec((1, SB_QROWS, d_sb), lambda bi, i: (bi, i, 0))
    whole = pl.BlockSpec((1, s, d_sb), lambda bi, i: (bi, 0, 0))
    return pl.pallas_call(
        functools.partial(_sb_kernel, n_pairs=n_pairs),
        out_shape=jax.ShapeDtypeStruct((b, s, d_sb), BF16),
        grid=(b, s // SB_QROWS),
        in_specs=[blk, whole, whole, _full(tri.shape)],
        out_specs=blk,
        scratch_shapes=[pltpu.VMEM((SB_TILES, n_pairs, rows, LANES), BF16),
                        pltpu.VMEM((SB_TILES, n_pairs, rows, LANES), F32),
                        pltpu.VMEM((SB_TILES, n_pairs, rows, LANES), F32)],
        compiler_params=_cparams(2),
        name="stick_breaking",
    )(q, k, v, tri)


def _ret_kernel(qk_ref, vg_ref, dmat_ref, qdec_ref, kdec_ref, cdec_ref, bd_ref, avg_ref, gain_ref,
                o_ref, state_scr, *, d_ret, n_batch):
    n_heads = d_ret // HEAD_DIM
    batches = range(n_batch)
    nt = (((1,), (1,)), ((), ()))

    @pl.when(pl.program_id(0) == 0)
    def _():
        state_scr[...] = jnp.zeros_like(state_scr)

    lane = lax.broadcasted_iota(jnp.int32, (1, d_ret), 1)
    in_head = [lane // HEAD_DIM == h for h in range(n_heads)]
    q = [qk_ref[b, :, :d_ret] for b in batches]
    k = [qk_ref[b, :, d_ret:] for b in batches]
    v = [vg_ref[b, :, :d_ret] for b in batches]
    probs = []
    for b in batches:
        kb = k[b].astype(BF16)
        probs.append(jnp.concatenate(
            [(lax.dot_general(jnp.where(in_head[h], q[b], 0.0).astype(BF16), kb, nt,
                              preferred_element_type=F32) * dmat_ref[h]).astype(BF16)
             for h in range(n_heads)], axis=1))
    y = []
    for b in batches:
        vals = jnp.concatenate([jnp.where(in_head[h], v[b], 0.0).astype(BF16)
                                for h in range(n_heads)], axis=0)
        y.append(jnp.dot(probs[b], vals, preferred_element_type=F32))
    for b in batches:
        state = state_scr[b]
        y[b] = y[b] + jnp.dot((q[b] * qdec_ref[...]).astype(BF16), state.astype(BF16),
                              preferred_element_type=F32)
        kv = lax.dot_general((k[b] * kdec_ref[...]).astype(BF16), v[b].astype(BF16),
                             (((0,), (0,)), ((), ())), preferred_element_type=F32)
        state_scr[b] = cdec_ref[...] * state + bd_ref[...] * kv

    def head_mean(t):
        hi = t.astype(BF16)
        lo = (t - hi.astype(F32)).astype(BF16)
        return jnp.dot(jnp.concatenate([hi, lo], axis=1), avg_ref[...], preferred_element_type=F32)

    yc = [y[b] - head_mean(y[b]) for b in batches]
    var = [head_mean(yc[b] * yc[b]) for b in batches]
    for b in batches:
        g = vg_ref[b, :, d_ret:]
        o_ref[b] = (_silu(g) * (yc[b] * lax.rsqrt(var[b] + EPS) * gain_ref[...])).astype(BF16)


def _retention(qk, vg, tabs, gain):
    b, s, d2 = qk.shape
    d_ret = d2 // 2
    blk = pl.BlockSpec((b, RET_BLOCK, d2), lambda i: (0, i, 0))
    return pl.pallas_call(
        functools.partial(_ret_kernel, d_ret=d_ret, n_batch=b),
        out_shape=jax.ShapeDtypeStruct((b, s, d_ret), BF16),
        grid=(s // RET_BLOCK,),
        in_specs=[blk, blk] + [_full(t.shape) for t in tabs] + [_full((1, d_ret))],
        out_specs=pl.BlockSpec((b, RET_BLOCK, d_ret), lambda i: (0, i, 0)),
        scratch_shapes=[pltpu.VMEM((b, d_ret, d_ret), F32)],
        compiler_params=_cparams(1),
        name="retention",
    )(qk, vg, *tabs, gain)


def _rotary_tables(seq):
    half = HEAD_DIM // 2
    inv = 1.0 / (ROPE_BASE ** (jnp.arange(half, dtype=F32) / half))
    lane = jnp.arange(LANES)
    ang = jnp.arange(seq).astype(F32)[:, None] * inv[lane % half][None, :]
    sign = jnp.where((lane % HEAD_DIM) < half, -1.0, 1.0).astype(F32)
    return jnp.cos(ang), jnp.sin(ang) * sign[None, :]


def _retention_tables(n_heads):
    d_ret = n_heads * HEAD_DIM
    log_gamma = jnp.log1p(-jnp.exp2(-5.0 - jnp.arange(n_heads, dtype=F32)))
    idx = jnp.arange(RET_BLOCK, dtype=F32)
    t, s = idx[:, None], idx[None, :]
    same = (t // CHUNK) == (s // CHUNK)
    dist = jnp.where(same, jnp.abs(t - s), t - s)
    seen = same | ((s // CHUNK) < (t // CHUNK))
    dmat = jnp.where(seen[None], jnp.exp(log_gamma[:, None, None] * dist[None]), 0.0)
    lane_gamma = jnp.repeat(log_gamma, HEAD_DIM)
    qdec = jnp.exp(lane_gamma[None, :] * (idx + 1.0)[:, None])
    kdec = jnp.exp(lane_gamma[None, :] * (RET_BLOCK - 1.0 - idx)[:, None])
    head = jnp.arange(d_ret) // HEAD_DIM
    bd = (head[:, None] == head[None, :]).astype(F32)
    cdec = bd * jnp.exp(lane_gamma * RET_BLOCK)[:, None]
    avg = jnp.concatenate([bd, bd], axis=0).astype(BF16) * (1.0 / HEAD_DIM)
    return dmat, qdec, kdec, cdec, bd, avg.astype(BF16)


def _tri_table():
    r = np.arange(2 * SB_BLOCK)[:, None] % SB_BLOCK
    c = np.arange(2 * SB_BLOCK)[None, :]
    return jnp.asarray((c >= SB_BLOCK) | (r >= c), dtype=BF16)


def kernel(x, ffn1_norm, ffn1_w_in, ffn1_w_out, mix_norm, mix_w_in, conv_w, conv_b, conv_ln_g,
           conv_ln_b, ret_norm_g, mix_w_out, ffn2_norm, ffn2_w_in, ffn2_w_out, final_norm):
    b, s, d = x.shape
    depth = ffn1_norm.shape[0]
    d_conv = conv_w.shape[2]
    d_ret = ret_norm_g.shape[1]
    d_sb = d - d_conv - d_ret
    assert s % ROW_TILE == 0 and s % FFN_TILE == 0 and s % RET_BLOCK == 0 and s % SB_QROWS == 0
    assert ffn1_w_out.shape[1] % FF_CHUNK == 0 and d % OUT_TILE == 0

    cos_tab, sin_tab = _rotary_tables(s)
    ret_tabs = _retention_tables(d_ret // HEAD_DIM)
    tri = _tri_table()
    scale = HEAD_DIM ** -0.5
    col = jnp.arange(mix_w_in.shape[2])
    q_sb_cols = (col >= 2 * d_conv) & (col < 2 * d_conv + d_sb)
    q_r_cols = (col >= 2 * d_conv + 3 * d_sb) & (col < 2 * d_conv + 3 * d_sb + d_ret)
    col_scale = jnp.where(q_sb_cols | q_r_cols, scale, 1.0).astype(F32)

    row = lambda g: g.reshape(1, -1)
    xf = x.reshape(b * s, d)
    for l in range(depth):
        win1, wout1 = ffn1_w_in[l].astype(BF16), ffn1_w_out[l].astype(BF16)
        win2, wout2 = ffn2_w_in[l].astype(BF16), ffn2_w_out[l].astype(BF16)
        w_mix = (mix_w_in[l] * col_scale[None, :]).astype(BF16)

        xf = _ffn(xf, row(ffn1_norm[l]), win1, wout1, FFN_TILE)
        uconv, q, k, v, qkr, vgr = _inproj(xf, row(mix_norm[l]), w_mix, cos_tab, sin_tab, s,
                                           2 * d_conv, d_sb, d_ret)
        seq3 = lambda t: t.reshape(b, s, t.shape[1])
        y_sb = _stick_breaking(seq3(q), seq3(k), seq3(v), tri)
        y_ret = _retention(seq3(qkr), seq3(vgr), ret_tabs, row(ret_norm_g[l]))
        flat = lambda t: t.reshape(b * s, t.shape[2])
        last = l == depth - 1
        xf = _ffn(xf, row(ffn2_norm[l]), win2, wout2, ROW_TILE,
                  mix=(flat(y_sb), flat(y_ret)), wmix=mix_w_out[l].astype(BF16),
                  final_g=row(final_norm) if last else None,
                  conv=(uconv, s, conv_w[l], row(conv_b[l]), row(conv_ln_g[l]), row(conv_ln_b[l])))
    return xf.reshape(b, s, d)
```

```python
import functools

import jax
import jax.numpy as jnp
import numpy as np
from jax import lax
from jax.experimental import pallas as pl
from jax.experimental.pallas import tpu as pltpu

F32 = jnp.float32
BF16 = jnp.bfloat16

EPS = 1e-6
ROPE_BASE = 10000.0
CHUNK = 64
CONV_WIDTH = 31
HEAD_DIM = 64
LANES = 128
SUBLANES = 8
MXU_DIM = 256

ROW_TILE = 512
FFN_TILE = 1024
FF_CHUNK = MXU_DIM
OUT_TILE = 512
CONV_HALO = 32
CONV_SUBTILES = 8
CONV_CHUNK_GAP = 1
CONV_CHUNK_SPAN = 2
SB_BLOCK = 128
SB_QROWS = 1024
SB_TILE = 64
SB_TILES = SB_QROWS // SB_TILE
SB_HEADS_PER_GROUP = LANES // HEAD_DIM
SB_FIRST_BLOCKS = 2
SB_DEAD_LOG = 110.0
SIGN_BIT = 0x80000000
BF16_BITS = 0xFFFF0000
RET_BLOCK = 256
VMEM_LIMIT = 56 * 1024 * 1024


def _cparams(n_axes):
    return pltpu.CompilerParams(dimension_semantics=("arbitrary",) * n_axes,
                                vmem_limit_bytes=VMEM_LIMIT)


def _rms(x, g):
    return (x * lax.rsqrt(jnp.mean(x * x, axis=-1, keepdims=True) + EPS)) * g


def _silu(x):
    return x * jax.nn.sigmoid(x)


def _full(shape):
    return pl.BlockSpec(shape, lambda *_: (0,) * len(shape))


def _resident(shape):
    return pl.BlockSpec(shape, lambda *_: (0,) * len(shape), pipeline_mode=pl.Buffered(1))


def _bit_zero(x):
    bits = lax.bitcast_convert_type(x, jnp.uint32)
    return lax.bitcast_convert_type((bits >> 16) >> 16, F32)


def _conv_fill(u, v_scr, starts_sequence, tile_rows):
    d_conv = v_scr.shape[1]
    if starts_sequence is None:
        v_scr[0:CONV_HALO, :] = jnp.zeros((CONV_HALO, d_conv), F32)
    else:
        v_scr[0:CONV_HALO, :] = jnp.where(starts_sequence, 0.0,
                                          v_scr[tile_rows:tile_rows + CONV_HALO, :])
    v_scr[CONV_HALO:CONV_HALO + tile_rows, :] = u[:, :d_conv] * jax.nn.sigmoid(u[:, d_conv:])
    v_scr[CONV_HALO + tile_rows:, :] = jnp.zeros((SUBLANES, d_conv), F32)


def _conv_rows(v_scr, p_scr, cw_ref, cb_ref, lg_ref, lb_ref, out_ref, r, anchor=None):
    off = CONV_HALO - (CONV_WIDTH - 1)
    sub = p_scr.shape[2] - SUBLANES
    cw = cw_ref[...] if anchor is None else cw_ref[...] + anchor
    for shift in range(SUBLANES):
        part = None
        for o in range(shift, off + CONV_WIDTH, SUBLANES):
            if o < off:
                continue
            rows = v_scr[pl.ds(r * sub + o - shift, sub + SUBLANES), :]
            term = cw[o - off:o - off + 1, :] * rows
            part = term if part is None else part + term
        p_scr[r % 2, shift] = part
    y = cb_ref[...] + p_scr[r % 2, 0, 0:sub, :]
    for shift in range(1, SUBLANES):
        y = y + p_scr[r % 2, shift, pl.ds(shift, sub), :]
    mu = jnp.mean(y, axis=-1, keepdims=True)
    yc = y - mu
    var = jnp.mean(yc * yc, axis=-1, keepdims=True)
    ln = yc * lax.rsqrt(var + EPS) * lg_ref[...] + lb_ref[...]
    out = _silu(ln)
    out_ref[r * sub:(r + 1) * sub, :] = out.astype(BF16)
    return out[0:1, :]


def _ffn_kernel(*refs, tile, n_chunks, n_out, n_mix, final, conv_tiles_per_seq):
    refs = list(refs)
    x_ref = refs.pop(0)
    mix_refs = [refs.pop(0) for _ in range(n_mix)]
    wmix_ref = refs.pop(0) if n_mix else None
    g_ref, win_ref, wout_ref = refs.pop(0), refs.pop(0), refs.pop(0)
    fg_ref = refs.pop(0) if final else None
    has_conv = conv_tiles_per_seq is not None
    if has_conv:
        u_first_ref, u_next_ref = refs.pop(0), refs.pop(0)
        conv_params = [refs.pop(0) for _ in range(4)]
    o_ref, h_scr, a_scr = refs.pop(0), refs.pop(0), refs.pop(0)
    out_tiles = [slice(m * OUT_TILE, (m + 1) * OUT_TILE) for m in range(n_out)]
    d_ff = wout_ref.shape[0]
    n_sub = CONV_SUBTILES
    if has_conv:
        v_scr, p_scr, yc_scr = refs.pop(0), refs.pop(0), refs.pop(0)
        step = pl.program_id(0)

        @pl.when(step == 0)
        def _():
            _conv_fill(u_first_ref[...], v_scr, None, tile)
            for r in range(n_sub):
                _conv_rows(v_scr, p_scr, *conv_params, yc_scr, r)

        mix_refs = [yc_scr] + mix_refs
    if mix_refs:
        for sl in out_tiles:
            y, lo = None, 0
            for y_ref in mix_refs:
                part = jnp.dot(y_ref[...], wmix_ref[lo:lo + y_ref.shape[1], sl],
                               preferred_element_type=F32)
                y = part if y is None else y + part
                lo += y_ref.shape[1]
            o_ref[:, sl] = x_ref[:, sl] + y
        x_ref = o_ref
    h_scr[...] = _rms(x_ref[...], g_ref[...]).astype(BF16)
    if has_conv:
        nxt = jnp.minimum(step + 1, pl.num_programs(0) - 1)
        _conv_fill(u_next_ref[...], v_scr, nxt % conv_tiles_per_seq == 0, tile)
    done = {}
    for j in range(n_chunks):
        sl = slice(j * FF_CHUNK, (j + 1) * FF_CHUNK)
        up_sl = slice(d_ff + j * FF_CHUNK, d_ff + (j + 1) * FF_CHUNK)
        gate = jnp.dot(h_scr[...], win_ref[:, sl], preferred_element_type=F32)
        up = jnp.dot(h_scr[...], win_ref[:, up_sl], preferred_element_type=F32)
        if j in done:
            gate = gate + done.pop(j)
        a_scr[:, sl] = (_silu(gate) * up).astype(BF16)
        if has_conv and j % CONV_CHUNK_GAP == 0 and j // CONV_CHUNK_GAP < n_sub:
            r = j // CONV_CHUNK_GAP
            out_row = _conv_rows(v_scr, p_scr, *conv_params, yc_scr, r,
                                 anchor=_bit_zero(gate[0:1, :conv_params[0].shape[1]]))
            done[j + CONV_CHUNK_SPAN] = _bit_zero(out_row)
    assert not done
    for sl in out_tiles:
        y = jnp.dot(a_scr[...], wout_ref[:, sl], preferred_element_type=F32)
        o_ref[:, sl] = x_ref[:, sl] + 0.5 * y
    if final:
        o_ref[...] = _rms(o_ref[...], fg_ref[...])


def _ffn(x, g, win, wout, tile, mix=(), wmix=None, final_g=None, conv=None):
    n, d = x.shape
    d_ff = wout.shape[0]
    n_chunks = d_ff // FF_CHUNK
    n_tiles = n // tile
    final = final_g is not None
    row = lambda i: (i, 0)
    in_specs = [pl.BlockSpec((tile, d), row)]
    in_specs += [pl.BlockSpec((tile, y.shape[1]), row) for y in mix]
    args = [x, *mix]
    if mix:
        in_specs.append(_resident(wmix.shape))
        args.append(wmix)
    in_specs += [_full((1, d)), _resident(win.shape), _resident(wout.shape)]
    args += [g, win, wout]
    if final:
        in_specs.append(_full((1, d)))
        args.append(final_g)
    scratch = [pltpu.VMEM((tile, d), BF16), pltpu.VMEM((tile, d_ff), BF16)]
    conv_tiles_per_seq = None
    if conv is not None:
        u, seq, cw, cb, lg, lb = conv
        d_conv = cw.shape[1]
        assert seq % tile == 0 and tile % (CONV_SUBTILES * SUBLANES) == 0
        assert CONV_HALO >= CONV_WIDTH - 1
        conv_tiles_per_seq = seq // tile
        in_specs += [pl.BlockSpec((tile, u.shape[1]), lambda i: (0, 0), pipeline_mode=pl.Buffered(1)),
                     pl.BlockSpec((tile, u.shape[1]), lambda i: (jnp.minimum(i + 1, n_tiles - 1), 0)),
                     _full(cw.shape), _full((1, d_conv)), _full((1, d_conv)), _full((1, d_conv))]
        args += [u, u, cw, cb, lg, lb]
        scratch += [pltpu.VMEM((CONV_HALO + tile + SUBLANES, d_conv), F32),
                    pltpu.VMEM((2, SUBLANES, tile // CONV_SUBTILES + SUBLANES, d_conv), F32),
                    pltpu.VMEM((tile, d_conv), BF16)]
    return pl.pallas_call(
        functools.partial(_ffn_kernel, tile=tile, n_chunks=n_chunks, n_out=d // OUT_TILE,
                          n_mix=len(mix), final=final, conv_tiles_per_seq=conv_tiles_per_seq),
        out_shape=jax.ShapeDtypeStruct((n, d), F32),
        grid=(n_tiles,),
        in_specs=in_specs,
        out_specs=pl.BlockSpec((tile, d), row),
        scratch_shapes=scratch,
        compiler_params=_cparams(1),
        name="ffn",
    )(*args)


def _rotary(t, cos, sin_signed, first_half):
    partner = jnp.where(first_half, pltpu.roll(t, LANES - HEAD_DIM // 2, 1),
                        pltpu.roll(t, HEAD_DIM // 2, 1))
    return t * cos + partner * sin_signed


def _inproj_kernel(x_ref, g_ref, w_ref, cos_ref, sin_ref, uconv_ref, q_ref, k_ref, v_ref,
                   qkr_ref, vgr_ref, h_scr, *, d_conv2, d_sb, d_ret):
    h_scr[...] = _rms(x_ref[...], g_ref[...]).astype(BF16)

    def proj(lo, width):
        return jnp.dot(h_scr[...], w_ref[:, lo:lo + width], preferred_element_type=F32)

    uconv_ref[...] = proj(0, d_conv2)
    o = d_conv2
    q_ref[...] = proj(o, d_sb).astype(BF16)
    k_ref[...] = proj(o + d_sb, d_sb).astype(BF16)
    v_ref[...] = proj(o + 2 * d_sb, d_sb).astype(BF16)
    o += 3 * d_sb
    qk = proj(o, 2 * d_ret)
    cos = cos_ref[...]
    sin = sin_ref[...]
    lane = lax.broadcasted_iota(jnp.int32, (1, LANES), 1)
    first_half = (lane % HEAD_DIM) < (HEAD_DIM // 2)
    for c in range(2 * d_ret // LANES):
        sl = slice(c * LANES, (c + 1) * LANES)
        qkr_ref[:, sl] = _rotary(qk[:, sl], cos, sin, first_half)
    vgr_ref[...] = proj(o + 2 * d_ret, 2 * d_ret)


def _inproj(x, g, w, cos_tab, sin_tab, seq, d_conv2, d_sb, d_ret):
    n, d = x.shape
    tiles_per_seq = seq // FFN_TILE
    row = lambda i: (i, 0)
    pos = lambda i: (i % tiles_per_seq, 0)
    outs = [jax.ShapeDtypeStruct((n, d_conv2), F32)] + [jax.ShapeDtypeStruct((n, d_sb), BF16)] * 3 \
        + [jax.ShapeDtypeStruct((n, 2 * d_ret), F32)] * 2
    return pl.pallas_call(
        functools.partial(_inproj_kernel, d_conv2=d_conv2, d_sb=d_sb, d_ret=d_ret),
        out_shape=outs,
        grid=(n // FFN_TILE,),
        in_specs=[pl.BlockSpec((FFN_TILE, d), row), _full((1, d)), _resident(w.shape),
                  pl.BlockSpec((FFN_TILE, LANES), pos), pl.BlockSpec((FFN_TILE, LANES), pos)],
        out_specs=[pl.BlockSpec((FFN_TILE, s.shape[1]), row) for s in outs],
        scratch_shapes=[pltpu.VMEM((FFN_TILE, d), BF16)],
        compiler_params=_cparams(1),
        name="mixer_inproj",
    )(x, g, w, cos_tab, sin_tab)


def _sb_kernel(q_ref, k_ref, v_ref, tri_ref, o_ref, qs_scr, acc_scr, carry_scr, *, n_pairs):
    q0 = pl.program_id(1) * SB_QROWS
    tiles, pairs = range(SB_TILES), range(n_pairs)
    rows = SB_HEADS_PER_GROUP * SB_TILE
    lane = lax.broadcasted_iota(jnp.int32, (1, LANES), 1)
    col = lax.broadcasted_iota(jnp.int32, (rows, SB_BLOCK), 1)
    row = lax.broadcasted_iota(jnp.int32, (rows, SB_BLOCK), 0) % SB_TILE
    for p in pairs:
        q = q_ref[0, :, p * LANES:(p + 1) * LANES]
        for t in tiles:
            qt = q[t * SB_TILE:(t + 1) * SB_TILE]
            qs_scr[t, p] = jnp.concatenate(
                [jnp.where(lane // HEAD_DIM == hh, qt, jnp.zeros_like(qt))
                 for hh in range(SB_HEADS_PER_GROUP)], axis=0)

    def sweep(blocks, resume):
        depth = range(len(blocks[0]))

        def rows_of(ref, t, n, p):
            return ref[0, pl.ds(blocks[t][n][0], SB_BLOCK), p * LANES:(p + 1) * LANES]

        units = [(t, n) for t in tiles for n in depth]

        def logits(t, n):
            return [lax.dot_general(qs_scr[t, p], rows_of(k_ref, t, n, p),
                                    (((1,), (1,)), ((), ())), preferred_element_type=F32)
                    for p in pairs]

        def cumsums(t, n, zs):
            hls = []
            for p in pairs:
                zbits = lax.bitcast_convert_type(zs[p], jnp.uint32)
                neg_abs = lax.bitcast_convert_type(zbits | jnp.uint32(SIGN_BIT), F32)
                sp = jnp.maximum(zs[p], 0.0) + jnp.log(1.0 + jnp.exp(neg_abs))
                if blocks[t][n][1] is not None:
                    sp = jnp.where(blocks[t][n][1], sp, 0.0)
                hi = lax.bitcast_convert_type(
                    lax.bitcast_convert_type(sp, jnp.uint32) & jnp.uint32(BF16_BITS), F32)
                hls.append(jnp.concatenate([hi.astype(BF16), (sp - hi).astype(BF16)], axis=1))
            return jnp.dot(jnp.concatenate(hls, axis=0), tri_ref[...], preferred_element_type=F32)

        carry = {(t, p): carry_scr[t, p] if resume else None for t in tiles for p in pairs}
        acc = {(t, p): acc_scr[t, p] if resume else None for t in tiles for p in pairs}

        def weigh(t, n, zs, rt):
            for p in pairs:
                within = rt[p * rows:(p + 1) * rows, :SB_BLOCK]
                total = rt[p * rows:(p + 1) * rows, SB_BLOCK:]
                before = carry[t, p]
                w = jnp.exp(zs[p] - (within if before is None else within + before))
                if blocks[t][n][1] is not None:
                    w = jnp.where(blocks[t][n][1], w, 0.0)
                carry[t, p] = total if before is None else before + total
                pv = jnp.dot(w.astype(BF16), rows_of(v_ref, t, n, p), preferred_element_type=F32)
                acc[t, p] = pv if acc[t, p] is None else acc[t, p] + pv

        zs, rts = {}, {}
        for u in range(len(units) + 2):
            if u < len(units):
                zs[u] = logits(*units[u])
            if 1 <= u <= len(units):
                rts[u - 1] = cumsums(*units[u - 1], zs[u - 1])
            if u >= 2:
                weigh(*units[u - 2], zs.pop(u - 2), rts.pop(u - 2))
        lowest = None
        for t in tiles:
            for p in pairs:
                carry_scr[t, p] = carry[t, p]
                acc_scr[t, p] = acc[t, p]
                lowest = carry[t, p] if lowest is None else jnp.minimum(lowest, carry[t, p])
        return jnp.min(lowest)

    def aligned(start):
        return pl.multiple_of(start, SB_TILE)

    def first_fast():
        near_mask = col < row + (SB_BLOCK - SB_TILE)
        return sweep([[(aligned(q0 + (t + 1) * SB_TILE - (n + 1) * SB_BLOCK),
                        near_mask if n == 0 else None) for n in range(SB_FIRST_BLOCKS)]
                      for t in tiles], False)

    def first_clamped():
        blocks = []
        for t in tiles:
            limit = q0 + t * SB_TILE + row
            blocks.append([])
            for n in range(SB_FIRST_BLOCKS):
                start = aligned(jnp.maximum(q0 + (t + 1) * SB_TILE - (n + 1) * SB_BLOCK, 0))
                blocks[t].append((start, start + col < limit))
                limit = start
        return sweep(blocks, False)

    def more(st):
        reach = q0 + SB_QROWS - st[0] * SB_BLOCK
        return jnp.logical_and(reach > 0, st[1] < SB_DEAD_LOG)

    def body(st):
        blocks = []
        for t in tiles:
            limit = jnp.maximum(q0 + (t + 1) * SB_TILE - st[0] * SB_BLOCK, 0)
            start = aligned(jnp.maximum(limit - SB_BLOCK, 0))
            blocks.append([(start, start + col < limit)])
        return st[0] + 1, sweep(blocks, True)

    fits = q0 + SB_TILE - SB_FIRST_BLOCKS * SB_BLOCK >= 0
    lowest = lax.cond(fits, first_fast, first_clamped)
    lax.while_loop(more, body, (SB_FIRST_BLOCKS, lowest))
    for p in pairs:
        for t in tiles:
            acc = acc_scr[t, p]
            out = acc[0:SB_TILE]
            for hh in range(1, SB_HEADS_PER_GROUP):
                out = jnp.where(lane // HEAD_DIM == hh, acc[hh * SB_TILE:(hh + 1) * SB_TILE], out)
            o_ref[0, t * SB_TILE:(t + 1) * SB_TILE, p * LANES:(p + 1) * LANES] = out.astype(BF16)


def _stick_breaking(q, k, v, tri):
    b, s, d_sb = q.shape
    n_pairs = d_sb // LANES
    rows = SB_HEADS_PER_GROUP * SB_TILE
    blk = pl.BlockSpec((1, SB_QROWS, d_sb), lambda bi, i: (bi, i, 0))
    whole = pl.BlockSpec((1, s, d_sb), lambda bi, i: (bi, 0, 0))
    return pl.pallas_call(
        functools.partial(_sb_kernel, n_pairs=n_pairs),
        out_shape=jax.ShapeDtypeStruct((b, s, d_sb), BF16),
        grid=(b, s // SB_QROWS),
        in_specs=[blk, whole, whole, _full(tri.shape)],
        out_specs=blk,
        scratch_shapes=[pltpu.VMEM((SB_TILES, n_pairs, rows, LANES), BF16),
                        pltpu.VMEM((SB_TILES, n_pairs, rows, LANES), F32),
                        pltpu.VMEM((SB_TILES, n_pairs, rows, LANES), F32)],
        compiler_params=_cparams(2),
        name="stick_breaking",
    )(q, k, v, tri)


def _ret_kernel(qk_ref, vg_ref, dmat_ref, qdec_ref, kdec_ref, cdec_ref, bd_ref, avg_ref, gain_ref,
                o_ref, state_scr, *, d_ret, n_batch):
    n_heads = d_ret // HEAD_DIM
    batches = range(n_batch)
    nt = (((1,), (1,)), ((), ()))

    @pl.when(pl.program_id(0) == 0)
    def _():
        state_scr[...] = jnp.zeros_like(state_scr)

    lane = lax.broadcasted_iota(jnp.int32, (1, d_ret), 1)
    in_head = [lane // HEAD_DIM == h for h in range(n_heads)]
    q = [qk_ref[b, :, :d_ret] for b in batches]
    k = [qk_ref[b, :, d_ret:] for b in batches]
    v = [vg_ref[b, :, :d_ret] for b in batches]
    probs = []
    for b in batches:
        kb = k[b].astype(BF16)
        probs.append(jnp.concatenate(
            [(lax.dot_general(jnp.where(in_head[h], q[b], 0.0).astype(BF16), kb, nt,
                              preferred_element_type=F32) * dmat_ref[h]).astype(BF16)
             for h in range(n_heads)], axis=1))
    y = []
    for b in batches:
        vals = jnp.concatenate([jnp.where(in_head[h], v[b], 0.0).astype(BF16)
                                for h in range(n_heads)], axis=0)
        y.append(jnp.dot(probs[b], vals, preferred_element_type=F32))
    for b in batches:
        state = state_scr[b]
        y[b] = y[b] + jnp.dot((q[b] * qdec_ref[...]).astype(BF16), state.astype(BF16),
                              preferred_element_type=F32)
        kv = lax.dot_general((k[b] * kdec_ref[...]).astype(BF16), v[b].astype(BF16),
                             (((0,), (0,)), ((), ())), preferred_element_type=F32)
        state_scr[b] = cdec_ref[...] * state + bd_ref[...] * kv

    def head_mean(t):
        hi = t.astype(BF16)
        lo = (t - hi.astype(F32)).astype(BF16)
        return jnp.dot(jnp.concatenate([hi, lo], axis=1), avg_ref[...], preferred_element_type=F32)

    yc = [y[b] - head_mean(y[b]) for b in batches]
    var = [head_mean(yc[b] * yc[b]) for b in batches]
    for b in batches:
        g = vg_ref[b, :, d_ret:]
        o_ref[b] = (_silu(g) * (yc[b] * lax.rsqrt(var[b] + EPS) * gain_ref[...])).astype(BF16)


def _retention(qk, vg, tabs, gain):
    b, s, d2 = qk.shape
    d_ret = d2 // 2
    blk = pl.BlockSpec((b, RET_BLOCK, d2), lambda i: (0, i, 0))
    return pl.pallas_call(
        functools.partial(_ret_kernel, d_ret=d_ret, n_batch=b),
        out_shape=jax.ShapeDtypeStruct((b, s, d_ret), BF16),
        grid=(s // RET_BLOCK,),
        in_specs=[blk, blk] + [_full(t.shape) for t in tabs] + [_full((1, d_ret))],
        out_specs=pl.BlockSpec((b, RET_BLOCK, d_ret), lambda i: (0, i, 0)),
        scratch_shapes=[pltpu.VMEM((b, d_ret, d_ret), F32)],
        compiler_params=_cparams(1),
        name="retention",
    )(qk, vg, *tabs, gain)


def _rotary_tables(seq):
    half = HEAD_DIM // 2
    inv = 1.0 / (ROPE_BASE ** (jnp.arange(half, dtype=F32) / half))
    lane = jnp.arange(LANES)
    ang = jnp.arange(seq).astype(F32)[:, None] * inv[lane % half][None, :]
    sign = jnp.where((lane % HEAD_DIM) < half, -1.0, 1.0).astype(F32)
    return jnp.cos(ang), jnp.sin(ang) * sign[None, :]


def _retention_tables(n_heads):
    d_ret = n_heads * HEAD_DIM
    log_gamma = jnp.log1p(-jnp.exp2(-5.0 - jnp.arange(n_heads, dtype=F32)))
    idx = jnp.arange(RET_BLOCK, dtype=F32)
    t, s = idx[:, None], idx[None, :]
    same = (t // CHUNK) == (s // CHUNK)
    dist = jnp.where(same, jnp.abs(t - s), t - s)
    seen = same | ((s // CHUNK) < (t // CHUNK))
    dmat = jnp.where(seen[None], jnp.exp(log_gamma[:, None, None] * dist[None]), 0.0)
    lane_gamma = jnp.repeat(log_gamma, HEAD_DIM)
    qdec = jnp.exp(lane_gamma[None, :] * (idx + 1.0)[:, None])
    kdec = jnp.exp(lane_gamma[None, :] * (RET_BLOCK - 1.0 - idx)[:, None])
    head = jnp.arange(d_ret) // HEAD_DIM
    bd = (head[:, None] == head[None, :]).astype(F32)
    cdec = bd * jnp.exp(lane_gamma * RET_BLOCK)[:, None]
    avg = jnp.concatenate([bd, bd], axis=0).astype(BF16) * (1.0 / HEAD_DIM)
    return dmat, qdec, kdec, cdec, bd, avg.astype(BF16)


def _tri_table():
    r = np.arange(2 * SB_BLOCK)[:, None] % SB_BLOCK
    c = np.arange(2 * SB_BLOCK)[None, :]
    return jnp.asarray((c >= SB_BLOCK) | (r >= c), dtype=BF16)


def kernel(x, ffn1_norm, ffn1_w_in, ffn1_w_out, mix_norm, mix_w_in, conv_w, conv_b, conv_ln_g,
           conv_ln_b, ret_norm_g, mix_w_out, ffn2_norm, ffn2_w_in, ffn2_w_out, final_norm):
    b, s, d = x.shape
    depth = ffn1_norm.shape[0]
    d_conv = conv_w.shape[2]
    d_ret = ret_norm_g.shape[1]
    d_sb = d - d_conv - d_ret
    assert s % ROW_TILE == 0 and s % FFN_TILE == 0 and s % RET_BLOCK == 0 and s % SB_QROWS == 0
    assert ffn1_w_out.shape[1] % FF_CHUNK == 0 and d % OUT_TILE == 0

    cos_tab, sin_tab = _rotary_tables(s)
    ret_tabs = _retention_tables(d_ret // HEAD_DIM)
    tri = _tri_table()
    scale = HEAD_DIM ** -0.5
    col = jnp.arange(mix_w_in.shape[2])
    q_sb_cols = (col >= 2 * d_conv) & (col < 2 * d_conv + d_sb)
    q_r_cols = (col >= 2 * d_conv + 3 * d_sb) & (col < 2 * d_conv + 3 * d_sb + d_ret)
    col_scale = jnp.where(q_sb_cols | q_r_cols, scale, 1.0).astype(F32)

    row = lambda g: g.reshape(1, -1)
    xf = x.reshape(b * s, d)
    for l in range(depth):
        win1, wout1 = ffn1_w_in[l].astype(BF16), ffn1_w_out[l].astype(BF16)
        win2, wout2 = ffn2_w_in[l].astype(BF16), ffn2_w_out[l].astype(BF16)
        w_mix = (mix_w_in[l] * col_scale[None, :]).astype(BF16)

        xf = _ffn(xf, row(ffn1_norm[l]), win1, wout1, FFN_TILE)
        uconv, q, k, v, qkr, vgr = _inproj(xf, row(mix_norm[l]), w_mix, cos_tab, sin_tab, s,
                                           2 * d_conv, d_sb, d_ret)
        seq3 = lambda t: t.reshape(b, s, t.shape[1])
        y_sb = _stick_breaking(seq3(q), seq3(k), seq3(v), tri)
        y_ret = _retention(seq3(qkr), seq3(vgr), ret_tabs, row(ret_norm_g[l]))
        flat = lambda t: t.reshape(b * s, t.shape[2])
        last = l == depth - 1
        xf = _ffn(xf, row(ffn2_norm[l]), win2, wout2, ROW_TILE,
                  mix=(flat(y_sb), flat(y_ret)), wmix=mix_w_out[l].astype(BF16),
                  final_g=row(final_norm) if last else None,
                  conv=(uconv, s, conv_w[l], row(conv_b[l]), row(conv_ln_g[l]), row(conv_ln_b[l])))
    return xf.reshape(b, s, d)
```

```python
import functools

import jax
import jax.numpy as jnp
import numpy as np
from jax import lax
from jax.experimental import pallas as pl
from jax.experimental.pallas import tpu as pltpu

F32 = jnp.float32
BF16 = jnp.bfloat16

EPS = 1e-6
ROPE_BASE = 10000.0
CHUNK = 64
CONV_WIDTH = 31
HEAD_DIM = 64
LANES = 128
SUBLANES = 8
MXU_DIM = 256

ROW_TILE = 512
FFN_TILE = 1024
FF_CHUNK = MXU_DIM
OUT_TILE = 512
CONV_HALO = 32
CONV_SUBTILES = 8
CONV_CHUNK_GAP = 1
CONV_CHUNK_SPAN = 2
SB_BLOCK = 128
SB_QROWS = 512
SB_TILE = 64
SB_TILES = SB_QROWS // SB_TILE
SB_HEADS_PER_GROUP = LANES // HEAD_DIM
SB_FIRST_BLOCKS = 2
SB_DEAD_LOG = 110.0
SIGN_BIT = 0x80000000
BF16_BITS = 0xFFFF0000
RET_BLOCK = 256
VMEM_LIMIT = 56 * 1024 * 1024


def _cparams(n_axes):
    return pltpu.CompilerParams(dimension_semantics=("arbitrary",) * n_axes,
                                vmem_limit_bytes=VMEM_LIMIT)


def _rms(x, g):
    return (x * lax.rsqrt(jnp.mean(x * x, axis=-1, keepdims=True) + EPS)) * g


def _silu(x):
    return x * jax.nn.sigmoid(x)


def _full(shape):
    return pl.BlockSpec(shape, lambda *_: (0,) * len(shape))


def _resident(shape):
    return pl.BlockSpec(shape, lambda *_: (0,) * len(shape), pipeline_mode=pl.Buffered(1))


def _bit_zero(x):
    bits = lax.bitcast_convert_type(x, jnp.uint32)
    return lax.bitcast_convert_type((bits >> 16) >> 16, F32)


def _conv_fill(u, v_scr, starts_sequence, tile_rows):
    d_conv = v_scr.shape[1]
    if starts_sequence is None:
        v_scr[0:CONV_HALO, :] = jnp.zeros((CONV_HALO, d_conv), F32)
    else:
        v_scr[0:CONV_HALO, :] = jnp.where(starts_sequence, 0.0,
                                          v_scr[tile_rows:tile_rows + CONV_HALO, :])
    v_scr[CONV_HALO:CONV_HALO + tile_rows, :] = u[:, :d_conv] * jax.nn.sigmoid(u[:, d_conv:])
    v_scr[CONV_HALO + tile_rows:, :] = jnp.zeros((SUBLANES, d_conv), F32)


def _conv_rows(v_scr, p_scr, cw_ref, cb_ref, lg_ref, lb_ref, out_ref, r, anchor=None):
    off = CONV_HALO - (CONV_WIDTH - 1)
    sub = p_scr.shape[2] - SUBLANES
    cw = cw_ref[...] if anchor is None else cw_ref[...] + anchor
    for shift in range(SUBLANES):
        part = None
        for o in range(shift, off + CONV_WIDTH, SUBLANES):
            if o < off:
                continue
            rows = v_scr[pl.ds(r * sub + o - shift, sub + SUBLANES), :]
            term = cw[o - off:o - off + 1, :] * rows
            part = term if part is None else part + term
        p_scr[r % 2, shift] = part
    y = cb_ref[...] + p_scr[r % 2, 0, 0:sub, :]
    for shift in range(1, SUBLANES):
        y = y + p_scr[r % 2, shift, pl.ds(shift, sub), :]
    mu = jnp.mean(y, axis=-1, keepdims=True)
    yc = y - mu
    var = jnp.mean(yc * yc, axis=-1, keepdims=True)
    ln = yc * lax.rsqrt(var + EPS) * lg_ref[...] + lb_ref[...]
    out = _silu(ln)
    out_ref[r * sub:(r + 1) * sub, :] = out.astype(BF16)
    return out[0:1, :]


def _ffn_kernel(*refs, tile, n_chunks, n_out, n_mix, final, conv_tiles_per_seq):
    refs = list(refs)
    x_ref = refs.pop(0)
    mix_refs = [refs.pop(0) for _ in range(n_mix)]
    wmix_ref = refs.pop(0) if n_mix else None
    g_ref, win_ref, wout_ref = refs.pop(0), refs.pop(0), refs.pop(0)
    fg_ref = refs.pop(0) if final else None
    has_conv = conv_tiles_per_seq is not None
    if has_conv:
        u_first_ref, u_next_ref = refs.pop(0), refs.pop(0)
        conv_params = [refs.pop(0) for _ in range(4)]
    o_ref, h_scr, a_scr = refs.pop(0), refs.pop(0), refs.pop(0)
    out_tiles = [slice(m * OUT_TILE, (m + 1) * OUT_TILE) for m in range(n_out)]
    d_ff = wout_ref.shape[0]
    n_sub = CONV_SUBTILES
    if has_conv:
        v_scr, p_scr, yc_scr = refs.pop(0), refs.pop(0), refs.pop(0)
        step = pl.program_id(0)

        @pl.when(step == 0)
        def _():
            _conv_fill(u_first_ref[...], v_scr, None, tile)
            for r in range(n_sub):
                _conv_rows(v_scr, p_scr, *conv_params, yc_scr, r)

        mix_refs = [yc_scr] + mix_refs
    if mix_refs:
        for sl in out_tiles:
            y, lo = None, 0
            for y_ref in mix_refs:
                part = jnp.dot(y_ref[...], wmix_ref[lo:lo + y_ref.shape[1], sl],
                               preferred_element_type=F32)
                y = part if y is None else y + part
                lo += y_ref.shape[1]
            o_ref[:, sl] = x_ref[:, sl] + y
        x_ref = o_ref
    h_scr[...] = _rms(x_ref[...], g_ref[...]).astype(BF16)
    if has_conv:
        nxt = jnp.minimum(step + 1, pl.num_programs(0) - 1)
        _conv_fill(u_next_ref[...], v_scr, nxt % conv_tiles_per_seq == 0, tile)
    done = {}
    for j in range(n_chunks):
        sl = slice(j * FF_CHUNK, (j + 1) * FF_CHUNK)
        up_sl = slice(d_ff + j * FF_CHUNK, d_ff + (j + 1) * FF_CHUNK)
        gate = jnp.dot(h_scr[...], win_ref[:, sl], preferred_element_type=F32)
        up = jnp.dot(h_scr[...], win_ref[:, up_sl], preferred_element_type=F32)
        if j in done:
            gate = gate + done.pop(j)
        a_scr[:, sl] = (_silu(gate) * up).astype(BF16)
        if has_conv and j % CONV_CHUNK_GAP == 0 and j // CONV_CHUNK_GAP < n_sub:
            r = j // CONV_CHUNK_GAP
            out_row = _conv_rows(v_scr, p_scr, *conv_params, yc_scr, r,
                                 anchor=_bit_zero(gate[0:1, :conv_params[0].shape[1]]))
            done[j + CONV_CHUNK_SPAN] = _bit_zero(out_row)
    assert not done
    for sl in out_tiles:
        y = jnp.dot(a_scr[...], wout_ref[:, sl], preferred_element_type=F32)
        o_ref[:, sl] = x_ref[:, sl] + 0.5 * y
    if final:
        o_ref[...] = _rms(o_ref[...], fg_ref[...])


def _ffn(x, g, win, wout, tile, mix=(), wmix=None, final_g=None, conv=None):
    n, d = x.shape
    d_ff = wout.shape[0]
    n_chunks = d_ff // FF_CHUNK
    n_tiles = n // tile
    final = final_g is not None
    row = lambda i: (i, 0)
    in_specs = [pl.BlockSpec((tile, d), row)]
    in_specs += [pl.BlockSpec((tile, y.shape[1]), row) for y in mix]
    args = [x, *mix]
    if mix:
        in_specs.append(_resident(wmix.shape))
        args.append(wmix)
    in_specs += [_full((1, d)), _resident(win.shape), _resident(wout.shape)]
    args += [g, win, wout]
    if final:
        in_specs.append(_full((1, d)))
        args.append(final_g)
    scratch = [pltpu.VMEM((tile, d), BF16), pltpu.VMEM((tile, d_ff), BF16)]
    conv_tiles_per_seq = None
    if conv is not None:
        u, seq, cw, cb, lg, lb = conv
        d_conv = cw.shape[1]
        assert seq % tile == 0 and tile % (CONV_SUBTILES * SUBLANES) == 0
        assert CONV_HALO >= CONV_WIDTH - 1
        conv_tiles_per_seq = seq // tile
        in_specs += [pl.BlockSpec((tile, u.shape[1]), lambda i: (0, 0), pipeline_mode=pl.Buffered(1)),
                     pl.BlockSpec((tile, u.shape[1]), lambda i: (jnp.minimum(i + 1, n_tiles - 1), 0)),
                     _full(cw.shape), _full((1, d_conv)), _full((1, d_conv)), _full((1, d_conv))]
        args += [u, u, cw, cb, lg, lb]
        scratch += [pltpu.VMEM((CONV_HALO + tile + SUBLANES, d_conv), F32),
                    pltpu.VMEM((2, SUBLANES, tile // CONV_SUBTILES + SUBLANES, d_conv), F32),
                    pltpu.VMEM((tile, d_conv), BF16)]
    return pl.pallas_call(
        functools.partial(_ffn_kernel, tile=tile, n_chunks=n_chunks, n_out=d // OUT_TILE,
                          n_mix=len(mix), final=final, conv_tiles_per_seq=conv_tiles_per_seq),
        out_shape=jax.ShapeDtypeStruct((n, d), F32),
        grid=(n_tiles,),
        in_specs=in_specs,
        out_specs=pl.BlockSpec((tile, d), row),
        scratch_shapes=scratch,
        compiler_params=_cparams(1),
        name="ffn",
    )(*args)


def _rotary(t, cos, sin_signed, first_half):
    partner = jnp.where(first_half, pltpu.roll(t, LANES - HEAD_DIM // 2, 1),
                        pltpu.roll(t, HEAD_DIM // 2, 1))
    return t * cos + partner * sin_signed


def _inproj_kernel(x_ref, g_ref, w_ref, cos_ref, sin_ref, uconv_ref, q_ref, k_ref, v_ref,
                   qkr_ref, vgr_ref, h_scr, *, d_conv2, d_sb, d_ret):
    h_scr[...] = _rms(x_ref[...], g_ref[...]).astype(BF16)

    def proj(lo, width):
        return jnp.dot(h_scr[...], w_ref[:, lo:lo + width], preferred_element_type=F32)

    uconv_ref[...] = proj(0, d_conv2)
    o = d_conv2
    q_ref[...] = proj(o, d_sb).astype(BF16)
    k_ref[...] = proj(o + d_sb, d_sb).astype(BF16)
    v_ref[...] = proj(o + 2 * d_sb, d_sb).astype(BF16)
    o += 3 * d_sb
    qk = proj(o, 2 * d_ret)
    cos = cos_ref[...]
    sin = sin_ref[...]
    lane = lax.broadcasted_iota(jnp.int32, (1, LANES), 1)
    first_half = (lane % HEAD_DIM) < (HEAD_DIM // 2)
    for c in range(2 * d_ret // LANES):
        sl = slice(c * LANES, (c + 1) * LANES)
        qkr_ref[:, sl] = _rotary(qk[:, sl], cos, sin, first_half)
    vgr_ref[...] = proj(o + 2 * d_ret, 2 * d_ret)


def _inproj(x, g, w, cos_tab, sin_tab, seq, d_conv2, d_sb, d_ret):
    n, d = x.shape
    tiles_per_seq = seq // FFN_TILE
    row = lambda i: (i, 0)
    pos = lambda i: (i % tiles_per_seq, 0)
    outs = [jax.ShapeDtypeStruct((n, d_conv2), F32)] + [jax.ShapeDtypeStruct((n, d_sb), BF16)] * 3 \
        + [jax.ShapeDtypeStruct((n, 2 * d_ret), F32)] * 2
    return pl.pallas_call(
        functools.partial(_inproj_kernel, d_conv2=d_conv2, d_sb=d_sb, d_ret=d_ret),
        out_shape=outs,
        grid=(n // FFN_TILE,),
        in_specs=[pl.BlockSpec((FFN_TILE, d), row), _full((1, d)), _resident(w.shape),
                  pl.BlockSpec((FFN_TILE, LANES), pos), pl.BlockSpec((FFN_TILE, LANES), pos)],
        out_specs=[pl.BlockSpec((FFN_TILE, s.shape[1]), row) for s in outs],
        scratch_shapes=[pltpu.VMEM((FFN_TILE, d), BF16)],
        compiler_params=_cparams(1),
        name="mixer_inproj",
    )(x, g, w, cos_tab, sin_tab)


def _sb_kernel(q_ref, k_ref, v_ref, tri_ref, o_ref, qs_scr, acc_scr, carry_scr, *, n_pairs):
    q0 = pl.program_id(1) * SB_QROWS
    tiles, pairs = range(SB_TILES), range(n_pairs)
    rows = SB_HEADS_PER_GROUP * SB_TILE
    lane = lax.broadcasted_iota(jnp.int32, (1, LANES), 1)
    col = lax.broadcasted_iota(jnp.int32, (rows, SB_BLOCK), 1)
    row = lax.broadcasted_iota(jnp.int32, (rows, SB_BLOCK), 0) % SB_TILE
    for p in pairs:
        q = q_ref[0, :, p * LANES:(p + 1) * LANES]
        for t in tiles:
            qt = q[t * SB_TILE:(t + 1) * SB_TILE]
            qs_scr[t, p] = jnp.concatenate(
                [jnp.where(lane // HEAD_DIM == hh, qt, jnp.zeros_like(qt))
                 for hh in range(SB_HEADS_PER_GROUP)], axis=0)

    def sweep(blocks, resume):
        depth = range(len(blocks[0]))

        def rows_of(ref, t, n, p):
            return ref[0, pl.ds(blocks[t][n][0], SB_BLOCK), p * LANES:(p + 1) * LANES]

        units = [(t, n) for t in tiles for n in depth]

        def logits(t, n):
            return [lax.dot_general(qs_scr[t, p], rows_of(k_ref, t, n, p),
                                    (((1,), (1,)), ((), ())), preferred_element_type=F32)
                    for p in pairs]

        def cumsums(t, n, zs):
            hls = []
            for p in pairs:
                zbits = lax.bitcast_convert_type(zs[p], jnp.uint32)
                neg_abs = lax.bitcast_convert_type(zbits | jnp.uint32(SIGN_BIT), F32)
                sp = jnp.maximum(zs[p], 0.0) + jnp.log(1.0 + jnp.exp(neg_abs))
                if blocks[t][n][1] is not None:
                    sp = jnp.where(blocks[t][n][1], sp, 0.0)
                hi = lax.bitcast_convert_type(
                    lax.bitcast_convert_type(sp, jnp.uint32) & jnp.uint32(BF16_BITS), F32)
                hls.append(jnp.concatenate([hi.astype(BF16), (sp - hi).astype(BF16)], axis=1))
            return jnp.dot(jnp.concatenate(hls, axis=0), tri_ref[...], preferred_element_type=F32)

        carry = {(t, p): carry_scr[t, p] if resume else None for t in tiles for p in pairs}
        acc = {(t, p): acc_scr[t, p] if resume else None for t in tiles for p in pairs}

        def weigh(t, n, zs, rt):
            for p in pairs:
                within = rt[p * rows:(p + 1) * rows, :SB_BLOCK]
                total = rt[p * rows:(p + 1) * rows, SB_BLOCK:]
                before = carry[t, p]
                w = jnp.exp(zs[p] - (within if before is None else within + before))
                if blocks[t][n][1] is not None:
                    w = jnp.where(blocks[t][n][1], w, 0.0)
                carry[t, p] = total if before is None else before + total
                pv = jnp.dot(w.astype(BF16), rows_of(v_ref, t, n, p), preferred_element_type=F32)
                acc[t, p] = pv if acc[t, p] is None else acc[t, p] + pv

        zs, rts = {}, {}
        for u in range(len(units) + 2):
            if u < len(units):
                zs[u] = logits(*units[u])
            if 1 <= u <= len(units):
                rts[u - 1] = cumsums(*units[u - 1], zs[u - 1])
            if u >= 2:
                weigh(*units[u - 2], zs.pop(u - 2), rts.pop(u - 2))
        lowest = None
        for t in tiles:
            for p in pairs:
                carry_scr[t, p] = carry[t, p]
                acc_scr[t, p] = acc[t, p]
                lowest = carry[t, p] if lowest is None else jnp.minimum(lowest, carry[t, p])
        return jnp.min(lowest)

    def aligned(start):
        return pl.multiple_of(start, SB_TILE)

    def first_fast():
        near_mask = col < row + (SB_BLOCK - SB_TILE)
        return sweep([[(aligned(q0 + (t + 1) * SB_TILE - (n + 1) * SB_BLOCK),
                        near_mask if n == 0 else None) for n in range(SB_FIRST_BLOCKS)]
                      for t in tiles], False)

    def first_clamped():
        blocks = []
        for t in tiles:
            limit = q0 + t * SB_TILE + row
            blocks.append([])
            for n in range(SB_FIRST_BLOCKS):
                start = aligned(jnp.maximum(q0 + (t + 1) * SB_TILE - (n + 1) * SB_BLOCK, 0))
                blocks[t].append((start, start + col < limit))
                limit = start
        return sweep(blocks, False)

    def more(st):
        reach = q0 + SB_QROWS - st[0] * SB_BLOCK
        return jnp.logical_and(reach > 0, st[1] < SB_DEAD_LOG)

    def body(st):
        blocks = []
        for t in tiles:
            limit = jnp.maximum(q0 + (t + 1) * SB_TILE - st[0] * SB_BLOCK, 0)
            start = aligned(jnp.maximum(limit - SB_BLOCK, 0))
            blocks.append([(start, start + col < limit)])
        return st[0] + 1, sweep(blocks, True)

    fits = q0 + SB_TILE - SB_FIRST_BLOCKS * SB_BLOCK >= 0
    lowest = lax.cond(fits, first_fast, first_clamped)
    lax.while_loop(more, body, (SB_FIRST_BLOCKS, lowest))
    for p in pairs:
        for t in tiles:
            acc = acc_scr[t, p]
            out = acc[0:SB_TILE]
            for hh in range(1, SB_HEADS_PER_GROUP):
                out = jnp.where(lane // HEAD_DIM == hh, acc[hh * SB_TILE:(hh + 1) * SB_TILE], out)
            o_ref[0, t * SB_TILE:(t + 1) * SB_TILE, p * LANES:(p + 1) * LANES] = out.astype(BF16)


def _stick_breaking(q, k, v, tri):
    b, s, d_sb = q.shape
    n_pairs = d_sb // LANES
    rows = SB_HEADS_PER_GROUP * SB_TILE
    blk = pl.BlockSpec((1, SB_QROWS, d_sb), lambda bi, i: (bi, i, 0))
    whole = pl.BlockSpec((1, s, d_sb), lambda bi, i: (bi, 0, 0))
    return pl.pallas_call(
        functools.partial(_sb_kernel, n_pairs=n_pairs),
        out_shape=jax.ShapeDtypeStruct((b, s, d_sb), BF16),
        grid=(b, s // SB_QROWS),
        in_specs=[blk, whole, whole, _full(tri.shape)],
        out_specs=blk,
        scratch_shapes=[pltpu.VMEM((SB_TILES, n_pairs, rows, LANES), BF16),
                        pltpu.VMEM((SB_TILES, n_pairs, rows, LANES), F32),
                        pltpu.VMEM((SB_TILES, n_pairs, rows, LANES), F32)],
        compiler_params=_cparams(2),
        name="stick_breaking",
    )(q, k, v, tri)


def _ret_kernel(qk_ref, vg_ref, dmat_ref, qdec_ref, kdec_ref, cdec_ref, bd_ref, avg_ref, gain_ref,
                o_ref, state_scr, *, d_ret, n_batch):
    n_heads = d_ret // HEAD_DIM
    batches = range(n_batch)
    nt = (((1,), (1,)), ((), ()))

    @pl.when(pl.program_id(0) == 0)
    def _():
        state_scr[...] = jnp.zeros_like(state_scr)

    lane = lax.broadcasted_iota(jnp.int32, (1, d_ret), 1)
    in_head = [lane // HEAD_DIM == h for h in range(n_heads)]
    q = [qk_ref[b, :, :d_ret] for b in batches]
    k = [qk_ref[b, :, d_ret:] for b in batches]
    v = [vg_ref[b, :, :d_ret] for b in batches]
    probs = []
    for b in batches:
        kb = k[b].astype(BF16)
        probs.append(jnp.concatenate(
            [(lax.dot_general(jnp.where(in_head[h], q[b], 0.0).astype(BF16), kb, nt,
                              preferred_element_type=F32) * dmat_ref[h]).astype(BF16)
             for h in range(n_heads)], axis=1))
    y = []
    for b in batches:
        vals = jnp.concatenate([jnp.where(in_head[h], v[b], 0.0).astype(BF16)
                                for h in range(n_heads)], axis=0)
        y.append(jnp.dot(probs[b], vals, preferred_element_type=F32))
    for b in batches:
        state = state_scr[b]
        y[b] = y[b] + jnp.dot((q[b] * qdec_ref[...]).astype(BF16), state.astype(BF16),
                              preferred_element_type=F32)
        kv = lax.dot_general((k[b] * kdec_ref[...]).astype(BF16), v[b].astype(BF16),
                             (((0,), (0,)), ((), ())), preferred_element_type=F32)
        state_scr[b] = cdec_ref[...] * state + bd_ref[...] * kv

    def head_mean(t):
        hi = t.astype(BF16)
        lo = (t - hi.astype(F32)).astype(BF16)
        return jnp.dot(jnp.concatenate([hi, lo], axis=1), avg_ref[...], preferred_element_type=F32)

    yc = [y[b] - head_mean(y[b]) for b in batches]
    var = [head_mean(yc[b] * yc[b]) for b in batches]
    for b in batches:
        g = vg_ref[b, :, d_ret:]
        o_ref[b] = (_silu(g) * (yc[b] * lax.rsqrt(var[b] + EPS) * gain_ref[...])).astype(BF16)


def _retention(qk, vg, tabs, gain):
    b, s, d2 = qk.shape
    d_ret = d2 // 2
    blk = pl.BlockSpec((b, RET_BLOCK, d2), lambda i: (0, i, 0))
    return pl.pallas_call(
        functools.partial(_ret_kernel, d_ret=d_ret, n_batch=b),
        out_shape=jax.ShapeDtypeStruct((b, s, d_ret), BF16),
        grid=(s // RET_BLOCK,),
        in_specs=[blk, blk] + [_full(t.shape) for t in tabs] + [_full((1, d_ret))],
        out_specs=pl.BlockSpec((b, RET_BLOCK, d_ret), lambda i: (0, i, 0)),
        scratch_shapes=[pltpu.VMEM((b, d_ret, d_ret), F32)],
        compiler_params=_cparams(1),
        name="retention",
    )(qk, vg, *tabs, gain)


def _rotary_tables(seq):
    half = HEAD_DIM // 2
    inv = 1.0 / (ROPE_BASE ** (jnp.arange(half, dtype=F32) / half))
    lane = jnp.arange(LANES)
    ang = jnp.arange(seq).astype(F32)[:, None] * inv[lane % half][None, :]
    sign = jnp.where((lane % HEAD_DIM) < half, -1.0, 1.0).astype(F32)
    return jnp.cos(ang), jnp.sin(ang) * sign[None, :]


def _retention_tables(n_heads):
    d_ret = n_heads * HEAD_DIM
    log_gamma = jnp.log1p(-jnp.exp2(-5.0 - jnp.arange(n_heads, dtype=F32)))
    idx = jnp.arange(RET_BLOCK, dtype=F32)
    t, s = idx[:, None], idx[None, :]
    same = (t // CHUNK) == (s // CHUNK)
    dist = jnp.where(same, jnp.abs(t - s), t - s)
    seen = same | ((s // CHUNK) < (t // CHUNK))
    dmat = jnp.where(seen[None], jnp.exp(log_gamma[:, None, None] * dist[None]), 0.0)
    lane_gamma = jnp.repeat(log_gamma, HEAD_DIM)
    qdec = jnp.exp(lane_gamma[None, :] * (idx + 1.0)[:, None])
    kdec = jnp.exp(lane_gamma[None, :] * (RET_BLOCK - 1.0 - idx)[:, None])
    head = jnp.arange(d_ret) // HEAD_DIM
    bd = (head[:, None] == head[None, :]).astype(F32)
    cdec = bd * jnp.exp(lane_gamma * RET_BLOCK)[:, None]
    avg = jnp.concatenate([bd, bd], axis=0).astype(BF16) * (1.0 / HEAD_DIM)
    return dmat, qdec, kdec, cdec, bd, avg.astype(BF16)


def _tri_table():
    r = np.arange(2 * SB_BLOCK)[:, None] % SB_BLOCK
    c = np.arange(2 * SB_BLOCK)[None, :]
    return jnp.asarray((c >= SB_BLOCK) | (r >= c), dtype=BF16)


def kernel(x, ffn1_norm, ffn1_w_in, ffn1_w_out, mix_norm, mix_w_in, conv_w, conv_b, conv_ln_g,
           conv_ln_b, ret_norm_g, mix_w_out, ffn2_norm, ffn2_w_in, ffn2_w_out, final_norm):
    b, s, d = x.shape
    depth = ffn1_norm.shape[0]
    d_conv = conv_w.shape[2]
    d_ret = ret_norm_g.shape[1]
    d_sb = d - d_conv - d_ret
    assert s % ROW_TILE == 0 and s % FFN_TILE == 0 and s % RET_BLOCK == 0 and s % SB_QROWS == 0
    assert ffn1_w_out.shape[1] % FF_CHUNK == 0 and d % OUT_TILE == 0

    cos_tab, sin_tab = _rotary_tables(s)
    ret_tabs = _retention_tables(d_ret // HEAD_DIM)
    tri = _tri_table()
    scale = HEAD_DIM ** -0.5
    col = jnp.arange(mix_w_in.shape[2])
    q_sb_cols = (col >= 2 * d_conv) & (col < 2 * d_conv + d_sb)
    q_r_cols = (col >= 2 * d_conv + 3 * d_sb) & (col < 2 * d_conv + 3 * d_sb + d_ret)
    col_scale = jnp.where(q_sb_cols | q_r_cols, scale, 1.0).astype(F32)

    row = lambda g: g.reshape(1, -1)
    xf = x.reshape(b * s, d)
    for l in range(depth):
        win1, wout1 = ffn1_w_in[l].astype(BF16), ffn1_w_out[l].astype(BF16)
        win2, wout2 = ffn2_w_in[l].astype(BF16), ffn2_w_out[l].astype(BF16)
        w_mix = (mix_w_in[l] * col_scale[None, :]).astype(BF16)

        xf = _ffn(xf, row(ffn1_norm[l]), win1, wout1, FFN_TILE)
        uconv, q, k, v, qkr, vgr = _inproj(xf, row(mix_norm[l]), w_mix, cos_tab, sin_tab, s,
                                           2 * d_conv, d_sb, d_ret)
        seq3 = lambda t: t.reshape(b, s, t.shape[1])
        y_sb = _stick_breaking(seq3(q), seq3(k), seq3(v), tri)
        y_ret = _retention(seq3(qkr), seq3(vgr), ret_tabs, row(ret_norm_g[l]))
        flat = lambda t: t.reshape(b * s, t.shape[2])
        last = l == depth - 1
        xf = _ffn(xf, row(ffn2_norm[l]), win2, wout2, ROW_TILE,
                  mix=(flat(y_sb), flat(y_ret)), wmix=mix_w_out[l].astype(BF16),
                  final_g=row(final_norm) if last else None,
                  conv=(uconv, s, conv_w[l], row(conv_b[l]), row(conv_ln_g[l]), row(conv_ln_b[l])))
    return xf.reshape(b, s, d)
```

```python
import functools

import jax
import jax.numpy as jnp
import numpy as np
from jax import lax
from jax.experimental import pallas as pl
from jax.experimental.pallas import tpu as pltpu

F32 = jnp.float32
BF16 = jnp.bfloat16

EPS = 1e-6
ROPE_BASE = 10000.0
CHUNK = 64
CONV_WIDTH = 31
HEAD_DIM = 64
LANES = 128
SUBLANES = 8
MXU_DIM = 256

ROW_TILE = 512
FFN_TILE = 1024
FF_CHUNK = MXU_DIM
OUT_TILE = 512
CONV_HALO = 32
CONV_SUBTILES = 8
CONV_CHUNK_GAP = 1
CONV_CHUNK_SPAN = 2
SB_BLOCK = 128
SB_QROWS = 512
SB_TILE = 64
SB_TILES = SB_QROWS // SB_TILE
SB_HEADS_PER_GROUP = LANES // HEAD_DIM
SB_FIRST_BLOCKS = 2
SB_DEAD_LOG = 110.0
SIGN_BIT = 0x80000000
BF16_BITS = 0xFFFF0000
RET_BLOCK = 256
VMEM_LIMIT = 56 * 1024 * 1024


def _cparams(n_axes):
    return pltpu.CompilerParams(dimension_semantics=("arbitrary",) * n_axes,
                                vmem_limit_bytes=VMEM_LIMIT)


def _rms(x, g):
    return (x * lax.rsqrt(jnp.mean(x * x, axis=-1, keepdims=True) + EPS)) * g


def _silu(x):
    return x * jax.nn.sigmoid(x)


def _full(shape):
    return pl.BlockSpec(shape, lambda *_: (0,) * len(shape))


def _resident(shape):
    return pl.BlockSpec(shape, lambda *_: (0,) * len(shape), pipeline_mode=pl.Buffered(1))


def _bit_zero(x):
    bits = lax.bitcast_convert_type(x, jnp.uint32)
    return lax.bitcast_convert_type((bits >> 16) >> 16, F32)


def _conv_fill(u, v_scr, starts_sequence, tile_rows):
    d_conv = v_scr.shape[1]
    if starts_sequence is None:
        v_scr[0:CONV_HALO, :] = jnp.zeros((CONV_HALO, d_conv), F32)
    else:
        v_scr[0:CONV_HALO, :] = jnp.where(starts_sequence, 0.0,
                                          v_scr[tile_rows:tile_rows + CONV_HALO, :])
    v_scr[CONV_HALO:CONV_HALO + tile_rows, :] = u[:, :d_conv] * jax.nn.sigmoid(u[:, d_conv:])
    v_scr[CONV_HALO + tile_rows:, :] = jnp.zeros((SUBLANES, d_conv), F32)


def _conv_rows(v_scr, p_scr, cw_ref, cb_ref, lg_ref, lb_ref, out_ref, r, anchor=None):
    off = CONV_HALO - (CONV_WIDTH - 1)
    sub = p_scr.shape[2] - SUBLANES
    cw = cw_ref[...] if anchor is None else cw_ref[...] + anchor
    for shift in range(SUBLANES):
        part = None
        for o in range(shift, off + CONV_WIDTH, SUBLANES):
            if o < off:
                continue
            rows = v_scr[pl.ds(r * sub + o - shift, sub + SUBLANES), :]
            term = cw[o - off:o - off + 1, :] * rows
            part = term if part is None else part + term
        p_scr[r % 2, shift] = part
    y = cb_ref[...] + p_scr[r % 2, 0, 0:sub, :]
    for shift in range(1, SUBLANES):
        y = y + p_scr[r % 2, shift, pl.ds(shift, sub), :]
    mu = jnp.mean(y, axis=-1, keepdims=True)
    yc = y - mu
    var = jnp.mean(yc * yc, axis=-1, keepdims=True)
    ln = yc * lax.rsqrt(var + EPS) * lg_ref[...] + lb_ref[...]
    out = _silu(ln)
    out_ref[r * sub:(r + 1) * sub, :] = out.astype(BF16)
    return out[0:1, :]


def _ffn_kernel(*refs, tile, n_chunks, n_out, n_mix, final, conv_tiles_per_seq):
    refs = list(refs)
    x_ref = refs.pop(0)
    mix_refs = [refs.pop(0) for _ in range(n_mix)]
    wmix_ref = refs.pop(0) if n_mix else None
    g_ref, win_ref, wout_ref = refs.pop(0), refs.pop(0), refs.pop(0)
    fg_ref = refs.pop(0) if final else None
    has_conv = conv_tiles_per_seq is not None
    if has_conv:
        u_first_ref, u_next_ref = refs.pop(0), refs.pop(0)
        conv_params = [refs.pop(0) for _ in range(4)]
    o_ref, h_scr, a_scr = refs.pop(0), refs.pop(0), refs.pop(0)
    out_tiles = [slice(m * OUT_TILE, (m + 1) * OUT_TILE) for m in range(n_out)]
    d_ff = wout_ref.shape[0]
    n_sub = CONV_SUBTILES
    if has_conv:
        v_scr, p_scr, yc_scr = refs.pop(0), refs.pop(0), refs.pop(0)
        step = pl.program_id(0)

        @pl.when(step == 0)
        def _():
            _conv_fill(u_first_ref[...], v_scr, None, tile)
            for r in range(n_sub):
                _conv_rows(v_scr, p_scr, *conv_params, yc_scr, r)

        mix_refs = [yc_scr] + mix_refs
    if mix_refs:
        for sl in out_tiles:
            y, lo = None, 0
            for y_ref in mix_refs:
                part = jnp.dot(y_ref[...], wmix_ref[lo:lo + y_ref.shape[1], sl],
                               preferred_element_type=F32)
                y = part if y is None else y + part
                lo += y_ref.shape[1]
            o_ref[:, sl] = x_ref[:, sl] + y
        x_ref = o_ref
    h_scr[...] = _rms(x_ref[...], g_ref[...]).astype(BF16)
    if has_conv:
        nxt = jnp.minimum(step + 1, pl.num_programs(0) - 1)
        _conv_fill(u_next_ref[...], v_scr, nxt % conv_tiles_per_seq == 0, tile)
    done = {}
    for j in range(n_chunks):
        sl = slice(j * FF_CHUNK, (j + 1) * FF_CHUNK)
        up_sl = slice(d_ff + j * FF_CHUNK, d_ff + (j + 1) * FF_CHUNK)
        gate = jnp.dot(h_scr[...], win_ref[:, sl], preferred_element_type=F32)
        up = jnp.dot(h_scr[...], win_ref[:, up_sl], preferred_element_type=F32)
        if j in done:
            gate = gate + done.pop(j)
        a_scr[:, sl] = (_silu(gate) * up).astype(BF16)
        if has_conv and j % CONV_CHUNK_GAP == 0 and j // CONV_CHUNK_GAP < n_sub:
            r = j // CONV_CHUNK_GAP
            out_row = _conv_rows(v_scr, p_scr, *conv_params, yc_scr, r,
                                 anchor=_bit_zero(gate[0:1, :conv_params[0].shape[1]]))
            done[j + CONV_CHUNK_SPAN] = _bit_zero(out_row)
    assert not done
    for sl in out_tiles:
        y = jnp.dot(a_scr[...], wout_ref[:, sl], preferred_element_type=F32)
        o_ref[:, sl] = x_ref[:, sl] + 0.5 * y
    if final:
        o_ref[...] = _rms(o_ref[...], fg_ref[...])


def _ffn(x, g, win, wout, tile, mix=(), wmix=None, final_g=None, conv=None):
    n, d = x.shape
    d_ff = wout.shape[0]
    n_chunks = d_ff // FF_CHUNK
    n_tiles = n // tile
    final = final_g is not None
    row = lambda i: (i, 0)
    in_specs = [pl.BlockSpec((tile, d), row)]
    in_specs += [pl.BlockSpec((tile, y.shape[1]), row) for y in mix]
    args = [x, *mix]
    if mix:
        in_specs.append(_resident(wmix.shape))
        args.append(wmix)
    in_specs += [_full((1, d)), _resident(win.shape), _resident(wout.shape)]
    args += [g, win, wout]
    if final:
        in_specs.append(_full((1, d)))
        args.append(final_g)
    scratch = [pltpu.VMEM((tile, d), BF16), pltpu.VMEM((tile, d_ff), BF16)]
    conv_tiles_per_seq = None
    if conv is not None:
        u, seq, cw, cb, lg, lb = conv
        d_conv = cw.shape[1]
        assert seq % tile == 0 and tile % (CONV_SUBTILES * SUBLANES) == 0
        assert CONV_HALO >= CONV_WIDTH - 1
        conv_tiles_per_seq = seq // tile
        in_specs += [pl.BlockSpec((tile, u.shape[1]), lambda i: (0, 0), pipeline_mode=pl.Buffered(1)),
                     pl.BlockSpec((tile, u.shape[1]), lambda i: (jnp.minimum(i + 1, n_tiles - 1), 0)),
                     _full(cw.shape), _full((1, d_conv)), _full((1, d_conv)), _full((1, d_conv))]
        args += [u, u, cw, cb, lg, lb]
        scratch += [pltpu.VMEM((CONV_HALO + tile + SUBLANES, d_conv), F32),
                    pltpu.VMEM((2, SUBLANES, tile // CONV_SUBTILES + SUBLANES, d_conv), F32),
                    pltpu.VMEM((tile, d_conv), BF16)]
    return pl.pallas_call(
        functools.partial(_ffn_kernel, tile=tile, n_chunks=n_chunks, n_out=d // OUT_TILE,
                          n_mix=len(mix), final=final, conv_tiles_per_seq=conv_tiles_per_seq),
        out_shape=jax.ShapeDtypeStruct((n, d), F32),
        grid=(n_tiles,),
        in_specs=in_specs,
        out_specs=pl.BlockSpec((tile, d), row),
        scratch_shapes=scratch,
        compiler_params=_cparams(1),
        name="ffn",
    )(*args)


def _rotary(t, cos, sin_signed, first_half):
    partner = jnp.where(first_half, pltpu.roll(t, LANES - HEAD_DIM // 2, 1),
                        pltpu.roll(t, HEAD_DIM // 2, 1))
    return t * cos + partner * sin_signed


def _inproj_kernel(x_ref, g_ref, w_ref, cos_ref, sin_ref, uconv_ref, q_ref, k_ref, v_ref,
                   qkr_ref, vgr_ref, h_scr, *, d_conv2, d_sb, d_ret):
    h_scr[...] = _rms(x_ref[...], g_ref[...]).astype(BF16)

    def proj(lo, width):
        return jnp.dot(h_scr[...], w_ref[:, lo:lo + width], preferred_element_type=F32)

    uconv_ref[...] = proj(0, d_conv2)
    o = d_conv2
    q_ref[...] = proj(o, d_sb).astype(BF16)
    k_ref[...] = proj(o + d_sb, d_sb).astype(BF16)
    v_ref[...] = proj(o + 2 * d_sb, d_sb).astype(BF16)
    o += 3 * d_sb
    qk = proj(o, 2 * d_ret)
    cos = cos_ref[...]
    sin = sin_ref[...]
    lane = lax.broadcasted_iota(jnp.int32, (1, LANES), 1)
    first_half = (lane % HEAD_DIM) < (HEAD_DIM // 2)
    for c in range(2 * d_ret // LANES):
        sl = slice(c * LANES, (c + 1) * LANES)
        qkr_ref[:, sl] = _rotary(qk[:, sl], cos, sin, first_half)
    vgr_ref[...] = proj(o + 2 * d_ret, 2 * d_ret)


def _inproj(x, g, w, cos_tab, sin_tab, seq, d_conv2, d_sb, d_ret):
    n, d = x.shape
    tiles_per_seq = seq // FFN_TILE
    row = lambda i: (i, 0)
    pos = lambda i: (i % tiles_per_seq, 0)
    outs = [jax.ShapeDtypeStruct((n, d_conv2), F32)] + [jax.ShapeDtypeStruct((n, d_sb), BF16)] * 3 \
        + [jax.ShapeDtypeStruct((n, 2 * d_ret), F32)] * 2
    return pl.pallas_call(
        functools.partial(_inproj_kernel, d_conv2=d_conv2, d_sb=d_sb, d_ret=d_ret),
        out_shape=outs,
        grid=(n // FFN_TILE,),
        in_specs=[pl.BlockSpec((FFN_TILE, d), row), _full((1, d)), _resident(w.shape),
                  pl.BlockSpec((FFN_TILE, LANES), pos), pl.BlockSpec((FFN_TILE, LANES), pos)],
        out_specs=[pl.BlockSpec((FFN_TILE, s.shape[1]), row) for s in outs],
        scratch_shapes=[pltpu.VMEM((FFN_TILE, d), BF16)],
        compiler_params=_cparams(1),
        name="mixer_inproj",
    )(x, g, w, cos_tab, sin_tab)


def _sb_kernel(q_ref, knew_ref, vnew_ref, tri_ref, o_ref, k_scr, v_scr, qs_scr, acc_scr, carry_scr,
               *, n_pairs):
    q0 = pl.program_id(1) * SB_QROWS
    k_scr[pl.ds(pl.multiple_of(q0, SB_QROWS), SB_QROWS), :] = knew_ref[0]
    v_scr[pl.ds(pl.multiple_of(q0, SB_QROWS), SB_QROWS), :] = vnew_ref[0]
    tiles, pairs = range(SB_TILES), range(n_pairs)
    rows = SB_HEADS_PER_GROUP * SB_TILE
    lane = lax.broadcasted_iota(jnp.int32, (1, LANES), 1)
    col = lax.broadcasted_iota(jnp.int32, (rows, SB_BLOCK), 1)
    row = lax.broadcasted_iota(jnp.int32, (rows, SB_BLOCK), 0) % SB_TILE
    for p in pairs:
        q = q_ref[0, :, p * LANES:(p + 1) * LANES]
        for t in tiles:
            qt = q[t * SB_TILE:(t + 1) * SB_TILE]
            qs_scr[t, p] = jnp.concatenate(
                [jnp.where(lane // HEAD_DIM == hh, qt, jnp.zeros_like(qt))
                 for hh in range(SB_HEADS_PER_GROUP)], axis=0)

    def sweep(blocks, resume):
        depth = range(len(blocks[0]))

        def rows_of(ref, t, n, p):
            return ref[pl.ds(blocks[t][n][0], SB_BLOCK), p * LANES:(p + 1) * LANES]

        units = [(t, n) for t in tiles for n in depth]

        def logits(t, n):
            return [lax.dot_general(qs_scr[t, p], rows_of(k_scr, t, n, p),
                                    (((1,), (1,)), ((), ())), preferred_element_type=F32)
                    for p in pairs]

        def cumsums(t, n, zs):
            hls = []
            for p in pairs:
                zbits = lax.bitcast_convert_type(zs[p], jnp.uint32)
                neg_abs = lax.bitcast_convert_type(zbits | jnp.uint32(SIGN_BIT), F32)
                sp = jnp.maximum(zs[p], 0.0) + jnp.log(1.0 + jnp.exp(neg_abs))
                if blocks[t][n][1] is not None:
                    sp = jnp.where(blocks[t][n][1], sp, 0.0)
                hi = lax.bitcast_convert_type(
                    lax.bitcast_convert_type(sp, jnp.uint32) & jnp.uint32(BF16_BITS), F32)
                hls.append(jnp.concatenate([hi.astype(BF16), (sp - hi).astype(BF16)], axis=1))
            return jnp.dot(jnp.concatenate(hls, axis=0), tri_ref[...], preferred_element_type=F32)

        carry = {(t, p): carry_scr[t, p] if resume else None for t in tiles for p in pairs}
        acc = {(t, p): acc_scr[t, p] if resume else None for t in tiles for p in pairs}

        def weigh(t, n, zs, rt):
            for p in pairs:
                within = rt[p * rows:(p + 1) * rows, :SB_BLOCK]
                total = rt[p * rows:(p + 1) * rows, SB_BLOCK:]
                before = carry[t, p]
                w = jnp.exp(zs[p] - (within if before is None else within + before))
                if blocks[t][n][1] is not None:
                    w = jnp.where(blocks[t][n][1], w, 0.0)
                carry[t, p] = total if before is None else before + total
                pv = jnp.dot(w.astype(BF16), rows_of(v_scr, t, n, p), preferred_element_type=F32)
                acc[t, p] = pv if acc[t, p] is None else acc[t, p] + pv

        zs, rts = {}, {}
        for u in range(len(units) + 2):
            if u < len(units):
                zs[u] = logits(*units[u])
            if 1 <= u <= len(units):
                rts[u - 1] = cumsums(*units[u - 1], zs[u - 1])
            if u >= 2:
                weigh(*units[u - 2], zs.pop(u - 2), rts.pop(u - 2))
        lowest = None
        for t in tiles:
            for p in pairs:
                carry_scr[t, p] = carry[t, p]
                acc_scr[t, p] = acc[t, p]
                lowest = carry[t, p] if lowest is None else jnp.minimum(lowest, carry[t, p])
        return jnp.min(lowest)

    def aligned(start):
        return pl.multiple_of(start, SB_TILE)

    def first_fast():
        near_mask = col < row + (SB_BLOCK - SB_TILE)
        return sweep([[(aligned(q0 + (t + 1) * SB_TILE - (n + 1) * SB_BLOCK),
                        near_mask if n == 0 else None) for n in range(SB_FIRST_BLOCKS)]
                      for t in tiles], False)

    def first_clamped():
        blocks = []
        for t in tiles:
            limit = q0 + t * SB_TILE + row
            blocks.append([])
            for n in range(SB_FIRST_BLOCKS):
                start = aligned(jnp.maximum(q0 + (t + 1) * SB_TILE - (n + 1) * SB_BLOCK, 0))
                blocks[t].append((start, start + col < limit))
                limit = start
        return sweep(blocks, False)

    def more(st):
        reach = q0 + SB_QROWS - st[0] * SB_BLOCK
        return jnp.logical_and(reach > 0, st[1] < SB_DEAD_LOG)

    def body(st):
        blocks = []
        for t in tiles:
            limit = jnp.maximum(q0 + (t + 1) * SB_TILE - st[0] * SB_BLOCK, 0)
            start = aligned(jnp.maximum(limit - SB_BLOCK, 0))
            blocks.append([(start, start + col < limit)])
        return st[0] + 1, sweep(blocks, True)

    fits = q0 + SB_TILE - SB_FIRST_BLOCKS * SB_BLOCK >= 0
    lowest = lax.cond(fits, first_fast, first_clamped)
    lax.while_loop(more, body, (SB_FIRST_BLOCKS, lowest))
    for p in pairs:
        for t in tiles:
            acc = acc_scr[t, p]
            out = acc[0:SB_TILE]
            for hh in range(1, SB_HEADS_PER_GROUP):
                out = jnp.where(lane // HEAD_DIM == hh, acc[hh * SB_TILE:(hh + 1) * SB_TILE], out)
            o_ref[0, t * SB_TILE:(t + 1) * SB_TILE, p * LANES:(p + 1) * LANES] = out.astype(BF16)


def _stick_breaking(q, k, v, tri):
    b, s, d_sb = q.shape
    n_pairs = d_sb // LANES
    rows = SB_HEADS_PER_GROUP * SB_TILE
    blk = pl.BlockSpec((1, SB_QROWS, d_sb), lambda bi, i: (bi, i, 0))
    return pl.pallas_call(
        functools.partial(_sb_kernel, n_pairs=n_pairs),
        out_shape=jax.ShapeDtypeStruct((b, s, d_sb), BF16),
        grid=(b, s // SB_QROWS),
        in_specs=[blk, blk, blk, _full(tri.shape)],
        out_specs=blk,
        scratch_shapes=[pltpu.VMEM((s, d_sb), BF16), pltpu.VMEM((s, d_sb), BF16),
                        pltpu.VMEM((SB_TILES, n_pairs, rows, LANES), BF16),
                        pltpu.VMEM((SB_TILES, n_pairs, rows, LANES), F32),
                        pltpu.VMEM((SB_TILES, n_pairs, rows, LANES), F32)],
        compiler_params=_cparams(2),
        name="stick_breaking",
    )(q, k, v, tri)


def _ret_kernel(qk_ref, vg_ref, dmat_ref, qdec_ref, kdec_ref, cdec_ref, bd_ref, avg_ref, gain_ref,
                o_ref, state_scr, *, d_ret, n_batch):
    n_heads = d_ret // HEAD_DIM
    batches = range(n_batch)
    nt = (((1,), (1,)), ((), ()))

    @pl.when(pl.program_id(0) == 0)
    def _():
        state_scr[...] = jnp.zeros_like(state_scr)

    lane = lax.broadcasted_iota(jnp.int32, (1, d_ret), 1)
    in_head = [lane // HEAD_DIM == h for h in range(n_heads)]
    q = [qk_ref[b, :, :d_ret] for b in batches]
    k = [qk_ref[b, :, d_ret:] for b in batches]
    v = [vg_ref[b, :, :d_ret] for b in batches]
    probs = []
    for b in batches:
        kb = k[b].astype(BF16)
        probs.append(jnp.concatenate(
            [(lax.dot_general(jnp.where(in_head[h], q[b], 0.0).astype(BF16), kb, nt,
                              preferred_element_type=F32) * dmat_ref[h]).astype(BF16)
             for h in range(n_heads)], axis=1))
    y = []
    for b in batches:
        vals = jnp.concatenate([jnp.where(in_head[h], v[b], 0.0).astype(BF16)
                                for h in range(n_heads)], axis=0)
        y.append(jnp.dot(probs[b], vals, preferred_element_type=F32))
    for b in batches:
        state = state_scr[b]
        y[b] = y[b] + jnp.dot((q[b] * qdec_ref[...]).astype(BF16), state.astype(BF16),
                              preferred_element_type=F32)
        kv = lax.dot_general((k[b] * kdec_ref[...]).astype(BF16), v[b].astype(BF16),
                             (((0,), (0,)), ((), ())), preferred_element_type=F32)
        state_scr[b] = cdec_ref[...] * state + bd_ref[...] * kv

    def head_mean(t):
        hi = t.astype(BF16)
        lo = (t - hi.astype(F32)).astype(BF16)
        return jnp.dot(jnp.concatenate([hi, lo], axis=1), avg_ref[...], preferred_element_type=F32)

    yc = [y[b] - head_mean(y[b]) for b in batches]
    var = [head_mean(yc[b] * yc[b]) for b in batches]
    for b in batches:
        g = vg_ref[b, :, d_ret:]
        o_ref[b] = (_silu(g) * (yc[b] * lax.rsqrt(var[b] + EPS) * gain_ref[...])).astype(BF16)


def _retention(qk, vg, tabs, gain):
    b, s, d2 = qk.shape
    d_ret = d2 // 2
    blk = pl.BlockSpec((b, RET_BLOCK, d2), lambda i: (0, i, 0))
    return pl.pallas_call(
        functools.partial(_ret_kernel, d_ret=d_ret, n_batch=b),
        out_shape=jax.ShapeDtypeStruct((b, s, d_ret), BF16),
        grid=(s // RET_BLOCK,),
        in_specs=[blk, blk] + [_full(t.shape) for t in tabs] + [_full((1, d_ret))],
        out_specs=pl.BlockSpec((b, RET_BLOCK, d_ret), lambda i: (0, i, 0)),
        scratch_shapes=[pltpu.VMEM((b, d_ret, d_ret), F32)],
        compiler_params=_cparams(1),
        name="retention",
    )(qk, vg, *tabs, gain)


def _rotary_tables(seq):
    half = HEAD_DIM // 2
    inv = 1.0 / (ROPE_BASE ** (jnp.arange(half, dtype=F32) / half))
    lane = jnp.arange(LANES)
    ang = jnp.arange(seq).astype(F32)[:, None] * inv[lane % half][None, :]
    sign = jnp.where((lane % HEAD_DIM) < half, -1.0, 1.0).astype(F32)
    return jnp.cos(ang), jnp.sin(ang) * sign[None, :]


def _retention_tables(n_heads):
    d_ret = n_heads * HEAD_DIM
    log_gamma = jnp.log1p(-jnp.exp2(-5.0 - jnp.arange(n_heads, dtype=F32)))
    idx = jnp.arange(RET_BLOCK, dtype=F32)
    t, s = idx[:, None], idx[None, :]
    same = (t // CHUNK) == (s // CHUNK)
    dist = jnp.where(same, jnp.abs(t - s), t - s)
    seen = same | ((s // CHUNK) < (t // CHUNK))
    dmat = jnp.where(seen[None], jnp.exp(log_gamma[:, None, None] * dist[None]), 0.0)
    lane_gamma = jnp.repeat(log_gamma, HEAD_DIM)
    qdec = jnp.exp(lane_gamma[None, :] * (idx + 1.0)[:, None])
    kdec = jnp.exp(lane_gamma[None, :] * (RET_BLOCK - 1.0 - idx)[:, None])
    head = jnp.arange(d_ret) // HEAD_DIM
    bd = (head[:, None] == head[None, :]).astype(F32)
    cdec = bd * jnp.exp(lane_gamma * RET_BLOCK)[:, None]
    avg = jnp.concatenate([bd, bd], axis=0).astype(BF16) * (1.0 / HEAD_DIM)
    return dmat, qdec, kdec, cdec, bd, avg.astype(BF16)


def _tri_table():
    r = np.arange(2 * SB_BLOCK)[:, None] % SB_BLOCK
    c = np.arange(2 * SB_BLOCK)[None, :]
    return jnp.asarray((c >= SB_BLOCK) | (r >= c), dtype=BF16)


def kernel(x, ffn1_norm, ffn1_w_in, ffn1_w_out, mix_norm, mix_w_in, conv_w, conv_b, conv_ln_g,
           conv_ln_b, ret_norm_g, mix_w_out, ffn2_norm, ffn2_w_in, ffn2_w_out, final_norm):
    b, s, d = x.shape
    depth = ffn1_norm.shape[0]
    d_conv = conv_w.shape[2]
    d_ret = ret_norm_g.shape[1]
    d_sb = d - d_conv - d_ret
    assert s % ROW_TILE == 0 and s % FFN_TILE == 0 and s % RET_BLOCK == 0 and s % SB_QROWS == 0
    assert ffn1_w_out.shape[1] % FF_CHUNK == 0 and d % OUT_TILE == 0

    cos_tab, sin_tab = _rotary_tables(s)
    ret_tabs = _retention_tables(d_ret // HEAD_DIM)
    tri = _tri_table()
    scale = HEAD_DIM ** -0.5
    col = jnp.arange(mix_w_in.shape[2])
    q_sb_cols = (col >= 2 * d_conv) & (col < 2 * d_conv + d_sb)
    q_r_cols = (col >= 2 * d_conv + 3 * d_sb) & (col < 2 * d_conv + 3 * d_sb + d_ret)
    col_scale = jnp.where(q_sb_cols | q_r_cols, scale, 1.0).astype(F32)

    row = lambda g: g.reshape(1, -1)
    xf = x.reshape(b * s, d)
    for l in range(depth):
        win1, wout1 = ffn1_w_in[l].astype(BF16), ffn1_w_out[l].astype(BF16)
        win2, wout2 = ffn2_w_in[l].astype(BF16), ffn2_w_out[l].astype(BF16)
        w_mix = (mix_w_in[l] * col_scale[None, :]).astype(BF16)

        xf = _ffn(xf, row(ffn1_norm[l]), win1, wout1, FFN_TILE)
        uconv, q, k, v, qkr, vgr = _inproj(xf, row(mix_norm[l]), w_mix, cos_tab, sin_tab, s,
                                           2 * d_conv, d_sb, d_ret)
        seq3 = lambda t: t.reshape(b, s, t.shape[1])
        y_sb = _stick_breaking(seq3(q), seq3(k), seq3(v), tri)
        y_ret = _retention(seq3(qkr), seq3(vgr), ret_tabs, row(ret_norm_g[l]))
        flat = lambda t: t.reshape(b * s, t.shape[2])
        last = l == depth - 1
        xf = _ffn(xf, row(ffn2_norm[l]), win2, wout2, ROW_TILE,
                  mix=(flat(y_sb), flat(y_ret)), wmix=mix_w_out[l].astype(BF16),
                  final_g=row(final_norm) if last else None,
                  conv=(uconv, s, conv_w[l], row(conv_b[l]), row(conv_ln_g[l]), row(conv_ln_b[l])))
    return xf.reshape(b, s, d)
```

```python
import functools

import jax
import jax.numpy as jnp
import numpy as np
from jax import lax
from jax.experimental import pallas as pl
from jax.experimental.pallas import tpu as pltpu

F32 = jnp.float32
BF16 = jnp.bfloat16

EPS = 1e-6
ROPE_BASE = 10000.0
CHUNK = 64
CONV_WIDTH = 31
HEAD_DIM = 64
LANES = 128
SUBLANES = 8
MXU_DIM = 256

ROW_TILE = 512
FFN_TILE = 1024
FF_CHUNK = MXU_DIM
OUT_TILE = 512
CONV_HALO = 32
CONV_SUBTILES = 8
CONV_CHUNK_GAP = 1
CONV_CHUNK_SPAN = 2
SB_BLOCK = 128
SB_QROWS = 512
SB_TILE = 64
SB_TILES = SB_QROWS // SB_TILE
SB_HEADS_PER_GROUP = LANES // HEAD_DIM
SB_FIRST_BLOCKS = 2
SB_DEAD_LOG = 110.0
SIGN_BIT = 0x80000000
BF16_BITS = 0xFFFF0000
RET_BLOCK = 256
VMEM_LIMIT = 56 * 1024 * 1024


def _cparams(n_axes):
    return pltpu.CompilerParams(dimension_semantics=("arbitrary",) * n_axes,
                                vmem_limit_bytes=VMEM_LIMIT)


def _rms(x, g):
    return (x * lax.rsqrt(jnp.mean(x * x, axis=-1, keepdims=True) + EPS)) * g


def _silu(x):
    return x * jax.nn.sigmoid(x)


def _full(shape):
    return pl.BlockSpec(shape, lambda *_: (0,) * len(shape))


def _resident(shape):
    return pl.BlockSpec(shape, lambda *_: (0,) * len(shape), pipeline_mode=pl.Buffered(1))


def _bit_zero(x):
    bits = lax.bitcast_convert_type(x, jnp.uint32)
    return lax.bitcast_convert_type((bits >> 16) >> 16, F32)


def _conv_fill(u, v_scr, starts_sequence, tile_rows):
    d_conv = v_scr.shape[1]
    if starts_sequence is None:
        v_scr[0:CONV_HALO, :] = jnp.zeros((CONV_HALO, d_conv), F32)
    else:
        v_scr[0:CONV_HALO, :] = jnp.where(starts_sequence, 0.0,
                                          v_scr[tile_rows:tile_rows + CONV_HALO, :])
    v_scr[CONV_HALO:CONV_HALO + tile_rows, :] = u[:, :d_conv] * jax.nn.sigmoid(u[:, d_conv:])
    v_scr[CONV_HALO + tile_rows:, :] = jnp.zeros((SUBLANES, d_conv), F32)


def _conv_rows(v_scr, p_scr, cw_ref, cb_ref, lg_ref, lb_ref, out_ref, r, anchor=None):
    off = CONV_HALO - (CONV_WIDTH - 1)
    sub = p_scr.shape[2] - SUBLANES
    cw = cw_ref[...] if anchor is None else cw_ref[...] + anchor
    for shift in range(SUBLANES):
        part = None
        for o in range(shift, off + CONV_WIDTH, SUBLANES):
            if o < off:
                continue
            rows = v_scr[pl.ds(r * sub + o - shift, sub + SUBLANES), :]
            term = cw[o - off:o - off + 1, :] * rows
            part = term if part is None else part + term
        p_scr[r % 2, shift] = part
    y = cb_ref[...] + p_scr[r % 2, 0, 0:sub, :]
    for shift in range(1, SUBLANES):
        y = y + p_scr[r % 2, shift, pl.ds(shift, sub), :]
    mu = jnp.mean(y, axis=-1, keepdims=True)
    yc = y - mu
    var = jnp.mean(yc * yc, axis=-1, keepdims=True)
    ln = yc * lax.rsqrt(var + EPS) * lg_ref[...] + lb_ref[...]
    out = _silu(ln)
    out_ref[r * sub:(r + 1) * sub, :] = out.astype(BF16)
    return out[0:1, :]


def _ffn_kernel(*refs, tile, n_chunks, n_out, n_mix, final, conv_tiles_per_seq):
    refs = list(refs)
    x_ref = refs.pop(0)
    mix_refs = [refs.pop(0) for _ in range(n_mix)]
    wmix_ref = refs.pop(0) if n_mix else None
    g_ref, win_ref, wout_ref = refs.pop(0), refs.pop(0), refs.pop(0)
    fg_ref = refs.pop(0) if final else None
    has_conv = conv_tiles_per_seq is not None
    if has_conv:
        u_first_ref, u_next_ref = refs.pop(0), refs.pop(0)
        conv_params = [refs.pop(0) for _ in range(4)]
    o_ref, h_scr, a_scr = refs.pop(0), refs.pop(0), refs.pop(0)
    out_tiles = [slice(m * OUT_TILE, (m + 1) * OUT_TILE) for m in range(n_out)]
    d_ff = wout_ref.shape[0]
    n_sub = CONV_SUBTILES
    if has_conv:
        v_scr, p_scr, yc_scr = refs.pop(0), refs.pop(0), refs.pop(0)
        step = pl.program_id(0)

        @pl.when(step == 0)
        def _():
            _conv_fill(u_first_ref[...], v_scr, None, tile)
            for r in range(n_sub):
                _conv_rows(v_scr, p_scr, *conv_params, yc_scr, r)

        mix_refs = [yc_scr] + mix_refs
    if mix_refs:
        for sl in out_tiles:
            y, lo = None, 0
            for y_ref in mix_refs:
                part = jnp.dot(y_ref[...], wmix_ref[lo:lo + y_ref.shape[1], sl],
                               preferred_element_type=F32)
                y = part if y is None else y + part
                lo += y_ref.shape[1]
            o_ref[:, sl] = x_ref[:, sl] + y
        x_ref = o_ref
    h_scr[...] = _rms(x_ref[...], g_ref[...]).astype(BF16)
    if has_conv:
        nxt = jnp.minimum(step + 1, pl.num_programs(0) - 1)
        _conv_fill(u_next_ref[...], v_scr, nxt % conv_tiles_per_seq == 0, tile)
    done = {}
    for j in range(n_chunks):
        sl = slice(j * FF_CHUNK, (j + 1) * FF_CHUNK)
        up_sl = slice(d_ff + j * FF_CHUNK, d_ff + (j + 1) * FF_CHUNK)
        gate = jnp.dot(h_scr[...], win_ref[:, sl], preferred_element_type=F32)
        up = jnp.dot(h_scr[...], win_ref[:, up_sl], preferred_element_type=F32)
        if j in done:
            gate = gate + done.pop(j)
        a_scr[:, sl] = (_silu(gate) * up).astype(BF16)
        if has_conv and j % CONV_CHUNK_GAP == 0 and j // CONV_CHUNK_GAP < n_sub:
            r = j // CONV_CHUNK_GAP
            out_row = _conv_rows(v_scr, p_scr, *conv_params, yc_scr, r,
                                 anchor=_bit_zero(gate[0:1, :conv_params[0].shape[1]]))
            done[j + CONV_CHUNK_SPAN] = _bit_zero(out_row)
    assert not done
    for sl in out_tiles:
        y = jnp.dot(a_scr[...], wout_ref[:, sl], preferred_element_type=F32)
        o_ref[:, sl] = x_ref[:, sl] + 0.5 * y
    if final:
        o_ref[...] = _rms(o_ref[...], fg_ref[...])


def _ffn(x, g, win, wout, tile, mix=(), wmix=None, final_g=None, conv=None):
    n, d = x.shape
    d_ff = wout.shape[0]
    n_chunks = d_ff // FF_CHUNK
    n_tiles = n // tile
    final = final_g is not None
    row = lambda i: (i, 0)
    in_specs = [pl.BlockSpec((tile, d), row)]
    in_specs += [pl.BlockSpec((tile, y.shape[1]), row) for y in mix]
    args = [x, *mix]
    if mix:
        in_specs.append(_resident(wmix.shape))
        args.append(wmix)
    in_specs += [_full((1, d)), _resident(win.shape), _resident(wout.shape)]
    args += [g, win, wout]
    if final:
        in_specs.append(_full((1, d)))
        args.append(final_g)
    scratch = [pltpu.VMEM((tile, d), BF16), pltpu.VMEM((tile, d_ff), BF16)]
    conv_tiles_per_seq = None
    if conv is not None:
        u, seq, cw, cb, lg, lb = conv
        d_conv = cw.shape[1]
        assert seq % tile == 0 and tile % (CONV_SUBTILES * SUBLANES) == 0
        assert CONV_HALO >= CONV_WIDTH - 1
        conv_tiles_per_seq = seq // tile
        in_specs += [pl.BlockSpec((tile, u.shape[1]), lambda i: (0, 0), pipeline_mode=pl.Buffered(1)),
                     pl.BlockSpec((tile, u.shape[1]), lambda i: (jnp.minimum(i + 1, n_tiles - 1), 0)),
                     _full(cw.shape), _full((1, d_conv)), _full((1, d_conv)), _full((1, d_conv))]
        args += [u, u, cw, cb, lg, lb]
        scratch += [pltpu.VMEM((CONV_HALO + tile + SUBLANES, d_conv), F32),
                    pltpu.VMEM((2, SUBLANES, tile // CONV_SUBTILES + SUBLANES, d_conv), F32),
                    pltpu.VMEM((tile, d_conv), BF16)]
    return pl.pallas_call(
        functools.partial(_ffn_kernel, tile=tile, n_chunks=n_chunks, n_out=d // OUT_TILE,
                          n_mix=len(mix), final=final, conv_tiles_per_seq=conv_tiles_per_seq),
        out_shape=jax.ShapeDtypeStruct((n, d), F32),
        grid=(n_tiles,),
        in_specs=in_specs,
        out_specs=pl.BlockSpec((tile, d), row),
        scratch_shapes=scratch,
        compiler_params=_cparams(1),
        name="ffn",
    )(*args)


def _rotary(t, cos, sin_signed, first_half):
    partner = jnp.where(first_half, pltpu.roll(t, LANES - HEAD_DIM // 2, 1),
                        pltpu.roll(t, HEAD_DIM // 2, 1))
    return t * cos + partner * sin_signed


def _inproj_kernel(x_ref, g_ref, w_ref, cos_ref, sin_ref, uconv_ref, q_ref, k_ref, v_ref,
                   qkr_ref, vgr_ref, h_scr, *, d_conv2, d_sb, d_ret):
    h_scr[...] = _rms(x_ref[...], g_ref[...]).astype(BF16)

    def proj(lo, width):
        return jnp.dot(h_scr[...], w_ref[:, lo:lo + width], preferred_element_type=F32)

    uconv_ref[...] = proj(0, d_conv2)
    o = d_conv2
    q_ref[...] = proj(o, d_sb).astype(BF16)
    k_ref[...] = proj(o + d_sb, d_sb).astype(BF16)
    v_ref[...] = proj(o + 2 * d_sb, d_sb).astype(BF16)
    o += 3 * d_sb
    qk = proj(o, 2 * d_ret)
    cos = cos_ref[...]
    sin = sin_ref[...]
    lane = lax.broadcasted_iota(jnp.int32, (1, LANES), 1)
    first_half = (lane % HEAD_DIM) < (HEAD_DIM // 2)
    for c in range(2 * d_ret // LANES):
        sl = slice(c * LANES, (c + 1) * LANES)
        qkr_ref[:, sl] = _rotary(qk[:, sl], cos, sin, first_half)
    vgr_ref[...] = proj(o + 2 * d_ret, 2 * d_ret)


def _inproj(x, g, w, cos_tab, sin_tab, seq, d_conv2, d_sb, d_ret):
    n, d = x.shape
    tiles_per_seq = seq // FFN_TILE
    row = lambda i: (i, 0)
    pos = lambda i: (i % tiles_per_seq, 0)
    outs = [jax.ShapeDtypeStruct((n, d_conv2), F32)] + [jax.ShapeDtypeStruct((n, d_sb), BF16)] * 3 \
        + [jax.ShapeDtypeStruct((n, 2 * d_ret), F32)] * 2
    return pl.pallas_call(
        functools.partial(_inproj_kernel, d_conv2=d_conv2, d_sb=d_sb, d_ret=d_ret),
        out_shape=outs,
        grid=(n // FFN_TILE,),
        in_specs=[pl.BlockSpec((FFN_TILE, d), row), _full((1, d)), _resident(w.shape),
                  pl.BlockSpec((FFN_TILE, LANES), pos), pl.BlockSpec((FFN_TILE, LANES), pos)],
        out_specs=[pl.BlockSpec((FFN_TILE, s.shape[1]), row) for s in outs],
        scratch_shapes=[pltpu.VMEM((FFN_TILE, d), BF16)],
        compiler_params=_cparams(1),
        name="mixer_inproj",
    )(x, g, w, cos_tab, sin_tab)


def _sb_kernel(q_ref, knew_ref, vnew_ref, tri_ref, o_ref, k_scr, v_scr, qs_scr, acc_scr, carry_scr,
               *, n_pairs):
    q0 = pl.program_id(1) * SB_QROWS
    k_scr[pl.ds(pl.multiple_of(q0, SB_QROWS), SB_QROWS), :] = knew_ref[0]
    v_scr[pl.ds(pl.multiple_of(q0, SB_QROWS), SB_QROWS), :] = vnew_ref[0]
    tiles, pairs = range(SB_TILES), range(n_pairs)
    rows = SB_HEADS_PER_GROUP * SB_TILE
    lane = lax.broadcasted_iota(jnp.int32, (1, LANES), 1)
    col = lax.broadcasted_iota(jnp.int32, (rows, SB_BLOCK), 1)
    row = lax.broadcasted_iota(jnp.int32, (rows, SB_BLOCK), 0) % SB_TILE
    for p in pairs:
        q = q_ref[0, :, p * LANES:(p + 1) * LANES]
        for t in tiles:
            qt = q[t * SB_TILE:(t + 1) * SB_TILE]
            qs_scr[t, p] = jnp.concatenate(
                [jnp.where(lane // HEAD_DIM == hh, qt, jnp.zeros_like(qt))
                 for hh in range(SB_HEADS_PER_GROUP)], axis=0)

    def sweep(blocks, resume):
        depth = range(len(blocks[0]))

        def rows_of(ref, t, n, p):
            return ref[pl.ds(blocks[t][n][0], SB_BLOCK), p * LANES:(p + 1) * LANES]

        units = [(t, n) for n in depth for t in tiles]

        def logits(t, n):
            return [lax.dot_general(qs_scr[t, p], rows_of(k_scr, t, n, p),
                                    (((1,), (1,)), ((), ())), preferred_element_type=F32)
                    for p in pairs]

        def cumsums(t, n, zs):
            hls = []
            for p in pairs:
                zbits = lax.bitcast_convert_type(zs[p], jnp.uint32)
                neg_abs = lax.bitcast_convert_type(zbits | jnp.uint32(SIGN_BIT), F32)
                sp = jnp.maximum(zs[p], 0.0) + jnp.log(1.0 + jnp.exp(neg_abs))
                if blocks[t][n][1] is not None:
                    sp = jnp.where(blocks[t][n][1], sp, 0.0)
                hi = lax.bitcast_convert_type(
                    lax.bitcast_convert_type(sp, jnp.uint32) & jnp.uint32(BF16_BITS), F32)
                hls.append(jnp.concatenate([hi.astype(BF16), (sp - hi).astype(BF16)], axis=1))
            return jnp.dot(jnp.concatenate(hls, axis=0), tri_ref[...], preferred_element_type=F32)

        carry = {(t, p): carry_scr[t, p] if resume else None for t in tiles for p in pairs}
        acc = {(t, p): acc_scr[t, p] if resume else None for t in tiles for p in pairs}

        def weigh(t, n, zs, rt):
            for p in pairs:
                within = rt[p * rows:(p + 1) * rows, :SB_BLOCK]
                total = rt[p * rows:(p + 1) * rows, SB_BLOCK:]
                before = carry[t, p]
                w = jnp.exp(zs[p] - (within if before is None else within + before))
                if blocks[t][n][1] is not None:
                    w = jnp.where(blocks[t][n][1], w, 0.0)
                carry[t, p] = total if before is None else before + total
                pv = jnp.dot(w.astype(BF16), rows_of(v_scr, t, n, p), preferred_element_type=F32)
                acc[t, p] = pv if acc[t, p] is None else acc[t, p] + pv

        zs, rts = {}, {}
        for u in range(len(units) + 2):
            if u < len(units):
                zs[u] = logits(*units[u])
            if 1 <= u <= len(units):
                rts[u - 1] = cumsums(*units[u - 1], zs[u - 1])
            if u >= 2:
                weigh(*units[u - 2], zs.pop(u - 2), rts.pop(u - 2))
        lowest = None
        for t in tiles:
            for p in pairs:
                carry_scr[t, p] = carry[t, p]
                acc_scr[t, p] = acc[t, p]
                lowest = carry[t, p] if lowest is None else jnp.minimum(lowest, carry[t, p])
        return jnp.min(lowest)

    def aligned(start):
        return pl.multiple_of(start, SB_TILE)

    def first_fast():
        near_mask = col < row + (SB_BLOCK - SB_TILE)
        return sweep([[(aligned(q0 + (t + 1) * SB_TILE - (n + 1) * SB_BLOCK),
                        near_mask if n == 0 else None) for n in range(SB_FIRST_BLOCKS)]
                      for t in tiles], False)

    def first_clamped():
        blocks = []
        for t in tiles:
            limit = q0 + t * SB_TILE + row
            blocks.append([])
            for n in range(SB_FIRST_BLOCKS):
                start = aligned(jnp.maximum(q0 + (t + 1) * SB_TILE - (n + 1) * SB_BLOCK, 0))
                blocks[t].append((start, start + col < limit))
                limit = start
        return sweep(blocks, False)

    def more(st):
        reach = q0 + SB_QROWS - st[0] * SB_BLOCK
        return jnp.logical_and(reach > 0, st[1] < SB_DEAD_LOG)

    def body(st):
        blocks = []
        for t in tiles:
            limit = jnp.maximum(q0 + (t + 1) * SB_TILE - st[0] * SB_BLOCK, 0)
            start = aligned(jnp.maximum(limit - SB_BLOCK, 0))
            blocks.append([(start, start + col < limit)])
        return st[0] + 1, sweep(blocks, True)

    fits = q0 + SB_TILE - SB_FIRST_BLOCKS * SB_BLOCK >= 0
    lowest = lax.cond(fits, first_fast, first_clamped)
    lax.while_loop(more, body, (SB_FIRST_BLOCKS, lowest))
    for p in pairs:
        for t in tiles:
            acc = acc_scr[t, p]
            out = acc[0:SB_TILE]
            for hh in range(1, SB_HEADS_PER_GROUP):
                out = jnp.where(lane // HEAD_DIM == hh, acc[hh * SB_TILE:(hh + 1) * SB_TILE], out)
            o_ref[0, t * SB_TILE:(t + 1) * SB_TILE, p * LANES:(p + 1) * LANES] = out.astype(BF16)


def _stick_breaking(q, k, v, tri):
    b, s, d_sb = q.shape
    n_pairs = d_sb // LANES
    rows = SB_HEADS_PER_GROUP * SB_TILE
    blk = pl.BlockSpec((1, SB_QROWS, d_sb), lambda bi, i: (bi, i, 0))
    return pl.pallas_call(
        functools.partial(_sb_kernel, n_pairs=n_pairs),
        out_shape=jax.ShapeDtypeStruct((b, s, d_sb), BF16),
        grid=(b, s // SB_QROWS),
        in_specs=[blk, blk, blk, _full(tri.shape)],
        out_specs=blk,
        scratch_shapes=[pltpu.VMEM((s, d_sb), BF16), pltpu.VMEM((s, d_sb), BF16),
                        pltpu.VMEM((SB_TILES, n_pairs, rows, LANES), BF16),
                        pltpu.VMEM((SB_TILES, n_pairs, rows, LANES), F32),
                        pltpu.VMEM((SB_TILES, n_pairs, rows, LANES), F32)],
        compiler_params=_cparams(2),
        name="stick_breaking",
    )(q, k, v, tri)


def _ret_kernel(qk_ref, vg_ref, dmat_ref, qdec_ref, kdec_ref, cdec_ref, bd_ref, avg_ref, gain_ref,
                o_ref, state_scr, *, d_ret, n_batch):
    n_heads = d_ret // HEAD_DIM
    batches = range(n_batch)
    nt = (((1,), (1,)), ((), ()))

    @pl.when(pl.program_id(0) == 0)
    def _():
        state_scr[...] = jnp.zeros_like(state_scr)

    lane = lax.broadcasted_iota(jnp.int32, (1, d_ret), 1)
    in_head = [lane // HEAD_DIM == h for h in range(n_heads)]
    q = [qk_ref[b, :, :d_ret] for b in batches]
    k = [qk_ref[b, :, d_ret:] for b in batches]
    v = [vg_ref[b, :, :d_ret] for b in batches]
    probs = []
    for b in batches:
        kb = k[b].astype(BF16)
        probs.append(jnp.concatenate(
            [(lax.dot_general(jnp.where(in_head[h], q[b], 0.0).astype(BF16), kb, nt,
                              preferred_element_type=F32) * dmat_ref[h]).astype(BF16)
             for h in range(n_heads)], axis=1))
    y = []
    for b in batches:
        vals = jnp.concatenate([jnp.where(in_head[h], v[b], 0.0).astype(BF16)
                                for h in range(n_heads)], axis=0)
        y.append(jnp.dot(probs[b], vals, preferred_element_type=F32))
    for b in batches:
        state = state_scr[b]
        y[b] = y[b] + jnp.dot((q[b] * qdec_ref[...]).astype(BF16), state.astype(BF16),
                              preferred_element_type=F32)
        kv = lax.dot_general((k[b] * kdec_ref[...]).astype(BF16), v[b].astype(BF16),
                             (((0,), (0,)), ((), ())), preferred_element_type=F32)
        state_scr[b] = cdec_ref[...] * state + bd_ref[...] * kv

    def head_mean(t):
        hi = t.astype(BF16)
        lo = (t - hi.astype(F32)).astype(BF16)
        return jnp.dot(jnp.concatenate([hi, lo], axis=1), avg_ref[...], preferred_element_type=F32)

    yc = [y[b] - head_mean(y[b]) for b in batches]
    var = [head_mean(yc[b] * yc[b]) for b in batches]
    for b in batches:
        g = vg_ref[b, :, d_ret:]
        o_ref[b] = (_silu(g) * (yc[b] * lax.rsqrt(var[b] + EPS) * gain_ref[...])).astype(BF16)


def _retention(qk, vg, tabs, gain):
    b, s, d2 = qk.shape
    d_ret = d2 // 2
    blk = pl.BlockSpec((b, RET_BLOCK, d2), lambda i: (0, i, 0))
    return pl.pallas_call(
        functools.partial(_ret_kernel, d_ret=d_ret, n_batch=b),
        out_shape=jax.ShapeDtypeStruct((b, s, d_ret), BF16),
        grid=(s // RET_BLOCK,),
        in_specs=[blk, blk] + [_full(t.shape) for t in tabs] + [_full((1, d_ret))],
        out_specs=pl.BlockSpec((b, RET_BLOCK, d_ret), lambda i: (0, i, 0)),
        scratch_shapes=[pltpu.VMEM((b, d_ret, d_ret), F32)],
        compiler_params=_cparams(1),
        name="retention",
    )(qk, vg, *tabs, gain)


def _rotary_tables(seq):
    half = HEAD_DIM // 2
    inv = 1.0 / (ROPE_BASE ** (jnp.arange(half, dtype=F32) / half))
    lane = jnp.arange(LANES)
    ang = jnp.arange(seq).astype(F32)[:, None] * inv[lane % half][None, :]
    sign = jnp.where((lane % HEAD_DIM) < half, -1.0, 1.0).astype(F32)
    return jnp.cos(ang), jnp.sin(ang) * sign[None, :]


def _retention_tables(n_heads):
    d_ret = n_heads * HEAD_DIM
    log_gamma = jnp.log1p(-jnp.exp2(-5.0 - jnp.arange(n_heads, dtype=F32)))
    idx = jnp.arange(RET_BLOCK, dtype=F32)
    t, s = idx[:, None], idx[None, :]
    same = (t // CHUNK) == (s // CHUNK)
    dist = jnp.where(same, jnp.abs(t - s), t - s)
    seen = same | ((s // CHUNK) < (t // CHUNK))
    dmat = jnp.where(seen[None], jnp.exp(log_gamma[:, None, None] * dist[None]), 0.0)
    lane_gamma = jnp.repeat(log_gamma, HEAD_DIM)
    qdec = jnp.exp(lane_gamma[None, :] * (idx + 1.0)[:, None])
    kdec = jnp.exp(lane_gamma[None, :] * (RET_BLOCK - 1.0 - idx)[:, None])
    head = jnp.arange(d_ret) // HEAD_DIM
    bd = (head[:, None] == head[None, :]).astype(F32)
    cdec = bd * jnp.exp(lane_gamma * RET_BLOCK)[:, None]
    avg = jnp.concatenate([bd, bd], axis=0).astype(BF16) * (1.0 / HEAD_DIM)
    return dmat, qdec, kdec, cdec, bd, avg.astype(BF16)


def _tri_table():
    r = np.arange(2 * SB_BLOCK)[:, None] % SB_BLOCK
    c = np.arange(2 * SB_BLOCK)[None, :]
    return jnp.asarray((c >= SB_BLOCK) | (r >= c), dtype=BF16)


def kernel(x, ffn1_norm, ffn1_w_in, ffn1_w_out, mix_norm, mix_w_in, conv_w, conv_b, conv_ln_g,
           conv_ln_b, ret_norm_g, mix_w_out, ffn2_norm, ffn2_w_in, ffn2_w_out, final_norm):
    b, s, d = x.shape
    depth = ffn1_norm.shape[0]
    d_conv = conv_w.shape[2]
    d_ret = ret_norm_g.shape[1]
    d_sb = d - d_conv - d_ret
    assert s % ROW_TILE == 0 and s % FFN_TILE == 0 and s % RET_BLOCK == 0 and s % SB_QROWS == 0
    assert ffn1_w_out.shape[1] % FF_CHUNK == 0 and d % OUT_TILE == 0

    cos_tab, sin_tab = _rotary_tables(s)
    ret_tabs = _retention_tables(d_ret // HEAD_DIM)
    tri = _tri_table()
    scale = HEAD_DIM ** -0.5
    col = jnp.arange(mix_w_in.shape[2])
    q_sb_cols = (col >= 2 * d_conv) & (col < 2 * d_conv + d_sb)
    q_r_cols = (col >= 2 * d_conv + 3 * d_sb) & (col < 2 * d_conv + 3 * d_sb + d_ret)
    col_scale = jnp.where(q_sb_cols | q_r_cols, scale, 1.0).astype(F32)

    row = lambda g: g.reshape(1, -1)
    xf = x.reshape(b * s, d)
    for l in range(depth):
        win1, wout1 = ffn1_w_in[l].astype(BF16), ffn1_w_out[l].astype(BF16)
        win2, wout2 = ffn2_w_in[l].astype(BF16), ffn2_w_out[l].astype(BF16)
        w_mix = (mix_w_in[l] * col_scale[None, :]).astype(BF16)

        xf = _ffn(xf, row(ffn1_norm[l]), win1, wout1, FFN_TILE)
        uconv, q, k, v, qkr, vgr = _inproj(xf, row(mix_norm[l]), w_mix, cos_tab, sin_tab, s,
                                           2 * d_conv, d_sb, d_ret)
        seq3 = lambda t: t.reshape(b, s, t.shape[1])
        y_sb = _stick_breaking(seq3(q), seq3(k), seq3(v), tri)
        y_ret = _retention(seq3(qkr), seq3(vgr), ret_tabs, row(ret_norm_g[l]))
        flat = lambda t: t.reshape(b * s, t.shape[2])
        last = l == depth - 1
        xf = _ffn(xf, row(ffn2_norm[l]), win2, wout2, ROW_TILE,
                  mix=(flat(y_sb), flat(y_ret)), wmix=mix_w_out[l].astype(BF16),
                  final_g=row(final_norm) if last else None,
                  conv=(uconv, s, conv_w[l], row(conv_b[l]), row(conv_ln_g[l]), row(conv_ln_b[l])))
    return xf.reshape(b, s, d)
```

```python
import functools

import jax
import jax.numpy as jnp
import numpy as np
from jax import lax
from jax.experimental import pallas as pl
from jax.experimental.pallas import tpu as pltpu

F32 = jnp.float32
BF16 = jnp.bfloat16

EPS = 1e-6
ROPE_BASE = 10000.0
CHUNK = 64
CONV_WIDTH = 31
HEAD_DIM = 64
LANES = 128
SUBLANES = 8
MXU_DIM = 256

ROW_TILE = 512
FFN_TILE = 1024
FF_CHUNK = MXU_DIM
OUT_TILE = 512
CONV_HALO = 32
CONV_SUBTILES = 8
CONV_CHUNK_GAP = 1
CONV_CHUNK_SPAN = 2
SB_BLOCK = 128
SB_QROWS = 512
SB_TILE = 64
SB_TILES = SB_QROWS // SB_TILE
SB_HEADS_PER_GROUP = LANES // HEAD_DIM
SB_FIRST_BLOCKS = 2
SB_DEAD_LOG = 110.0
SIGN_BIT = 0x80000000
BF16_BITS = 0xFFFF0000
RET_BLOCK = 256
VMEM_LIMIT = 56 * 1024 * 1024


def _cparams(n_axes):
    return pltpu.CompilerParams(dimension_semantics=("arbitrary",) * n_axes,
                                vmem_limit_bytes=VMEM_LIMIT)


def _rms(x, g):
    return (x * lax.rsqrt(jnp.mean(x * x, axis=-1, keepdims=True) + EPS)) * g


def _silu(x):
    return x * jax.nn.sigmoid(x)


def _full(shape):
    return pl.BlockSpec(shape, lambda *_: (0,) * len(shape))


def _resident(shape):
    return pl.BlockSpec(shape, lambda *_: (0,) * len(shape), pipeline_mode=pl.Buffered(1))


def _bit_zero(x):
    bits = lax.bitcast_convert_type(x, jnp.uint32)
    return lax.bitcast_convert_type((bits >> 16) >> 16, F32)


def _conv_fill(u, v_scr, starts_sequence, tile_rows):
    d_conv = v_scr.shape[1]
    if starts_sequence is None:
        v_scr[0:CONV_HALO, :] = jnp.zeros((CONV_HALO, d_conv), F32)
    else:
        v_scr[0:CONV_HALO, :] = jnp.where(starts_sequence, 0.0,
                                          v_scr[tile_rows:tile_rows + CONV_HALO, :])
    v_scr[CONV_HALO:CONV_HALO + tile_rows, :] = u[:, :d_conv] * jax.nn.sigmoid(u[:, d_conv:])
    v_scr[CONV_HALO + tile_rows:, :] = jnp.zeros((SUBLANES, d_conv), F32)


def _conv_rows(v_scr, p_scr, cw_ref, cb_ref, lg_ref, lb_ref, out_ref, r, anchor=None):
    off = CONV_HALO - (CONV_WIDTH - 1)
    sub = p_scr.shape[2] - SUBLANES
    cw = cw_ref[...] if anchor is None else cw_ref[...] + anchor
    for shift in range(SUBLANES):
        part = None
        for o in range(shift, off + CONV_WIDTH, SUBLANES):
            if o < off:
                continue
            rows = v_scr[pl.ds(r * sub + o - shift, sub + SUBLANES), :]
            term = cw[o - off:o - off + 1, :] * rows
            part = term if part is None else part + term
        p_scr[r % 2, shift] = part
    y = cb_ref[...] + p_scr[r % 2, 0, 0:sub, :]
    for shift in range(1, SUBLANES):
        y = y + p_scr[r % 2, shift, pl.ds(shift, sub), :]
    mu = jnp.mean(y, axis=-1, keepdims=True)
    yc = y - mu
    var = jnp.mean(yc * yc, axis=-1, keepdims=True)
    ln = yc * lax.rsqrt(var + EPS) * lg_ref[...] + lb_ref[...]
    out = _silu(ln)
    out_ref[r * sub:(r + 1) * sub, :] = out.astype(BF16)
    return out[0:1, :]


def _ffn_kernel(*refs, tile, n_chunks, n_out, n_mix, final, conv_tiles_per_seq):
    refs = list(refs)
    x_ref = refs.pop(0)
    mix_refs = [refs.pop(0) for _ in range(n_mix)]
    wmix_ref = refs.pop(0) if n_mix else None
    g_ref, win_ref, wout_ref = refs.pop(0), refs.pop(0), refs.pop(0)
    fg_ref = refs.pop(0) if final else None
    has_conv = conv_tiles_per_seq is not None
    if has_conv:
        u_first_ref, u_next_ref = refs.pop(0), refs.pop(0)
        conv_params = [refs.pop(0) for _ in range(4)]
    o_ref, h_scr, a_scr = refs.pop(0), refs.pop(0), refs.pop(0)
    out_tiles = [slice(m * OUT_TILE, (m + 1) * OUT_TILE) for m in range(n_out)]
    d_ff = wout_ref.shape[0]
    n_sub = CONV_SUBTILES
    if has_conv:
        v_scr, p_scr, yc_scr = refs.pop(0), refs.pop(0), refs.pop(0)
        step = pl.program_id(0)

        @pl.when(step == 0)
        def _():
            _conv_fill(u_first_ref[...], v_scr, None, tile)
            for r in range(n_sub):
                _conv_rows(v_scr, p_scr, *conv_params, yc_scr, r)

        mix_refs = [yc_scr] + mix_refs
    if mix_refs:
        for sl in out_tiles:
            y, lo = None, 0
            for y_ref in mix_refs:
                part = jnp.dot(y_ref[...], wmix_ref[lo:lo + y_ref.shape[1], sl],
                               preferred_element_type=F32)
                y = part if y is None else y + part
                lo += y_ref.shape[1]
            o_ref[:, sl] = x_ref[:, sl] + y
        x_ref = o_ref
    h_scr[...] = _rms(x_ref[...], g_ref[...]).astype(BF16)
    if has_conv:
        nxt = jnp.minimum(step + 1, pl.num_programs(0) - 1)
        _conv_fill(u_next_ref[...], v_scr, nxt % conv_tiles_per_seq == 0, tile)
    done = {}
    for j in range(n_chunks):
        sl = slice(j * FF_CHUNK, (j + 1) * FF_CHUNK)
        up_sl = slice(d_ff + j * FF_CHUNK, d_ff + (j + 1) * FF_CHUNK)
        gate = jnp.dot(h_scr[...], win_ref[:, sl], preferred_element_type=F32)
        up = jnp.dot(h_scr[...], win_ref[:, up_sl], preferred_element_type=F32)
        if j in done:
            gate = gate + done.pop(j)
        a_scr[:, sl] = (_silu(gate) * up).astype(BF16)
        if has_conv and j % CONV_CHUNK_GAP == 0 and j // CONV_CHUNK_GAP < n_sub:
            r = j // CONV_CHUNK_GAP
            out_row = _conv_rows(v_scr, p_scr, *conv_params, yc_scr, r,
                                 anchor=_bit_zero(gate[0:1, :conv_params[0].shape[1]]))
            done[j + CONV_CHUNK_SPAN] = _bit_zero(out_row)
    assert not done
    for sl in out_tiles:
        y = jnp.dot(a_scr[...], wout_ref[:, sl], preferred_element_type=F32)
        o_ref[:, sl] = x_ref[:, sl] + 0.5 * y
    if final:
        o_ref[...] = _rms(o_ref[...], fg_ref[...])


def _ffn(x, g, win, wout, tile, mix=(), wmix=None, final_g=None, conv=None):
    n, d = x.shape
    d_ff = wout.shape[0]
    n_chunks = d_ff // FF_CHUNK
    n_tiles = n // tile
    final = final_g is not None
    row = lambda i: (i, 0)
    in_specs = [pl.BlockSpec((tile, d), row)]
    in_specs += [pl.BlockSpec((tile, y.shape[1]), row) for y in mix]
    args = [x, *mix]
    if mix:
        in_specs.append(_resident(wmix.shape))
        args.append(wmix)
    in_specs += [_full((1, d)), _resident(win.shape), _resident(wout.shape)]
    args += [g, win, wout]
    if final:
        in_specs.append(_full((1, d)))
        args.append(final_g)
    scratch = [pltpu.VMEM((tile, d), BF16), pltpu.VMEM((tile, d_ff), BF16)]
    conv_tiles_per_seq = None
    if conv is not None:
        u, seq, cw, cb, lg, lb = conv
        d_conv = cw.shape[1]
        assert seq % tile == 0 and tile % (CONV_SUBTILES * SUBLANES) == 0
        assert CONV_HALO >= CONV_WIDTH - 1
        conv_tiles_per_seq = seq // tile
        in_specs += [pl.BlockSpec((tile, u.shape[1]), lambda i: (0, 0), pipeline_mode=pl.Buffered(1)),
                     pl.BlockSpec((tile, u.shape[1]), lambda i: (jnp.minimum(i + 1, n_tiles - 1), 0)),
                     _full(cw.shape), _full((1, d_conv)), _full((1, d_conv)), _full((1, d_conv))]
        args += [u, u, cw, cb, lg, lb]
        scratch += [pltpu.VMEM((CONV_HALO + tile + SUBLANES, d_conv), F32),
                    pltpu.VMEM((2, SUBLANES, tile // CONV_SUBTILES + SUBLANES, d_conv), F32),
                    pltpu.VMEM((tile, d_conv), BF16)]
    return pl.pallas_call(
        functools.partial(_ffn_kernel, tile=tile, n_chunks=n_chunks, n_out=d // OUT_TILE,
                          n_mix=len(mix), final=final, conv_tiles_per_seq=conv_tiles_per_seq),
        out_shape=jax.ShapeDtypeStruct((n, d), F32),
        grid=(n_tiles,),
        in_specs=in_specs,
        out_specs=pl.BlockSpec((tile, d), row),
        scratch_shapes=scratch,
        compiler_params=_cparams(1),
        name="ffn",
    )(*args)


def _rotary(t, cos, sin_signed, first_half):
    partner = jnp.where(first_half, pltpu.roll(t, LANES - HEAD_DIM // 2, 1),
                        pltpu.roll(t, HEAD_DIM // 2, 1))
    return t * cos + partner * sin_signed


def _inproj_kernel(x_ref, g_ref, w_ref, cos_ref, sin_ref, uconv_ref, q_ref, k_ref, v_ref,
                   qkr_ref, vgr_ref, h_scr, *, d_conv2, d_sb, d_ret):
    h_scr[...] = _rms(x_ref[...], g_ref[...]).astype(BF16)

    def proj(lo, width):
        return jnp.dot(h_scr[...], w_ref[:, lo:lo + width], preferred_element_type=F32)

    uconv_ref[...] = proj(0, d_conv2)
    o = d_conv2
    q_ref[...] = proj(o, d_sb).astype(BF16)
    k_ref[...] = proj(o + d_sb, d_sb).astype(BF16)
    v_ref[...] = proj(o + 2 * d_sb, d_sb).astype(BF16)
    o += 3 * d_sb
    qk = proj(o, 2 * d_ret)
    cos = cos_ref[...]
    sin = sin_ref[...]
    lane = lax.broadcasted_iota(jnp.int32, (1, LANES), 1)
    first_half = (lane % HEAD_DIM) < (HEAD_DIM // 2)
    for c in range(2 * d_ret // LANES):
        sl = slice(c * LANES, (c + 1) * LANES)
        qkr_ref[:, sl] = _rotary(qk[:, sl], cos, sin, first_half)
    vgr_ref[...] = proj(o + 2 * d_ret, 2 * d_ret)


def _inproj(x, g, w, cos_tab, sin_tab, seq, d_conv2, d_sb, d_ret):
    n, d = x.shape
    tiles_per_seq = seq // FFN_TILE
    row = lambda i: (i, 0)
    pos = lambda i: (i % tiles_per_seq, 0)
    outs = [jax.ShapeDtypeStruct((n, d_conv2), F32)] + [jax.ShapeDtypeStruct((n, d_sb), BF16)] * 3 \
        + [jax.ShapeDtypeStruct((n, 2 * d_ret), F32)] * 2
    return pl.pallas_call(
        functools.partial(_inproj_kernel, d_conv2=d_conv2, d_sb=d_sb, d_ret=d_ret),
        out_shape=outs,
        grid=(n // FFN_TILE,),
        in_specs=[pl.BlockSpec((FFN_TILE, d), row), _full((1, d)), _resident(w.shape),
                  pl.BlockSpec((FFN_TILE, LANES), pos), pl.BlockSpec((FFN_TILE, LANES), pos)],
        out_specs=[pl.BlockSpec((FFN_TILE, s.shape[1]), row) for s in outs],
        scratch_shapes=[pltpu.VMEM((FFN_TILE, d), BF16)],
        compiler_params=_cparams(1),
        name="mixer_inproj",
    )(x, g, w, cos_tab, sin_tab)


def _sb_kernel(q_ref, knew_ref, vnew_ref, tri_ref, o_ref, k_scr, v_scr, qs_scr, acc_scr, carry_scr,
               *, n_pairs):
    q0 = pl.program_id(1) * SB_QROWS
    k_scr[pl.ds(pl.multiple_of(q0, SB_QROWS), SB_QROWS), :] = knew_ref[0]
    v_scr[pl.ds(pl.multiple_of(q0, SB_QROWS), SB_QROWS), :] = vnew_ref[0]
    tiles, pairs = range(SB_TILES), range(n_pairs)
    rows = SB_HEADS_PER_GROUP * SB_TILE
    lane = lax.broadcasted_iota(jnp.int32, (1, LANES), 1)
    col = lax.broadcasted_iota(jnp.int32, (rows, SB_BLOCK), 1)
    row = lax.broadcasted_iota(jnp.int32, (rows, SB_BLOCK), 0) % SB_TILE
    for p in pairs:
        q = q_ref[0, :, p * LANES:(p + 1) * LANES]
        for t in tiles:
            qt = q[t * SB_TILE:(t + 1) * SB_TILE]
            qs_scr[t, p] = jnp.concatenate(
                [jnp.where(lane // HEAD_DIM == hh, qt, jnp.zeros_like(qt))
                 for hh in range(SB_HEADS_PER_GROUP)], axis=0)

    def sweep(blocks, resume):
        depth = range(len(blocks[0]))

        def rows_of(ref, t, n, p):
            return ref[pl.ds(blocks[t][n][0], SB_BLOCK), p * LANES:(p + 1) * LANES]

        units = [(t, n) for t in tiles for n in depth]

        def logits(t, n):
            return [lax.dot_general(qs_scr[t, p], rows_of(k_scr, t, n, p),
                                    (((1,), (1,)), ((), ())), preferred_element_type=F32)
                    for p in pairs]

        def cumsums(t, n, zs):
            hls = []
            for p in pairs:
                zbits = lax.bitcast_convert_type(zs[p], jnp.uint32)
                neg_abs = lax.bitcast_convert_type(zbits | jnp.uint32(SIGN_BIT), F32)
                sp = jnp.maximum(zs[p], 0.0) + jnp.log(1.0 + jnp.exp(neg_abs))
                if blocks[t][n][1] is not None:
                    sp = jnp.where(blocks[t][n][1], sp, 0.0)
                hi = lax.bitcast_convert_type(
                    lax.bitcast_convert_type(sp, jnp.uint32) & jnp.uint32(BF16_BITS), F32)
                hls.append(jnp.concatenate([hi.astype(BF16), (sp - hi).astype(BF16)], axis=1))
            return jnp.dot(jnp.concatenate(hls, axis=0), tri_ref[...], preferred_element_type=F32)

        carry = {(t, p): carry_scr[t, p] if resume else None for t in tiles for p in pairs}
        acc = {(t, p): acc_scr[t, p] if resume else None for t in tiles for p in pairs}

        def weigh(t, n, zs, rt):
            for p in pairs:
                within = rt[p * rows:(p + 1) * rows, :SB_BLOCK]
                total = rt[p * rows:(p + 1) * rows, SB_BLOCK:]
                before = carry[t, p]
                w = jnp.exp(zs[p] - (within if before is None else within + before))
                if blocks[t][n][1] is not None:
                    w = jnp.where(blocks[t][n][1], w, 0.0)
                carry[t, p] = total if before is None else before + total
                pv = jnp.dot(w.astype(BF16), rows_of(v_scr, t, n, p), preferred_element_type=F32)
                acc[t, p] = pv if acc[t, p] is None else acc[t, p] + pv

        zs, rts = {}, {}
        for u in range(len(units) + 2):
            if u < len(units):
                zs[u] = logits(*units[u])
            if 1 <= u <= len(units):
                rts[u - 1] = cumsums(*units[u - 1], zs[u - 1])
            if u >= 2:
                weigh(*units[u - 2], zs.pop(u - 2), rts.pop(u - 2))
        lowest = None
        for t in tiles:
            for p in pairs:
                carry_scr[t, p] = carry[t, p]
                acc_scr[t, p] = acc[t, p]
                lowest = carry[t, p] if lowest is None else jnp.minimum(lowest, carry[t, p])
        return jnp.min(lowest)

    def aligned(start):
        return pl.multiple_of(start, SB_TILE)

    def first_fast():
        near_mask = col < row + (SB_BLOCK - SB_TILE)
        return sweep([[(aligned(q0 + (t + 1) * SB_TILE - (n + 1) * SB_BLOCK),
                        near_mask if n == 0 else None) for n in range(SB_FIRST_BLOCKS)]
                      for t in tiles], False)

    def first_clamped():
        blocks = []
        for t in tiles:
            limit = q0 + t * SB_TILE + row
            blocks.append([])
            for n in range(SB_FIRST_BLOCKS):
                start = aligned(jnp.maximum(q0 + (t + 1) * SB_TILE - (n + 1) * SB_BLOCK, 0))
                blocks[t].append((start, start + col < limit))
                limit = start
        return sweep(blocks, False)

    def more(st):
        reach = q0 + SB_QROWS - st[0] * SB_BLOCK
        return jnp.logical_and(reach > 0, st[1] < SB_DEAD_LOG)

    def body(st):
        blocks = []
        for t in tiles:
            limit = jnp.maximum(q0 + (t + 1) * SB_TILE - st[0] * SB_BLOCK, 0)
            start = aligned(jnp.maximum(limit - SB_BLOCK, 0))
            blocks.append([(start, start + col < limit)])
        return st[0] + 1, sweep(blocks, True)

    fits = q0 + SB_TILE - SB_FIRST_BLOCKS * SB_BLOCK >= 0
    lowest = lax.cond(fits, first_fast, first_clamped)
    lax.while_loop(more, body, (SB_FIRST_BLOCKS, lowest))
    for p in pairs:
        for t in tiles:
            acc = acc_scr[t, p]
            out = acc[0:SB_TILE]
            for hh in range(1, SB_HEADS_PER_GROUP):
                out = jnp.where(lane // HEAD_DIM == hh, acc[hh * SB_TILE:(hh + 1) * SB_TILE], out)
            o_ref[0, t * SB_TILE:(t + 1) * SB_TILE, p * LANES:(p + 1) * LANES] = out.astype(BF16)


def _stick_breaking(q, k, v, tri):
    b, s, d_sb = q.shape
    n_pairs = d_sb // LANES
    rows = SB_HEADS_PER_GROUP * SB_TILE
    blk = pl.BlockSpec((1, SB_QROWS, d_sb), lambda bi, i: (bi, i, 0))
    return pl.pallas_call(
        functools.partial(_sb_kernel, n_pairs=n_pairs),
        out_shape=jax.ShapeDtypeStruct((b, s, d_sb), BF16),
        grid=(b, s // SB_QROWS),
        in_specs=[blk, blk, blk, _full(tri.shape)],
        out_specs=blk,
        scratch_shapes=[pltpu.VMEM((s, d_sb), BF16), pltpu.VMEM((s, d_sb), BF16),
                        pltpu.VMEM((SB_TILES, n_pairs, rows, LANES), BF16),
                        pltpu.VMEM((SB_TILES, n_pairs, rows, LANES), F32),
                        pltpu.VMEM((SB_TILES, n_pairs, rows, LANES), F32)],
        compiler_params=_cparams(2),
        name="stick_breaking",
    )(q, k, v, tri)


def _ret_kernel(qk_ref, vg_ref, dmat_ref, qdec_ref, kdec_ref, cdec_ref, bd_ref, avg_ref, gain_ref,
                o_ref, state_scr, *, d_ret, n_batch):
    n_heads = d_ret // HEAD_DIM
    batches = range(n_batch)
    nt = (((1,), (1,)), ((), ()))

    @pl.when(pl.program_id(0) == 0)
    def _():
        state_scr[...] = jnp.zeros_like(state_scr)

    lane = lax.broadcasted_iota(jnp.int32, (1, d_ret), 1)
    in_head = [lane // HEAD_DIM == h for h in range(n_heads)]
    q = [qk_ref[b, :, :d_ret] for b in batches]
    k = [qk_ref[b, :, d_ret:] for b in batches]
    v = [vg_ref[b, :, :d_ret] for b in batches]
    probs = []
    for b in batches:
        kb = k[b].astype(BF16)
        probs.append(jnp.concatenate(
            [(lax.dot_general(jnp.where(in_head[h], q[b], 0.0).astype(BF16), kb, nt,
                              preferred_element_type=F32) * dmat_ref[h]).astype(BF16)
             for h in range(n_heads)], axis=1))
    y = []
    for b in batches:
        vals = jnp.concatenate([jnp.where(in_head[h], v[b], 0.0).astype(BF16)
                                for h in range(n_heads)], axis=0)
        y.append(jnp.dot(probs[b], vals, preferred_element_type=F32))
    for b in batches:
        state = state_scr[b]
        y[b] = y[b] + jnp.dot((q[b] * qdec_ref[...]).astype(BF16), state.astype(BF16),
                              preferred_element_type=F32)
        kv = lax.dot_general((k[b] * kdec_ref[...]).astype(BF16), v[b].astype(BF16),
                             (((0,), (0,)), ((), ())), preferred_element_type=F32)
        state_scr[b] = cdec_ref[...] * state + bd_ref[...] * kv

    def head_mean(t):
        hi = t.astype(BF16)
        lo = (t - hi.astype(F32)).astype(BF16)
        return jnp.dot(jnp.concatenate([hi, lo], axis=1), avg_ref[...], preferred_element_type=F32)

    yc = [y[b] - head_mean(y[b]) for b in batches]
    var = [head_mean(yc[b] * yc[b]) for b in batches]
    for b in batches:
        g = vg_ref[b, :, d_ret:]
        o_ref[b] = (_silu(g) * (yc[b] * lax.rsqrt(var[b] + EPS) * gain_ref[...])).astype(BF16)


def _retention(qk, vg, tabs, gain):
    b, s, d2 = qk.shape
    d_ret = d2 // 2
    blk = pl.BlockSpec((b, RET_BLOCK, d2), lambda i: (0, i, 0))
    return pl.pallas_call(
        functools.partial(_ret_kernel, d_ret=d_ret, n_batch=b),
        out_shape=jax.ShapeDtypeStruct((b, s, d_ret), BF16),
        grid=(s // RET_BLOCK,),
        in_specs=[blk, blk] + [_full(t.shape) for t in tabs] + [_full((1, d_ret))],
        out_specs=pl.BlockSpec((b, RET_BLOCK, d_ret), lambda i: (0, i, 0)),
        scratch_shapes=[pltpu.VMEM((b, d_ret, d_ret), F32)],
        compiler_params=_cparams(1),
        name="retention",
    )(qk, vg, *tabs, gain)


def _rotary_tables(seq):
    half = HEAD_DIM // 2
    inv = 1.0 / (ROPE_BASE ** (jnp.arange(half, dtype=F32) / half))
    lane = jnp.arange(LANES)
    ang = jnp.arange(seq).astype(F32)[:, None] * inv[lane % half][None, :]
    sign = jnp.where((lane % HEAD_DIM) < half, -1.0, 1.0).astype(F32)
    return jnp.cos(ang), jnp.sin(ang) * sign[None, :]


def _retention_tables(n_heads):
    d_ret = n_heads * HEAD_DIM
    log_gamma = jnp.log1p(-jnp.exp2(-5.0 - jnp.arange(n_heads, dtype=F32)))
    idx = jnp.arange(RET_BLOCK, dtype=F32)
    t, s = idx[:, None], idx[None, :]
    same = (t // CHUNK) == (s // CHUNK)
    dist = jnp.where(same, jnp.abs(t - s), t - s)
    seen = same | ((s // CHUNK) < (t // CHUNK))
    dmat = jnp.where(seen[None], jnp.exp(log_gamma[:, None, None] * dist[None]), 0.0)
    lane_gamma = jnp.repeat(log_gamma, HEAD_DIM)
    qdec = jnp.exp(lane_gamma[None, :] * (idx + 1.0)[:, None])
    kdec = jnp.exp(lane_gamma[None, :] * (RET_BLOCK - 1.0 - idx)[:, None])
    head = jnp.arange(d_ret) // HEAD_DIM
    bd = (head[:, None] == head[None, :]).astype(F32)
    cdec = bd * jnp.exp(lane_gamma * RET_BLOCK)[:, None]
    avg = jnp.concatenate([bd, bd], axis=0).astype(BF16) * (1.0 / HEAD_DIM)
    return dmat, qdec, kdec, cdec, bd, avg.astype(BF16)


def _tri_table():
    r = np.arange(2 * SB_BLOCK)[:, None] % SB_BLOCK
    c = np.arange(2 * SB_BLOCK)[None, :]
    return jnp.asarray((c >= SB_BLOCK) | (r >= c), dtype=BF16)


def kernel(x, ffn1_norm, ffn1_w_in, ffn1_w_out, mix_norm, mix_w_in, conv_w, conv_b, conv_ln_g,
           conv_ln_b, ret_norm_g, mix_w_out, ffn2_norm, ffn2_w_in, ffn2_w_out, final_norm):
    b, s, d = x.shape
    depth = ffn1_norm.shape[0]
    d_conv = conv_w.shape[2]
    d_ret = ret_norm_g.shape[1]
    d_sb = d - d_conv - d_ret
    assert s % ROW_TILE == 0 and s % FFN_TILE == 0 and s % RET_BLOCK == 0 and s % SB_QROWS == 0
    assert ffn1_w_out.shape[1] % FF_CHUNK == 0 and d % OUT_TILE == 0

    cos_tab, sin_tab = _rotary_tables(s)
    ret_tabs = _retention_tables(d_ret // HEAD_DIM)
    tri = _tri_table()
    scale = HEAD_DIM ** -0.5
    col = jnp.arange(mix_w_in.shape[2])
    q_sb_cols = (col >= 2 * d_conv) & (col < 2 * d_conv + d_sb)
    q_r_cols = (col >= 2 * d_conv + 3 * d_sb) & (col < 2 * d_conv + 3 * d_sb + d_ret)
    col_scale = jnp.where(q_sb_cols | q_r_cols, scale, 1.0).astype(F32)

    row = lambda g: g.reshape(1, -1)
    xf = x.reshape(b * s, d)
    for l in range(depth):
        win1, wout1 = ffn1_w_in[l].astype(BF16), ffn1_w_out[l].astype(BF16)
        win2, wout2 = ffn2_w_in[l].astype(BF16), ffn2_w_out[l].astype(BF16)
        w_mix = (mix_w_in[l] * col_scale[None, :]).astype(BF16)

        xf = _ffn(xf, row(ffn1_norm[l]), win1, wout1, FFN_TILE)
        uconv, q, k, v, qkr, vgr = _inproj(xf, row(mix_norm[l]), w_mix, cos_tab, sin_tab, s,
                                           2 * d_conv, d_sb, d_ret)
        seq3 = lambda t: t.reshape(b, s, t.shape[1])
        y_sb = _stick_breaking(seq3(q), seq3(k), seq3(v), tri)
        y_ret = _retention(seq3(qkr), seq3(vgr), ret_tabs, row(ret_norm_g[l]))
        flat = lambda t: t.reshape(b * s, t.shape[2])
        last = l == depth - 1
        xf = _ffn(xf, row(ffn2_norm[l]), win2, wout2, ROW_TILE,
                  mix=(flat(y_sb), flat(y_ret)), wmix=mix_w_out[l].astype(BF16),
                  final_g=row(final_norm) if last else None,
                  conv=(uconv, s, conv_w[l], row(conv_b[l]), row(conv_ln_g[l]), row(conv_ln_b[l])))
    return xf.reshape(b, s, d)
```

```python
import functools

import jax
import jax.numpy as jnp
import numpy as np
from jax import lax
from jax.experimental import pallas as pl
from jax.experimental.pallas import tpu as pltpu

F32 = jnp.float32
BF16 = jnp.bfloat16

EPS = 1e-6
ROPE_BASE = 10000.0
CHUNK = 64
CONV_WIDTH = 31
HEAD_DIM = 64
LANES = 128
SUBLANES = 8
MXU_DIM = 256

ROW_TILE = 512
FFN_TILE = 1024
FF_CHUNK = MXU_DIM
OUT_TILE = 512
CONV_HALO = 32
CONV_SUBTILES = 8
CONV_CHUNK_GAP = 1
CONV_CHUNK_SPAN = 2
SB_BLOCK = 128
SB_QROWS = 512
SB_TILE = 64
SB_TILES = SB_QROWS // SB_TILE
SB_HEADS_PER_GROUP = LANES // HEAD_DIM
SB_FIRST_BLOCKS = 2
SB_DEAD_LOG = 110.0
SIGN_BIT = 0x80000000
BF16_BITS = 0xFFFF0000
RET_BLOCK = 256
VMEM_LIMIT = 56 * 1024 * 1024


def _cparams(n_axes):
    return pltpu.CompilerParams(dimension_semantics=("arbitrary",) * n_axes,
                                vmem_limit_bytes=VMEM_LIMIT)


def _rms(x, g):
    return (x * lax.rsqrt(jnp.mean(x * x, axis=-1, keepdims=True) + EPS)) * g


def _silu(x):
    return x * jax.nn.sigmoid(x)


def _full(shape):
    return pl.BlockSpec(shape, lambda *_: (0,) * len(shape))


def _resident(stacked, layer):
    tail = stacked.shape[1:]
    return pl.BlockSpec((None,) + tail, lambda *_: (layer,) + (0,) * len(tail),
                        pipeline_mode=pl.Buffered(1))


def _bit_zero(x):
    bits = lax.bitcast_convert_type(x, jnp.uint32)
    return lax.bitcast_convert_type((bits >> 16) >> 16, F32)


def _conv_fill(u, v_scr, starts_sequence, tile_rows):
    d_conv = v_scr.shape[1]
    if starts_sequence is None:
        v_scr[0:CONV_HALO, :] = jnp.zeros((CONV_HALO, d_conv), F32)
    else:
        v_scr[0:CONV_HALO, :] = jnp.where(starts_sequence, 0.0,
                                          v_scr[tile_rows:tile_rows + CONV_HALO, :])
    v_scr[CONV_HALO:CONV_HALO + tile_rows, :] = u[:, :d_conv] * jax.nn.sigmoid(u[:, d_conv:])
    v_scr[CONV_HALO + tile_rows:, :] = jnp.zeros((SUBLANES, d_conv), F32)


def _conv_rows(v_scr, p_scr, cw_ref, cb_ref, lg_ref, lb_ref, out_ref, r, anchor=None):
    off = CONV_HALO - (CONV_WIDTH - 1)
    sub = p_scr.shape[2] - SUBLANES
    cw = cw_ref[...] if anchor is None else cw_ref[...] + anchor
    for shift in range(SUBLANES):
        part = None
        for o in range(shift, off + CONV_WIDTH, SUBLANES):
            if o < off:
                continue
            rows = v_scr[pl.ds(r * sub + o - shift, sub + SUBLANES), :]
            term = cw[o - off:o - off + 1, :] * rows
            part = term if part is None else part + term
        p_scr[r % 2, shift] = part
    y = cb_ref[...] + p_scr[r % 2, 0, 0:sub, :]
    for shift in range(1, SUBLANES):
        y = y + p_scr[r % 2, shift, pl.ds(shift, sub), :]
    mu = jnp.mean(y, axis=-1, keepdims=True)
    yc = y - mu
    var = jnp.mean(yc * yc, axis=-1, keepdims=True)
    ln = yc * lax.rsqrt(var + EPS) * lg_ref[...] + lb_ref[...]
    out = _silu(ln)
    out_ref[r * sub:(r + 1) * sub, :] = out.astype(BF16)
    return out[0:1, :]


def _ffn_kernel(*refs, tile, n_chunks, n_out, n_mix, final, conv_tiles_per_seq):
    refs = list(refs)
    x_ref = refs.pop(0)
    mix_refs = [refs.pop(0) for _ in range(n_mix)]
    wmix_ref = refs.pop(0) if n_mix else None
    g_ref, win_ref, wout_ref = refs.pop(0), refs.pop(0), refs.pop(0)
    fg_ref = refs.pop(0) if final else None
    has_conv = conv_tiles_per_seq is not None
    if has_conv:
        u_first_ref, u_next_ref = refs.pop(0), refs.pop(0)
        conv_params = [refs.pop(0) for _ in range(4)]
    o_ref, h_scr, a_scr = refs.pop(0), refs.pop(0), refs.pop(0)
    out_tiles = [slice(m * OUT_TILE, (m + 1) * OUT_TILE) for m in range(n_out)]
    d_ff = wout_ref.shape[0]
    n_sub = CONV_SUBTILES
    if has_conv:
        v_scr, p_scr, yc_scr = refs.pop(0), refs.pop(0), refs.pop(0)
        step = pl.program_id(0)

        @pl.when(step == 0)
        def _():
            _conv_fill(u_first_ref[...], v_scr, None, tile)
            for r in range(n_sub):
                _conv_rows(v_scr, p_scr, *conv_params, yc_scr, r)

        mix_refs = [yc_scr] + mix_refs
    if mix_refs:
        for sl in out_tiles:
            y, lo = None, 0
            for y_ref in mix_refs:
                part = jnp.dot(y_ref[...], wmix_ref[lo:lo + y_ref.shape[1], sl],
                               preferred_element_type=F32)
                y = part if y is None else y + part
                lo += y_ref.shape[1]
            o_ref[:, sl] = x_ref[:, sl] + y
        x_ref = o_ref
    h_scr[...] = _rms(x_ref[...], g_ref[...]).astype(BF16)
    if has_conv:
        nxt = jnp.minimum(step + 1, pl.num_programs(0) - 1)
        _conv_fill(u_next_ref[...], v_scr, nxt % conv_tiles_per_seq == 0, tile)
    done = {}
    for j in range(n_chunks):
        sl = slice(j * FF_CHUNK, (j + 1) * FF_CHUNK)
        up_sl = slice(d_ff + j * FF_CHUNK, d_ff + (j + 1) * FF_CHUNK)
        gate = jnp.dot(h_scr[...], win_ref[:, sl], preferred_element_type=F32)
        up = jnp.dot(h_scr[...], win_ref[:, up_sl], preferred_element_type=F32)
        if j in done:
            gate = gate + done.pop(j)
        a_scr[:, sl] = (_silu(gate) * up).astype(BF16)
        if has_conv and j % CONV_CHUNK_GAP == 0 and j // CONV_CHUNK_GAP < n_sub:
            r = j // CONV_CHUNK_GAP
            out_row = _conv_rows(v_scr, p_scr, *conv_params, yc_scr, r,
                                 anchor=_bit_zero(gate[0:1, :conv_params[0].shape[1]]))
            done[j + CONV_CHUNK_SPAN] = _bit_zero(out_row)
    assert not done
    for sl in out_tiles:
        y = jnp.dot(a_scr[...], wout_ref[:, sl], preferred_element_type=F32)
        o_ref[:, sl] = x_ref[:, sl] + 0.5 * y
    if final:
        o_ref[...] = _rms(o_ref[...], fg_ref[...])


def _ffn(x, g, win, wout, layer, tile, mix=(), wmix=None, final_g=None, conv=None):
    n, d = x.shape
    d_ff = wout.shape[1]
    n_chunks = d_ff // FF_CHUNK
    n_tiles = n // tile
    final = final_g is not None
    row = lambda i: (i, 0)
    in_specs = [pl.BlockSpec((tile, d), row)]
    in_specs += [pl.BlockSpec((tile, y.shape[1]), row) for y in mix]
    args = [x, *mix]
    if mix:
        in_specs.append(_resident(wmix, layer))
        args.append(wmix)
    in_specs += [_full((1, d)), _resident(win, layer), _resident(wout, layer)]
    args += [g, win, wout]
    if final:
        in_specs.append(_full((1, d)))
        args.append(final_g)
    scratch = [pltpu.VMEM((tile, d), BF16), pltpu.VMEM((tile, d_ff), BF16)]
    conv_tiles_per_seq = None
    if conv is not None:
        u, seq, cw, cb, lg, lb = conv
        d_conv = cw.shape[1]
        assert seq % tile == 0 and tile % (CONV_SUBTILES * SUBLANES) == 0
        assert CONV_HALO >= CONV_WIDTH - 1
        conv_tiles_per_seq = seq // tile
        in_specs += [pl.BlockSpec((tile, u.shape[1]), lambda i: (0, 0), pipeline_mode=pl.Buffered(1)),
                     pl.BlockSpec((tile, u.shape[1]), lambda i: (jnp.minimum(i + 1, n_tiles - 1), 0)),
                     _full(cw.shape), _full((1, d_conv)), _full((1, d_conv)), _full((1, d_conv))]
        args += [u, u, cw, cb, lg, lb]
        scratch += [pltpu.VMEM((CONV_HALO + tile + SUBLANES, d_conv), F32),
                    pltpu.VMEM((2, SUBLANES, tile // CONV_SUBTILES + SUBLANES, d_conv), F32),
                    pltpu.VMEM((tile, d_conv), BF16)]
    return pl.pallas_call(
        functools.partial(_ffn_kernel, tile=tile, n_chunks=n_chunks, n_out=d // OUT_TILE,
                          n_mix=len(mix), final=final, conv_tiles_per_seq=conv_tiles_per_seq),
        out_shape=jax.ShapeDtypeStruct((n, d), F32),
        grid=(n_tiles,),
        in_specs=in_specs,
        out_specs=pl.BlockSpec((tile, d), row),
        scratch_shapes=scratch,
        compiler_params=_cparams(1),
        name="ffn",
    )(*args)


def _rotary(t, cos, sin_signed, first_half):
    partner = jnp.where(first_half, pltpu.roll(t, LANES - HEAD_DIM // 2, 1),
                        pltpu.roll(t, HEAD_DIM // 2, 1))
    return t * cos + partner * sin_signed


def _inproj_kernel(x_ref, g_ref, w_ref, cos_ref, sin_ref, uconv_ref, q_ref, k_ref, v_ref,
                   qkr_ref, vgr_ref, h_scr, *, d_conv2, d_sb, d_ret):
    q_scale = HEAD_DIM ** -0.5
    h_scr[...] = _rms(x_ref[...], g_ref[...]).astype(BF16)

    def proj(lo, width):
        return jnp.dot(h_scr[...], w_ref[:, lo:lo + width], preferred_element_type=F32)

    uconv_ref[...] = proj(0, d_conv2)
    o = d_conv2
    q_ref[...] = (proj(o, d_sb) * q_scale).astype(BF16)
    k_ref[...] = proj(o + d_sb, d_sb).astype(BF16)
    v_ref[...] = proj(o + 2 * d_sb, d_sb).astype(BF16)
    o += 3 * d_sb
    qk = proj(o, 2 * d_ret)
    cos = cos_ref[...]
    sin = sin_ref[...]
    lane = lax.broadcasted_iota(jnp.int32, (1, LANES), 1)
    first_half = (lane % HEAD_DIM) < (HEAD_DIM // 2)
    for c in range(2 * d_ret // LANES):
        sl = slice(c * LANES, (c + 1) * LANES)
        part = qk[:, sl] * q_scale if c < d_ret // LANES else qk[:, sl]
        qkr_ref[:, sl] = _rotary(part, cos, sin, first_half)
    vgr_ref[...] = proj(o + 2 * d_ret, 2 * d_ret)


def _inproj(x, g, w, layer, cos_tab, sin_tab, seq, d_conv2, d_sb, d_ret):
    n, d = x.shape
    tiles_per_seq = seq // FFN_TILE
    row = lambda i: (i, 0)
    pos = lambda i: (i % tiles_per_seq, 0)
    outs = [jax.ShapeDtypeStruct((n, d_conv2), F32)] + [jax.ShapeDtypeStruct((n, d_sb), BF16)] * 3 \
        + [jax.ShapeDtypeStruct((n, 2 * d_ret), F32)] * 2
    return pl.pallas_call(
        functools.partial(_inproj_kernel, d_conv2=d_conv2, d_sb=d_sb, d_ret=d_ret),
        out_shape=outs,
        grid=(n // FFN_TILE,),
        in_specs=[pl.BlockSpec((FFN_TILE, d), row), _full((1, d)), _resident(w, layer),
                  pl.BlockSpec((FFN_TILE, LANES), pos), pl.BlockSpec((FFN_TILE, LANES), pos)],
        out_specs=[pl.BlockSpec((FFN_TILE, s.shape[1]), row) for s in outs],
        scratch_shapes=[pltpu.VMEM((FFN_TILE, d), BF16)],
        compiler_params=_cparams(1),
        name="mixer_inproj",
    )(x, g, w, cos_tab, sin_tab)


def _sb_kernel(q_ref, knew_ref, vnew_ref, tri_ref, o_ref, k_scr, v_scr, qs_scr, acc_scr, carry_scr,
               *, n_pairs):
    q0 = pl.program_id(1) * SB_QROWS
    k_scr[pl.ds(pl.multiple_of(q0, SB_QROWS), SB_QROWS), :] = knew_ref[0]
    v_scr[pl.ds(pl.multiple_of(q0, SB_QROWS), SB_QROWS), :] = vnew_ref[0]
    tiles, pairs = range(SB_TILES), range(n_pairs)
    rows = SB_HEADS_PER_GROUP * SB_TILE
    lane = lax.broadcasted_iota(jnp.int32, (1, LANES), 1)
    col = lax.broadcasted_iota(jnp.int32, (rows, SB_BLOCK), 1)
    row = lax.broadcasted_iota(jnp.int32, (rows, SB_BLOCK), 0) % SB_TILE
    for p in pairs:
        q = q_ref[0, :, p * LANES:(p + 1) * LANES]
        for t in tiles:
            qt = q[t * SB_TILE:(t + 1) * SB_TILE]
            qs_scr[t, p] = jnp.concatenate(
                [jnp.where(lane // HEAD_DIM == hh, qt, jnp.zeros_like(qt))
                 for hh in range(SB_HEADS_PER_GROUP)], axis=0)

    def sweep(blocks, resume):
        depth = range(len(blocks[0]))

        def rows_of(ref, t, n, p):
            return ref[pl.ds(blocks[t][n][0], SB_BLOCK), p * LANES:(p + 1) * LANES]

        units = [(t, n) for t in tiles for n in depth]

        def logits(t, n):
            return [lax.dot_general(qs_scr[t, p], rows_of(k_scr, t, n, p),
                                    (((1,), (1,)), ((), ())), preferred_element_type=F32)
                    for p in pairs]

        def cumsums(t, n, zs):
            hls = []
            for p in pairs:
                zbits = lax.bitcast_convert_type(zs[p], jnp.uint32)
                neg_abs = lax.bitcast_convert_type(zbits | jnp.uint32(SIGN_BIT), F32)
                sp = jnp.maximum(zs[p], 0.0) + jnp.log(1.0 + jnp.exp(neg_abs))
                if blocks[t][n][1] is not None:
                    sp = jnp.where(blocks[t][n][1], sp, 0.0)
                hi = lax.bitcast_convert_type(
                    lax.bitcast_convert_type(sp, jnp.uint32) & jnp.uint32(BF16_BITS), F32)
                hls.append(jnp.concatenate([hi.astype(BF16), (sp - hi).astype(BF16)], axis=1))
            return jnp.dot(jnp.concatenate(hls, axis=0), tri_ref[...], preferred_element_type=F32)

        carry = {(t, p): carry_scr[t, p] if resume else None for t in tiles for p in pairs}
        acc = {(t, p): acc_scr[t, p] if resume else None for t in tiles for p in pairs}

        def weigh(t, n, zs, rt):
            for p in pairs:
                within = rt[p * rows:(p + 1) * rows, :SB_BLOCK]
                total = rt[p * rows:(p + 1) * rows, SB_BLOCK:]
                before = carry[t, p]
                w = jnp.exp(zs[p] - (within if before is None else within + before))
                if blocks[t][n][1] is not None:
                    w = jnp.where(blocks[t][n][1], w, 0.0)
                carry[t, p] = total if before is None else before + total
                pv = jnp.dot(w.astype(BF16), rows_of(v_scr, t, n, p), preferred_element_type=F32)
                acc[t, p] = pv if acc[t, p] is None else acc[t, p] + pv

        zs, rts = {}, {}
        for u in range(len(units) + 2):
            if u < len(units):
                zs[u] = logits(*units[u])
            if 1 <= u <= len(units):
                rts[u - 1] = cumsums(*units[u - 1], zs[u - 1])
            if u >= 2:
                weigh(*units[u - 2], zs.pop(u - 2), rts.pop(u - 2))
        lowest = None
        for t in tiles:
            for p in pairs:
                carry_scr[t, p] = carry[t, p]
                acc_scr[t, p] = acc[t, p]
                lowest = carry[t, p] if lowest is None else jnp.minimum(lowest, carry[t, p])
        return jnp.min(lowest)

    def aligned(start):
        return pl.multiple_of(start, SB_TILE)

    def first_fast():
        near_mask = col < row + (SB_BLOCK - SB_TILE)
        return sweep([[(aligned(q0 + (t + 1) * SB_TILE - (n + 1) * SB_BLOCK),
                        near_mask if n == 0 else None) for n in range(SB_FIRST_BLOCKS)]
                      for t in tiles], False)

    def first_clamped():
        blocks = []
        for t in tiles:
            limit = q0 + t * SB_TILE + row
            blocks.append([])
            for n in range(SB_FIRST_BLOCKS):
                start = aligned(jnp.maximum(q0 + (t + 1) * SB_TILE - (n + 1) * SB_BLOCK, 0))
                blocks[t].append((start, start + col < limit))
                limit = start
        return sweep(blocks, False)

    def more(st):
        reach = q0 + SB_QROWS - st[0] * SB_BLOCK
        return jnp.logical_and(reach > 0, st[1] < SB_DEAD_LOG)

    def body(st):
        blocks = []
        for t in tiles:
            limit = jnp.maximum(q0 + (t + 1) * SB_TILE - st[0] * SB_BLOCK, 0)
            start = aligned(jnp.maximum(limit - SB_BLOCK, 0))
            blocks.append([(start, start + col < limit)])
        return st[0] + 1, sweep(blocks, True)

    fits = q0 + SB_TILE - SB_FIRST_BLOCKS * SB_BLOCK >= 0
    lowest = lax.cond(fits, first_fast, first_clamped)
    lax.while_loop(more, body, (SB_FIRST_BLOCKS, lowest))
    for p in pairs:
        for t in tiles:
            acc = acc_scr[t, p]
            out = acc[0:SB_TILE]
            for hh in range(1, SB_HEADS_PER_GROUP):
                out = jnp.where(lane // HEAD_DIM == hh, acc[hh * SB_TILE:(hh + 1) * SB_TILE], out)
            o_ref[0, t * SB_TILE:(t + 1) * SB_TILE, p * LANES:(p + 1) * LANES] = out.astype(BF16)


def _stick_breaking(q, k, v, tri):
    b, s, d_sb = q.shape
    n_pairs = d_sb // LANES
    rows = SB_HEADS_PER_GROUP * SB_TILE
    blk = pl.BlockSpec((1, SB_QROWS, d_sb), lambda bi, i: (bi, i, 0))
    return pl.pallas_call(
        functools.partial(_sb_kernel, n_pairs=n_pairs),
        out_shape=jax.ShapeDtypeStruct((b, s, d_sb), BF16),
        grid=(b, s // SB_QROWS),
        in_specs=[blk, blk, blk, _full(tri.shape)],
        out_specs=blk,
        scratch_shapes=[pltpu.VMEM((s, d_sb), BF16), pltpu.VMEM((s, d_sb), BF16),
                        pltpu.VMEM((SB_TILES, n_pairs, rows, LANES), BF16),
                        pltpu.VMEM((SB_TILES, n_pairs, rows, LANES), F32),
                        pltpu.VMEM((SB_TILES, n_pairs, rows, LANES), F32)],
        compiler_params=_cparams(2),
        name="stick_breaking",
    )(q, k, v, tri)


def _ret_kernel(qk_ref, vg_ref, dmat_ref, qdec_ref, kdec_ref, cdec_ref, bd_ref, avg_ref, gain_ref,
                o_ref, state_scr, *, d_ret, n_batch):
    n_heads = d_ret // HEAD_DIM
    batches = range(n_batch)
    nt = (((1,), (1,)), ((), ()))

    @pl.when(pl.program_id(0) == 0)
    def _():
        state_scr[...] = jnp.zeros_like(state_scr)

    lane = lax.broadcasted_iota(jnp.int32, (1, d_ret), 1)
    in_head = [lane // HEAD_DIM == h for h in range(n_heads)]
    q = [qk_ref[b, :, :d_ret] for b in batches]
    k = [qk_ref[b, :, d_ret:] for b in batches]
    v = [vg_ref[b, :, :d_ret] for b in batches]
    probs = []
    for b in batches:
        kb = k[b].astype(BF16)
        probs.append(jnp.concatenate(
            [(lax.dot_general(jnp.where(in_head[h], q[b], 0.0).astype(BF16), kb, nt,
                              preferred_element_type=F32) * dmat_ref[h]).astype(BF16)
             for h in range(n_heads)], axis=1))
    y = []
    for b in batches:
        vals = jnp.concatenate([jnp.where(in_head[h], v[b], 0.0).astype(BF16)
                                for h in range(n_heads)], axis=0)
        y.append(jnp.dot(probs[b], vals, preferred_element_type=F32))
    for b in batches:
        state = state_scr[b]
        y[b] = y[b] + jnp.dot((q[b] * qdec_ref[...]).astype(BF16), state.astype(BF16),
                              preferred_element_type=F32)
        kv = lax.dot_general((k[b] * kdec_ref[...]).astype(BF16), v[b].astype(BF16),
                             (((0,), (0,)), ((), ())), preferred_element_type=F32)
        state_scr[b] = cdec_ref[...] * state + bd_ref[...] * kv

    def head_mean(t):
        hi = t.astype(BF16)
        lo = (t - hi.astype(F32)).astype(BF16)
        return jnp.dot(jnp.concatenate([hi, lo], axis=1), avg_ref[...], preferred_element_type=F32)

    yc = [y[b] - head_mean(y[b]) for b in batches]
    var = [head_mean(yc[b] * yc[b]) for b in batches]
    for b in batches:
        g = vg_ref[b, :, d_ret:]
        o_ref[b] = (_silu(g) * (yc[b] * lax.rsqrt(var[b] + EPS) * gain_ref[...])).astype(BF16)


def _retention(qk, vg, tabs, gain):
    b, s, d2 = qk.shape
    d_ret = d2 // 2
    blk = pl.BlockSpec((b, RET_BLOCK, d2), lambda i: (0, i, 0))
    return pl.pallas_call(
        functools.partial(_ret_kernel, d_ret=d_ret, n_batch=b),
        out_shape=jax.ShapeDtypeStruct((b, s, d_ret), BF16),
        grid=(s // RET_BLOCK,),
        in_specs=[blk, blk] + [_full(t.shape) for t in tabs] + [_full((1, d_ret))],
        out_specs=pl.BlockSpec((b, RET_BLOCK, d_ret), lambda i: (0, i, 0)),
        scratch_shapes=[pltpu.VMEM((b, d_ret, d_ret), F32)],
        compiler_params=_cparams(1),
        name="retention",
    )(qk, vg, *tabs, gain)


def _rotary_tables(seq):
    half = HEAD_DIM // 2
    inv = 1.0 / (ROPE_BASE ** (jnp.arange(half, dtype=F32) / half))
    lane = jnp.arange(LANES)
    ang = jnp.arange(seq).astype(F32)[:, None] * inv[lane % half][None, :]
    sign = jnp.where((lane % HEAD_DIM) < half, -1.0, 1.0).astype(F32)
    return jnp.cos(ang), jnp.sin(ang) * sign[None, :]


def _retention_tables(n_heads):
    d_ret = n_heads * HEAD_DIM
    log_gamma = jnp.log1p(-jnp.exp2(-5.0 - jnp.arange(n_heads, dtype=F32)))
    idx = jnp.arange(RET_BLOCK, dtype=F32)
    t, s = idx[:, None], idx[None, :]
    same = (t // CHUNK) == (s // CHUNK)
    dist = jnp.where(same, jnp.abs(t - s), t - s)
    seen = same | ((s // CHUNK) < (t // CHUNK))
    dmat = jnp.where(seen[None], jnp.exp(log_gamma[:, None, None] * dist[None]), 0.0)
    lane_gamma = jnp.repeat(log_gamma, HEAD_DIM)
    qdec = jnp.exp(lane_gamma[None, :] * (idx + 1.0)[:, None])
    kdec = jnp.exp(lane_gamma[None, :] * (RET_BLOCK - 1.0 - idx)[:, None])
    head = jnp.arange(d_ret) // HEAD_DIM
    bd = (head[:, None] == head[None, :]).astype(F32)
    cdec = bd * jnp.exp(lane_gamma * RET_BLOCK)[:, None]
    avg = jnp.concatenate([bd, bd], axis=0).astype(BF16) * (1.0 / HEAD_DIM)
    return dmat, qdec, kdec, cdec, bd, avg.astype(BF16)


def _tri_table():
    r = np.arange(2 * SB_BLOCK)[:, None] % SB_BLOCK
    c = np.arange(2 * SB_BLOCK)[None, :]
    return jnp.asarray((c >= SB_BLOCK) | (r >= c), dtype=BF16)


def kernel(x, ffn1_norm, ffn1_w_in, ffn1_w_out, mix_norm, mix_w_in, conv_w, conv_b, conv_ln_g,
           conv_ln_b, ret_norm_g, mix_w_out, ffn2_norm, ffn2_w_in, ffn2_w_out, final_norm):
    b, s, d = x.shape
    depth = ffn1_norm.shape[0]
    d_conv = conv_w.shape[2]
    d_ret = ret_norm_g.shape[1]
    d_sb = d - d_conv - d_ret
    assert s % ROW_TILE == 0 and s % FFN_TILE == 0 and s % RET_BLOCK == 0 and s % SB_QROWS == 0
    assert ffn1_w_out.shape[1] % FF_CHUNK == 0 and d % OUT_TILE == 0

    cos_tab, sin_tab = _rotary_tables(s)
    ret_tabs = _retention_tables(d_ret // HEAD_DIM)
    tri = _tri_table()
    win1, wout1 = ffn1_w_in.astype(BF16), ffn1_w_out.astype(BF16)
    win2, wout2 = ffn2_w_in.astype(BF16), ffn2_w_out.astype(BF16)
    w_mix, w_mix_out = mix_w_in.astype(BF16), mix_w_out.astype(BF16)

    row = lambda g: g.reshape(1, -1)
    xf = x.reshape(b * s, d)
    for l in range(depth):
        xf = _ffn(xf, row(ffn1_norm[l]), win1, wout1, l, FFN_TILE)
        uconv, q, k, v, qkr, vgr = _inproj(xf, row(mix_norm[l]), w_mix, l, cos_tab, sin_tab, s,
                                           2 * d_conv, d_sb, d_ret)
        seq3 = lambda t: t.reshape(b, s, t.shape[1])
        y_sb = _stick_breaking(seq3(q), seq3(k), seq3(v), tri)
        y_ret = _retention(seq3(qkr), seq3(vgr), ret_tabs, row(ret_norm_g[l]))
        flat = lambda t: t.reshape(b * s, t.shape[2])
        last = l == depth - 1
        xf = _ffn(xf, row(ffn2_norm[l]), win2, wout2, l, ROW_TILE,
                  mix=(flat(y_sb), flat(y_ret)), wmix=w_mix_out,
                  final_g=row(final_norm) if last else None,
                  conv=(uconv, s, conv_w[l], row(conv_b[l]), row(conv_ln_g[l]), row(conv_ln_b[l])))
    return xf.reshape(b, s, d)
```

```python
import functools

import jax
import jax.numpy as jnp
import numpy as np
from jax import lax
from jax.experimental import pallas as pl
from jax.experimental.pallas import tpu as pltpu

F32 = jnp.float32
BF16 = jnp.bfloat16

EPS = 1e-6
ROPE_BASE = 10000.0
CHUNK = 64
CONV_WIDTH = 31
HEAD_DIM = 64
LANES = 128
SUBLANES = 8
MXU_DIM = 256

ROW_TILE = 512
FFN_TILE = 1024
FF_CHUNK = MXU_DIM
OUT_TILE = 512
CONV_HALO = 32
CONV_SUBTILES = 8
CONV_CHUNK_GAP = 1
CONV_CHUNK_SPAN = 2
SB_BLOCK = 128
SB_QROWS = 512
SB_TILE = 64
SB_TILES = SB_QROWS // SB_TILE
SB_HEADS_PER_GROUP = LANES // HEAD_DIM
SB_FIRST_BLOCKS = 2
SB_DEAD_LOG = 110.0
SIGN_BIT = 0x80000000
BF16_BITS = 0xFFFF0000
RET_BLOCK = 256
RET_STEP_BLOCKS = 2
VMEM_LIMIT = 56 * 1024 * 1024


def _cparams(n_axes):
    return pltpu.CompilerParams(dimension_semantics=("arbitrary",) * n_axes,
                                vmem_limit_bytes=VMEM_LIMIT)


def _rms(x, g):
    return (x * lax.rsqrt(jnp.mean(x * x, axis=-1, keepdims=True) + EPS)) * g


def _silu(x):
    return x * jax.nn.sigmoid(x)


def _full(shape):
    return pl.BlockSpec(shape, lambda *_: (0,) * len(shape))


def _resident(stacked, layer):
    tail = stacked.shape[1:]
    return pl.BlockSpec((None,) + tail, lambda *_: (layer,) + (0,) * len(tail),
                        pipeline_mode=pl.Buffered(1))


def _bit_zero(x):
    bits = lax.bitcast_convert_type(x, jnp.uint32)
    return lax.bitcast_convert_type((bits >> 16) >> 16, F32)


def _conv_fill(u, v_scr, starts_sequence, tile_rows):
    d_conv = v_scr.shape[1]
    if starts_sequence is None:
        v_scr[0:CONV_HALO, :] = jnp.zeros((CONV_HALO, d_conv), F32)
    else:
        v_scr[0:CONV_HALO, :] = jnp.where(starts_sequence, 0.0,
                                          v_scr[tile_rows:tile_rows + CONV_HALO, :])
    v_scr[CONV_HALO:CONV_HALO + tile_rows, :] = u[:, :d_conv] * jax.nn.sigmoid(u[:, d_conv:])
    v_scr[CONV_HALO + tile_rows:, :] = jnp.zeros((SUBLANES, d_conv), F32)


def _conv_rows(v_scr, p_scr, cw_ref, cb_ref, lg_ref, lb_ref, out_ref, r, anchor=None):
    off = CONV_HALO - (CONV_WIDTH - 1)
    sub = p_scr.shape[2] - SUBLANES
    cw = cw_ref[...] if anchor is None else cw_ref[...] + anchor
    for shift in range(SUBLANES):
        part = None
        for o in range(shift, off + CONV_WIDTH, SUBLANES):
            if o < off:
                continue
            rows = v_scr[pl.ds(r * sub + o - shift, sub + SUBLANES), :]
            term = cw[o - off:o - off + 1, :] * rows
            part = term if part is None else part + term
        p_scr[r % 2, shift] = part
    y = cb_ref[...] + p_scr[r % 2, 0, 0:sub, :]
    for shift in range(1, SUBLANES):
        y = y + p_scr[r % 2, shift, pl.ds(shift, sub), :]
    mu = jnp.mean(y, axis=-1, keepdims=True)
    yc = y - mu
    var = jnp.mean(yc * yc, axis=-1, keepdims=True)
    ln = yc * lax.rsqrt(var + EPS) * lg_ref[...] + lb_ref[...]
    out = _silu(ln)
    out_ref[r * sub:(r + 1) * sub, :] = out.astype(BF16)
    return out[0:1, :]


def _ffn_kernel(*refs, tile, n_chunks, n_out, n_mix, final, conv_tiles_per_seq):
    refs = list(refs)
    x_ref = refs.pop(0)
    mix_refs = [refs.pop(0) for _ in range(n_mix)]
    wmix_ref = refs.pop(0) if n_mix else None
    g_ref, win_ref, wout_ref = refs.pop(0), refs.pop(0), refs.pop(0)
    fg_ref = refs.pop(0) if final else None
    has_conv = conv_tiles_per_seq is not None
    if has_conv:
        u_first_ref, u_next_ref = refs.pop(0), refs.pop(0)
        conv_params = [refs.pop(0) for _ in range(4)]
    o_ref, h_scr, a_scr = refs.pop(0), refs.pop(0), refs.pop(0)
    out_tiles = [slice(m * OUT_TILE, (m + 1) * OUT_TILE) for m in range(n_out)]
    d_ff = wout_ref.shape[0]
    n_sub = CONV_SUBTILES
    if has_conv:
        v_scr, p_scr, yc_scr = refs.pop(0), refs.pop(0), refs.pop(0)
        step = pl.program_id(0)

        @pl.when(step == 0)
        def _():
            _conv_fill(u_first_ref[...], v_scr, None, tile)
            for r in range(n_sub):
                _conv_rows(v_scr, p_scr, *conv_params, yc_scr, r)

        mix_refs = [yc_scr] + mix_refs
    if mix_refs:
        for sl in out_tiles:
            y, lo = None, 0
            for y_ref in mix_refs:
                part = jnp.dot(y_ref[...], wmix_ref[lo:lo + y_ref.shape[1], sl],
                               preferred_element_type=F32)
                y = part if y is None else y + part
                lo += y_ref.shape[1]
            o_ref[:, sl] = x_ref[:, sl] + y
        x_ref = o_ref
    h_scr[...] = _rms(x_ref[...], g_ref[...]).astype(BF16)
    if has_conv:
        nxt = jnp.minimum(step + 1, pl.num_programs(0) - 1)
        _conv_fill(u_next_ref[...], v_scr, nxt % conv_tiles_per_seq == 0, tile)
    done = {}
    for j in range(n_chunks):
        sl = slice(j * FF_CHUNK, (j + 1) * FF_CHUNK)
        up_sl = slice(d_ff + j * FF_CHUNK, d_ff + (j + 1) * FF_CHUNK)
        gate = jnp.dot(h_scr[...], win_ref[:, sl], preferred_element_type=F32)
        up = jnp.dot(h_scr[...], win_ref[:, up_sl], preferred_element_type=F32)
        if j in done:
            gate = gate + done.pop(j)
        a_scr[:, sl] = (_silu(gate) * up).astype(BF16)
        if has_conv and j % CONV_CHUNK_GAP == 0 and j // CONV_CHUNK_GAP < n_sub:
            r = j // CONV_CHUNK_GAP
            out_row = _conv_rows(v_scr, p_scr, *conv_params, yc_scr, r,
                                 anchor=_bit_zero(gate[0:1, :conv_params[0].shape[1]]))
            done[j + CONV_CHUNK_SPAN] = _bit_zero(out_row)
    assert not done
    for sl in out_tiles:
        y = jnp.dot(a_scr[...], wout_ref[:, sl], preferred_element_type=F32)
        o_ref[:, sl] = x_ref[:, sl] + 0.5 * y
    if final:
        o_ref[...] = _rms(o_ref[...], fg_ref[...])


def _ffn(x, g, win, wout, layer, tile, mix=(), wmix=None, final_g=None, conv=None):
    n, d = x.shape
    d_ff = wout.shape[1]
    n_chunks = d_ff // FF_CHUNK
    n_tiles = n // tile
    final = final_g is not None
    row = lambda i: (i, 0)
    in_specs = [pl.BlockSpec((tile, d), row)]
    in_specs += [pl.BlockSpec((tile, y.shape[1]), row) for y in mix]
    args = [x, *mix]
    if mix:
        in_specs.append(_resident(wmix, layer))
        args.append(wmix)
    in_specs += [_full((1, d)), _resident(win, layer), _resident(wout, layer)]
    args += [g, win, wout]
    if final:
        in_specs.append(_full((1, d)))
        args.append(final_g)
    scratch = [pltpu.VMEM((tile, d), BF16), pltpu.VMEM((tile, d_ff), BF16)]
    conv_tiles_per_seq = None
    if conv is not None:
        u, seq, cw, cb, lg, lb = conv
        d_conv = cw.shape[1]
        assert seq % tile == 0 and tile % (CONV_SUBTILES * SUBLANES) == 0
        assert CONV_HALO >= CONV_WIDTH - 1
        conv_tiles_per_seq = seq // tile
        in_specs += [pl.BlockSpec((tile, u.shape[1]), lambda i: (0, 0), pipeline_mode=pl.Buffered(1)),
                     pl.BlockSpec((tile, u.shape[1]), lambda i: (jnp.minimum(i + 1, n_tiles - 1), 0)),
                     _full(cw.shape), _full((1, d_conv)), _full((1, d_conv)), _full((1, d_conv))]
        args += [u, u, cw, cb, lg, lb]
        scratch += [pltpu.VMEM((CONV_HALO + tile + SUBLANES, d_conv), F32),
                    pltpu.VMEM((2, SUBLANES, tile // CONV_SUBTILES + SUBLANES, d_conv), F32),
                    pltpu.VMEM((tile, d_conv), BF16)]
    return pl.pallas_call(
        functools.partial(_ffn_kernel, tile=tile, n_chunks=n_chunks, n_out=d // OUT_TILE,
                          n_mix=len(mix), final=final, conv_tiles_per_seq=conv_tiles_per_seq),
        out_shape=jax.ShapeDtypeStruct((n, d), F32),
        grid=(n_tiles,),
        in_specs=in_specs,
        out_specs=pl.BlockSpec((tile, d), row),
        scratch_shapes=scratch,
        compiler_params=_cparams(1),
        name="ffn",
    )(*args)


def _rotary(t, cos, sin_signed, first_half):
    partner = jnp.where(first_half, pltpu.roll(t, LANES - HEAD_DIM // 2, 1),
                        pltpu.roll(t, HEAD_DIM // 2, 1))
    return t * cos + partner * sin_signed


def _inproj_kernel(x_ref, g_ref, w_ref, cos_ref, sin_ref, uconv_ref, q_ref, k_ref, v_ref,
                   qkr_ref, vgr_ref, h_scr, *, d_conv2, d_sb, d_ret):
    q_scale = HEAD_DIM ** -0.5
    h_scr[...] = _rms(x_ref[...], g_ref[...]).astype(BF16)

    def proj(lo, width):
        return jnp.dot(h_scr[...], w_ref[:, lo:lo + width], preferred_element_type=F32)

    uconv_ref[...] = proj(0, d_conv2)
    o = d_conv2
    q_ref[...] = (proj(o, d_sb) * q_scale).astype(BF16)
    k_ref[...] = proj(o + d_sb, d_sb).astype(BF16)
    v_ref[...] = proj(o + 2 * d_sb, d_sb).astype(BF16)
    o += 3 * d_sb
    qk = proj(o, 2 * d_ret)
    cos = cos_ref[...]
    sin = sin_ref[...]
    lane = lax.broadcasted_iota(jnp.int32, (1, LANES), 1)
    first_half = (lane % HEAD_DIM) < (HEAD_DIM // 2)
    for c in range(2 * d_ret // LANES):
        sl = slice(c * LANES, (c + 1) * LANES)
        part = qk[:, sl] * q_scale if c < d_ret // LANES else qk[:, sl]
        qkr_ref[:, sl] = _rotary(part, cos, sin, first_half)
    vgr_ref[...] = proj(o + 2 * d_ret, 2 * d_ret)


def _inproj(x, g, w, layer, cos_tab, sin_tab, seq, d_conv2, d_sb, d_ret):
    n, d = x.shape
    tiles_per_seq = seq // FFN_TILE
    row = lambda i: (i, 0)
    pos = lambda i: (i % tiles_per_seq, 0)
    outs = [jax.ShapeDtypeStruct((n, d_conv2), F32)] + [jax.ShapeDtypeStruct((n, d_sb), BF16)] * 3 \
        + [jax.ShapeDtypeStruct((n, 2 * d_ret), F32)] * 2
    return pl.pallas_call(
        functools.partial(_inproj_kernel, d_conv2=d_conv2, d_sb=d_sb, d_ret=d_ret),
        out_shape=outs,
        grid=(n // FFN_TILE,),
        in_specs=[pl.BlockSpec((FFN_TILE, d), row), _full((1, d)), _resident(w, layer),
                  pl.BlockSpec((FFN_TILE, LANES), pos), pl.BlockSpec((FFN_TILE, LANES), pos)],
        out_specs=[pl.BlockSpec((FFN_TILE, s.shape[1]), row) for s in outs],
        scratch_shapes=[pltpu.VMEM((FFN_TILE, d), BF16)],
        compiler_params=_cparams(1),
        name="mixer_inproj",
    )(x, g, w, cos_tab, sin_tab)


def _sb_kernel(q_ref, knew_ref, vnew_ref, tri_ref, o_ref, k_scr, v_scr, qs_scr, acc_scr, carry_scr,
               *, n_pairs):
    q0 = pl.program_id(1) * SB_QROWS
    k_scr[pl.ds(pl.multiple_of(q0, SB_QROWS), SB_QROWS), :] = knew_ref[0]
    v_scr[pl.ds(pl.multiple_of(q0, SB_QROWS), SB_QROWS), :] = vnew_ref[0]
    tiles, pairs = range(SB_TILES), range(n_pairs)
    rows = SB_HEADS_PER_GROUP * SB_TILE
    lane = lax.broadcasted_iota(jnp.int32, (1, LANES), 1)
    col = lax.broadcasted_iota(jnp.int32, (rows, SB_BLOCK), 1)
    row = lax.broadcasted_iota(jnp.int32, (rows, SB_BLOCK), 0) % SB_TILE
    for p in pairs:
        q = q_ref[0, :, p * LANES:(p + 1) * LANES]
        for t in tiles:
            qt = q[t * SB_TILE:(t + 1) * SB_TILE]
            qs_scr[t, p] = jnp.concatenate(
                [jnp.where(lane // HEAD_DIM == hh, qt, jnp.zeros_like(qt))
                 for hh in range(SB_HEADS_PER_GROUP)], axis=0)

    def sweep(blocks, resume):
        depth = range(len(blocks[0]))

        def rows_of(ref, t, n, p):
            return ref[pl.ds(blocks[t][n][0], SB_BLOCK), p * LANES:(p + 1) * LANES]

        units = [(t, n) for t in tiles for n in depth]

        def logits(t, n):
            return [lax.dot_general(qs_scr[t, p], rows_of(k_scr, t, n, p),
                                    (((1,), (1,)), ((), ())), preferred_element_type=F32)
                    for p in pairs]

        def cumsums(t, n, zs):
            hls = []
            for p in pairs:
                zbits = lax.bitcast_convert_type(zs[p], jnp.uint32)
                neg_abs = lax.bitcast_convert_type(zbits | jnp.uint32(SIGN_BIT), F32)
                sp = jnp.maximum(zs[p], 0.0) + jnp.log(1.0 + jnp.exp(neg_abs))
                if blocks[t][n][1] is not None:
                    sp = jnp.where(blocks[t][n][1], sp, 0.0)
                hi = lax.bitcast_convert_type(
                    lax.bitcast_convert_type(sp, jnp.uint32) & jnp.uint32(BF16_BITS), F32)
                hls.append(jnp.concatenate([hi.astype(BF16), (sp - hi).astype(BF16)], axis=1))
            return jnp.dot(jnp.concatenate(hls, axis=0), tri_ref[...], preferred_element_type=F32)

        carry = {(t, p): carry_scr[t, p] if resume else None for t in tiles for p in pairs}
        acc = {(t, p): acc_scr[t, p] if resume else None for t in tiles for p in pairs}

        def weigh(t, n, zs, rt):
            for p in pairs:
                within = rt[p * rows:(p + 1) * rows, :SB_BLOCK]
                total = rt[p * rows:(p + 1) * rows, SB_BLOCK:]
                before = carry[t, p]
                w = jnp.exp(zs[p] - (within if before is None else within + before))
                if blocks[t][n][1] is not None:
                    w = jnp.where(blocks[t][n][1], w, 0.0)
                carry[t, p] = total if before is None else before + total
                pv = jnp.dot(w.astype(BF16), rows_of(v_scr, t, n, p), preferred_element_type=F32)
                acc[t, p] = pv if acc[t, p] is None else acc[t, p] + pv

        zs, rts = {}, {}
        for u in range(len(units) + 2):
            if u < len(units):
                zs[u] = logits(*units[u])
            if 1 <= u <= len(units):
                rts[u - 1] = cumsums(*units[u - 1], zs[u - 1])
            if u >= 2:
                weigh(*units[u - 2], zs.pop(u - 2), rts.pop(u - 2))
        lowest = None
        for t in tiles:
            for p in pairs:
                carry_scr[t, p] = carry[t, p]
                acc_scr[t, p] = acc[t, p]
                lowest = carry[t, p] if lowest is None else jnp.minimum(lowest, carry[t, p])
        return jnp.min(lowest)

    def aligned(start):
        return pl.multiple_of(start, SB_TILE)

    def first_fast():
        near_mask = col < row + (SB_BLOCK - SB_TILE)
        return sweep([[(aligned(q0 + (t + 1) * SB_TILE - (n + 1) * SB_BLOCK),
                        near_mask if n == 0 else None) for n in range(SB_FIRST_BLOCKS)]
                      for t in tiles], False)

    def first_clamped():
        blocks = []
        for t in tiles:
            limit = q0 + t * SB_TILE + row
            blocks.append([])
            for n in range(SB_FIRST_BLOCKS):
                start = aligned(jnp.maximum(q0 + (t + 1) * SB_TILE - (n + 1) * SB_BLOCK, 0))
                blocks[t].append((start, start + col < limit))
                limit = start
        return sweep(blocks, False)

    def more(st):
        reach = q0 + SB_QROWS - st[0] * SB_BLOCK
        return jnp.logical_and(reach > 0, st[1] < SB_DEAD_LOG)

    def body(st):
        blocks = []
        for t in tiles:
            limit = jnp.maximum(q0 + (t + 1) * SB_TILE - st[0] * SB_BLOCK, 0)
            start = aligned(jnp.maximum(limit - SB_BLOCK, 0))
            blocks.append([(start, start + col < limit)])
        return st[0] + 1, sweep(blocks, True)

    fits = q0 + SB_TILE - SB_FIRST_BLOCKS * SB_BLOCK >= 0
    lowest = lax.cond(fits, first_fast, first_clamped)
    lax.while_loop(more, body, (SB_FIRST_BLOCKS, lowest))
    for p in pairs:
        for t in tiles:
            acc = acc_scr[t, p]
            out = acc[0:SB_TILE]
            for hh in range(1, SB_HEADS_PER_GROUP):
                out = jnp.where(lane // HEAD_DIM == hh, acc[hh * SB_TILE:(hh + 1) * SB_TILE], out)
            o_ref[0, t * SB_TILE:(t + 1) * SB_TILE, p * LANES:(p + 1) * LANES] = out.astype(BF16)


def _stick_breaking(q, k, v, tri):
    b, s, d_sb = q.shape
    n_pairs = d_sb // LANES
    rows = SB_HEADS_PER_GROUP * SB_TILE
    blk = pl.BlockSpec((1, SB_QROWS, d_sb), lambda bi, i: (bi, i, 0))
    return pl.pallas_call(
        functools.partial(_sb_kernel, n_pairs=n_pairs),
        out_shape=jax.ShapeDtypeStruct((b, s, d_sb), BF16),
        grid=(b, s // SB_QROWS),
        in_specs=[blk, blk, blk, _full(tri.shape)],
        out_specs=blk,
        scratch_shapes=[pltpu.VMEM((s, d_sb), BF16), pltpu.VMEM((s, d_sb), BF16),
                        pltpu.VMEM((SB_TILES, n_pairs, rows, LANES), BF16),
                        pltpu.VMEM((SB_TILES, n_pairs, rows, LANES), F32),
                        pltpu.VMEM((SB_TILES, n_pairs, rows, LANES), F32)],
        compiler_params=_cparams(2),
        name="stick_breaking",
    )(q, k, v, tri)


def _ret_kernel(qk_ref, vg_ref, dmat_ref, qdec_ref, kdec_ref, cdec_ref, bd_ref, avg_ref, gain_ref,
                o_ref, state_scr, *, d_ret, n_batch):
    n_heads = d_ret // HEAD_DIM
    chains = [(b, slice(c * RET_BLOCK, (c + 1) * RET_BLOCK))
              for c in range(RET_STEP_BLOCKS) for b in range(n_batch)]
    nt = (((1,), (1,)), ((), ()))

    @pl.when(pl.program_id(0) == 0)
    def _():
        state_scr[...] = jnp.zeros_like(state_scr)

    lane = lax.broadcasted_iota(jnp.int32, (1, d_ret), 1)
    in_head = [lane // HEAD_DIM == h for h in range(n_heads)]
    q = [qk_ref[b, rows, :d_ret] for b, rows in chains]
    k = [qk_ref[b, rows, d_ret:] for b, rows in chains]
    v = [vg_ref[b, rows, :d_ret] for b, rows in chains]
    probs = []
    for i in range(len(chains)):
        kb = k[i].astype(BF16)
        probs.append(jnp.concatenate(
            [(lax.dot_general(jnp.where(in_head[h], q[i], 0.0).astype(BF16), kb, nt,
                              preferred_element_type=F32) * dmat_ref[h]).astype(BF16)
             for h in range(n_heads)], axis=1))
    y = []
    for i in range(len(chains)):
        vals = jnp.concatenate([jnp.where(in_head[h], v[i], 0.0).astype(BF16)
                                for h in range(n_heads)], axis=0)
        y.append(jnp.dot(probs[i], vals, preferred_element_type=F32))
    for i, (b, _) in enumerate(chains):
        state = state_scr[b]
        y[i] = y[i] + jnp.dot((q[i] * qdec_ref[...]).astype(BF16), state.astype(BF16),
                              preferred_element_type=F32)
        kv = lax.dot_general((k[i] * kdec_ref[...]).astype(BF16), v[i].astype(BF16),
                             (((0,), (0,)), ((), ())), preferred_element_type=F32)
        state_scr[b] = cdec_ref[...] * state + bd_ref[...] * kv

    def head_mean(t):
        hi = t.astype(BF16)
        lo = (t - hi.astype(F32)).astype(BF16)
        return jnp.dot(jnp.concatenate([hi, lo], axis=1), avg_ref[...], preferred_element_type=F32)

    yc = [y[i] - head_mean(y[i]) for i in range(len(chains))]
    var = [head_mean(yc[i] * yc[i]) for i in range(len(chains))]
    for i, (b, rows) in enumerate(chains):
        g = vg_ref[b, rows, d_ret:]
        o_ref[b, rows, :] = (_silu(g) * (yc[i] * lax.rsqrt(var[i] + EPS) * gain_ref[...])).astype(BF16)


def _retention(qk, vg, tabs, gain):
    b, s, d2 = qk.shape
    d_ret = d2 // 2
    step_rows = RET_STEP_BLOCKS * RET_BLOCK
    blk = pl.BlockSpec((b, step_rows, d2), lambda i: (0, i, 0))
    return pl.pallas_call(
        functools.partial(_ret_kernel, d_ret=d_ret, n_batch=b),
        out_shape=jax.ShapeDtypeStruct((b, s, d_ret), BF16),
        grid=(s // step_rows,),
        in_specs=[blk, blk] + [_full(t.shape) for t in tabs] + [_full((1, d_ret))],
        out_specs=pl.BlockSpec((b, step_rows, d_ret), lambda i: (0, i, 0)),
        scratch_shapes=[pltpu.VMEM((b, d_ret, d_ret), F32)],
        compiler_params=_cparams(1),
        name="retention",
    )(qk, vg, *tabs, gain)


def _rotary_tables(seq):
    half = HEAD_DIM // 2
    inv = 1.0 / (ROPE_BASE ** (jnp.arange(half, dtype=F32) / half))
    lane = jnp.arange(LANES)
    ang = jnp.arange(seq).astype(F32)[:, None] * inv[lane % half][None, :]
    sign = jnp.where((lane % HEAD_DIM) < half, -1.0, 1.0).astype(F32)
    return jnp.cos(ang), jnp.sin(ang) * sign[None, :]


def _retention_tables(n_heads):
    d_ret = n_heads * HEAD_DIM
    log_gamma = jnp.log1p(-jnp.exp2(-5.0 - jnp.arange(n_heads, dtype=F32)))
    idx = jnp.arange(RET_BLOCK, dtype=F32)
    t, s = idx[:, None], idx[None, :]
    same = (t // CHUNK) == (s // CHUNK)
    dist = jnp.where(same, jnp.abs(t - s), t - s)
    seen = same | ((s // CHUNK) < (t // CHUNK))
    dmat = jnp.where(seen[None], jnp.exp(log_gamma[:, None, None] * dist[None]), 0.0)
    lane_gamma = jnp.repeat(log_gamma, HEAD_DIM)
    qdec = jnp.exp(lane_gamma[None, :] * (idx + 1.0)[:, None])
    kdec = jnp.exp(lane_gamma[None, :] * (RET_BLOCK - 1.0 - idx)[:, None])
    head = jnp.arange(d_ret) // HEAD_DIM
    bd = (head[:, None] == head[None, :]).astype(F32)
    cdec = bd * jnp.exp(lane_gamma * RET_BLOCK)[:, None]
    avg = jnp.concatenate([bd, bd], axis=0).astype(BF16) * (1.0 / HEAD_DIM)
    return dmat, qdec, kdec, cdec, bd, avg.astype(BF16)


def _tri_table():
    r = np.arange(2 * SB_BLOCK)[:, None] % SB_BLOCK
    c = np.arange(2 * SB_BLOCK)[None, :]
    return jnp.asarray((c >= SB_BLOCK) | (r >= c), dtype=BF16)


def kernel(x, ffn1_norm, ffn1_w_in, ffn1_w_out, mix_norm, mix_w_in, conv_w, conv_b, conv_ln_g,
           conv_ln_b, ret_norm_g, mix_w_out, ffn2_norm, ffn2_w_in, ffn2_w_out, final_norm):
    b, s, d = x.shape
    depth = ffn1_norm.shape[0]
    d_conv = conv_w.shape[2]
    d_ret = ret_norm_g.shape[1]
    d_sb = d - d_conv - d_ret
    assert s % ROW_TILE == 0 and s % FFN_TILE == 0 and s % (RET_STEP_BLOCKS * RET_BLOCK) == 0 and s % SB_QROWS == 0
    assert ffn1_w_out.shape[1] % FF_CHUNK == 0 and d % OUT_TILE == 0

    cos_tab, sin_tab = _rotary_tables(s)
    ret_tabs = _retention_tables(d_ret // HEAD_DIM)
    tri = _tri_table()
    win1, wout1 = ffn1_w_in.astype(BF16), ffn1_w_out.astype(BF16)
    win2, wout2 = ffn2_w_in.astype(BF16), ffn2_w_out.astype(BF16)
    w_mix, w_mix_out = mix_w_in.astype(BF16), mix_w_out.astype(BF16)

    row = lambda g: g.reshape(1, -1)
    xf = x.reshape(b * s, d)
    for l in range(depth):
        xf = _ffn(xf, row(ffn1_norm[l]), win1, wout1, l, FFN_TILE)
        uconv, q, k, v, qkr, vgr = _inproj(xf, row(mix_norm[l]), w_mix, l, cos_tab, sin_tab, s,
                                           2 * d_conv, d_sb, d_ret)
        seq3 = lambda t: t.reshape(b, s, t.shape[1])
        y_sb = _stick_breaking(seq3(q), seq3(k), seq3(v), tri)
        y_ret = _retention(seq3(qkr), seq3(vgr), ret_tabs, row(ret_norm_g[l]))
        flat = lambda t: t.reshape(b * s, t.shape[2])
        last = l == depth - 1
        xf = _ffn(xf, row(ffn2_norm[l]), win2, wout2, l, ROW_TILE,
                  mix=(flat(y_sb), flat(y_ret)), wmix=w_mix_out,
                  final_g=row(final_norm) if last else None,
                  conv=(uconv, s, conv_w[l], row(conv_b[l]), row(conv_ln_g[l]), row(conv_ln_b[l])))
    return xf.reshape(b, s, d)
```

```python
import functools

import jax
import jax.numpy as jnp
import numpy as np
from jax import lax
from jax.experimental import pallas as pl
from jax.experimental.pallas import tpu as pltpu

F32 = jnp.float32
BF16 = jnp.bfloat16

EPS = 1e-6
ROPE_BASE = 10000.0
CHUNK = 64
CONV_WIDTH = 31
HEAD_DIM = 64
LANES = 128
SUBLANES = 8
MXU_DIM = 256

ROW_TILE = 512
FFN_TILE = 1024
FF_CHUNK = MXU_DIM
OUT_TILE = 512
CONV_HALO = 32
CONV_SUBTILES = 8
CONV_CHUNK_GAP = 1
CONV_CHUNK_SPAN = 2
SB_BLOCK = 128
SB_QROWS = 512
SB_TILE = 64
SB_TILES = SB_QROWS // SB_TILE
SB_HEADS_PER_GROUP = LANES // HEAD_DIM
SB_FIRST_BLOCKS = 2
SB_DEAD_LOG = 110.0
SIGN_BIT = 0x80000000
BF16_BITS = 0xFFFF0000
RET_BLOCK = 128
RET_STEP_BLOCKS = 4
VMEM_LIMIT = 56 * 1024 * 1024


def _cparams(n_axes):
    return pltpu.CompilerParams(dimension_semantics=("arbitrary",) * n_axes,
                                vmem_limit_bytes=VMEM_LIMIT)


def _rms(x, g):
    return (x * lax.rsqrt(jnp.mean(x * x, axis=-1, keepdims=True) + EPS)) * g


def _silu(x):
    return x * jax.nn.sigmoid(x)


def _full(shape):
    return pl.BlockSpec(shape, lambda *_: (0,) * len(shape))


def _resident(stacked, layer):
    tail = stacked.shape[1:]
    return pl.BlockSpec((None,) + tail, lambda *_: (layer,) + (0,) * len(tail),
                        pipeline_mode=pl.Buffered(1))


def _bit_zero(x):
    bits = lax.bitcast_convert_type(x, jnp.uint32)
    return lax.bitcast_convert_type((bits >> 16) >> 16, F32)


def _conv_fill(u, v_scr, starts_sequence, tile_rows):
    d_conv = v_scr.shape[1]
    if starts_sequence is None:
        v_scr[0:CONV_HALO, :] = jnp.zeros((CONV_HALO, d_conv), F32)
    else:
        v_scr[0:CONV_HALO, :] = jnp.where(starts_sequence, 0.0,
                                          v_scr[tile_rows:tile_rows + CONV_HALO, :])
    v_scr[CONV_HALO:CONV_HALO + tile_rows, :] = u[:, :d_conv] * jax.nn.sigmoid(u[:, d_conv:])
    v_scr[CONV_HALO + tile_rows:, :] = jnp.zeros((SUBLANES, d_conv), F32)


def _conv_rows(v_scr, p_scr, cw_ref, cb_ref, lg_ref, lb_ref, out_ref, r, anchor=None):
    off = CONV_HALO - (CONV_WIDTH - 1)
    sub = p_scr.shape[2] - SUBLANES
    cw = cw_ref[...] if anchor is None else cw_ref[...] + anchor
    for shift in range(SUBLANES):
        part = None
        for o in range(shift, off + CONV_WIDTH, SUBLANES):
            if o < off:
                continue
            rows = v_scr[pl.ds(r * sub + o - shift, sub + SUBLANES), :]
            term = cw[o - off:o - off + 1, :] * rows
            part = term if part is None else part + term
        p_scr[r % 2, shift] = part
    y = cb_ref[...] + p_scr[r % 2, 0, 0:sub, :]
    for shift in range(1, SUBLANES):
        y = y + p_scr[r % 2, shift, pl.ds(shift, sub), :]
    mu = jnp.mean(y, axis=-1, keepdims=True)
    yc = y - mu
    var = jnp.mean(yc * yc, axis=-1, keepdims=True)
    ln = yc * lax.rsqrt(var + EPS) * lg_ref[...] + lb_ref[...]
    out = _silu(ln)
    out_ref[r * sub:(r + 1) * sub, :] = out.astype(BF16)
    return out[0:1, :]


def _ffn_kernel(*refs, tile, n_chunks, n_out, n_mix, final, conv_tiles_per_seq):
    refs = list(refs)
    x_ref = refs.pop(0)
    mix_refs = [refs.pop(0) for _ in range(n_mix)]
    wmix_ref = refs.pop(0) if n_mix else None
    g_ref, win_ref, wout_ref = refs.pop(0), refs.pop(0), refs.pop(0)
    fg_ref = refs.pop(0) if final else None
    has_conv = conv_tiles_per_seq is not None
    if has_conv:
        u_first_ref, u_next_ref = refs.pop(0), refs.pop(0)
        conv_params = [refs.pop(0) for _ in range(4)]
    o_ref, h_scr, a_scr = refs.pop(0), refs.pop(0), refs.pop(0)
    out_tiles = [slice(m * OUT_TILE, (m + 1) * OUT_TILE) for m in range(n_out)]
    d_ff = wout_ref.shape[0]
    n_sub = CONV_SUBTILES
    if has_conv:
        v_scr, p_scr, yc_scr = refs.pop(0), refs.pop(0), refs.pop(0)
        step = pl.program_id(0)

        @pl.when(step == 0)
        def _():
            _conv_fill(u_first_ref[...], v_scr, None, tile)
            for r in range(n_sub):
                _conv_rows(v_scr, p_scr, *conv_params, yc_scr, r)

        mix_refs = [yc_scr] + mix_refs
    if mix_refs:
        for sl in out_tiles:
            y, lo = None, 0
            for y_ref in mix_refs:
                part = jnp.dot(y_ref[...], wmix_ref[lo:lo + y_ref.shape[1], sl],
                               preferred_element_type=F32)
                y = part if y is None else y + part
                lo += y_ref.shape[1]
            o_ref[:, sl] = x_ref[:, sl] + y
        x_ref = o_ref
    h_scr[...] = _rms(x_ref[...], g_ref[...]).astype(BF16)
    if has_conv:
        nxt = jnp.minimum(step + 1, pl.num_programs(0) - 1)
        _conv_fill(u_next_ref[...], v_scr, nxt % conv_tiles_per_seq == 0, tile)
    done = {}
    for j in range(n_chunks):
        sl = slice(j * FF_CHUNK, (j + 1) * FF_CHUNK)
        up_sl = slice(d_ff + j * FF_CHUNK, d_ff + (j + 1) * FF_CHUNK)
        gate = jnp.dot(h_scr[...], win_ref[:, sl], preferred_element_type=F32)
        up = jnp.dot(h_scr[...], win_ref[:, up_sl], preferred_element_type=F32)
        if j in done:
            gate = gate + done.pop(j)
        a_scr[:, sl] = (_silu(gate) * up).astype(BF16)
        if has_conv and j % CONV_CHUNK_GAP == 0 and j // CONV_CHUNK_GAP < n_sub:
            r = j // CONV_CHUNK_GAP
            out_row = _conv_rows(v_scr, p_scr, *conv_params, yc_scr, r,
                                 anchor=_bit_zero(gate[0:1, :conv_params[0].shape[1]]))
            done[j + CONV_CHUNK_SPAN] = _bit_zero(out_row)
    assert not done
    for sl in out_tiles:
        y = jnp.dot(a_scr[...], wout_ref[:, sl], preferred_element_type=F32)
        o_ref[:, sl] = x_ref[:, sl] + 0.5 * y
    if final:
        o_ref[...] = _rms(o_ref[...], fg_ref[...])


def _ffn(x, g, win, wout, layer, tile, mix=(), wmix=None, final_g=None, conv=None):
    n, d = x.shape
    d_ff = wout.shape[1]
    n_chunks = d_ff // FF_CHUNK
    n_tiles = n // tile
    final = final_g is not None
    row = lambda i: (i, 0)
    in_specs = [pl.BlockSpec((tile, d), row)]
    in_specs += [pl.BlockSpec((tile, y.shape[1]), row) for y in mix]
    args = [x, *mix]
    if mix:
        in_specs.append(_resident(wmix, layer))
        args.append(wmix)
    in_specs += [_full((1, d)), _resident(win, layer), _resident(wout, layer)]
    args += [g, win, wout]
    if final:
        in_specs.append(_full((1, d)))
        args.append(final_g)
    scratch = [pltpu.VMEM((tile, d), BF16), pltpu.VMEM((tile, d_ff), BF16)]
    conv_tiles_per_seq = None
    if conv is not None:
        u, seq, cw, cb, lg, lb = conv
        d_conv = cw.shape[1]
        assert seq % tile == 0 and tile % (CONV_SUBTILES * SUBLANES) == 0
        assert CONV_HALO >= CONV_WIDTH - 1
        conv_tiles_per_seq = seq // tile
        in_specs += [pl.BlockSpec((tile, u.shape[1]), lambda i: (0, 0), pipeline_mode=pl.Buffered(1)),
                     pl.BlockSpec((tile, u.shape[1]), lambda i: (jnp.minimum(i + 1, n_tiles - 1), 0)),
                     _full(cw.shape), _full((1, d_conv)), _full((1, d_conv)), _full((1, d_conv))]
        args += [u, u, cw, cb, lg, lb]
        scratch += [pltpu.VMEM((CONV_HALO + tile + SUBLANES, d_conv), F32),
                    pltpu.VMEM((2, SUBLANES, tile // CONV_SUBTILES + SUBLANES, d_conv), F32),
                    pltpu.VMEM((tile, d_conv), BF16)]
    return pl.pallas_call(
        functools.partial(_ffn_kernel, tile=tile, n_chunks=n_chunks, n_out=d // OUT_TILE,
                          n_mix=len(mix), final=final, conv_tiles_per_seq=conv_tiles_per_seq),
        out_shape=jax.ShapeDtypeStruct((n, d), F32),
        grid=(n_tiles,),
        in_specs=in_specs,
        out_specs=pl.BlockSpec((tile, d), row),
        scratch_shapes=scratch,
        compiler_params=_cparams(1),
        name="ffn",
    )(*args)


def _rotary(t, cos, sin_signed, first_half):
    partner = jnp.where(first_half, pltpu.roll(t, LANES - HEAD_DIM // 2, 1),
                        pltpu.roll(t, HEAD_DIM // 2, 1))
    return t * cos + partner * sin_signed


def _inproj_kernel(x_ref, g_ref, w_ref, cos_ref, sin_ref, uconv_ref, q_ref, k_ref, v_ref,
                   qkr_ref, vgr_ref, h_scr, *, d_conv2, d_sb, d_ret):
    q_scale = HEAD_DIM ** -0.5
    h_scr[...] = _rms(x_ref[...], g_ref[...]).astype(BF16)

    def proj(lo, width):
        return jnp.dot(h_scr[...], w_ref[:, lo:lo + width], preferred_element_type=F32)

    uconv_ref[...] = proj(0, d_conv2)
    o = d_conv2
    q_ref[...] = (proj(o, d_sb) * q_scale).astype(BF16)
    k_ref[...] = proj(o + d_sb, d_sb).astype(BF16)
    v_ref[...] = proj(o + 2 * d_sb, d_sb).astype(BF16)
    o += 3 * d_sb
    qk = proj(o, 2 * d_ret)
    cos = cos_ref[...]
    sin = sin_ref[...]
    lane = lax.broadcasted_iota(jnp.int32, (1, LANES), 1)
    first_half = (lane % HEAD_DIM) < (HEAD_DIM // 2)
    for c in range(2 * d_ret // LANES):
        sl = slice(c * LANES, (c + 1) * LANES)
        part = qk[:, sl] * q_scale if c < d_ret // LANES else qk[:, sl]
        qkr_ref[:, sl] = _rotary(part, cos, sin, first_half)
    vgr_ref[...] = proj(o + 2 * d_ret, 2 * d_ret)


def _inproj(x, g, w, layer, cos_tab, sin_tab, seq, d_conv2, d_sb, d_ret):
    n, d = x.shape
    tiles_per_seq = seq // FFN_TILE
    row = lambda i: (i, 0)
    pos = lambda i: (i % tiles_per_seq, 0)
    outs = [jax.ShapeDtypeStruct((n, d_conv2), F32)] + [jax.ShapeDtypeStruct((n, d_sb), BF16)] * 3 \
        + [jax.ShapeDtypeStruct((n, 2 * d_ret), F32)] * 2
    return pl.pallas_call(
        functools.partial(_inproj_kernel, d_conv2=d_conv2, d_sb=d_sb, d_ret=d_ret),
        out_shape=outs,
        grid=(n // FFN_TILE,),
        in_specs=[pl.BlockSpec((FFN_TILE, d), row), _full((1, d)), _resident(w, layer),
                  pl.BlockSpec((FFN_TILE, LANES), pos), pl.BlockSpec((FFN_TILE, LANES), pos)],
        out_specs=[pl.BlockSpec((FFN_TILE, s.shape[1]), row) for s in outs],
        scratch_shapes=[pltpu.VMEM((FFN_TILE, d), BF16)],
        compiler_params=_cparams(1),
        name="mixer_inproj",
    )(x, g, w, cos_tab, sin_tab)


def _sb_kernel(q_ref, knew_ref, vnew_ref, tri_ref, o_ref, k_scr, v_scr, qs_scr, acc_scr, carry_scr,
               *, n_pairs):
    q0 = pl.program_id(1) * SB_QROWS
    k_scr[pl.ds(pl.multiple_of(q0, SB_QROWS), SB_QROWS), :] = knew_ref[0]
    v_scr[pl.ds(pl.multiple_of(q0, SB_QROWS), SB_QROWS), :] = vnew_ref[0]
    tiles, pairs = range(SB_TILES), range(n_pairs)
    rows = SB_HEADS_PER_GROUP * SB_TILE
    lane = lax.broadcasted_iota(jnp.int32, (1, LANES), 1)
    col = lax.broadcasted_iota(jnp.int32, (rows, SB_BLOCK), 1)
    row = lax.broadcasted_iota(jnp.int32, (rows, SB_BLOCK), 0) % SB_TILE
    for p in pairs:
        q = q_ref[0, :, p * LANES:(p + 1) * LANES]
        for t in tiles:
            qt = q[t * SB_TILE:(t + 1) * SB_TILE]
            qs_scr[t, p] = jnp.concatenate(
                [jnp.where(lane // HEAD_DIM == hh, qt, jnp.zeros_like(qt))
                 for hh in range(SB_HEADS_PER_GROUP)], axis=0)

    def sweep(blocks, resume):
        depth = range(len(blocks[0]))

        def rows_of(ref, t, n, p):
            return ref[pl.ds(blocks[t][n][0], SB_BLOCK), p * LANES:(p + 1) * LANES]

        units = [(t, n) for t in tiles for n in depth]

        def logits(t, n):
            return [lax.dot_general(qs_scr[t, p], rows_of(k_scr, t, n, p),
                                    (((1,), (1,)), ((), ())), preferred_element_type=F32)
                    for p in pairs]

        def cumsums(t, n, zs):
            hls = []
            for p in pairs:
                zbits = lax.bitcast_convert_type(zs[p], jnp.uint32)
                neg_abs = lax.bitcast_convert_type(zbits | jnp.uint32(SIGN_BIT), F32)
                sp = jnp.maximum(zs[p], 0.0) + jnp.log(1.0 + jnp.exp(neg_abs))
                if blocks[t][n][1] is not None:
                    sp = jnp.where(blocks[t][n][1], sp, 0.0)
                hi = lax.bitcast_convert_type(
                    lax.bitcast_convert_type(sp, jnp.uint32) & jnp.uint32(BF16_BITS), F32)
                hls.append(jnp.concatenate([hi.astype(BF16), (sp - hi).astype(BF16)], axis=1))
            return jnp.dot(jnp.concatenate(hls, axis=0), tri_ref[...], preferred_element_type=F32)

        carry = {(t, p): carry_scr[t, p] if resume else None for t in tiles for p in pairs}
        acc = {(t, p): acc_scr[t, p] if resume else None for t in tiles for p in pairs}

        def weigh(t, n, zs, rt):
            for p in pairs:
                within = rt[p * rows:(p + 1) * rows, :SB_BLOCK]
                total = rt[p * rows:(p + 1) * rows, SB_BLOCK:]
                before = carry[t, p]
                w = jnp.exp(zs[p] - (within if before is None else within + before))
                if blocks[t][n][1] is not None:
                    w = jnp.where(blocks[t][n][1], w, 0.0)
                carry[t, p] = total if before is None else before + total
                pv = jnp.dot(w.astype(BF16), rows_of(v_scr, t, n, p), preferred_element_type=F32)
                acc[t, p] = pv if acc[t, p] is None else acc[t, p] + pv

        zs, rts = {}, {}
        for u in range(len(units) + 2):
            if u < len(units):
                zs[u] = logits(*units[u])
            if 1 <= u <= len(units):
                rts[u - 1] = cumsums(*units[u - 1], zs[u - 1])
            if u >= 2:
                weigh(*units[u - 2], zs.pop(u - 2), rts.pop(u - 2))
        lowest = None
        for t in tiles:
            for p in pairs:
                carry_scr[t, p] = carry[t, p]
                acc_scr[t, p] = acc[t, p]
                lowest = carry[t, p] if lowest is None else jnp.minimum(lowest, carry[t, p])
        return jnp.min(lowest)

    def aligned(start):
        return pl.multiple_of(start, SB_TILE)

    def first_fast():
        near_mask = col < row + (SB_BLOCK - SB_TILE)
        return sweep([[(aligned(q0 + (t + 1) * SB_TILE - (n + 1) * SB_BLOCK),
                        near_mask if n == 0 else None) for n in range(SB_FIRST_BLOCKS)]
                      for t in tiles], False)

    def first_clamped():
        blocks = []
        for t in tiles:
            limit = q0 + t * SB_TILE + row
            blocks.append([])
            for n in range(SB_FIRST_BLOCKS):
                start = aligned(jnp.maximum(q0 + (t + 1) * SB_TILE - (n + 1) * SB_BLOCK, 0))
                blocks[t].append((start, start + col < limit))
                limit = start
        return sweep(blocks, False)

    def more(st):
        reach = q0 + SB_QROWS - st[0] * SB_BLOCK
        return jnp.logical_and(reach > 0, st[1] < SB_DEAD_LOG)

    def body(st):
        blocks = []
        for t in tiles:
            limit = jnp.maximum(q0 + (t + 1) * SB_TILE - st[0] * SB_BLOCK, 0)
            start = aligned(jnp.maximum(limit - SB_BLOCK, 0))
            blocks.append([(start, start + col < limit)])
        return st[0] + 1, sweep(blocks, True)

    fits = q0 + SB_TILE - SB_FIRST_BLOCKS * SB_BLOCK >= 0
    lowest = lax.cond(fits, first_fast, first_clamped)
    lax.while_loop(more, body, (SB_FIRST_BLOCKS, lowest))
    for p in pairs:
        for t in tiles:
            acc = acc_scr[t, p]
            out = acc[0:SB_TILE]
            for hh in range(1, SB_HEADS_PER_GROUP):
                out = jnp.where(lane // HEAD_DIM == hh, acc[hh * SB_TILE:(hh + 1) * SB_TILE], out)
            o_ref[0, t * SB_TILE:(t + 1) * SB_TILE, p * LANES:(p + 1) * LANES] = out.astype(BF16)


def _stick_breaking(q, k, v, tri):
    b, s, d_sb = q.shape
    n_pairs = d_sb // LANES
    rows = SB_HEADS_PER_GROUP * SB_TILE
    blk = pl.BlockSpec((1, SB_QROWS, d_sb), lambda bi, i: (bi, i, 0))
    return pl.pallas_call(
        functools.partial(_sb_kernel, n_pairs=n_pairs),
        out_shape=jax.ShapeDtypeStruct((b, s, d_sb), BF16),
        grid=(b, s // SB_QROWS),
        in_specs=[blk, blk, blk, _full(tri.shape)],
        out_specs=blk,
        scratch_shapes=[pltpu.VMEM((s, d_sb), BF16), pltpu.VMEM((s, d_sb), BF16),
                        pltpu.VMEM((SB_TILES, n_pairs, rows, LANES), BF16),
                        pltpu.VMEM((SB_TILES, n_pairs, rows, LANES), F32),
                        pltpu.VMEM((SB_TILES, n_pairs, rows, LANES), F32)],
        compiler_params=_cparams(2),
        name="stick_breaking",
    )(q, k, v, tri)


def _ret_kernel(qk_ref, vg_ref, dmat_ref, qdec_ref, kdec_ref, cdec_ref, bd_ref, avg_ref, gain_ref,
                o_ref, state_scr, *, d_ret, n_batch):
    n_heads = d_ret // HEAD_DIM
    chains = [(b, slice(c * RET_BLOCK, (c + 1) * RET_BLOCK))
              for c in range(RET_STEP_BLOCKS) for b in range(n_batch)]
    nt = (((1,), (1,)), ((), ()))

    @pl.when(pl.program_id(0) == 0)
    def _():
        state_scr[...] = jnp.zeros_like(state_scr)

    lane = lax.broadcasted_iota(jnp.int32, (1, d_ret), 1)
    in_head = [lane // HEAD_DIM == h for h in range(n_heads)]
    q = [qk_ref[b, rows, :d_ret] for b, rows in chains]
    k = [qk_ref[b, rows, d_ret:] for b, rows in chains]
    v = [vg_ref[b, rows, :d_ret] for b, rows in chains]
    probs = []
    for i in range(len(chains)):
        kb = k[i].astype(BF16)
        probs.append(jnp.concatenate(
            [(lax.dot_general(jnp.where(in_head[h], q[i], 0.0).astype(BF16), kb, nt,
                              preferred_element_type=F32) * dmat_ref[h]).astype(BF16)
             for h in range(n_heads)], axis=1))
    y = []
    for i in range(len(chains)):
        vals = jnp.concatenate([jnp.where(in_head[h], v[i], 0.0).astype(BF16)
                                for h in range(n_heads)], axis=0)
        y.append(jnp.dot(probs[i], vals, preferred_element_type=F32))
    for i, (b, _) in enumerate(chains):
        state = state_scr[b]
        y[i] = y[i] + jnp.dot((q[i] * qdec_ref[...]).astype(BF16), state.astype(BF16),
                              preferred_element_type=F32)
        kv = lax.dot_general((k[i] * kdec_ref[...]).astype(BF16), v[i].astype(BF16),
                             (((0,), (0,)), ((), ())), preferred_element_type=F32)
        state_scr[b] = cdec_ref[...] * state + bd_ref[...] * kv

    def head_mean(t):
        hi = t.astype(BF16)
        lo = (t - hi.astype(F32)).astype(BF16)
        return jnp.dot(jnp.concatenate([hi, lo], axis=1), avg_ref[...], preferred_element_type=F32)

    yc = [y[i] - head_mean(y[i]) for i in range(len(chains))]
    var = [head_mean(yc[i] * yc[i]) for i in range(len(chains))]
    for i, (b, rows) in enumerate(chains):
        g = vg_ref[b, rows, d_ret:]
        o_ref[b, rows, :] = (_silu(g) * (yc[i] * lax.rsqrt(var[i] + EPS) * gain_ref[...])).astype(BF16)


def _retention(qk, vg, tabs, gain):
    b, s, d2 = qk.shape
    d_ret = d2 // 2
    step_rows = RET_STEP_BLOCKS * RET_BLOCK
    blk = pl.BlockSpec((b, step_rows, d2), lambda i: (0, i, 0))
    return pl.pallas_call(
        functools.partial(_ret_kernel, d_ret=d_ret, n_batch=b),
        out_shape=jax.ShapeDtypeStruct((b, s, d_ret), BF16),
        grid=(s // step_rows,),
        in_specs=[blk, blk] + [_full(t.shape) for t in tabs] + [_full((1, d_ret))],
        out_specs=pl.BlockSpec((b, step_rows, d_ret), lambda i: (0, i, 0)),
        scratch_shapes=[pltpu.VMEM((b, d_ret, d_ret), F32)],
        compiler_params=_cparams(1),
        name="retention",
    )(qk, vg, *tabs, gain)


def _rotary_tables(seq):
    half = HEAD_DIM // 2
    inv = 1.0 / (ROPE_BASE ** (jnp.arange(half, dtype=F32) / half))
    lane = jnp.arange(LANES)
    ang = jnp.arange(seq).astype(F32)[:, None] * inv[lane % half][None, :]
    sign = jnp.where((lane % HEAD_DIM) < half, -1.0, 1.0).astype(F32)
    return jnp.cos(ang), jnp.sin(ang) * sign[None, :]


def _retention_tables(n_heads):
    d_ret = n_heads * HEAD_DIM
    log_gamma = jnp.log1p(-jnp.exp2(-5.0 - jnp.arange(n_heads, dtype=F32)))
    idx = jnp.arange(RET_BLOCK, dtype=F32)
    t, s = idx[:, None], idx[None, :]
    same = (t // CHUNK) == (s // CHUNK)
    dist = jnp.where(same, jnp.abs(t - s), t - s)
    seen = same | ((s // CHUNK) < (t // CHUNK))
    dmat = jnp.where(seen[None], jnp.exp(log_gamma[:, None, None] * dist[None]), 0.0)
    lane_gamma = jnp.repeat(log_gamma, HEAD_DIM)
    qdec = jnp.exp(lane_gamma[None, :] * (idx + 1.0)[:, None])
    kdec = jnp.exp(lane_gamma[None, :] * (RET_BLOCK - 1.0 - idx)[:, None])
    head = jnp.arange(d_ret) // HEAD_DIM
    bd = (head[:, None] == head[None, :]).astype(F32)
    cdec = bd * jnp.exp(lane_gamma * RET_BLOCK)[:, None]
    avg = jnp.concatenate([bd, bd], axis=0).astype(BF16) * (1.0 / HEAD_DIM)
    return dmat, qdec, kdec, cdec, bd, avg.astype(BF16)


def _tri_table():
    r = np.arange(2 * SB_BLOCK)[:, None] % SB_BLOCK
    c = np.arange(2 * SB_BLOCK)[None, :]
    return jnp.asarray((c >= SB_BLOCK) | (r >= c), dtype=BF16)


def kernel(x, ffn1_norm, ffn1_w_in, ffn1_w_out, mix_norm, mix_w_in, conv_w, conv_b, conv_ln_g,
           conv_ln_b, ret_norm_g, mix_w_out, ffn2_norm, ffn2_w_in, ffn2_w_out, final_norm):
    b, s, d = x.shape
    depth = ffn1_norm.shape[0]
    d_conv = conv_w.shape[2]
    d_ret = ret_norm_g.shape[1]
    d_sb = d - d_conv - d_ret
    assert s % ROW_TILE == 0 and s % FFN_TILE == 0 and s % (RET_STEP_BLOCKS * RET_BLOCK) == 0 and s % SB_QROWS == 0
    assert ffn1_w_out.shape[1] % FF_CHUNK == 0 and d % OUT_TILE == 0

    cos_tab, sin_tab = _rotary_tables(s)
    ret_tabs = _retention_tables(d_ret // HEAD_DIM)
    tri = _tri_table()
    win1, wout1 = ffn1_w_in.astype(BF16), ffn1_w_out.astype(BF16)
    win2, wout2 = ffn2_w_in.astype(BF16), ffn2_w_out.astype(BF16)
    w_mix, w_mix_out = mix_w_in.astype(BF16), mix_w_out.astype(BF16)

    row = lambda g: g.reshape(1, -1)
    xf = x.reshape(b * s, d)
    for l in range(depth):
        xf = _ffn(xf, row(ffn1_norm[l]), win1, wout1, l, FFN_TILE)
        uconv, q, k, v, qkr, vgr = _inproj(xf, row(mix_norm[l]), w_mix, l, cos_tab, sin_tab, s,
                                           2 * d_conv, d_sb, d_ret)
        seq3 = lambda t: t.reshape(b, s, t.shape[1])
        y_sb = _stick_breaking(seq3(q), seq3(k), seq3(v), tri)
        y_ret = _retention(seq3(qkr), seq3(vgr), ret_tabs, row(ret_norm_g[l]))
        flat = lambda t: t.reshape(b * s, t.shape[2])
        last = l == depth - 1
        xf = _ffn(xf, row(ffn2_norm[l]), win2, wout2, l, ROW_TILE,
                  mix=(flat(y_sb), flat(y_ret)), wmix=w_mix_out,
                  final_g=row(final_norm) if last else None,
                  conv=(uconv, s, conv_w[l], row(conv_b[l]), row(conv_ln_g[l]), row(conv_ln_b[l])))
    return xf.reshape(b, s, d)
```

```python
import functools

import jax
import jax.numpy as jnp
import numpy as np
from jax import lax
from jax.experimental import pallas as pl
from jax.experimental.pallas import tpu as pltpu

F32 = jnp.float32
BF16 = jnp.bfloat16

EPS = 1e-6
ROPE_BASE = 10000.0
CHUNK = 64
CONV_WIDTH = 31
HEAD_DIM = 64
LANES = 128
SUBLANES = 8
MXU_DIM = 256

ROW_TILE = 512
FFN_TILE = 1024
FF_CHUNK = MXU_DIM
OUT_TILE = 512
CONV_HALO = 32
CONV_SUBTILES = 8
CONV_CHUNK_GAP = 1
CONV_CHUNK_SPAN = 2
SB_BLOCK = 128
SB_QROWS = 512
SB_TILE = 64
SB_TILES = SB_QROWS // SB_TILE
SB_HEADS_PER_GROUP = LANES // HEAD_DIM
SB_FIRST_BLOCKS = 2
SB_DEAD_LOG = 110.0
SIGN_BIT = 0x80000000
BF16_BITS = 0xFFFF0000
RET_BLOCK = 128
RET_STEP_BLOCKS = 4
VMEM_LIMIT = 56 * 1024 * 1024


def _cparams(n_axes):
    return pltpu.CompilerParams(dimension_semantics=("arbitrary",) * n_axes,
                                vmem_limit_bytes=VMEM_LIMIT)


def _rms(x, g):
    return (x * lax.rsqrt(jnp.mean(x * x, axis=-1, keepdims=True) + EPS)) * g


def _silu(x):
    return x * jax.nn.sigmoid(x)


def _full(shape):
    return pl.BlockSpec(shape, lambda *_: (0,) * len(shape))


def _resident(stacked, layer):
    tail = stacked.shape[1:]
    return pl.BlockSpec((None,) + tail, lambda *_: (layer,) + (0,) * len(tail),
                        pipeline_mode=pl.Buffered(1))


def _bit_zero(x):
    bits = lax.bitcast_convert_type(x, jnp.uint32)
    return lax.bitcast_convert_type((bits >> 16) >> 16, F32)


def _conv_fill(u, v_scr, starts_sequence, tile_rows):
    d_conv = v_scr.shape[1]
    if starts_sequence is None:
        v_scr[0:CONV_HALO, :] = jnp.zeros((CONV_HALO, d_conv), F32)
    else:
        v_scr[0:CONV_HALO, :] = jnp.where(starts_sequence, 0.0,
                                          v_scr[tile_rows:tile_rows + CONV_HALO, :])
    v_scr[CONV_HALO:CONV_HALO + tile_rows, :] = u[:, :d_conv] * jax.nn.sigmoid(u[:, d_conv:])
    v_scr[CONV_HALO + tile_rows:, :] = jnp.zeros((SUBLANES, d_conv), F32)


def _conv_rows(v_scr, p_scr, cw_ref, cb_ref, lg_ref, lb_ref, out_ref, r, anchor=None):
    off = CONV_HALO - (CONV_WIDTH - 1)
    sub = p_scr.shape[2] - SUBLANES
    cw = cw_ref[...] if anchor is None else cw_ref[...] + anchor
    for shift in range(SUBLANES):
        part = None
        for o in range(shift, off + CONV_WIDTH, SUBLANES):
            if o < off:
                continue
            rows = v_scr[pl.ds(r * sub + o - shift, sub + SUBLANES), :]
            term = cw[o - off:o - off + 1, :] * rows
            part = term if part is None else part + term
        p_scr[r % 2, shift] = part
    y = cb_ref[...] + p_scr[r % 2, 0, 0:sub, :]
    for shift in range(1, SUBLANES):
        y = y + p_scr[r % 2, shift, pl.ds(shift, sub), :]
    mu = jnp.mean(y, axis=-1, keepdims=True)
    yc = y - mu
    var = jnp.mean(yc * yc, axis=-1, keepdims=True)
    ln = yc * lax.rsqrt(var + EPS) * lg_ref[...] + lb_ref[...]
    out = _silu(ln)
    out_ref[r * sub:(r + 1) * sub, :] = out.astype(BF16)
    return out[0:1, :]


def _ffn_kernel(*refs, tile, n_chunks, n_out, n_mix, final, conv_tiles_per_seq, n_cast):
    refs = list(refs)
    x_ref = refs.pop(0)
    mix_refs = [refs.pop(0) for _ in range(n_mix)]
    wmix_ref = refs.pop(0) if n_mix else None
    g_ref, win_ref, wout_ref = refs.pop(0), refs.pop(0), refs.pop(0)
    fg_ref = refs.pop(0) if final else None
    has_conv = conv_tiles_per_seq is not None
    if has_conv:
        u_first_ref, u_next_ref = refs.pop(0), refs.pop(0)
        conv_params = [refs.pop(0) for _ in range(4)]
    cast_in = [refs.pop(0) for _ in range(n_cast)]
    o_ref = refs.pop(0)
    cast_out = [refs.pop(0) for _ in range(n_cast)]
    h_scr, a_scr = refs.pop(0), refs.pop(0)
    for src, dst in zip(cast_in, cast_out):
        dst[...] = src[...].astype(BF16)
    out_tiles = [slice(m * OUT_TILE, (m + 1) * OUT_TILE) for m in range(n_out)]
    d_ff = wout_ref.shape[0]
    n_sub = CONV_SUBTILES
    if has_conv:
        v_scr, p_scr, yc_scr = refs.pop(0), refs.pop(0), refs.pop(0)
        step = pl.program_id(0)

        @pl.when(step == 0)
        def _():
            _conv_fill(u_first_ref[...], v_scr, None, tile)
            for r in range(n_sub):
                _conv_rows(v_scr, p_scr, *conv_params, yc_scr, r)

        mix_refs = [yc_scr] + mix_refs
    if mix_refs:
        for sl in out_tiles:
            y, lo = None, 0
            for y_ref in mix_refs:
                part = jnp.dot(y_ref[...], wmix_ref[lo:lo + y_ref.shape[1], sl],
                               preferred_element_type=F32)
                y = part if y is None else y + part
                lo += y_ref.shape[1]
            o_ref[:, sl] = x_ref[:, sl] + y
        x_ref = o_ref
    h_scr[...] = _rms(x_ref[...], g_ref[...]).astype(BF16)
    if has_conv:
        nxt = jnp.minimum(step + 1, pl.num_programs(0) - 1)
        _conv_fill(u_next_ref[...], v_scr, nxt % conv_tiles_per_seq == 0, tile)
    done = {}
    for j in range(n_chunks):
        sl = slice(j * FF_CHUNK, (j + 1) * FF_CHUNK)
        up_sl = slice(d_ff + j * FF_CHUNK, d_ff + (j + 1) * FF_CHUNK)
        gate = jnp.dot(h_scr[...], win_ref[:, sl], preferred_element_type=F32)
        up = jnp.dot(h_scr[...], win_ref[:, up_sl], preferred_element_type=F32)
        if j in done:
            gate = gate + done.pop(j)
        a_scr[:, sl] = (_silu(gate) * up).astype(BF16)
        if has_conv and j % CONV_CHUNK_GAP == 0 and j // CONV_CHUNK_GAP < n_sub:
            r = j // CONV_CHUNK_GAP
            out_row = _conv_rows(v_scr, p_scr, *conv_params, yc_scr, r,
                                 anchor=_bit_zero(gate[0:1, :conv_params[0].shape[1]]))
            done[j + CONV_CHUNK_SPAN] = _bit_zero(out_row)
    assert not done
    for sl in out_tiles:
        y = jnp.dot(a_scr[...], wout_ref[:, sl], preferred_element_type=F32)
        o_ref[:, sl] = x_ref[:, sl] + 0.5 * y
    if final:
        o_ref[...] = _rms(o_ref[...], fg_ref[...])


def _ffn(x, g, win, wout, layer, tile, mix=(), wmix=None, final_g=None, conv=None, cast=()):
    n, d = x.shape
    d_ff = wout.shape[1]
    n_chunks = d_ff // FF_CHUNK
    n_tiles = n // tile
    final = final_g is not None
    row = lambda i: (i, 0)
    in_specs = [pl.BlockSpec((tile, d), row)]
    in_specs += [pl.BlockSpec((tile, y.shape[1]), row) for y in mix]
    args = [x, *mix]
    if mix:
        in_specs.append(_resident(wmix, layer))
        args.append(wmix)
    in_specs += [_full((1, d)), _resident(win, layer), _resident(wout, layer)]
    args += [g, win, wout]
    if final:
        in_specs.append(_full((1, d)))
        args.append(final_g)
    scratch = [pltpu.VMEM((tile, d), BF16), pltpu.VMEM((tile, d_ff), BF16)]
    conv_tiles_per_seq = None
    if conv is not None:
        u, seq, cw, cb, lg, lb = conv
        d_conv = cw.shape[1]
        assert seq % tile == 0 and tile % (CONV_SUBTILES * SUBLANES) == 0
        assert CONV_HALO >= CONV_WIDTH - 1
        conv_tiles_per_seq = seq // tile
        in_specs += [pl.BlockSpec((tile, u.shape[1]), lambda i: (0, 0), pipeline_mode=pl.Buffered(1)),
                     pl.BlockSpec((tile, u.shape[1]), lambda i: (jnp.minimum(i + 1, n_tiles - 1), 0)),
                     _full(cw.shape), _full((1, d_conv)), _full((1, d_conv)), _full((1, d_conv))]
        args += [u, u, cw, cb, lg, lb]
        scratch += [pltpu.VMEM((CONV_HALO + tile + SUBLANES, d_conv), F32),
                    pltpu.VMEM((2, SUBLANES, tile // CONV_SUBTILES + SUBLANES, d_conv), F32),
                    pltpu.VMEM((tile, d_conv), BF16)]
    out_shape, out_specs = jax.ShapeDtypeStruct((n, d), F32), pl.BlockSpec((tile, d), row)
    if cast:
        assert all(w.shape[0] % (n_tiles * 2 * SUBLANES) == 0 for w in cast)
        slices = [pl.BlockSpec((w.shape[0] // n_tiles, w.shape[1]), row) for w in cast]
        in_specs += slices
        args += list(cast)
        out_shape = [out_shape] + [jax.ShapeDtypeStruct(w.shape, BF16) for w in cast]
        out_specs = [out_specs] + slices
    return pl.pallas_call(
        functools.partial(_ffn_kernel, tile=tile, n_chunks=n_chunks, n_out=d // OUT_TILE,
                          n_mix=len(mix), final=final, conv_tiles_per_seq=conv_tiles_per_seq,
                          n_cast=len(cast)),
        out_shape=out_shape,
        grid=(n_tiles,),
        in_specs=in_specs,
        out_specs=out_specs,
        scratch_shapes=scratch,
        compiler_params=_cparams(1),
        name="ffn",
    )(*args)


def _rotary(t, cos, sin_signed, first_half):
    partner = jnp.where(first_half, pltpu.roll(t, LANES - HEAD_DIM // 2, 1),
                        pltpu.roll(t, HEAD_DIM // 2, 1))
    return t * cos + partner * sin_signed


def _inproj_kernel(x_ref, g_ref, w_ref, cos_ref, sin_ref, uconv_ref, q_ref, k_ref, v_ref,
                   qkr_ref, vgr_ref, h_scr, *, d_conv2, d_sb, d_ret):
    q_scale = HEAD_DIM ** -0.5
    h_scr[...] = _rms(x_ref[...], g_ref[...]).astype(BF16)

    def proj(lo, width):
        return jnp.dot(h_scr[...], w_ref[:, lo:lo + width], preferred_element_type=F32)

    uconv_ref[...] = proj(0, d_conv2)
    o = d_conv2
    q_ref[...] = (proj(o, d_sb) * q_scale).astype(BF16)
    k_ref[...] = proj(o + d_sb, d_sb).astype(BF16)
    v_ref[...] = proj(o + 2 * d_sb, d_sb).astype(BF16)
    o += 3 * d_sb
    qk = proj(o, 2 * d_ret)
    cos = cos_ref[...]
    sin = sin_ref[...]
    lane = lax.broadcasted_iota(jnp.int32, (1, LANES), 1)
    first_half = (lane % HEAD_DIM) < (HEAD_DIM // 2)
    for c in range(2 * d_ret // LANES):
        sl = slice(c * LANES, (c + 1) * LANES)
        part = qk[:, sl] * q_scale if c < d_ret // LANES else qk[:, sl]
        qkr_ref[:, sl] = _rotary(part, cos, sin, first_half)
    vgr_ref[...] = proj(o + 2 * d_ret, 2 * d_ret)


def _inproj(x, g, w, layer, cos_tab, sin_tab, seq, d_conv2, d_sb, d_ret):
    n, d = x.shape
    tiles_per_seq = seq // FFN_TILE
    row = lambda i: (i, 0)
    pos = lambda i: (i % tiles_per_seq, 0)
    outs = [jax.ShapeDtypeStruct((n, d_conv2), F32)] + [jax.ShapeDtypeStruct((n, d_sb), BF16)] * 3 \
        + [jax.ShapeDtypeStruct((n, 2 * d_ret), F32)] * 2
    return pl.pallas_call(
        functools.partial(_inproj_kernel, d_conv2=d_conv2, d_sb=d_sb, d_ret=d_ret),
        out_shape=outs,
        grid=(n // FFN_TILE,),
        in_specs=[pl.BlockSpec((FFN_TILE, d), row), _full((1, d)), _resident(w, layer),
                  pl.BlockSpec((FFN_TILE, LANES), pos), pl.BlockSpec((FFN_TILE, LANES), pos)],
        out_specs=[pl.BlockSpec((FFN_TILE, s.shape[1]), row) for s in outs],
        scratch_shapes=[pltpu.VMEM((FFN_TILE, d), BF16)],
        compiler_params=_cparams(1),
        name="mixer_inproj",
    )(x, g, w, cos_tab, sin_tab)


def _sb_kernel(q_ref, knew_ref, vnew_ref, tri_ref, o_ref, k_scr, v_scr, qs_scr, acc_scr, carry_scr,
               *, n_pairs):
    q0 = pl.program_id(1) * SB_QROWS
    k_scr[pl.ds(pl.multiple_of(q0, SB_QROWS), SB_QROWS), :] = knew_ref[0]
    v_scr[pl.ds(pl.multiple_of(q0, SB_QROWS), SB_QROWS), :] = vnew_ref[0]
    tiles, pairs = range(SB_TILES), range(n_pairs)
    rows = SB_HEADS_PER_GROUP * SB_TILE
    lane = lax.broadcasted_iota(jnp.int32, (1, LANES), 1)
    col = lax.broadcasted_iota(jnp.int32, (rows, SB_BLOCK), 1)
    row = lax.broadcasted_iota(jnp.int32, (rows, SB_BLOCK), 0) % SB_TILE
    for p in pairs:
        q = q_ref[0, :, p * LANES:(p + 1) * LANES]
        for t in tiles:
            qt = q[t * SB_TILE:(t + 1) * SB_TILE]
            qs_scr[t, p] = jnp.concatenate(
                [jnp.where(lane // HEAD_DIM == hh, qt, jnp.zeros_like(qt))
                 for hh in range(SB_HEADS_PER_GROUP)], axis=0)

    def sweep(blocks, resume):
        depth = range(len(blocks[0]))

        def rows_of(ref, t, n, p):
            return ref[pl.ds(blocks[t][n][0], SB_BLOCK), p * LANES:(p + 1) * LANES]

        units = [(t, n) for t in tiles for n in depth]

        def logits(t, n):
            return [lax.dot_general(qs_scr[t, p], rows_of(k_scr, t, n, p),
                                    (((1,), (1,)), ((), ())), preferred_element_type=F32)
                    for p in pairs]

        def cumsums(t, n, zs):
            hls = []
            for p in pairs:
                zbits = lax.bitcast_convert_type(zs[p], jnp.uint32)
                neg_abs = lax.bitcast_convert_type(zbits | jnp.uint32(SIGN_BIT), F32)
                sp = jnp.maximum(zs[p], 0.0) + jnp.log(1.0 + jnp.exp(neg_abs))
                if blocks[t][n][1] is not None:
                    sp = jnp.where(blocks[t][n][1], sp, 0.0)
                hi = lax.bitcast_convert_type(
                    lax.bitcast_convert_type(sp, jnp.uint32) & jnp.uint32(BF16_BITS), F32)
                hls.append(jnp.concatenate([hi.astype(BF16), (sp - hi).astype(BF16)], axis=1))
            return jnp.dot(jnp.concatenate(hls, axis=0), tri_ref[...], preferred_element_type=F32)

        carry = {(t, p): carry_scr[t, p] if resume else None for t in tiles for p in pairs}
        acc = {(t, p): acc_scr[t, p] if resume else None for t in tiles for p in pairs}

        def weigh(t, n, zs, rt):
            for p in pairs:
                within = rt[p * rows:(p + 1) * rows, :SB_BLOCK]
                total = rt[p * rows:(p + 1) * rows, SB_BLOCK:]
                before = carry[t, p]
                w = jnp.exp(zs[p] - (within if before is None else within + before))
                if blocks[t][n][1] is not None:
                    w = jnp.where(blocks[t][n][1], w, 0.0)
                carry[t, p] = total if before is None else before + total
                pv = jnp.dot(w.astype(BF16), rows_of(v_scr, t, n, p), preferred_element_type=F32)
                acc[t, p] = pv if acc[t, p] is None else acc[t, p] + pv

        zs, rts = {}, {}
        for u in range(len(units) + 2):
            if u < len(units):
                zs[u] = logits(*units[u])
            if 1 <= u <= len(units):
                rts[u - 1] = cumsums(*units[u - 1], zs[u - 1])
            if u >= 2:
                weigh(*units[u - 2], zs.pop(u - 2), rts.pop(u - 2))
        lowest = None
        for t in tiles:
            for p in pairs:
                carry_scr[t, p] = carry[t, p]
                acc_scr[t, p] = acc[t, p]
                lowest = carry[t, p] if lowest is None else jnp.minimum(lowest, carry[t, p])
        return jnp.min(lowest)

    def aligned(start):
        return pl.multiple_of(start, SB_TILE)

    def first_fast():
        near_mask = col < row + (SB_BLOCK - SB_TILE)
        return sweep([[(aligned(q0 + (t + 1) * SB_TILE - (n + 1) * SB_BLOCK),
                        near_mask if n == 0 else None) for n in range(SB_FIRST_BLOCKS)]
                      for t in tiles], False)

    def first_clamped():
        blocks = []
        for t in tiles:
            limit = q0 + t * SB_TILE + row
            blocks.append([])
            for n in range(SB_FIRST_BLOCKS):
                start = aligned(jnp.maximum(q0 + (t + 1) * SB_TILE - (n + 1) * SB_BLOCK, 0))
                blocks[t].append((start, start + col < limit))
                limit = start
        return sweep(blocks, False)

    def more(st):
        reach = q0 + SB_QROWS - st[0] * SB_BLOCK
        return jnp.logical_and(reach > 0, st[1] < SB_DEAD_LOG)

    def body(st):
        blocks = []
        for t in tiles:
            limit = jnp.maximum(q0 + (t + 1) * SB_TILE - st[0] * SB_BLOCK, 0)
            start = aligned(jnp.maximum(limit - SB_BLOCK, 0))
            blocks.append([(start, start + col < limit)])
        return st[0] + 1, sweep(blocks, True)

    fits = q0 + SB_TILE - SB_FIRST_BLOCKS * SB_BLOCK >= 0
    lowest = lax.cond(fits, first_fast, first_clamped)
    lax.while_loop(more, body, (SB_FIRST_BLOCKS, lowest))
    for p in pairs:
        for t in tiles:
            acc = acc_scr[t, p]
            out = acc[0:SB_TILE]
            for hh in range(1, SB_HEADS_PER_GROUP):
                out = jnp.where(lane // HEAD_DIM == hh, acc[hh * SB_TILE:(hh + 1) * SB_TILE], out)
            o_ref[0, t * SB_TILE:(t + 1) * SB_TILE, p * LANES:(p + 1) * LANES] = out.astype(BF16)


def _stick_breaking(q, k, v, tri):
    b, s, d_sb = q.shape
    n_pairs = d_sb // LANES
    rows = SB_HEADS_PER_GROUP * SB_TILE
    blk = pl.BlockSpec((1, SB_QROWS, d_sb), lambda bi, i: (bi, i, 0))
    return pl.pallas_call(
        functools.partial(_sb_kernel, n_pairs=n_pairs),
        out_shape=jax.ShapeDtypeStruct((b, s, d_sb), BF16),
        grid=(b, s // SB_QROWS),
        in_specs=[blk, blk, blk, _full(tri.shape)],
        out_specs=blk,
        scratch_shapes=[pltpu.VMEM((s, d_sb), BF16), pltpu.VMEM((s, d_sb), BF16),
                        pltpu.VMEM((SB_TILES, n_pairs, rows, LANES), BF16),
                        pltpu.VMEM((SB_TILES, n_pairs, rows, LANES), F32),
                        pltpu.VMEM((SB_TILES, n_pairs, rows, LANES), F32)],
        compiler_params=_cparams(2),
        name="stick_breaking",
    )(q, k, v, tri)


def _ret_kernel(qk_ref, vg_ref, dmat_ref, qdec_ref, kdec_ref, cdec_ref, bd_ref, avg_ref, gain_ref,
                o_ref, state_scr, *, d_ret, n_batch):
    n_heads = d_ret // HEAD_DIM
    chains = [(b, slice(c * RET_BLOCK, (c + 1) * RET_BLOCK))
              for c in range(RET_STEP_BLOCKS) for b in range(n_batch)]
    nt = (((1,), (1,)), ((), ()))

    @pl.when(pl.program_id(0) == 0)
    def _():
        state_scr[...] = jnp.zeros_like(state_scr)

    lane = lax.broadcasted_iota(jnp.int32, (1, d_ret), 1)
    in_head = [lane // HEAD_DIM == h for h in range(n_heads)]
    q = [qk_ref[b, rows, :d_ret] for b, rows in chains]
    k = [qk_ref[b, rows, d_ret:] for b, rows in chains]
    v = [vg_ref[b, rows, :d_ret] for b, rows in chains]
    probs = []
    for i in range(len(chains)):
        kb = k[i].astype(BF16)
        probs.append(jnp.concatenate(
            [(lax.dot_general(jnp.where(in_head[h], q[i], 0.0).astype(BF16), kb, nt,
                              preferred_element_type=F32) * dmat_ref[h]).astype(BF16)
             for h in range(n_heads)], axis=1))
    y = []
    for i in range(len(chains)):
        vals = jnp.concatenate([jnp.where(in_head[h], v[i], 0.0).astype(BF16)
                                for h in range(n_heads)], axis=0)
        y.append(jnp.dot(probs[i], vals, preferred_element_type=F32))
    for i, (b, _) in enumerate(chains):
        state = state_scr[b]
        y[i] = y[i] + jnp.dot((q[i] * qdec_ref[...]).astype(BF16), state.astype(BF16),
                              preferred_element_type=F32)
        kv = lax.dot_general((k[i] * kdec_ref[...]).astype(BF16), v[i].astype(BF16),
                             (((0,), (0,)), ((), ())), preferred_element_type=F32)
        state_scr[b] = cdec_ref[...] * state + bd_ref[...] * kv

    def head_mean(t):
        hi = t.astype(BF16)
        lo = (t - hi.astype(F32)).astype(BF16)
        return jnp.dot(jnp.concatenate([hi, lo], axis=1), avg_ref[...], preferred_element_type=F32)

    yc = [y[i] - head_mean(y[i]) for i in range(len(chains))]
    var = [head_mean(yc[i] * yc[i]) for i in range(len(chains))]
    for i, (b, rows) in enumerate(chains):
        g = vg_ref[b, rows, d_ret:]
        o_ref[b, rows, :] = (_silu(g) * (yc[i] * lax.rsqrt(var[i] + EPS) * gain_ref[...])).astype(BF16)


def _retention(qk, vg, tabs, gain):
    b, s, d2 = qk.shape
    d_ret = d2 // 2
    step_rows = RET_STEP_BLOCKS * RET_BLOCK
    blk = pl.BlockSpec((b, step_rows, d2), lambda i: (0, i, 0))
    return pl.pallas_call(
        functools.partial(_ret_kernel, d_ret=d_ret, n_batch=b),
        out_shape=jax.ShapeDtypeStruct((b, s, d_ret), BF16),
        grid=(s // step_rows,),
        in_specs=[blk, blk] + [_full(t.shape) for t in tabs] + [_full((1, d_ret))],
        out_specs=pl.BlockSpec((b, step_rows, d_ret), lambda i: (0, i, 0)),
        scratch_shapes=[pltpu.VMEM((b, d_ret, d_ret), F32)],
        compiler_params=_cparams(1),
        name="retention",
    )(qk, vg, *tabs, gain)


def _rotary_tables(seq):
    half = HEAD_DIM // 2
    inv = 1.0 / (ROPE_BASE ** (jnp.arange(half, dtype=F32) / half))
    lane = jnp.arange(LANES)
    ang = jnp.arange(seq).astype(F32)[:, None] * inv[lane % half][None, :]
    sign = jnp.where((lane % HEAD_DIM) < half, -1.0, 1.0).astype(F32)
    return jnp.cos(ang), jnp.sin(ang) * sign[None, :]


def _retention_tables(n_heads):
    d_ret = n_heads * HEAD_DIM
    log_gamma = jnp.log1p(-jnp.exp2(-5.0 - jnp.arange(n_heads, dtype=F32)))
    idx = jnp.arange(RET_BLOCK, dtype=F32)
    t, s = idx[:, None], idx[None, :]
    same = (t // CHUNK) == (s // CHUNK)
    dist = jnp.where(same, jnp.abs(t - s), t - s)
    seen = same | ((s // CHUNK) < (t // CHUNK))
    dmat = jnp.where(seen[None], jnp.exp(log_gamma[:, None, None] * dist[None]), 0.0)
    lane_gamma = jnp.repeat(log_gamma, HEAD_DIM)
    qdec = jnp.exp(lane_gamma[None, :] * (idx + 1.0)[:, None])
    kdec = jnp.exp(lane_gamma[None, :] * (RET_BLOCK - 1.0 - idx)[:, None])
    head = jnp.arange(d_ret) // HEAD_DIM
    bd = (head[:, None] == head[None, :]).astype(F32)
    cdec = bd * jnp.exp(lane_gamma * RET_BLOCK)[:, None]
    avg = jnp.concatenate([bd, bd], axis=0).astype(BF16) * (1.0 / HEAD_DIM)
    return dmat, qdec, kdec, cdec, bd, avg.astype(BF16)


def _tri_table():
    r = np.arange(2 * SB_BLOCK)[:, None] % SB_BLOCK
    c = np.arange(2 * SB_BLOCK)[None, :]
    return jnp.asarray((c >= SB_BLOCK) | (r >= c), dtype=BF16)


def kernel(x, ffn1_norm, ffn1_w_in, ffn1_w_out, mix_norm, mix_w_in, conv_w, conv_b, conv_ln_g,
           conv_ln_b, ret_norm_g, mix_w_out, ffn2_norm, ffn2_w_in, ffn2_w_out, final_norm):
    b, s, d = x.shape
    depth = ffn1_norm.shape[0]
    d_conv = conv_w.shape[2]
    d_ret = ret_norm_g.shape[1]
    d_sb = d - d_conv - d_ret
    assert s % ROW_TILE == 0 and s % FFN_TILE == 0 and s % (RET_STEP_BLOCKS * RET_BLOCK) == 0 and s % SB_QROWS == 0
    assert ffn1_w_out.shape[1] % FF_CHUNK == 0 and d % OUT_TILE == 0

    cos_tab, sin_tab = _rotary_tables(s)
    ret_tabs = _retention_tables(d_ret // HEAD_DIM)
    tri = _tri_table()
    win1, wout1 = ffn1_w_in.astype(BF16), ffn1_w_out.astype(BF16)
    later = (mix_w_in, mix_w_out, ffn2_w_in, ffn2_w_out)

    row = lambda g: g.reshape(1, -1)
    xf = x.reshape(b * s, d)
    for l in range(depth):
        if l == 0:
            xf, *rounded = _ffn(xf, row(ffn1_norm[l]), win1, wout1, l, FFN_TILE,
                                cast=[w.reshape(-1, w.shape[-1]) for w in later])
            w_mix, w_mix_out, win2, wout2 = [r.reshape(w.shape) for r, w in zip(rounded, later)]
        else:
            xf = _ffn(xf, row(ffn1_norm[l]), win1, wout1, l, FFN_TILE)
        uconv, q, k, v, qkr, vgr = _inproj(xf, row(mix_norm[l]), w_mix, l, cos_tab, sin_tab, s,
                                           2 * d_conv, d_sb, d_ret)
        seq3 = lambda t: t.reshape(b, s, t.shape[1])
        y_sb = _stick_breaking(seq3(q), seq3(k), seq3(v), tri)
        y_ret = _retention(seq3(qkr), seq3(vgr), ret_tabs, row(ret_norm_g[l]))
        flat = lambda t: t.reshape(b * s, t.shape[2])
        last = l == depth - 1
        xf = _ffn(xf, row(ffn2_norm[l]), win2, wout2, l, ROW_TILE,
                  mix=(flat(y_sb), flat(y_ret)), wmix=w_mix_out,
                  final_g=row(final_norm) if last else None,
                  conv=(uconv, s, conv_w[l], row(conv_b[l]), row(conv_ln_g[l]), row(conv_ln_b[l])))
    return xf.reshape(b, s, d)
```

```python
import functools

import jax
import jax.numpy as jnp
import numpy as np
from jax import lax
from jax.experimental import pallas as pl
from jax.experimental.pallas import tpu as pltpu

F32 = jnp.float32
BF16 = jnp.bfloat16

EPS = 1e-6
ROPE_BASE = 10000.0
CHUNK = 64
CONV_WIDTH = 31
HEAD_DIM = 64
LANES = 128
SUBLANES = 8
MXU_DIM = 256

ROW_TILE = 512
FFN_TILE = 1024
FF_CHUNK = MXU_DIM
OUT_TILE = 512
CONV_HALO = 32
CONV_SUBTILES = 8
CONV_CHUNK_GAP = 1
CONV_CHUNK_SPAN = 2
SB_BLOCK = 128
SB_QROWS = 512
SB_TILE = 64
SB_TILES = SB_QROWS // SB_TILE
SB_HEADS_PER_GROUP = LANES // HEAD_DIM
SB_FIRST_BLOCKS = 2
SB_DEAD_LOG = 110.0
SIGN_BIT = 0x80000000
BF16_BITS = 0xFFFF0000
RET_BLOCK = 128
RET_STEP_BLOCKS = 4
VMEM_LIMIT = 56 * 1024 * 1024


def _cparams(n_axes):
    return pltpu.CompilerParams(dimension_semantics=("arbitrary",) * n_axes,
                                vmem_limit_bytes=VMEM_LIMIT)


def _rms(x, g):
    return (x * lax.rsqrt(jnp.mean(x * x, axis=-1, keepdims=True) + EPS)) * g


def _silu(x):
    return x * jax.nn.sigmoid(x)


def _full(shape):
    return pl.BlockSpec(shape, lambda *_: (0,) * len(shape))


def _resident(stacked, layer):
    tail = stacked.shape[1:]
    return pl.BlockSpec((None,) + tail, lambda *_: (layer,) + (0,) * len(tail),
                        pipeline_mode=pl.Buffered(1))


def _bit_zero(x):
    bits = lax.bitcast_convert_type(x, jnp.uint32)
    return lax.bitcast_convert_type((bits >> 16) >> 16, F32)


def _conv_fill(u, v_scr, starts_sequence, tile_rows):
    d_conv = v_scr.shape[1]
    if starts_sequence is None:
        v_scr[0:CONV_HALO, :] = jnp.zeros((CONV_HALO, d_conv), F32)
    else:
        v_scr[0:CONV_HALO, :] = jnp.where(starts_sequence, 0.0,
                                          v_scr[tile_rows:tile_rows + CONV_HALO, :])
    v_scr[CONV_HALO:CONV_HALO + tile_rows, :] = u[:, :d_conv] * jax.nn.sigmoid(u[:, d_conv:])
    v_scr[CONV_HALO + tile_rows:, :] = jnp.zeros((SUBLANES, d_conv), F32)


def _conv_rows(v_scr, p_scr, cw_ref, cb_ref, lg_ref, lb_ref, out_ref, r, anchor=None):
    off = CONV_HALO - (CONV_WIDTH - 1)
    sub = p_scr.shape[2] - SUBLANES
    cw = cw_ref[...] if anchor is None else cw_ref[...] + anchor
    for shift in range(SUBLANES):
        part = None
        for o in range(shift, off + CONV_WIDTH, SUBLANES):
            if o < off:
                continue
            rows = v_scr[pl.ds(r * sub + o - shift, sub + SUBLANES), :]
            term = cw[o - off:o - off + 1, :] * rows
            part = term if part is None else part + term
        p_scr[r % 2, shift] = part
    y = cb_ref[...] + p_scr[r % 2, 0, 0:sub, :]
    for shift in range(1, SUBLANES):
        y = y + p_scr[r % 2, shift, pl.ds(shift, sub), :]
    mu = jnp.mean(y, axis=-1, keepdims=True)
    yc = y - mu
    var = jnp.mean(yc * yc, axis=-1, keepdims=True)
    ln = yc * lax.rsqrt(var + EPS) * lg_ref[...] + lb_ref[...]
    out = _silu(ln)
    out_ref[r * sub:(r + 1) * sub, :] = out.astype(BF16)
    return out[0:1, :]


def _ffn_kernel(*refs, tile, n_chunks, n_out, n_mix, final, conv_tiles_per_seq, n_cast):
    refs = list(refs)
    x_ref = refs.pop(0)
    mix_refs = [refs.pop(0) for _ in range(n_mix)]
    wmix_ref = refs.pop(0) if n_mix else None
    g_ref, win_ref, wout_ref = refs.pop(0), refs.pop(0), refs.pop(0)
    fg_ref = refs.pop(0) if final else None
    has_conv = conv_tiles_per_seq is not None
    if has_conv:
        u_first_ref, u_next_ref = refs.pop(0), refs.pop(0)
        conv_params = [refs.pop(0) for _ in range(4)]
    cast_in = [refs.pop(0) for _ in range(n_cast)]
    o_ref = refs.pop(0)
    cast_out = [refs.pop(0) for _ in range(n_cast)]
    h_scr, a_scr = refs.pop(0), refs.pop(0)
    for src, dst in zip(cast_in, cast_out):
        dst[...] = src[...].astype(BF16)
    out_tiles = [slice(m * OUT_TILE, (m + 1) * OUT_TILE) for m in range(n_out)]
    d_ff = wout_ref.shape[0]
    n_sub = CONV_SUBTILES
    if has_conv:
        v_scr, p_scr, yc_scr = refs.pop(0), refs.pop(0), refs.pop(0)
        step = pl.program_id(0)

        @pl.when(step == 0)
        def _():
            _conv_fill(u_first_ref[...], v_scr, None, tile)
            for r in range(n_sub):
                _conv_rows(v_scr, p_scr, *conv_params, yc_scr, r)

        mix_refs = [yc_scr] + mix_refs
    if mix_refs:
        for sl in out_tiles:
            y, lo = None, 0
            for y_ref in mix_refs:
                part = jnp.dot(y_ref[...], wmix_ref[lo:lo + y_ref.shape[1], sl],
                               preferred_element_type=F32)
                y = part if y is None else y + part
                lo += y_ref.shape[1]
            o_ref[:, sl] = x_ref[:, sl] + y
        x_ref = o_ref
    h_scr[...] = _rms(x_ref[...], g_ref[...]).astype(BF16)
    if has_conv:
        nxt = jnp.minimum(step + 1, pl.num_programs(0) - 1)
        _conv_fill(u_next_ref[...], v_scr, nxt % conv_tiles_per_seq == 0, tile)
    done = {}
    for j in range(n_chunks):
        sl = slice(j * FF_CHUNK, (j + 1) * FF_CHUNK)
        up_sl = slice(d_ff + j * FF_CHUNK, d_ff + (j + 1) * FF_CHUNK)
        gate = jnp.dot(h_scr[...], win_ref[:, sl], preferred_element_type=F32)
        up = jnp.dot(h_scr[...], win_ref[:, up_sl], preferred_element_type=F32)
        if j in done:
            gate = gate + done.pop(j)
        a_scr[:, sl] = (_silu(gate) * up).astype(BF16)
        if has_conv and j % CONV_CHUNK_GAP == 0 and j // CONV_CHUNK_GAP < n_sub:
            r = j // CONV_CHUNK_GAP
            out_row = _conv_rows(v_scr, p_scr, *conv_params, yc_scr, r,
                                 anchor=_bit_zero(gate[0:1, :conv_params[0].shape[1]]))
            done[j + CONV_CHUNK_SPAN] = _bit_zero(out_row)
    assert not done
    for sl in out_tiles:
        y = jnp.dot(a_scr[...], wout_ref[:, sl], preferred_element_type=F32)
        o_ref[:, sl] = x_ref[:, sl] + 0.5 * y
    if final:
        o_ref[...] = _rms(o_ref[...], fg_ref[...])


def _ffn(x, g, win, wout, layer, tile, mix=(), wmix=None, final_g=None, conv=None, cast=()):
    n, d = x.shape
    d_ff = wout.shape[1]
    n_chunks = d_ff // FF_CHUNK
    n_tiles = n // tile
    final = final_g is not None
    row = lambda i: (i, 0)
    in_specs = [pl.BlockSpec((tile, d), row)]
    in_specs += [pl.BlockSpec((tile, y.shape[1]), row) for y in mix]
    args = [x, *mix]
    if mix:
        in_specs.append(_resident(wmix, layer))
        args.append(wmix)
    in_specs += [_full((1, d)), _resident(win, layer), _resident(wout, layer)]
    args += [g, win, wout]
    if final:
        in_specs.append(_full((1, d)))
        args.append(final_g)
    scratch = [pltpu.VMEM((tile, d), BF16), pltpu.VMEM((tile, d_ff), BF16)]
    conv_tiles_per_seq = None
    if conv is not None:
        u, seq, cw, cb, lg, lb = conv
        d_conv = cw.shape[1]
        assert seq % tile == 0 and tile % (CONV_SUBTILES * SUBLANES) == 0
        assert CONV_HALO >= CONV_WIDTH - 1
        conv_tiles_per_seq = seq // tile
        in_specs += [pl.BlockSpec((tile, u.shape[1]), lambda i: (0, 0), pipeline_mode=pl.Buffered(1)),
                     pl.BlockSpec((tile, u.shape[1]), lambda i: (jnp.minimum(i + 1, n_tiles - 1), 0)),
                     _full(cw.shape), _full((1, d_conv)), _full((1, d_conv)), _full((1, d_conv))]
        args += [u, u, cw, cb, lg, lb]
        scratch += [pltpu.VMEM((CONV_HALO + tile + SUBLANES, d_conv), F32),
                    pltpu.VMEM((2, SUBLANES, tile // CONV_SUBTILES + SUBLANES, d_conv), F32),
                    pltpu.VMEM((tile, d_conv), BF16)]
    out_shape, out_specs = jax.ShapeDtypeStruct((n, d), F32), pl.BlockSpec((tile, d), row)
    if cast:
        assert all(w.shape[0] % (n_tiles * 2 * SUBLANES) == 0 for w in cast)
        slices = [pl.BlockSpec((w.shape[0] // n_tiles, w.shape[1]), row) for w in cast]
        in_specs += slices
        args += list(cast)
        out_shape = [out_shape] + [jax.ShapeDtypeStruct(w.shape, BF16) for w in cast]
        out_specs = [out_specs] + slices
    return pl.pallas_call(
        functools.partial(_ffn_kernel, tile=tile, n_chunks=n_chunks, n_out=d // OUT_TILE,
                          n_mix=len(mix), final=final, conv_tiles_per_seq=conv_tiles_per_seq,
                          n_cast=len(cast)),
        out_shape=out_shape,
        grid=(n_tiles,),
        in_specs=in_specs,
        out_specs=out_specs,
        scratch_shapes=scratch,
        compiler_params=_cparams(1),
        name="ffn",
    )(*args)


def _rotary(t, cos, sin_signed, first_half):
    partner = jnp.where(first_half, pltpu.roll(t, LANES - HEAD_DIM // 2, 1),
                        pltpu.roll(t, HEAD_DIM // 2, 1))
    return t * cos + partner * sin_signed


def _inproj_kernel(x_ref, g_ref, w_ref, cos_ref, sin_ref, uconv_ref, q_ref, k_ref, v_ref,
                   qkr_ref, vgr_ref, h_scr, *, d_conv2, d_sb, d_ret):
    q_scale = HEAD_DIM ** -0.5
    h_scr[...] = _rms(x_ref[...], g_ref[...]).astype(BF16)

    def proj(lo, width):
        return jnp.dot(h_scr[...], w_ref[:, lo:lo + width], preferred_element_type=F32)

    uconv_ref[...] = proj(0, d_conv2)
    o = d_conv2
    q_ref[...] = (proj(o, d_sb) * q_scale).astype(BF16)
    k_ref[...] = proj(o + d_sb, d_sb).astype(BF16)
    v_ref[...] = proj(o + 2 * d_sb, d_sb).astype(BF16)
    o += 3 * d_sb
    qk = proj(o, 2 * d_ret)
    cos = cos_ref[...]
    sin = sin_ref[...]
    lane = lax.broadcasted_iota(jnp.int32, (1, LANES), 1)
    first_half = (lane % HEAD_DIM) < (HEAD_DIM // 2)
    for c in range(2 * d_ret // LANES):
        sl = slice(c * LANES, (c + 1) * LANES)
        part = qk[:, sl] * q_scale if c < d_ret // LANES else qk[:, sl]
        qkr_ref[:, sl] = _rotary(part, cos, sin, first_half)
    vgr_ref[...] = proj(o + 2 * d_ret, 2 * d_ret)


def _inproj(x, g, w, layer, cos_tab, sin_tab, seq, d_conv2, d_sb, d_ret):
    n, d = x.shape
    tiles_per_seq = seq // FFN_TILE
    row = lambda i: (i, 0)
    pos = lambda i: (i % tiles_per_seq, 0)
    outs = [jax.ShapeDtypeStruct((n, d_conv2), F32)] + [jax.ShapeDtypeStruct((n, d_sb), BF16)] * 3 \
        + [jax.ShapeDtypeStruct((n, 2 * d_ret), F32)] * 2
    return pl.pallas_call(
        functools.partial(_inproj_kernel, d_conv2=d_conv2, d_sb=d_sb, d_ret=d_ret),
        out_shape=outs,
        grid=(n // FFN_TILE,),
        in_specs=[pl.BlockSpec((FFN_TILE, d), row), _full((1, d)), _resident(w, layer),
                  pl.BlockSpec((FFN_TILE, LANES), pos), pl.BlockSpec((FFN_TILE, LANES), pos)],
        out_specs=[pl.BlockSpec((FFN_TILE, s.shape[1]), row) for s in outs],
        scratch_shapes=[pltpu.VMEM((FFN_TILE, d), BF16)],
        compiler_params=_cparams(1),
        name="mixer_inproj",
    )(x, g, w, cos_tab, sin_tab)


def _sb_kernel(q_ref, knew_ref, vnew_ref, tri_ref, o_ref, k_scr, v_scr, qs_scr, acc_scr, carry_scr,
               *, n_pairs):
    q0 = pl.program_id(1) * SB_QROWS
    k_scr[pl.ds(pl.multiple_of(q0, SB_QROWS), SB_QROWS), :] = knew_ref[0]
    v_scr[pl.ds(pl.multiple_of(q0, SB_QROWS), SB_QROWS), :] = vnew_ref[0]
    tiles, pairs = range(SB_TILES), range(n_pairs)
    rows = SB_HEADS_PER_GROUP * SB_TILE
    lane = lax.broadcasted_iota(jnp.int32, (1, LANES), 1)
    col = lax.broadcasted_iota(jnp.int32, (rows, SB_BLOCK), 1)
    row = lax.broadcasted_iota(jnp.int32, (rows, SB_BLOCK), 0) % SB_TILE
    for p in pairs:
        q = q_ref[0, :, p * LANES:(p + 1) * LANES]
        for t in tiles:
            qt = q[t * SB_TILE:(t + 1) * SB_TILE]
            qs_scr[t, p] = jnp.concatenate(
                [jnp.where(lane // HEAD_DIM == hh, qt, jnp.zeros_like(qt))
                 for hh in range(SB_HEADS_PER_GROUP)], axis=0)

    def sweep(blocks, resume):
        depth = range(len(blocks[0]))

        def rows_of(ref, t, n, p):
            return ref[pl.ds(blocks[t][n][0], SB_BLOCK), p * LANES:(p + 1) * LANES]

        units = [(t, n) for t in tiles for n in depth]

        def logits(t, n):
            return [lax.dot_general(qs_scr[t, p], rows_of(k_scr, t, n, p),
                                    (((1,), (1,)), ((), ())), preferred_element_type=F32)
                    for p in pairs]

        def cumsums(t, n, zs):
            hls = []
            for p in pairs:
                zbits = lax.bitcast_convert_type(zs[p], jnp.uint32)
                neg_abs = lax.bitcast_convert_type(zbits | jnp.uint32(SIGN_BIT), F32)
                sp = jnp.maximum(zs[p], 0.0) + jnp.log(1.0 + jnp.exp(neg_abs))
                if blocks[t][n][1] is not None:
                    sp = jnp.where(blocks[t][n][1], sp, 0.0)
                hi = lax.bitcast_convert_type(
                    lax.bitcast_convert_type(sp, jnp.uint32) & jnp.uint32(BF16_BITS), F32)
                hls.append(jnp.concatenate([hi.astype(BF16), (sp - hi).astype(BF16)], axis=1))
            return jnp.dot(jnp.concatenate(hls, axis=0), tri_ref[...], preferred_element_type=F32)

        carry = {(t, p): carry_scr[t, p] if resume else None for t in tiles for p in pairs}
        acc = {(t, p): acc_scr[t, p] if resume else None for t in tiles for p in pairs}

        def weigh(t, n, zs, rt):
            for p in pairs:
                within = rt[p * rows:(p + 1) * rows, :SB_BLOCK]
                total = rt[p * rows:(p + 1) * rows, SB_BLOCK:]
                before = carry[t, p]
                w = jnp.exp(zs[p] - (within if before is None else within + before))
                if blocks[t][n][1] is not None:
                    w = jnp.where(blocks[t][n][1], w, 0.0)
                carry[t, p] = total if before is None else before + total
                pv = jnp.dot(w.astype(BF16), rows_of(v_scr, t, n, p), preferred_element_type=F32)
                acc[t, p] = pv if acc[t, p] is None else acc[t, p] + pv

        zs, rts = {}, {}
        for u in range(len(units) + 2):
            if u < len(units):
                zs[u] = logits(*units[u])
            if 1 <= u <= len(units):
                rts[u - 1] = cumsums(*units[u - 1], zs[u - 1])
            if u >= 2:
                weigh(*units[u - 2], zs.pop(u - 2), rts.pop(u - 2))
        lowest = None
        for t in tiles:
            for p in pairs:
                carry_scr[t, p] = carry[t, p]
                acc_scr[t, p] = acc[t, p]
                lowest = carry[t, p] if lowest is None else jnp.minimum(lowest, carry[t, p])
        return jnp.min(lowest)

    def aligned(start):
        return pl.multiple_of(start, SB_TILE)

    def first_fast():
        near_mask = col < row + (SB_BLOCK - SB_TILE)
        return sweep([[(aligned(q0 + (t + 1) * SB_TILE - (n + 1) * SB_BLOCK),
                        near_mask if n == 0 else None) for n in range(SB_FIRST_BLOCKS)]
                      for t in tiles], False)

    def first_clamped():
        blocks = []
        for t in tiles:
            limit = q0 + t * SB_TILE + row
            blocks.append([])
            for n in range(SB_FIRST_BLOCKS):
                start = aligned(jnp.maximum(q0 + (t + 1) * SB_TILE - (n + 1) * SB_BLOCK, 0))
                blocks[t].append((start, start + col < limit))
                limit = start
        return sweep(blocks, False)

    def more(st):
        reach = q0 + SB_QROWS - st[0] * SB_BLOCK
        return jnp.logical_and(reach > 0, st[1] < SB_DEAD_LOG)

    def body(st):
        blocks = []
        for t in tiles:
            limit = jnp.maximum(q0 + (t + 1) * SB_TILE - st[0] * SB_BLOCK, 0)
            start = aligned(jnp.maximum(limit - SB_BLOCK, 0))
            blocks.append([(start, start + col < limit)])
        return st[0] + 1, sweep(blocks, True)

    fits = q0 + SB_TILE - SB_FIRST_BLOCKS * SB_BLOCK >= 0
    lowest = lax.cond(fits, first_fast, first_clamped)
    lax.while_loop(more, body, (SB_FIRST_BLOCKS, lowest))
    for p in pairs:
        for t in tiles:
            acc = acc_scr[t, p]
            out = acc[0:SB_TILE]
            for hh in range(1, SB_HEADS_PER_GROUP):
                out = jnp.where(lane // HEAD_DIM == hh, acc[hh * SB_TILE:(hh + 1) * SB_TILE], out)
            o_ref[0, t * SB_TILE:(t + 1) * SB_TILE, p * LANES:(p + 1) * LANES] = out.astype(BF16)


def _stick_breaking(q, k, v, tri):
    b, s, d_sb = q.shape
    n_pairs = d_sb // LANES
    rows = SB_HEADS_PER_GROUP * SB_TILE
    blk = pl.BlockSpec((1, SB_QROWS, d_sb), lambda bi, i: (bi, i, 0))
    return pl.pallas_call(
        functools.partial(_sb_kernel, n_pairs=n_pairs),
        out_shape=jax.ShapeDtypeStruct((b, s, d_sb), BF16),
        grid=(b, s // SB_QROWS),
        in_specs=[blk, blk, blk, _full(tri.shape)],
        out_specs=blk,
        scratch_shapes=[pltpu.VMEM((s, d_sb), BF16), pltpu.VMEM((s, d_sb), BF16),
                        pltpu.VMEM((SB_TILES, n_pairs, rows, LANES), BF16),
                        pltpu.VMEM((SB_TILES, n_pairs, rows, LANES), F32),
                        pltpu.VMEM((SB_TILES, n_pairs, rows, LANES), F32)],
        compiler_params=_cparams(2),
        name="stick_breaking",
    )(q, k, v, tri)


def _ret_kernel(qk_ref, vg_ref, dmat_ref, qdec_ref, kdec_ref, cdec_ref, bd_ref, avg_ref, gain_ref,
                o_ref, state_scr, *, d_ret, n_batch):
    n_heads = d_ret // HEAD_DIM
    chains = [(b, slice(c * RET_BLOCK, (c + 1) * RET_BLOCK))
              for c in range(RET_STEP_BLOCKS) for b in range(n_batch)]
    nt = (((1,), (1,)), ((), ()))

    @pl.when(pl.program_id(0) == 0)
    def _():
        state_scr[...] = jnp.zeros_like(state_scr)

    lane = lax.broadcasted_iota(jnp.int32, (1, d_ret), 1)
    in_head = [lane // HEAD_DIM == h for h in range(n_heads)]
    q = [qk_ref[b, rows, :d_ret] for b, rows in chains]
    k = [qk_ref[b, rows, d_ret:] for b, rows in chains]
    v = [vg_ref[b, rows, :d_ret] for b, rows in chains]
    probs = []
    for i in range(len(chains)):
        kb = k[i].astype(BF16)
        probs.append(jnp.concatenate(
            [(lax.dot_general(jnp.where(in_head[h], q[i], 0.0).astype(BF16), kb, nt,
                              preferred_element_type=F32) * dmat_ref[h]).astype(BF16)
             for h in range(n_heads)], axis=1))
    y = []
    for i in range(len(chains)):
        vals = jnp.concatenate([jnp.where(in_head[h], v[i], 0.0).astype(BF16)
                                for h in range(n_heads)], axis=0)
        y.append(jnp.dot(probs[i], vals, preferred_element_type=F32))
    for i, (b, _) in enumerate(chains):
        state = state_scr[b]
        y[i] = y[i] + jnp.dot((q[i] * qdec_ref[...]).astype(BF16), state.astype(BF16),
                              preferred_element_type=F32)
        kv = lax.dot_general((k[i] * kdec_ref[...]).astype(BF16), v[i].astype(BF16),
                             (((0,), (0,)), ((), ())), preferred_element_type=F32)
        state_scr[b] = cdec_ref[...] * state + bd_ref[...] * kv

    def head_mean(t):
        hi = t.astype(BF16)
        lo = (t - hi.astype(F32)).astype(BF16)
        return jnp.dot(jnp.concatenate([hi, lo], axis=1), avg_ref[...], preferred_element_type=F32)

    yc = [y[i] - head_mean(y[i]) for i in range(len(chains))]
    var = [head_mean(yc[i] * yc[i]) for i in range(len(chains))]
    for i, (b, rows) in enumerate(chains):
        g = vg_ref[b, rows, d_ret:]
        o_ref[b, rows, :] = (_silu(g) * (yc[i] * lax.rsqrt(var[i] + EPS) * gain_ref[...])).astype(BF16)


def _retention(qk, vg, tabs, gain):
    b, s, d2 = qk.shape
    d_ret = d2 // 2
    step_rows = RET_STEP_BLOCKS * RET_BLOCK
    blk = pl.BlockSpec((b, step_rows, d2), lambda i: (0, i, 0))
    return pl.pallas_call(
        functools.partial(_ret_kernel, d_ret=d_ret, n_batch=b),
        out_shape=jax.ShapeDtypeStruct((b, s, d_ret), BF16),
        grid=(s // step_rows,),
        in_specs=[blk, blk] + [_full(t.shape) for t in tabs] + [_full((1, d_ret))],
        out_specs=pl.BlockSpec((b, step_rows, d_ret), lambda i: (0, i, 0)),
        scratch_shapes=[pltpu.VMEM((b, d_ret, d_ret), F32)],
        compiler_params=_cparams(1),
        name="retention",
    )(qk, vg, *tabs, gain)


def _rotary_tables(seq):
    half = HEAD_DIM // 2
    inv = 1.0 / (ROPE_BASE ** (jnp.arange(half, dtype=F32) / half))
    lane = jnp.arange(LANES)
    freq = inv[lane % half][None, :]
    sign = jnp.where((lane % HEAD_DIM) < half, -1.0, 1.0).astype(F32)
    coarse = (jnp.arange(seq // CHUNK) * CHUNK).astype(F32)[:, None] * freq
    fine = jnp.arange(CHUNK).astype(F32)[:, None] * freq
    ch, sh = jnp.cos(coarse)[:, None, :], jnp.sin(coarse)[:, None, :]
    cf, sf = jnp.cos(fine)[None], jnp.sin(fine)[None]
    cos = (ch * cf - sh * sf).reshape(seq, LANES)
    sin = (sh * cf + ch * sf).reshape(seq, LANES)
    return cos, sin * sign[None, :]


def _retention_tables(n_heads):
    d_ret = n_heads * HEAD_DIM
    log_gamma = jnp.log1p(-jnp.exp2(-5.0 - jnp.arange(n_heads, dtype=F32)))
    idx = jnp.arange(RET_BLOCK, dtype=F32)
    t, s = idx[:, None], idx[None, :]
    same = (t // CHUNK) == (s // CHUNK)
    dist = jnp.where(same, jnp.abs(t - s), t - s)
    seen = same | ((s // CHUNK) < (t // CHUNK))
    dmat = jnp.where(seen[None], jnp.exp(log_gamma[:, None, None] * dist[None]), 0.0)
    lane_gamma = jnp.repeat(log_gamma, HEAD_DIM)
    qdec = jnp.exp(lane_gamma[None, :] * (idx + 1.0)[:, None])
    kdec = jnp.exp(lane_gamma[None, :] * (RET_BLOCK - 1.0 - idx)[:, None])
    head = jnp.arange(d_ret) // HEAD_DIM
    bd = (head[:, None] == head[None, :]).astype(F32)
    cdec = bd * jnp.exp(lane_gamma * RET_BLOCK)[:, None]
    avg = jnp.concatenate([bd, bd], axis=0).astype(BF16) * (1.0 / HEAD_DIM)
    return dmat, qdec, kdec, cdec, bd, avg.astype(BF16)


def _tri_table():
    r = np.arange(2 * SB_BLOCK)[:, None] % SB_BLOCK
    c = np.arange(2 * SB_BLOCK)[None, :]
    return jnp.asarray((c >= SB_BLOCK) | (r >= c), dtype=BF16)


def kernel(x, ffn1_norm, ffn1_w_in, ffn1_w_out, mix_norm, mix_w_in, conv_w, conv_b, conv_ln_g,
           conv_ln_b, ret_norm_g, mix_w_out, ffn2_norm, ffn2_w_in, ffn2_w_out, final_norm):
    b, s, d = x.shape
    depth = ffn1_norm.shape[0]
    d_conv = conv_w.shape[2]
    d_ret = ret_norm_g.shape[1]
    d_sb = d - d_conv - d_ret
    assert s % ROW_TILE == 0 and s % FFN_TILE == 0 and s % (RET_STEP_BLOCKS * RET_BLOCK) == 0 and s % SB_QROWS == 0
    assert ffn1_w_out.shape[1] % FF_CHUNK == 0 and d % OUT_TILE == 0

    cos_tab, sin_tab = _rotary_tables(s)
    ret_tabs = _retention_tables(d_ret // HEAD_DIM)
    tri = _tri_table()
    win1, wout1 = ffn1_w_in.astype(BF16), ffn1_w_out.astype(BF16)
    later = (mix_w_in, mix_w_out, ffn2_w_in, ffn2_w_out)

    row = lambda g: g.reshape(1, -1)
    xf = x.reshape(b * s, d)
    for l in range(depth):
        if l == 0:
            xf, *rounded = _ffn(xf, row(ffn1_norm[l]), win1, wout1, l, FFN_TILE,
                                cast=[w.reshape(-1, w.shape[-1]) for w in later])
            w_mix, w_mix_out, win2, wout2 = [r.reshape(w.shape) for r, w in zip(rounded, later)]
        else:
            xf = _ffn(xf, row(ffn1_norm[l]), win1, wout1, l, FFN_TILE)
        uconv, q, k, v, qkr, vgr = _inproj(xf, row(mix_norm[l]), w_mix, l, cos_tab, sin_tab, s,
                                           2 * d_conv, d_sb, d_ret)
        seq3 = lambda t: t.reshape(b, s, t.shape[1])
        y_sb = _stick_breaking(seq3(q), seq3(k), seq3(v), tri)
        y_ret = _retention(seq3(qkr), seq3(vgr), ret_tabs, row(ret_norm_g[l]))
        flat = lambda t: t.reshape(b * s, t.shape[2])
        last = l == depth - 1
        xf = _ffn(xf, row(ffn2_norm[l]), win2, wout2, l, ROW_TILE,
                  mix=(flat(y_sb), flat(y_ret)), wmix=w_mix_out,
                  final_g=row(final_norm) if last else None,
                  conv=(uconv, s, conv_w[l], row(conv_b[l]), row(conv_ln_g[l]), row(conv_ln_b[l])))
    return xf.reshape(b, s, d)
```

```python
import functools

import jax
import jax.numpy as jnp
import numpy as np
from jax import lax
from jax.experimental import pallas as pl
from jax.experimental.pallas import tpu as pltpu

F32 = jnp.float32
BF16 = jnp.bfloat16

EPS = 1e-6
ROPE_BASE = 10000.0
CHUNK = 64
CONV_WIDTH = 31
HEAD_DIM = 64
LANES = 128
SUBLANES = 8
MXU_DIM = 256

ROW_TILE = 512
FFN_TILE = 1024
INPROJ_AHEAD = 2
INPROJ_RING = INPROJ_AHEAD + 1
FF_CHUNK = MXU_DIM
OUT_TILE = 512
CONV_HALO = 32
CONV_SUBTILES = 8
CONV_CHUNK_GAP = 1
CONV_CHUNK_SPAN = 2
SB_BLOCK = 128
SB_QROWS = 512
SB_TILE = 64
SB_TILES = SB_QROWS // SB_TILE
SB_HEADS_PER_GROUP = LANES // HEAD_DIM
SB_FIRST_BLOCKS = 2
SB_DEAD_LOG = 110.0
SIGN_BIT = 0x80000000
BF16_BITS = 0xFFFF0000
RET_BLOCK = 128
RET_STEP_BLOCKS = 4
VMEM_LIMIT = 56 * 1024 * 1024


def _cparams(n_axes):
    return pltpu.CompilerParams(dimension_semantics=("arbitrary",) * n_axes,
                                vmem_limit_bytes=VMEM_LIMIT)


def _rms(x, g):
    return (x * lax.rsqrt(jnp.mean(x * x, axis=-1, keepdims=True) + EPS)) * g


def _silu(x):
    return x * jax.nn.sigmoid(x)


def _full(shape):
    return pl.BlockSpec(shape, lambda *_: (0,) * len(shape))


def _resident(stacked, layer):
    tail = stacked.shape[1:]
    return pl.BlockSpec((None,) + tail, lambda *_: (layer,) + (0,) * len(tail),
                        pipeline_mode=pl.Buffered(1))


def _bit_zero(x):
    bits = lax.bitcast_convert_type(x, jnp.uint32)
    return lax.bitcast_convert_type((bits >> 16) >> 16, F32)


def _conv_fill(u, v_scr, starts_sequence, tile_rows):
    d_conv = v_scr.shape[1]
    if starts_sequence is None:
        v_scr[0:CONV_HALO, :] = jnp.zeros((CONV_HALO, d_conv), F32)
    else:
        v_scr[0:CONV_HALO, :] = jnp.where(starts_sequence, 0.0,
                                          v_scr[tile_rows:tile_rows + CONV_HALO, :])
    v_scr[CONV_HALO:CONV_HALO + tile_rows, :] = u[:, :d_conv] * jax.nn.sigmoid(u[:, d_conv:])
    v_scr[CONV_HALO + tile_rows:, :] = jnp.zeros((SUBLANES, d_conv), F32)


def _conv_rows(v_scr, p_scr, cw_ref, cb_ref, lg_ref, lb_ref, out_ref, r, anchor=None):
    off = CONV_HALO - (CONV_WIDTH - 1)
    sub = p_scr.shape[2] - SUBLANES
    cw = cw_ref[...] if anchor is None else cw_ref[...] + anchor
    for shift in range(SUBLANES):
        part = None
        for o in range(shift, off + CONV_WIDTH, SUBLANES):
            if o < off:
                continue
            rows = v_scr[pl.ds(r * sub + o - shift, sub + SUBLANES), :]
            term = cw[o - off:o - off + 1, :] * rows
            part = term if part is None else part + term
        p_scr[r % 2, shift] = part
    y = cb_ref[...] + p_scr[r % 2, 0, 0:sub, :]
    for shift in range(1, SUBLANES):
        y = y + p_scr[r % 2, shift, pl.ds(shift, sub), :]
    mu = jnp.mean(y, axis=-1, keepdims=True)
    yc = y - mu
    var = jnp.mean(yc * yc, axis=-1, keepdims=True)
    ln = yc * lax.rsqrt(var + EPS) * lg_ref[...] + lb_ref[...]
    out = _silu(ln)
    out_ref[r * sub:(r + 1) * sub, :] = out.astype(BF16)
    return out[0:1, :]


def _ffn_kernel(*refs, tile, n_chunks, n_out, n_mix, final, conv_tiles_per_seq, n_cast):
    refs = list(refs)
    x_ref = refs.pop(0)
    mix_refs = [refs.pop(0) for _ in range(n_mix)]
    wmix_ref = refs.pop(0) if n_mix else None
    g_ref, win_ref, wout_ref = refs.pop(0), refs.pop(0), refs.pop(0)
    fg_ref = refs.pop(0) if final else None
    has_conv = conv_tiles_per_seq is not None
    if has_conv:
        u_first_ref, u_next_ref = refs.pop(0), refs.pop(0)
        conv_params = [refs.pop(0) for _ in range(4)]
    cast_in = [refs.pop(0) for _ in range(n_cast)]
    o_ref = refs.pop(0)
    cast_out = [refs.pop(0) for _ in range(n_cast)]
    h_scr, a_scr = refs.pop(0), refs.pop(0)
    for src, dst in zip(cast_in, cast_out):
        dst[...] = src[...].astype(BF16)
    out_tiles = [slice(m * OUT_TILE, (m + 1) * OUT_TILE) for m in range(n_out)]
    d_ff = wout_ref.shape[0]
    n_sub = CONV_SUBTILES
    if has_conv:
        v_scr, p_scr, yc_scr = refs.pop(0), refs.pop(0), refs.pop(0)
        step = pl.program_id(0)

        @pl.when(step == 0)
        def _():
            _conv_fill(u_first_ref[...], v_scr, None, tile)
            for r in range(n_sub):
                _conv_rows(v_scr, p_scr, *conv_params, yc_scr, r)

        mix_refs = [yc_scr] + mix_refs
    if mix_refs:
        for sl in out_tiles:
            y, lo = None, 0
            for y_ref in mix_refs:
                part = jnp.dot(y_ref[...], wmix_ref[lo:lo + y_ref.shape[1], sl],
                               preferred_element_type=F32)
                y = part if y is None else y + part
                lo += y_ref.shape[1]
            o_ref[:, sl] = x_ref[:, sl] + y
        x_ref = o_ref
    h_scr[...] = _rms(x_ref[...], g_ref[...]).astype(BF16)
    if has_conv:
        nxt = jnp.minimum(step + 1, pl.num_programs(0) - 1)
        _conv_fill(u_next_ref[...], v_scr, nxt % conv_tiles_per_seq == 0, tile)
    done = {}
    for j in range(n_chunks):
        sl = slice(j * FF_CHUNK, (j + 1) * FF_CHUNK)
        up_sl = slice(d_ff + j * FF_CHUNK, d_ff + (j + 1) * FF_CHUNK)
        gate = jnp.dot(h_scr[...], win_ref[:, sl], preferred_element_type=F32)
        up = jnp.dot(h_scr[...], win_ref[:, up_sl], preferred_element_type=F32)
        if j in done:
            gate = gate + done.pop(j)
        a_scr[:, sl] = (_silu(gate) * up).astype(BF16)
        if has_conv and j % CONV_CHUNK_GAP == 0 and j // CONV_CHUNK_GAP < n_sub:
            r = j // CONV_CHUNK_GAP
            out_row = _conv_rows(v_scr, p_scr, *conv_params, yc_scr, r,
                                 anchor=_bit_zero(gate[0:1, :conv_params[0].shape[1]]))
            done[j + CONV_CHUNK_SPAN] = _bit_zero(out_row)
    assert not done
    for sl in out_tiles:
        y = jnp.dot(a_scr[...], wout_ref[:, sl], preferred_element_type=F32)
        o_ref[:, sl] = x_ref[:, sl] + 0.5 * y
    if final:
        o_ref[...] = _rms(o_ref[...], fg_ref[...])


def _ffn(x, g, win, wout, layer, tile, mix=(), wmix=None, final_g=None, conv=None, cast=()):
    n, d = x.shape
    d_ff = wout.shape[1]
    n_chunks = d_ff // FF_CHUNK
    n_tiles = n // tile
    final = final_g is not None
    row = lambda i: (i, 0)
    in_specs = [pl.BlockSpec((tile, d), row)]
    in_specs += [pl.BlockSpec((tile, y.shape[1]), row) for y in mix]
    args = [x, *mix]
    if mix:
        in_specs.append(_resident(wmix, layer))
        args.append(wmix)
    in_specs += [_full((1, d)), _resident(win, layer), _resident(wout, layer)]
    args += [g, win, wout]
    if final:
        in_specs.append(_full((1, d)))
        args.append(final_g)
    scratch = [pltpu.VMEM((tile, d), BF16), pltpu.VMEM((tile, d_ff), BF16)]
    conv_tiles_per_seq = None
    if conv is not None:
        u, seq, cw, cb, lg, lb = conv
        d_conv = cw.shape[1]
        assert seq % tile == 0 and tile % (CONV_SUBTILES * SUBLANES) == 0
        assert CONV_HALO >= CONV_WIDTH - 1
        conv_tiles_per_seq = seq // tile
        in_specs += [pl.BlockSpec((tile, u.shape[1]), lambda i: (0, 0), pipeline_mode=pl.Buffered(1)),
                     pl.BlockSpec((tile, u.shape[1]), lambda i: (jnp.minimum(i + 1, n_tiles - 1), 0)),
                     _full(cw.shape), _full((1, d_conv)), _full((1, d_conv)), _full((1, d_conv))]
        args += [u, u, cw, cb, lg, lb]
        scratch += [pltpu.VMEM((CONV_HALO + tile + SUBLANES, d_conv), F32),
                    pltpu.VMEM((2, SUBLANES, tile // CONV_SUBTILES + SUBLANES, d_conv), F32),
                    pltpu.VMEM((tile, d_conv), BF16)]
    out_shape, out_specs = jax.ShapeDtypeStruct((n, d), F32), pl.BlockSpec((tile, d), row)
    if cast:
        assert all(w.shape[0] % (n_tiles * 2 * SUBLANES) == 0 for w in cast)
        slices = [pl.BlockSpec((w.shape[0] // n_tiles, w.shape[1]), row) for w in cast]
        in_specs += slices
        args += list(cast)
        out_shape = [out_shape] + [jax.ShapeDtypeStruct(w.shape, BF16) for w in cast]
        out_specs = [out_specs] + slices
    return pl.pallas_call(
        functools.partial(_ffn_kernel, tile=tile, n_chunks=n_chunks, n_out=d // OUT_TILE,
                          n_mix=len(mix), final=final, conv_tiles_per_seq=conv_tiles_per_seq,
                          n_cast=len(cast)),
        out_shape=out_shape,
        grid=(n_tiles,),
        in_specs=in_specs,
        out_specs=out_specs,
        scratch_shapes=scratch,
        compiler_params=_cparams(1),
        name="ffn",
    )(*args)


def _rotary(t, cos, sin_signed, first_half):
    partner = jnp.where(first_half, pltpu.roll(t, LANES - HEAD_DIM // 2, 1),
                        pltpu.roll(t, HEAD_DIM // 2, 1))
    return t * cos + partner * sin_signed


def _inproj_kernel(x_hbm, g_ref, w_ref, cos_ref, sin_ref, uconv_ref, q_ref, k_ref, v_ref,
                   qkr_ref, vgr_ref, h_scr, x_ring, x_sems, *, d_conv2, d_sb, d_ret):
    q_scale = HEAD_DIM ** -0.5
    step, n_steps = pl.program_id(0), pl.num_programs(0)

    def tile_copy(s):
        slot = s % INPROJ_RING
        return pltpu.make_async_copy(x_hbm.at[pl.ds(s * FFN_TILE, FFN_TILE), :],
                                     x_ring.at[slot], x_sems.at[slot])

    @pl.when(step == 0)
    def _():
        for s in range(INPROJ_AHEAD):
            @pl.when(s < n_steps)
            def _():
                tile_copy(s).start()

    @pl.when(step + INPROJ_AHEAD < n_steps)
    def _():
        tile_copy(step + INPROJ_AHEAD).start()

    tile_copy(step).wait()
    h_scr[...] = _rms(x_ring[step % INPROJ_RING], g_ref[...]).astype(BF16)

    def proj(lo, width):
        return jnp.dot(h_scr[...], w_ref[:, lo:lo + width], preferred_element_type=F32)

    uconv_ref[...] = proj(0, d_conv2)
    o = d_conv2
    q_ref[...] = (proj(o, d_sb) * q_scale).astype(BF16)
    k_ref[...] = proj(o + d_sb, d_sb).astype(BF16)
    v_ref[...] = proj(o + 2 * d_sb, d_sb).astype(BF16)
    o += 3 * d_sb
    qk = proj(o, 2 * d_ret)
    cos = cos_ref[...]
    sin = sin_ref[...]
    lane = lax.broadcasted_iota(jnp.int32, (1, LANES), 1)
    first_half = (lane % HEAD_DIM) < (HEAD_DIM // 2)
    for c in range(2 * d_ret // LANES):
        sl = slice(c * LANES, (c + 1) * LANES)
        part = qk[:, sl] * q_scale if c < d_ret // LANES else qk[:, sl]
        qkr_ref[:, sl] = _rotary(part, cos, sin, first_half)
    vgr_ref[...] = proj(o + 2 * d_ret, 2 * d_ret)


def _inproj(x, g, w, layer, cos_tab, sin_tab, seq, d_conv2, d_sb, d_ret):
    n, d = x.shape
    tiles_per_seq = seq // FFN_TILE
    row = lambda i: (i, 0)
    pos = lambda i: (i % tiles_per_seq, 0)
    outs = [jax.ShapeDtypeStruct((n, d_conv2), F32)] + [jax.ShapeDtypeStruct((n, d_sb), BF16)] * 3 \
        + [jax.ShapeDtypeStruct((n, 2 * d_ret), F32)] * 2
    return pl.pallas_call(
        functools.partial(_inproj_kernel, d_conv2=d_conv2, d_sb=d_sb, d_ret=d_ret),
        out_shape=outs,
        grid=(n // FFN_TILE,),
        in_specs=[pl.BlockSpec(memory_space=pl.ANY), _full((1, d)), _resident(w, layer),
                  pl.BlockSpec((FFN_TILE, LANES), pos), pl.BlockSpec((FFN_TILE, LANES), pos)],
        out_specs=[pl.BlockSpec((FFN_TILE, s.shape[1]), row) for s in outs],
        scratch_shapes=[pltpu.VMEM((FFN_TILE, d), BF16),
                        pltpu.VMEM((INPROJ_RING, FFN_TILE, d), F32),
                        pltpu.SemaphoreType.DMA((INPROJ_RING,))],
        compiler_params=_cparams(1),
        name="mixer_inproj",
    )(x, g, w, cos_tab, sin_tab)


def _sb_kernel(q_ref, knew_ref, vnew_ref, tri_ref, o_ref, k_scr, v_scr, qs_scr, acc_scr, carry_scr,
               *, n_pairs):
    q0 = pl.program_id(1) * SB_QROWS
    k_scr[pl.ds(pl.multiple_of(q0, SB_QROWS), SB_QROWS), :] = knew_ref[0]
    v_scr[pl.ds(pl.multiple_of(q0, SB_QROWS), SB_QROWS), :] = vnew_ref[0]
    tiles, pairs = range(SB_TILES), range(n_pairs)
    rows = SB_HEADS_PER_GROUP * SB_TILE
    lane = lax.broadcasted_iota(jnp.int32, (1, LANES), 1)
    col = lax.broadcasted_iota(jnp.int32, (rows, SB_BLOCK), 1)
    row = lax.broadcasted_iota(jnp.int32, (rows, SB_BLOCK), 0) % SB_TILE
    for p in pairs:
        q = q_ref[0, :, p * LANES:(p + 1) * LANES]
        for t in tiles:
            qt = q[t * SB_TILE:(t + 1) * SB_TILE]
            qs_scr[t, p] = jnp.concatenate(
                [jnp.where(lane // HEAD_DIM == hh, qt, jnp.zeros_like(qt))
                 for hh in range(SB_HEADS_PER_GROUP)], axis=0)

    def sweep(blocks, resume):
        depth = range(len(blocks[0]))

        def rows_of(ref, t, n, p):
            return ref[pl.ds(blocks[t][n][0], SB_BLOCK), p * LANES:(p + 1) * LANES]

        units = [(t, n) for t in tiles for n in depth]

        def logits(t, n):
            return [lax.dot_general(qs_scr[t, p], rows_of(k_scr, t, n, p),
                                    (((1,), (1,)), ((), ())), preferred_element_type=F32)
                    for p in pairs]

        def cumsums(t, n, zs):
            hls = []
            for p in pairs:
                zbits = lax.bitcast_convert_type(zs[p], jnp.uint32)
                neg_abs = lax.bitcast_convert_type(zbits | jnp.uint32(SIGN_BIT), F32)
                sp = jnp.maximum(zs[p], 0.0) + jnp.log(1.0 + jnp.exp(neg_abs))
                if blocks[t][n][1] is not None:
                    sp = jnp.where(blocks[t][n][1], sp, 0.0)
                hi = lax.bitcast_convert_type(
                    lax.bitcast_convert_type(sp, jnp.uint32) & jnp.uint32(BF16_BITS), F32)
                hls.append(jnp.concatenate([hi.astype(BF16), (sp - hi).astype(BF16)], axis=1))
            return jnp.dot(jnp.concatenate(hls, axis=0), tri_ref[...], preferred_element_type=F32)

        carry = {(t, p): carry_scr[t, p] if resume else None for t in tiles for p in pairs}
        acc = {(t, p): acc_scr[t, p] if resume else None for t in tiles for p in pairs}

        def weigh(t, n, zs, rt):
            for p in pairs:
                within = rt[p * rows:(p + 1) * rows, :SB_BLOCK]
                total = rt[p * rows:(p + 1) * rows, SB_BLOCK:]
                before = carry[t, p]
                w = jnp.exp(zs[p] - (within if before is None else within + before))
                if blocks[t][n][1] is not None:
                    w = jnp.where(blocks[t][n][1], w, 0.0)
                carry[t, p] = total if before is None else before + total
                pv = jnp.dot(w.astype(BF16), rows_of(v_scr, t, n, p), preferred_element_type=F32)
                acc[t, p] = pv if acc[t, p] is None else acc[t, p] + pv

        zs, rts = {}, {}
        for u in range(len(units) + 2):
            if u < len(units):
                zs[u] = logits(*units[u])
            if 1 <= u <= len(units):
                rts[u - 1] = cumsums(*units[u - 1], zs[u - 1])
            if u >= 2:
                weigh(*units[u - 2], zs.pop(u - 2), rts.pop(u - 2))
        lowest = None
        for t in tiles:
            for p in pairs:
                carry_scr[t, p] = carry[t, p]
                acc_scr[t, p] = acc[t, p]
                lowest = carry[t, p] if lowest is None else jnp.minimum(lowest, carry[t, p])
        return jnp.min(lowest)

    def aligned(start):
        return pl.multiple_of(start, SB_TILE)

    def first_fast():
        near_mask = col < row + (SB_BLOCK - SB_TILE)
        return sweep([[(aligned(q0 + (t + 1) * SB_TILE - (n + 1) * SB_BLOCK),
                        near_mask if n == 0 else None) for n in range(SB_FIRST_BLOCKS)]
                      for t in tiles], False)

    def first_clamped():
        blocks = []
        for t in tiles:
            limit = q0 + t * SB_TILE + row
            blocks.append([])
            for n in range(SB_FIRST_BLOCKS):
                start = aligned(jnp.maximum(q0 + (t + 1) * SB_TILE - (n + 1) * SB_BLOCK, 0))
                blocks[t].append((start, start + col < limit))
                limit = start
        return sweep(blocks, False)

    def more(st):
        reach = q0 + SB_QROWS - st[0] * SB_BLOCK
        return jnp.logical_and(reach > 0, st[1] < SB_DEAD_LOG)

    def body(st):
        blocks = []
        for t in tiles:
            limit = jnp.maximum(q0 + (t + 1) * SB_TILE - st[0] * SB_BLOCK, 0)
            start = aligned(jnp.maximum(limit - SB_BLOCK, 0))
            blocks.append([(start, start + col < limit)])
        return st[0] + 1, sweep(blocks, True)

    fits = q0 + SB_TILE - SB_FIRST_BLOCKS * SB_BLOCK >= 0
    lowest = lax.cond(fits, first_fast, first_clamped)
    lax.while_loop(more, body, (SB_FIRST_BLOCKS, lowest))
    for p in pairs:
        for t in tiles:
            acc = acc_scr[t, p]
            out = acc[0:SB_TILE]
            for hh in range(1, SB_HEADS_PER_GROUP):
                out = jnp.where(lane // HEAD_DIM == hh, acc[hh * SB_TILE:(hh + 1) * SB_TILE], out)
            o_ref[0, t * SB_TILE:(t + 1) * SB_TILE, p * LANES:(p + 1) * LANES] = out.astype(BF16)


def _stick_breaking(q, k, v, tri):
    b, s, d_sb = q.shape
    n_pairs = d_sb // LANES
    rows = SB_HEADS_PER_GROUP * SB_TILE
    blk = pl.BlockSpec((1, SB_QROWS, d_sb), lambda bi, i: (bi, i, 0))
    return pl.pallas_call(
        functools.partial(_sb_kernel, n_pairs=n_pairs),
        out_shape=jax.ShapeDtypeStruct((b, s, d_sb), BF16),
        grid=(b, s // SB_QROWS),
        in_specs=[blk, blk, blk, _full(tri.shape)],
        out_specs=blk,
        scratch_shapes=[pltpu.VMEM((s, d_sb), BF16), pltpu.VMEM((s, d_sb), BF16),
                        pltpu.VMEM((SB_TILES, n_pairs, rows, LANES), BF16),
                        pltpu.VMEM((SB_TILES, n_pairs, rows, LANES), F32),
                        pltpu.VMEM((SB_TILES, n_pairs, rows, LANES), F32)],
        compiler_params=_cparams(2),
        name="stick_breaking",
    )(q, k, v, tri)


def _ret_kernel(qk_ref, vg_ref, dmat_ref, qdec_ref, kdec_ref, cdec_ref, bd_ref, avg_ref, gain_ref,
                o_ref, state_scr, *, d_ret, n_batch):
    n_heads = d_ret // HEAD_DIM
    chains = [(b, slice(c * RET_BLOCK, (c + 1) * RET_BLOCK))
              for c in range(RET_STEP_BLOCKS) for b in range(n_batch)]
    nt = (((1,), (1,)), ((), ()))

    @pl.when(pl.program_id(0) == 0)
    def _():
        state_scr[...] = jnp.zeros_like(state_scr)

    lane = lax.broadcasted_iota(jnp.int32, (1, d_ret), 1)
    in_head = [lane // HEAD_DIM == h for h in range(n_heads)]
    q = [qk_ref[b, rows, :d_ret] for b, rows in chains]
    k = [qk_ref[b, rows, d_ret:] for b, rows in chains]
    v = [vg_ref[b, rows, :d_ret] for b, rows in chains]
    probs = []
    for i in range(len(chains)):
        kb = k[i].astype(BF16)
        probs.append(jnp.concatenate(
            [(lax.dot_general(jnp.where(in_head[h], q[i], 0.0).astype(BF16), kb, nt,
                              preferred_element_type=F32) * dmat_ref[h]).astype(BF16)
             for h in range(n_heads)], axis=1))
    y = []
    for i in range(len(chains)):
        vals = jnp.concatenate([jnp.where(in_head[h], v[i], 0.0).astype(BF16)
                                for h in range(n_heads)], axis=0)
        y.append(jnp.dot(probs[i], vals, preferred_element_type=F32))
    for i, (b, _) in enumerate(chains):
        state = state_scr[b]
        y[i] = y[i] + jnp.dot((q[i] * qdec_ref[...]).astype(BF16), state.astype(BF16),
                              preferred_element_type=F32)
        kv = lax.dot_general((k[i] * kdec_ref[...]).astype(BF16), v[i].astype(BF16),
                             (((0,), (0,)), ((), ())), preferred_element_type=F32)
        state_scr[b] = cdec_ref[...] * state + bd_ref[...] * kv

    def head_mean(t):
        hi = t.astype(BF16)
        lo = (t - hi.astype(F32)).astype(BF16)
        return jnp.dot(jnp.concatenate([hi, lo], axis=1), avg_ref[...], preferred_element_type=F32)

    yc = [y[i] - head_mean(y[i]) for i in range(len(chains))]
    var = [head_mean(yc[i] * yc[i]) for i in range(len(chains))]
    for i, (b, rows) in enumerate(chains):
        g = vg_ref[b, rows, d_ret:]
        o_ref[b, rows, :] = (_silu(g) * (yc[i] * lax.rsqrt(var[i] + EPS) * gain_ref[...])).astype(BF16)


def _retention(qk, vg, tabs, gain):
    b, s, d2 = qk.shape
    d_ret = d2 // 2
    step_rows = RET_STEP_BLOCKS * RET_BLOCK
    blk = pl.BlockSpec((b, step_rows, d2), lambda i: (0, i, 0))
    return pl.pallas_call(
        functools.partial(_ret_kernel, d_ret=d_ret, n_batch=b),
        out_shape=jax.ShapeDtypeStruct((b, s, d_ret), BF16),
        grid=(s // step_rows,),
        in_specs=[blk, blk] + [_full(t.shape) for t in tabs] + [_full((1, d_ret))],
        out_specs=pl.BlockSpec((b, step_rows, d_ret), lambda i: (0, i, 0)),
        scratch_shapes=[pltpu.VMEM((b, d_ret, d_ret), F32)],
        compiler_params=_cparams(1),
        name="retention",
    )(qk, vg, *tabs, gain)


def _rotary_tables(seq):
    half = HEAD_DIM // 2
    inv = 1.0 / (ROPE_BASE ** (jnp.arange(half, dtype=F32) / half))
    lane = jnp.arange(LANES)
    freq = inv[lane % half][None, :]
    sign = jnp.where((lane % HEAD_DIM) < half, -1.0, 1.0).astype(F32)
    coarse = (jnp.arange(seq // CHUNK) * CHUNK).astype(F32)[:, None] * freq
    fine = jnp.arange(CHUNK).astype(F32)[:, None] * freq
    ch, sh = jnp.cos(coarse)[:, None, :], jnp.sin(coarse)[:, None, :]
    cf, sf = jnp.cos(fine)[None], jnp.sin(fine)[None]
    cos = (ch * cf - sh * sf).reshape(seq, LANES)
    sin = (sh * cf + ch * sf).reshape(seq, LANES)
    return cos, sin * sign[None, :]


def _retention_tables(n_heads):
    d_ret = n_heads * HEAD_DIM
    log_gamma = jnp.log1p(-jnp.exp2(-5.0 - jnp.arange(n_heads, dtype=F32)))
    idx = jnp.arange(RET_BLOCK, dtype=F32)
    t, s = idx[:, None], idx[None, :]
    same = (t // CHUNK) == (s // CHUNK)
    dist = jnp.where(same, jnp.abs(t - s), t - s)
    seen = same | ((s // CHUNK) < (t // CHUNK))
    dmat = jnp.where(seen[None], jnp.exp(log_gamma[:, None, None] * dist[None]), 0.0)
    lane_gamma = jnp.repeat(log_gamma, HEAD_DIM)
    qdec = jnp.exp(lane_gamma[None, :] * (idx + 1.0)[:, None])
    kdec = jnp.exp(lane_gamma[None, :] * (RET_BLOCK - 1.0 - idx)[:, None])
    head = jnp.arange(d_ret) // HEAD_DIM
    bd = (head[:, None] == head[None, :]).astype(F32)
    cdec = bd * jnp.exp(lane_gamma * RET_BLOCK)[:, None]
    avg = jnp.concatenate([bd, bd], axis=0).astype(BF16) * (1.0 / HEAD_DIM)
    return dmat, qdec, kdec, cdec, bd, avg.astype(BF16)


def _tri_table():
    r = np.arange(2 * SB_BLOCK)[:, None] % SB_BLOCK
    c = np.arange(2 * SB_BLOCK)[None, :]
    return jnp.asarray((c >= SB_BLOCK) | (r >= c), dtype=BF16)


def kernel(x, ffn1_norm, ffn1_w_in, ffn1_w_out, mix_norm, mix_w_in, conv_w, conv_b, conv_ln_g,
           conv_ln_b, ret_norm_g, mix_w_out, ffn2_norm, ffn2_w_in, ffn2_w_out, final_norm):
    b, s, d = x.shape
    depth = ffn1_norm.shape[0]
    d_conv = conv_w.shape[2]
    d_ret = ret_norm_g.shape[1]
    d_sb = d - d_conv - d_ret
    assert s % ROW_TILE == 0 and s % FFN_TILE == 0 and s % (RET_STEP_BLOCKS * RET_BLOCK) == 0 and s % SB_QROWS == 0
    assert ffn1_w_out.shape[1] % FF_CHUNK == 0 and d % OUT_TILE == 0

    cos_tab, sin_tab = _rotary_tables(s)
    ret_tabs = _retention_tables(d_ret // HEAD_DIM)
    tri = _tri_table()
    win1, wout1 = ffn1_w_in.astype(BF16), ffn1_w_out.astype(BF16)
    later = (mix_w_in, mix_w_out, ffn2_w_in, ffn2_w_out)

    row = lambda g: g.reshape(1, -1)
    xf = x.reshape(b * s, d)
    for l in range(depth):
        if l == 0:
            xf, *rounded = _ffn(xf, row(ffn1_norm[l]), win1, wout1, l, FFN_TILE,
                                cast=[w.reshape(-1, w.shape[-1]) for w in later])
            w_mix, w_mix_out, win2, wout2 = [r.reshape(w.shape) for r, w in zip(rounded, later)]
        else:
            xf = _ffn(xf, row(ffn1_norm[l]), win1, wout1, l, FFN_TILE)
        uconv, q, k, v, qkr, vgr = _inproj(xf, row(mix_norm[l]), w_mix, l, cos_tab, sin_tab, s,
                                           2 * d_conv, d_sb, d_ret)
        seq3 = lambda t: t.reshape(b, s, t.shape[1])
        y_sb = _stick_breaking(seq3(q), seq3(k), seq3(v), tri)
        y_ret = _retention(seq3(qkr), seq3(vgr), ret_tabs, row(ret_norm_g[l]))
        flat = lambda t: t.reshape(b * s, t.shape[2])
        last = l == depth - 1
        xf = _ffn(xf, row(ffn2_norm[l]), win2, wout2, l, ROW_TILE,
                  mix=(flat(y_sb), flat(y_ret)), wmix=w_mix_out,
                  final_g=row(final_norm) if last else None,
                  conv=(uconv, s, conv_w[l], row(conv_b[l]), row(conv_ln_g[l]), row(conv_ln_b[l])))
    return xf.reshape(b, s, d)
```

```python
import functools

import jax
import jax.numpy as jnp
import numpy as np
from jax import lax
from jax.experimental import pallas as pl
from jax.experimental.pallas import tpu as pltpu

F32 = jnp.float32
BF16 = jnp.bfloat16

EPS = 1e-6
ROPE_BASE = 10000.0
CHUNK = 64
CONV_WIDTH = 31
HEAD_DIM = 64
LANES = 128
SUBLANES = 8
MXU_DIM = 256

ROW_TILE = 512
FFN_TILE = 1024
FF_CHUNK = MXU_DIM
OUT_TILE = 512
CONV_HALO = 32
CONV_SUBTILES = 8
CONV_CHUNK_FIRST = 2
CONV_CHUNK_SPAN = 1
SB_BLOCK = 128
SB_QROWS = 512
SB_TILE = 64
SB_TILES = SB_QROWS // SB_TILE
SB_HEADS_PER_GROUP = LANES // HEAD_DIM
SB_FIRST_BLOCKS = 2
SB_DEAD_LOG = 110.0
SIGN_BIT = 0x80000000
BF16_BITS = 0xFFFF0000
RET_BLOCK = 128
RET_STEP_BLOCKS = 4
VMEM_LIMIT = 56 * 1024 * 1024


def _cparams(n_axes):
    return pltpu.CompilerParams(dimension_semantics=("arbitrary",) * n_axes,
                                vmem_limit_bytes=VMEM_LIMIT)


def _rms(x, g):
    return (x * lax.rsqrt(jnp.mean(x * x, axis=-1, keepdims=True) + EPS)) * g


def _silu(x):
    return x * jax.nn.sigmoid(x)


def _full(shape):
    return pl.BlockSpec(shape, lambda *_: (0,) * len(shape))


def _resident(stacked, layer):
    tail = stacked.shape[1:]
    return pl.BlockSpec((None,) + tail, lambda *_: (layer,) + (0,) * len(tail),
                        pipeline_mode=pl.Buffered(1))


def _bit_zero(x):
    bits = lax.bitcast_convert_type(x, jnp.uint32)
    return lax.bitcast_convert_type((bits >> 16) >> 16, F32)


def _conv_fill(u, v_scr, starts_sequence, tile_rows):
    d_conv = v_scr.shape[1]
    if starts_sequence is None:
        v_scr[0:CONV_HALO, :] = jnp.zeros((CONV_HALO, d_conv), F32)
    else:
        v_scr[0:CONV_HALO, :] = jnp.where(starts_sequence, 0.0,
                                          v_scr[tile_rows:tile_rows + CONV_HALO, :])
    v_scr[CONV_HALO:CONV_HALO + tile_rows, :] = u[:, :d_conv] * jax.nn.sigmoid(u[:, d_conv:])
    v_scr[CONV_HALO + tile_rows:, :] = jnp.zeros((SUBLANES, d_conv), F32)


def _conv_rows(v_scr, p_scr, cw_ref, cb_ref, lg_ref, lb_ref, out_ref, r, anchor=None):
    off = CONV_HALO - (CONV_WIDTH - 1)
    sub = p_scr.shape[2] - SUBLANES
    cw = cw_ref[...] if anchor is None else cw_ref[...] + anchor
    for shift in range(SUBLANES):
        part = None
        for o in range(shift, off + CONV_WIDTH, SUBLANES):
            if o < off:
                continue
            rows = v_scr[pl.ds(r * sub + o - shift, sub + SUBLANES), :]
            term = cw[o - off:o - off + 1, :] * rows
            part = term if part is None else part + term
        p_scr[r % 2, shift] = part
    y = cb_ref[...] + p_scr[r % 2, 0, 0:sub, :]
    for shift in range(1, SUBLANES):
        y = y + p_scr[r % 2, shift, pl.ds(shift, sub), :]
    mu = jnp.mean(y, axis=-1, keepdims=True)
    yc = y - mu
    var = jnp.mean(yc * yc, axis=-1, keepdims=True)
    ln = yc * lax.rsqrt(var + EPS) * lg_ref[...] + lb_ref[...]
    out = _silu(ln)
    out_ref[r * sub:(r + 1) * sub, :] = out.astype(BF16)
    return out[0:1, :]


def _ffn_kernel(*refs, tile, n_chunks, n_out, n_mix, final, conv_tiles_per_seq, n_cast):
    refs = list(refs)
    x_ref = refs.pop(0)
    mix_refs = [refs.pop(0) for _ in range(n_mix)]
    wmix_ref = refs.pop(0) if n_mix else None
    g_ref, win_ref, wout_ref = refs.pop(0), refs.pop(0), refs.pop(0)
    fg_ref = refs.pop(0) if final else None
    has_conv = conv_tiles_per_seq is not None
    if has_conv:
        u_first_ref, u_next_ref = refs.pop(0), refs.pop(0)
        conv_params = [refs.pop(0) for _ in range(4)]
    cast_in = [refs.pop(0) for _ in range(n_cast)]
    o_ref = refs.pop(0)
    cast_out = [refs.pop(0) for _ in range(n_cast)]
    h_scr, a_scr = refs.pop(0), refs.pop(0)
    for src, dst in zip(cast_in, cast_out):
        dst[...] = src[...].astype(BF16)
    out_tiles = [slice(m * OUT_TILE, (m + 1) * OUT_TILE) for m in range(n_out)]
    d_ff = wout_ref.shape[0]
    n_sub = CONV_SUBTILES
    if has_conv:
        v_scr, p_scr, yc_scr = refs.pop(0), refs.pop(0), refs.pop(0)
        step = pl.program_id(0)

        @pl.when(step == 0)
        def _():
            _conv_fill(u_first_ref[...], v_scr, None, tile)
            for r in range(n_sub):
                _conv_rows(v_scr, p_scr, *conv_params, yc_scr, r)

        mix_refs = [yc_scr] + mix_refs
    if mix_refs:
        for sl in out_tiles:
            y, lo = None, 0
            for y_ref in mix_refs:
                part = jnp.dot(y_ref[...], wmix_ref[lo:lo + y_ref.shape[1], sl],
                               preferred_element_type=F32)
                y = part if y is None else y + part
                lo += y_ref.shape[1]
            o_ref[:, sl] = x_ref[:, sl] + y
        x_ref = o_ref
    h_scr[...] = _rms(x_ref[...], g_ref[...]).astype(BF16)
    if has_conv:
        nxt = jnp.minimum(step + 1, pl.num_programs(0) - 1)
        _conv_fill(u_next_ref[...], v_scr, nxt % conv_tiles_per_seq == 0, tile)
    done = {}
    for j in range(n_chunks):
        sl = slice(j * FF_CHUNK, (j + 1) * FF_CHUNK)
        up_sl = slice(d_ff + j * FF_CHUNK, d_ff + (j + 1) * FF_CHUNK)
        gate = jnp.dot(h_scr[...], win_ref[:, sl], preferred_element_type=F32)
        up = jnp.dot(h_scr[...], win_ref[:, up_sl], preferred_element_type=F32)
        if j in done:
            gate = gate + done.pop(j)
        a_scr[:, sl] = (_silu(gate) * up).astype(BF16)
        if has_conv and CONV_CHUNK_FIRST <= j < CONV_CHUNK_FIRST + n_sub:
            r = j - CONV_CHUNK_FIRST
            out_row = _conv_rows(v_scr, p_scr, *conv_params, yc_scr, r,
                                 anchor=_bit_zero(gate[0:1, :conv_params[0].shape[1]]))
            done[j + CONV_CHUNK_SPAN] = _bit_zero(out_row)
    assert not done
    for sl in out_tiles:
        y = jnp.dot(a_scr[...], wout_ref[:, sl], preferred_element_type=F32)
        o_ref[:, sl] = x_ref[:, sl] + 0.5 * y
    if final:
        o_ref[...] = _rms(o_ref[...], fg_ref[...])


def _ffn(x, g, win, wout, layer, tile, mix=(), wmix=None, final_g=None, conv=None, cast=()):
    n, d = x.shape
    d_ff = wout.shape[1]
    n_chunks = d_ff // FF_CHUNK
    n_tiles = n // tile
    final = final_g is not None
    row = lambda i: (i, 0)
    in_specs = [pl.BlockSpec((tile, d), row)]
    in_specs += [pl.BlockSpec((tile, y.shape[1]), row) for y in mix]
    args = [x, *mix]
    if mix:
        in_specs.append(_resident(wmix, layer))
        args.append(wmix)
    in_specs += [_full((1, d)), _resident(win, layer), _resident(wout, layer)]
    args += [g, win, wout]
    if final:
        in_specs.append(_full((1, d)))
        args.append(final_g)
    scratch = [pltpu.VMEM((tile, d), BF16), pltpu.VMEM((tile, d_ff), BF16)]
    conv_tiles_per_seq = None
    if conv is not None:
        u, seq, cw, cb, lg, lb = conv
        d_conv = cw.shape[1]
        assert seq % tile == 0 and tile % (CONV_SUBTILES * SUBLANES) == 0
        assert CONV_HALO >= CONV_WIDTH - 1
        conv_tiles_per_seq = seq // tile
        in_specs += [pl.BlockSpec((tile, u.shape[1]), lambda i: (0, 0), pipeline_mode=pl.Buffered(1)),
                     pl.BlockSpec((tile, u.shape[1]), lambda i: (jnp.minimum(i + 1, n_tiles - 1), 0)),
                     _full(cw.shape), _full((1, d_conv)), _full((1, d_conv)), _full((1, d_conv))]
        args += [u, u, cw, cb, lg, lb]
        scratch += [pltpu.VMEM((CONV_HALO + tile + SUBLANES, d_conv), F32),
                    pltpu.VMEM((2, SUBLANES, tile // CONV_SUBTILES + SUBLANES, d_conv), F32),
                    pltpu.VMEM((tile, d_conv), BF16)]
    out_shape, out_specs = jax.ShapeDtypeStruct((n, d), F32), pl.BlockSpec((tile, d), row)
    if cast:
        assert all(w.shape[0] % (n_tiles * 2 * SUBLANES) == 0 for w in cast)
        slices = [pl.BlockSpec((w.shape[0] // n_tiles, w.shape[1]), row) for w in cast]
        in_specs += slices
        args += list(cast)
        out_shape = [out_shape] + [jax.ShapeDtypeStruct(w.shape, BF16) for w in cast]
        out_specs = [out_specs] + slices
    return pl.pallas_call(
        functools.partial(_ffn_kernel, tile=tile, n_chunks=n_chunks, n_out=d // OUT_TILE,
                          n_mix=len(mix), final=final, conv_tiles_per_seq=conv_tiles_per_seq,
                          n_cast=len(cast)),
        out_shape=out_shape,
        grid=(n_tiles,),
        in_specs=in_specs,
        out_specs=out_specs,
        scratch_shapes=scratch,
        compiler_params=_cparams(1),
        name="ffn",
    )(*args)


def _rotary(t, cos, sin_signed, first_half):
    partner = jnp.where(first_half, pltpu.roll(t, LANES - HEAD_DIM // 2, 1),
                        pltpu.roll(t, HEAD_DIM // 2, 1))
    return t * cos + partner * sin_signed


def _inproj_kernel(x_ref, g_ref, w_ref, cos_ref, sin_ref, uconv_ref, q_ref, k_ref, v_ref,
                   qkr_ref, vgr_ref, h_scr, *, d_conv2, d_sb, d_ret):
    q_scale = HEAD_DIM ** -0.5
    h_scr[...] = _rms(x_ref[...], g_ref[...]).astype(BF16)

    def proj(lo, width):
        return jnp.dot(h_scr[...], w_ref[:, lo:lo + width], preferred_element_type=F32)

    uconv_ref[...] = proj(0, d_conv2)
    o = d_conv2
    q_ref[...] = (proj(o, d_sb) * q_scale).astype(BF16)
    k_ref[...] = proj(o + d_sb, d_sb).astype(BF16)
    v_ref[...] = proj(o + 2 * d_sb, d_sb).astype(BF16)
    o += 3 * d_sb
    qk = proj(o, 2 * d_ret)
    cos = cos_ref[...]
    sin = sin_ref[...]
    lane = lax.broadcasted_iota(jnp.int32, (1, LANES), 1)
    first_half = (lane % HEAD_DIM) < (HEAD_DIM // 2)
    for c in range(2 * d_ret // LANES):
        sl = slice(c * LANES, (c + 1) * LANES)
        part = qk[:, sl] * q_scale if c < d_ret // LANES else qk[:, sl]
        qkr_ref[:, sl] = _rotary(part, cos, sin, first_half)
    vgr_ref[...] = proj(o + 2 * d_ret, 2 * d_ret)


def _inproj(x, g, w, layer, cos_tab, sin_tab, seq, d_conv2, d_sb, d_ret):
    n, d = x.shape
    tiles_per_seq = seq // FFN_TILE
    row = lambda i: (i, 0)
    pos = lambda i: (i % tiles_per_seq, 0)
    outs = [jax.ShapeDtypeStruct((n, d_conv2), F32)] + [jax.ShapeDtypeStruct((n, d_sb), BF16)] * 3 \
        + [jax.ShapeDtypeStruct((n, 2 * d_ret), F32)] * 2
    return pl.pallas_call(
        functools.partial(_inproj_kernel, d_conv2=d_conv2, d_sb=d_sb, d_ret=d_ret),
        out_shape=outs,
        grid=(n // FFN_TILE,),
        in_specs=[pl.BlockSpec((FFN_TILE, d), row), _full((1, d)), _resident(w, layer),
                  pl.BlockSpec((FFN_TILE, LANES), pos), pl.BlockSpec((FFN_TILE, LANES), pos)],
        out_specs=[pl.BlockSpec((FFN_TILE, s.shape[1]), row) for s in outs],
        scratch_shapes=[pltpu.VMEM((FFN_TILE, d), BF16)],
        compiler_params=_cparams(1),
        name="mixer_inproj",
    )(x, g, w, cos_tab, sin_tab)


def _sb_kernel(q_ref, knew_ref, vnew_ref, tri_ref, o_ref, k_scr, v_scr, qs_scr, acc_scr, carry_scr,
               *, n_pairs):
    q0 = pl.program_id(1) * SB_QROWS
    k_scr[pl.ds(pl.multiple_of(q0, SB_QROWS), SB_QROWS), :] = knew_ref[0]
    v_scr[pl.ds(pl.multiple_of(q0, SB_QROWS), SB_QROWS), :] = vnew_ref[0]
    tiles, pairs = range(SB_TILES), range(n_pairs)
    rows = SB_HEADS_PER_GROUP * SB_TILE
    lane = lax.broadcasted_iota(jnp.int32, (1, LANES), 1)
    col = lax.broadcasted_iota(jnp.int32, (rows, SB_BLOCK), 1)
    row = lax.broadcasted_iota(jnp.int32, (rows, SB_BLOCK), 0) % SB_TILE
    for p in pairs:
        q = q_ref[0, :, p * LANES:(p + 1) * LANES]
        for t in tiles:
            qt = q[t * SB_TILE:(t + 1) * SB_TILE]
            qs_scr[t, p] = jnp.concatenate(
                [jnp.where(lane // HEAD_DIM == hh, qt, jnp.zeros_like(qt))
                 for hh in range(SB_HEADS_PER_GROUP)], axis=0)

    def sweep(blocks, resume):
        depth = range(len(blocks[0]))

        def rows_of(ref, t, n, p):
            return ref[pl.ds(blocks[t][n][0], SB_BLOCK), p * LANES:(p + 1) * LANES]

        units = [(t, n) for t in tiles for n in depth]

        def logits(t, n):
            return [lax.dot_general(qs_scr[t, p], rows_of(k_scr, t, n, p),
                                    (((1,), (1,)), ((), ())), preferred_element_type=F32)
                    for p in pairs]

        def cumsums(t, n, zs):
            hls = []
            for p in pairs:
                zbits = lax.bitcast_convert_type(zs[p], jnp.uint32)
                neg_abs = lax.bitcast_convert_type(zbits | jnp.uint32(SIGN_BIT), F32)
                sp = jnp.maximum(zs[p], 0.0) + jnp.log(1.0 + jnp.exp(neg_abs))
                if blocks[t][n][1] is not None:
                    sp = jnp.where(blocks[t][n][1], sp, 0.0)
                hi = lax.bitcast_convert_type(
                    lax.bitcast_convert_type(sp, jnp.uint32) & jnp.uint32(BF16_BITS), F32)
                hls.append(jnp.concatenate([hi.astype(BF16), (sp - hi).astype(BF16)], axis=1))
            return jnp.dot(jnp.concatenate(hls, axis=0), tri_ref[...], preferred_element_type=F32)

        carry = {(t, p): carry_scr[t, p] if resume else None for t in tiles for p in pairs}
        acc = {(t, p): acc_scr[t, p] if resume else None for t in tiles for p in pairs}

        def weigh(t, n, zs, rt):
            for p in pairs:
                within = rt[p * rows:(p + 1) * rows, :SB_BLOCK]
                total = rt[p * rows:(p + 1) * rows, SB_BLOCK:]
                before = carry[t, p]
                w = jnp.exp(zs[p] - (within if before is None else within + before))
                if blocks[t][n][1] is not None:
                    w = jnp.where(blocks[t][n][1], w, 0.0)
                carry[t, p] = total if before is None else before + total
                pv = jnp.dot(w.astype(BF16), rows_of(v_scr, t, n, p), preferred_element_type=F32)
                acc[t, p] = pv if acc[t, p] is None else acc[t, p] + pv

        zs, rts = {}, {}
        for u in range(len(units) + 2):
            if u < len(units):
                zs[u] = logits(*units[u])
            if 1 <= u <= len(units):
                rts[u - 1] = cumsums(*units[u - 1], zs[u - 1])
            if u >= 2:
                weigh(*units[u - 2], zs.pop(u - 2), rts.pop(u - 2))
        lowest = None
        for t in tiles:
            for p in pairs:
                carry_scr[t, p] = carry[t, p]
                acc_scr[t, p] = acc[t, p]
                lowest = carry[t, p] if lowest is None else jnp.minimum(lowest, carry[t, p])
        return jnp.min(lowest)

    def aligned(start):
        return pl.multiple_of(start, SB_TILE)

    def first_fast():
        near_mask = col < row + (SB_BLOCK - SB_TILE)
        return sweep([[(aligned(q0 + (t + 1) * SB_TILE - (n + 1) * SB_BLOCK),
                        near_mask if n == 0 else None) for n in range(SB_FIRST_BLOCKS)]
                      for t in tiles], False)

    def first_clamped():
        blocks = []
        for t in tiles:
            limit = q0 + t * SB_TILE + row
            blocks.append([])
            for n in range(SB_FIRST_BLOCKS):
                start = aligned(jnp.maximum(q0 + (t + 1) * SB_TILE - (n + 1) * SB_BLOCK, 0))
                blocks[t].append((start, start + col < limit))
                limit = start
        return sweep(blocks, False)

    def more(st):
        reach = q0 + SB_QROWS - st[0] * SB_BLOCK
        return jnp.logical_and(reach > 0, st[1] < SB_DEAD_LOG)

    def body(st):
        blocks = []
        for t in tiles:
            limit = jnp.maximum(q0 + (t + 1) * SB_TILE - st[0] * SB_BLOCK, 0)
            start = aligned(jnp.maximum(limit - SB_BLOCK, 0))
            blocks.append([(start, start + col < limit)])
        return st[0] + 1, sweep(blocks, True)

    fits = q0 + SB_TILE - SB_FIRST_BLOCKS * SB_BLOCK >= 0
    lowest = lax.cond(fits, first_fast, first_clamped)
    lax.while_loop(more, body, (SB_FIRST_BLOCKS, lowest))
    for p in pairs:
        for t in tiles:
            acc = acc_scr[t, p]
            out = acc[0:SB_TILE]
            for hh in range(1, SB_HEADS_PER_GROUP):
                out = jnp.where(lane // HEAD_DIM == hh, acc[hh * SB_TILE:(hh + 1) * SB_TILE], out)
            o_ref[0, t * SB_TILE:(t + 1) * SB_TILE, p * LANES:(p + 1) * LANES] = out.astype(BF16)


def _stick_breaking(q, k, v, tri):
    b, s, d_sb = q.shape
    n_pairs = d_sb // LANES
    rows = SB_HEADS_PER_GROUP * SB_TILE
    blk = pl.BlockSpec((1, SB_QROWS, d_sb), lambda bi, i: (bi, i, 0))
    return pl.pallas_call(
        functools.partial(_sb_kernel, n_pairs=n_pairs),
        out_shape=jax.ShapeDtypeStruct((b, s, d_sb), BF16),
        grid=(b, s // SB_QROWS),
        in_specs=[blk, blk, blk, _full(tri.shape)],
        out_specs=blk,
        scratch_shapes=[pltpu.VMEM((s, d_sb), BF16), pltpu.VMEM((s, d_sb), BF16),
                        pltpu.VMEM((SB_TILES, n_pairs, rows, LANES), BF16),
                        pltpu.VMEM((SB_TILES, n_pairs, rows, LANES), F32),
                        pltpu.VMEM((SB_TILES, n_pairs, rows, LANES), F32)],
        compiler_params=_cparams(2),
        name="stick_breaking",
    )(q, k, v, tri)


def _ret_kernel(qk_ref, vg_ref, dmat_ref, qdec_ref, kdec_ref, cdec_ref, bd_ref, avg_ref, gain_ref,
                o_ref, state_scr, *, d_ret, n_batch):
    n_heads = d_ret // HEAD_DIM
    chains = [(b, slice(c * RET_BLOCK, (c + 1) * RET_BLOCK))
              for c in range(RET_STEP_BLOCKS) for b in range(n_batch)]
    nt = (((1,), (1,)), ((), ()))

    @pl.when(pl.program_id(0) == 0)
    def _():
        state_scr[...] = jnp.zeros_like(state_scr)

    lane = lax.broadcasted_iota(jnp.int32, (1, d_ret), 1)
    in_head = [lane // HEAD_DIM == h for h in range(n_heads)]
    q = [qk_ref[b, rows, :d_ret] for b, rows in chains]
    k = [qk_ref[b, rows, d_ret:] for b, rows in chains]
    v = [vg_ref[b, rows, :d_ret] for b, rows in chains]
    probs = []
    for i in range(len(chains)):
        kb = k[i].astype(BF16)
        probs.append(jnp.concatenate(
            [(lax.dot_general(jnp.where(in_head[h], q[i], 0.0).astype(BF16), kb, nt,
                              preferred_element_type=F32) * dmat_ref[h]).astype(BF16)
             for h in range(n_heads)], axis=1))
    y = []
    for i in range(len(chains)):
        vals = jnp.concatenate([jnp.where(in_head[h], v[i], 0.0).astype(BF16)
                                for h in range(n_heads)], axis=0)
        y.append(jnp.dot(probs[i], vals, preferred_element_type=F32))
    for i, (b, _) in enumerate(chains):
        state = state_scr[b]
        y[i] = y[i] + jnp.dot((q[i] * qdec_ref[...]).astype(BF16), state.astype(BF16),
                              preferred_element_type=F32)
        kv = lax.dot_general((k[i] * kdec_ref[...]).astype(BF16), v[i].astype(BF16),
                             (((0,), (0,)), ((), ())), preferred_element_type=F32)
        state_scr[b] = cdec_ref[...] * state + bd_ref[...] * kv

    def head_mean(t):
        hi = t.astype(BF16)
        lo = (t - hi.astype(F32)).astype(BF16)
        return jnp.dot(jnp.concatenate([hi, lo], axis=1), avg_ref[...], preferred_element_type=F32)

    yc = [y[i] - head_mean(y[i]) for i in range(len(chains))]
    var = [head_mean(yc[i] * yc[i]) for i in range(len(chains))]
    for i, (b, rows) in enumerate(chains):
        g = vg_ref[b, rows, d_ret:]
        o_ref[b, rows, :] = (_silu(g) * (yc[i] * lax.rsqrt(var[i] + EPS) * gain_ref[...])).astype(BF16)


def _retention(qk, vg, tabs, gain):
    b, s, d2 = qk.shape
    d_ret = d2 // 2
    step_rows = RET_STEP_BLOCKS * RET_BLOCK
    blk = pl.BlockSpec((b, step_rows, d2), lambda i: (0, i, 0))
    return pl.pallas_call(
        functools.partial(_ret_kernel, d_ret=d_ret, n_batch=b),
        out_shape=jax.ShapeDtypeStruct((b, s, d_ret), BF16),
        grid=(s // step_rows,),
        in_specs=[blk, blk] + [_full(t.shape) for t in tabs] + [_full((1, d_ret))],
        out_specs=pl.BlockSpec((b, step_rows, d_ret), lambda i: (0, i, 0)),
        scratch_shapes=[pltpu.VMEM((b, d_ret, d_ret), F32)],
        compiler_params=_cparams(1),
        name="retention",
    )(qk, vg, *tabs, gain)


def _rotary_tables(seq):
    half = HEAD_DIM // 2
    inv = 1.0 / (ROPE_BASE ** (jnp.arange(half, dtype=F32) / half))
    lane = jnp.arange(LANES)
    freq = inv[lane % half][None, :]
    sign = jnp.where((lane % HEAD_DIM) < half, -1.0, 1.0).astype(F32)
    coarse = (jnp.arange(seq // CHUNK) * CHUNK).astype(F32)[:, None] * freq
    fine = jnp.arange(CHUNK).astype(F32)[:, None] * freq
    ch, sh = jnp.cos(coarse)[:, None, :], jnp.sin(coarse)[:, None, :]
    cf, sf = jnp.cos(fine)[None], jnp.sin(fine)[None]
    cos = (ch * cf - sh * sf).reshape(seq, LANES)
    sin = (sh * cf + ch * sf).reshape(seq, LANES)
    return cos, sin * sign[None, :]


def _retention_tables(n_heads):
    d_ret = n_heads * HEAD_DIM
    log_gamma = jnp.log1p(-jnp.exp2(-5.0 - jnp.arange(n_heads, dtype=F32)))
    idx = jnp.arange(RET_BLOCK, dtype=F32)
    t, s = idx[:, None], idx[None, :]
    same = (t // CHUNK) == (s // CHUNK)
    dist = jnp.where(same, jnp.abs(t - s), t - s)
    seen = same | ((s // CHUNK) < (t // CHUNK))
    dmat = jnp.where(seen[None], jnp.exp(log_gamma[:, None, None] * dist[None]), 0.0)
    lane_gamma = jnp.repeat(log_gamma, HEAD_DIM)
    qdec = jnp.exp(lane_gamma[None, :] * (idx + 1.0)[:, None])
    kdec = jnp.exp(lane_gamma[None, :] * (RET_BLOCK - 1.0 - idx)[:, None])
    head = jnp.arange(d_ret) // HEAD_DIM
    bd = (head[:, None] == head[None, :]).astype(F32)
    cdec = bd * jnp.exp(lane_gamma * RET_BLOCK)[:, None]
    avg = jnp.concatenate([bd, bd], axis=0).astype(BF16) * (1.0 / HEAD_DIM)
    return dmat, qdec, kdec, cdec, bd, avg.astype(BF16)


def _tri_table():
    r = np.arange(2 * SB_BLOCK)[:, None] % SB_BLOCK
    c = np.arange(2 * SB_BLOCK)[None, :]
    return jnp.asarray((c >= SB_BLOCK) | (r >= c), dtype=BF16)


def kernel(x, ffn1_norm, ffn1_w_in, ffn1_w_out, mix_norm, mix_w_in, conv_w, conv_b, conv_ln_g,
           conv_ln_b, ret_norm_g, mix_w_out, ffn2_norm, ffn2_w_in, ffn2_w_out, final_norm):
    b, s, d = x.shape
    depth = ffn1_norm.shape[0]
    d_conv = conv_w.shape[2]
    d_ret = ret_norm_g.shape[1]
    d_sb = d - d_conv - d_ret
    assert s % ROW_TILE == 0 and s % FFN_TILE == 0 and s % (RET_STEP_BLOCKS * RET_BLOCK) == 0 and s % SB_QROWS == 0
    assert ffn1_w_out.shape[1] % FF_CHUNK == 0 and d % OUT_TILE == 0

    cos_tab, sin_tab = _rotary_tables(s)
    ret_tabs = _retention_tables(d_ret // HEAD_DIM)
    tri = _tri_table()
    win1, wout1 = ffn1_w_in.astype(BF16), ffn1_w_out.astype(BF16)
    later = (mix_w_in, mix_w_out, ffn2_w_in, ffn2_w_out)

    row = lambda g: g.reshape(1, -1)
    xf = x.reshape(b * s, d)
    for l in range(depth):
        if l == 0:
            xf, *rounded = _ffn(xf, row(ffn1_norm[l]), win1, wout1, l, FFN_TILE,
                                cast=[w.reshape(-1, w.shape[-1]) for w in later])
            w_mix, w_mix_out, win2, wout2 = [r.reshape(w.shape) for r, w in zip(rounded, later)]
        else:
            xf = _ffn(xf, row(ffn1_norm[l]), win1, wout1, l, FFN_TILE)
        uconv, q, k, v, qkr, vgr = _inproj(xf, row(mix_norm[l]), w_mix, l, cos_tab, sin_tab, s,
                                           2 * d_conv, d_sb, d_ret)
        seq3 = lambda t: t.reshape(b, s, t.shape[1])
        y_sb = _stick_breaking(seq3(q), seq3(k), seq3(v), tri)
        y_ret = _retention(seq3(qkr), seq3(vgr), ret_tabs, row(ret_norm_g[l]))
        flat = lambda t: t.reshape(b * s, t.shape[2])
        last = l == depth - 1
        xf = _ffn(xf, row(ffn2_norm[l]), win2, wout2, l, ROW_TILE,
                  mix=(flat(y_sb), flat(y_ret)), wmix=w_mix_out,
                  final_g=row(final_norm) if last else None,
                  conv=(uconv, s, conv_w[l], row(conv_b[l]), row(conv_ln_g[l]), row(conv_ln_b[l])))
    return xf.reshape(b, s, d)
```
